```python
import math
import jax, jax.numpy as jnp
from jax import lax
import numpy as np

D_MODEL = 1024
BATCH = 4
SEQ = 4096
DEPTH = 1
DEC_BATCH = 32
DEC_SEQ = 32
PAST_LEN = 1024

CHUNK = 64
SSM_WIDTH = D_MODEL // 2
SSM_GROUP = 16
SSM_GROUPS = SSM_WIDTH // SSM_GROUP
SSM_STATE = 64
DT_MIN = 1e-3
DT_MAX = 1e-1
POOL_WIDTH = D_MODEL // 2
POOL_WINDOWS = (2, 4, 8, 16)
POOL_GROUPS = len(POOL_WINDOWS)
POOL_GROUP_IN = POOL_WIDTH // POOL_GROUPS
POOL_GROUP_OUT = D_MODEL // POOL_GROUPS
POOL_HIST = max(POOL_WINDOWS) - 1
N_BRANCHES = 2
IN_WIDTH = SSM_WIDTH + POOL_WIDTH + N_BRANCHES * D_MODEL
N_EXPERT_GROUPS = 4
EXPERTS_PER_GROUP = 8
N_EXPERTS = N_EXPERT_GROUPS * EXPERTS_PER_GROUP
TOP_K_INNER = 2
D_EXPERT = D_MODEL // 2
EPS = 1e-6

kernel_name = 'hybrid_s5_pool_hmoe_stream_step'


def _rmsnorm(x, g):
    xf = x.astype(jnp.float32)
    r = lax.rsqrt(jnp.mean(xf * xf, axis=-1, keepdims=True) + EPS)
    return (xf * r * g.astype(jnp.float32)).astype(x.dtype)


def _ssm_discretise(a_re, a_im, log_dt, b_re, b_im):
    lam_re = jnp.minimum(a_re.astype(jnp.float32), -1e-4)
    lam_im = a_im.astype(jnp.float32)
    dt = jnp.exp(log_dt.astype(jnp.float32))[:, None]
    mag = jnp.exp(lam_re * dt)
    ang = lam_im * dt
    abar_re = mag * jnp.cos(ang)
    abar_im = mag * jnp.sin(ang)
    num_re = abar_re - 1.0
    num_im = abar_im
    den = lam_re * lam_re + lam_im * lam_im
    k_re = (num_re * lam_re + num_im * lam_im) / den
    k_im = (num_im * lam_re - num_re * lam_im) / den
    br = b_re.astype(jnp.float32)
    bi = b_im.astype(jnp.float32)
    bbar_re = k_re[..., None] * br - k_im[..., None] * bi
    bbar_im = k_re[..., None] * bi + k_im[..., None] * br
    return abar_re, abar_im, bbar_re, bbar_im


def _scan_combine(e1, e2):
    a1r, a1i, b1r, b1i = e1
    a2r, a2i, b2r, b2i = e2
    ar = a2r * a1r - a2i * a1i
    ai = a2r * a1i + a2i * a1r
    br = a2r * b1r - a2i * b1i + b2r
    bi = a2r * b1i + a2i * b1r + b2i
    return ar, ai, br, bi


def _ssm_branch(u, h0_re, h0_im, abar_re, abar_im, bbar_re, bbar_im, c_re, c_im, d, w_glu_a, w_glu_b):
    b, l, _ = u.shape
    uf = u.astype(jnp.float32).reshape(b, l, SSM_GROUPS, SSM_GROUP)
    bu_re = jnp.einsum('blgp,gnp->blgn', uf, bbar_re)
    bu_im = jnp.einsum('blgp,gnp->blgn', uf, bbar_im)
    h0r = h0_re.astype(jnp.float32)
    h0i = h0_im.astype(jnp.float32)
    bu_re = bu_re.at[:, 0].add(abar_re * h0r - abar_im * h0i)
    bu_im = bu_im.at[:, 0].add(abar_re * h0i + abar_im * h0r)
    a_re = jnp.broadcast_to(abar_re, bu_re.shape)
    a_im = jnp.broadcast_to(abar_im, bu_im.shape)
    _, _, h_re, h_im = lax.associative_scan(_scan_combine, (a_re, a_im, bu_re, bu_im), axis=1)
    y = (jnp.einsum('blgn,gpn->blgp', h_re, c_re.astype(jnp.float32))
         - jnp.einsum('blgn,gpn->blgp', h_im, c_im.astype(jnp.float32))
         + d.astype(jnp.float32) * uf)
    g = jax.nn.gelu(y.reshape(b, l, SSM_WIDTH)).astype(u.dtype)
    out = (g @ w_glu_a) * jax.nn.sigmoid(g @ w_glu_b)
    return out, h_re[:, -1], h_im[:, -1]


def _pool_branch(u, hist, start, pool_w, pool_scale):
    b, l, _ = u.shape
    uf = u.astype(jnp.float32)
    xp = jnp.concatenate([hist.astype(jnp.float32), uf], axis=1)
    cs = jnp.cumsum(xp, axis=1)
    cs0 = jnp.concatenate([jnp.zeros((b, 1, POOL_WIDTH), jnp.float32), cs], axis=1)
    pos = start + jnp.arange(l)
    pooled = []
    for gi, w in enumerate(POOL_WINDOWS):
        c0, c1 = gi * POOL_GROUP_IN, (gi + 1) * POOL_GROUP_IN
        hi = cs0[:, POOL_HIST + 1:, c0:c1]
        lo = cs0[:, POOL_HIST + 1 - w:POOL_HIST + 1 - w + l, c0:c1]
        cnt = jnp.minimum(w, pos + 1).astype(jnp.float32)[None, :, None]
        pooled.append((hi - lo) / cnt)
    pooled = jnp.stack(pooled, axis=2)
    z = (pooled - uf.reshape(b, l, POOL_GROUPS, POOL_GROUP_IN)).astype(u.dtype)
    o = jnp.einsum('blgc,gcd->blgd', z, pool_w).reshape(b, l, D_MODEL) * pool_scale
    return o, xp[:, -POOL_HIST:].astype(u.dtype)


def _hmoe(x, g, w_router_group, w_router_expert, w_gate, w_up, w_down):
    b, l, dm = x.shape
    xn = _rmsnorm(x, g).reshape(b * l, dm)
    n = b * l
    lg = (xn @ w_router_group).astype(jnp.float32)
    pg = jax.nn.softmax(lg, axis=-1)
    grp = jnp.argmax(lg, axis=-1)
    wg = jnp.max(pg, axis=-1)
    le = jnp.einsum('nd,dge->nge', xn, w_router_expert).astype(jnp.float32)
    le_sel = le[jnp.arange(n), grp]
    top_v, top_i = lax.top_k(le_sel, TOP_K_INNER)
    wk = jax.nn.softmax(top_v, axis=-1) * wg[:, None]
    eidx = grp[:, None] * EXPERTS_PER_GROUP + top_i
    combine = jnp.sum(jax.nn.one_hot(eidx, N_EXPERTS, dtype=jnp.float32) * wk[..., None], axis=1)
    y = jnp.zeros((n, dm), jnp.float32)
    for e in range(N_EXPERTS):
        h = jax.nn.silu(xn @ w_gate[e]) * (xn @ w_up[e])
        y = y + combine[:, e:e + 1] * (h @ w_down[e]).astype(jnp.float32)
    return x + y.reshape(b, l, dm).astype(x.dtype)


def setup_inputs(seed: int = 0) -> dict:
    key = jax.random.key(seed)
    ks = jax.random.split(key, 26)
    f32 = jnp.float32
    nrm = lambda k, s, sc: jax.random.normal(k, s, f32) * sc
    n_idx = jnp.arange(SSM_STATE, dtype=f32)
    return {
        'x_prompt': nrm(ks[0], (BATCH, SEQ, D_MODEL), 1.0),
        'x_sample': nrm(ks[1], (DEC_BATCH, DEC_SEQ, D_MODEL), 1.0),
        'state_ssm_re': nrm(ks[2], (DEPTH, DEC_BATCH, SSM_GROUPS, SSM_STATE), 0.1),
        'state_ssm_im': nrm(ks[3], (DEPTH, DEC_BATCH, SSM_GROUPS, SSM_STATE), 0.1),
        'cache_pool': nrm(ks[4], (DEPTH, DEC_BATCH, POOL_HIST, POOL_WIDTH), 1.0),
        'g_mix': 1.0 + nrm(ks[5], (DEPTH, D_MODEL), 0.02),
        'w_in': nrm(ks[6], (DEPTH, D_MODEL, IN_WIDTH), D_MODEL ** -0.5),
        'ssm_a_re': -0.5 + nrm(ks[7], (DEPTH, SSM_GROUPS, SSM_STATE), 0.01),
        'ssm_a_im': math.pi * n_idx + nrm(ks[8], (DEPTH, SSM_GROUPS, SSM_STATE), 0.01),
        'ssm_log_dt': jax.random.uniform(ks[9], (DEPTH, SSM_GROUPS), f32, math.log(DT_MIN), math.log(DT_MAX)),
        'ssm_b_re': nrm(ks[10], (DEPTH, SSM_GROUPS, SSM_STATE, SSM_GROUP), (2 * SSM_GROUP) ** -0.5),
        'ssm_b_im': nrm(ks[11], (DEPTH, SSM_GROUPS, SSM_STATE, SSM_GROUP), (2 * SSM_GROUP) ** -0.5),
        'ssm_c_re': nrm(ks[12], (DEPTH, SSM_GROUPS, SSM_GROUP, SSM_STATE), SSM_STATE ** -0.5),
        'ssm_c_im': nrm(ks[13], (DEPTH, SSM_GROUPS, SSM_GROUP, SSM_STATE), SSM_STATE ** -0.5),
        'ssm_d': nrm(ks[14], (DEPTH, SSM_GROUPS, SSM_GROUP), 1.0),
        'w_glu_a': nrm(ks[15], (DEPTH, SSM_WIDTH, D_MODEL), SSM_WIDTH ** -0.5),
        'w_glu_b': nrm(ks[16], (DEPTH, SSM_WIDTH, D_MODEL), SSM_WIDTH ** -0.5),
        'pool_w': nrm(ks[17], (DEPTH, POOL_GROUPS, POOL_GROUP_IN, POOL_GROUP_OUT), POOL_GROUP_IN ** -0.5),
        'pool_scale': 1.0 + nrm(ks[18], (DEPTH, D_MODEL), 0.02),
        'w_out': nrm(ks[19], (DEPTH, D_MODEL, D_MODEL), D_MODEL ** -0.5),
        'g_ffn': 1.0 + nrm(ks[20], (DEPTH, D_MODEL), 0.02),
        'w_router_group': nrm(ks[21], (DEPTH, D_MODEL, N_EXPERT_GROUPS), D_MODEL ** -0.5),
        'w_router_expert': nrm(ks[22], (DEPTH, D_MODEL, N_EXPERT_GROUPS, EXPERTS_PER_GROUP), D_MODEL ** -0.5),
        'w_exp_gate': nrm(ks[23], (DEPTH, N_EXPERTS, D_MODEL, D_EXPERT), D_MODEL ** -0.5),
        'w_exp_up': nrm(ks[24], (DEPTH, N_EXPERTS, D_MODEL, D_EXPERT), D_MODEL ** -0.5),
        'w_exp_down': nrm(ks[25], (DEPTH, N_EXPERTS, D_EXPERT, D_MODEL), D_EXPERT ** -0.5),
        'g_final': 1.0 + nrm(jax.random.fold_in(key, 99), (D_MODEL,), 0.02),
    }


def reference(x_prompt, x_sample, state_ssm_re, state_ssm_im, cache_pool, g_mix, w_in, ssm_a_re, ssm_a_im,
              ssm_log_dt, ssm_b_re, ssm_b_im, ssm_c_re, ssm_c_im, ssm_d, w_glu_a, w_glu_b, pool_w, pool_scale,
              w_out, g_ffn, w_router_group, w_router_expert, w_exp_gate, w_exp_up, w_exp_down, g_final):
    assert x_sample.shape[1] <= CHUNK
    disc = [_ssm_discretise(ssm_a_re[li], ssm_a_im[li], ssm_log_dt[li], ssm_b_re[li], ssm_b_im[li])
            for li in range(DEPTH)]

    def trunk(x, h_re, h_im, hist, start):
        new_re, new_im, new_hist = [], [], []
        for li in range(DEPTH):
            abar_re, abar_im, bbar_re, bbar_im = disc[li]
            xn = _rmsnorm(x, g_mix[li])
            proj = xn @ w_in[li]
            u_s = proj[..., :SSM_WIDTH]
            u_p = proj[..., SSM_WIDTH:SSM_WIDTH + POOL_WIDTH]
            gate_s = proj[..., SSM_WIDTH + POOL_WIDTH:SSM_WIDTH + POOL_WIDTH + D_MODEL]
            gate_p = proj[..., SSM_WIDTH + POOL_WIDTH + D_MODEL:]
            o_s, hr, hi = _ssm_branch(u_s, h_re[li], h_im[li], abar_re, abar_im, bbar_re, bbar_im,
                                      ssm_c_re[li], ssm_c_im[li], ssm_d[li], w_glu_a[li], w_glu_b[li])
            o_p, nh = _pool_branch(u_p, hist[li], start, pool_w[li], pool_scale[li])
            merged = jax.nn.sigmoid(gate_s) * o_s + jax.nn.sigmoid(gate_p) * o_p
            x = x + merged @ w_out[li]
            x = _hmoe(x, g_ffn[li], w_router_group[li], w_router_expert[li],
                      w_exp_gate[li], w_exp_up[li], w_exp_down[li])
            new_re.append(hr)
            new_im.append(hi)
            new_hist.append(nh)
        return _rmsnorm(x, g_final), jnp.stack(new_re), jnp.stack(new_im), jnp.stack(new_hist)

    bp = x_prompt.shape[0]
    zero_state = jnp.zeros((DEPTH, bp, SSM_GROUPS, SSM_STATE), jnp.float32)
    zero_hist = jnp.zeros((DEPTH, bp, POOL_HIST, POOL_WIDTH), x_prompt.dtype)
    y_prompt, re_p, im_p, hist_p = trunk(x_prompt, zero_state, zero_state, zero_hist, 0)
    y_sample, re_s, im_s, hist_s = trunk(x_sample, state_ssm_re, state_ssm_im, cache_pool, PAST_LEN)
    sd = state_ssm_re.dtype
    cd = cache_pool.dtype
    return (y_prompt, y_sample, re_p.astype(sd), im_p.astype(sd), hist_p.astype(cd),
            re_s.astype(sd), im_s.astype(sd), hist_s.astype(cd))
```

```python
import functools
import math

import jax
import jax.numpy as jnp
from jax import lax
from jax.experimental import pallas as pl
from jax.experimental.pallas import tpu as pltpu

F32 = jnp.float32
BF16 = jnp.bfloat16

D_MODEL = 1024
SSM_WIDTH = 512
SSM_GROUPS = 32
SSM_GROUP = 16
SSM_STATE = 64
N_FLAT = SSM_GROUPS * SSM_STATE
HALF_FLAT = N_FLAT // 2
POOL_WIDTH = 512
POOL_WINDOWS = (2, 4, 8, 16)
POOL_GROUP_IN = 128
POOL_GROUP_OUT = 256
POOL_HIST = 15
N_EXPERTS = 32
EXPERTS_PER_GROUP = 8
N_EXPERT_GROUPS = 4
D_EXPERT = 512
EPS = 1e-6
PAST_LEN = 1024

TM = 256
N_CHAIN = 8
CHAIN_LEN = TM // N_CHAIN
HIST_ROWS = 16 * N_CHAIN

CHUNK = 8
TILE_CHUNKS = TM // CHUNK
CAP_CHUNKS = 96
CAP_ROWS = CAP_CHUNKS * CHUNK
LANES = 128

VMEM_LIMIT = 52 * 1024 * 1024


def _rms(x, g):
    r = lax.rsqrt(jnp.mean(x * x, axis=-1, keepdims=True) + EPS)
    return x * r * g


def _sigmoid(x):
    return 1.0 / (1.0 + jnp.exp(-x))


def _gelu_tanh(x):
    c = math.sqrt(2.0 / math.pi)
    return x * (0.5 * (1.0 + jnp.tanh(c * (x + 0.044715 * (x * x * x)))))


def _disc_kernel(lre_ref, lim_ref, ldt_ref, bre_ref, bim_ref,
                 are_ref, aim_ref, a32re_ref, a32im_ref, bbre_ref, bbim_ref):
    lam_re = jnp.minimum(lre_ref[...], -1e-4)
    lam_im = lim_ref[...]
    dt = jnp.exp(ldt_ref[...])
    mag = jnp.exp(lam_re * dt)
    ang = lam_im * dt
    a_re = mag * jnp.cos(ang)
    a_im = mag * jnp.sin(ang)
    num_re = a_re - 1.0
    num_im = a_im
    den = lam_re * lam_re + lam_im * lam_im
    k_re = (num_re * lam_re + num_im * lam_im) / den
    k_im = (num_im * lam_re - num_re * lam_im) / den
    br = bre_ref[...]
    bi = bim_ref[...]
    bbre_ref[...] = k_re * br - k_im * bi
    bbim_ref[...] = k_re * bi + k_im * br
    are_ref[...] = a_re
    aim_ref[...] = a_im
    pr, pi = a_re, a_im
    for _ in range(int(math.log2(CHAIN_LEN))):
        pr, pi = pr * pr - pi * pi, 2.0 * pr * pi
    a32re_ref[...] = pr
    a32im_ref[...] = pi


def _discretise(a_re, a_im, log_dt, b_re, b_im):
    col = lambda v: v.reshape(N_FLAT, 1)
    ldt = jnp.broadcast_to(log_dt[:, None], (SSM_GROUPS, SSM_STATE))
    outs = pl.pallas_call(
        _disc_kernel,
        out_shape=[jax.ShapeDtypeStruct((N_FLAT, 1), F32)] * 4
        + [jax.ShapeDtypeStruct((N_FLAT, SSM_GROUP), F32)] * 2,
        name="s5_discretise",
    )(col(a_re), col(a_im), col(ldt), b_re.reshape(N_FLAT, SSM_GROUP), b_im.reshape(N_FLAT, SSM_GROUP))
    return outs


def _scan_half(hbuf, h, ar, ai, init_re, init_im, store):
    cre = h * N_FLAT
    cim = cre + HALF_FLAT
    hr, hi = init_re, init_im
    for t in range(CHAIN_LEN):
        rows = pl.ds(N_CHAIN * t, N_CHAIN)
        br = hbuf[rows, cre:cre + HALF_FLAT]
        bi = hbuf[rows, cim:cim + HALF_FLAT]
        nr = ar * hr - ai * hi + br
        ni = ar * hi + ai * hr + bi
        if store:
            hbuf[rows, cre:cre + HALF_FLAT] = nr
            hbuf[rows, cim:cim + HALF_FLAT] = ni
        hr, hi = nr, ni
    return hr, hi


def _mixer_kernel(is_prompt, tiles_per_stream, *refs):
    if is_prompt:
        (x_ref, gmix, win, wb, wcre, wcim, dsk, wglu, poolw, pscale, wout, are, aim, a32re, a32im,
         x2_ref, stre_ref, stim_ref, hist_ref,
         xperm, hbuf, res, xpbuf, fre, fim, hre, him, cre_s, cim_s, pcarry) = refs
    else:
        (x_ref, h0re, h0im, cachet, gmix, win, wb, wcre, wcim, dsk, wglu, poolw, pscale, wout, are, aim,
         a32re, a32im,
         x2_ref, stre_ref, stim_ref, hist_ref,
         xperm, hbuf, res, xpbuf) = refs

    tile_in_stream = pl.program_id(0) % tiles_per_stream

    n_lane_blocks = D_MODEL // LANES
    for j in range(N_CHAIN):
        for cb in range(n_lane_blocks):
            xperm[cb, pl.ds(j, CHAIN_LEN, stride=N_CHAIN), :] = x_ref[
                CHAIN_LEN * j:CHAIN_LEN * (j + 1), cb * LANES:(cb + 1) * LANES]
    x = jnp.concatenate([xperm[cb] for cb in range(n_lane_blocks)], axis=1)
    xn = _rms(x, gmix[...]).astype(BF16)
    proj = jnp.dot(xn, win[...], preferred_element_type=F32)
    u_s = proj[:, :SSM_WIDTH]
    u_p = proj[:, SSM_WIDTH:SSM_WIDTH + POOL_WIDTH]
    gate_s = proj[:, SSM_WIDTH + POOL_WIDTH:SSM_WIDTH + POOL_WIDTH + D_MODEL]
    gate_p = proj[:, SSM_WIDTH + POOL_WIDTH + D_MODEL:]

    ub = u_s.astype(BF16)
    half_w = SSM_WIDTH // 2
    for h in range(2):
        hbuf[:, h * N_FLAT:(h + 1) * N_FLAT] = jnp.dot(
            ub[:, h * half_w:(h + 1) * half_w], wb[h], preferred_element_type=F32)

    ar_full = are[...]
    ai_full = aim[...]
    if is_prompt:
        @pl.when(tile_in_stream == 0)
        def _():
            cre_s[...] = jnp.zeros_like(cre_s)
            cim_s[...] = jnp.zeros_like(cim_s)

        zeros = jnp.zeros((N_CHAIN, HALF_FLAT), F32)
        for h in range(2):
            f0 = h * HALF_FLAT
            ar = jnp.broadcast_to(ar_full[:, f0:f0 + HALF_FLAT], (N_CHAIN, HALF_FLAT))
            ai = jnp.broadcast_to(ai_full[:, f0:f0 + HALF_FLAT], (N_CHAIN, HALF_FLAT))
            lr, li = _scan_half(hbuf, h, ar, ai, zeros, zeros, store=False)
            fre[:, f0:f0 + HALF_FLAT] = lr
            fim[:, f0:f0 + HALF_FLAT] = li
        hre[0:1, :] = cre_s[...]
        him[0:1, :] = cim_s[...]
        p_re = a32re[...]
        p_im = a32im[...]
        for j in range(N_CHAIN - 1):
            sr = hre[j:j + 1, :]
            si = him[j:j + 1, :]
            hre[j + 1:j + 2, :] = fre[j:j + 1, :] + p_re * sr - p_im * si
            him[j + 1:j + 2, :] = fim[j:j + 1, :] + p_re * si + p_im * sr
        init_re = hre[...]
        init_im = him[...]
    else:
        init_re = h0re[...]
        init_im = h0im[...]

    fin_re = []
    fin_im = []
    for h in range(2):
        f0 = h * HALF_FLAT
        ar = jnp.broadcast_to(ar_full[:, f0:f0 + HALF_FLAT], (N_CHAIN, HALF_FLAT))
        ai = jnp.broadcast_to(ai_full[:, f0:f0 + HALF_FLAT], (N_CHAIN, HALF_FLAT))
        er, ei = _scan_half(hbuf, h, ar, ai, init_re[:, f0:f0 + HALF_FLAT], init_im[:, f0:f0 + HALF_FLAT],
                            store=True)
        fin_re.append(er)
        fin_im.append(ei)
    end_re = jnp.concatenate(fin_re, axis=1)
    end_im = jnp.concatenate(fin_im, axis=1)
    if is_prompt:
        cre_s[...] = end_re[N_CHAIN - 1:N_CHAIN, :]
        cim_s[...] = end_im[N_CHAIN - 1:N_CHAIN, :]
        stre_ref[0] = end_re[N_CHAIN - 1:N_CHAIN, :]
        stim_ref[0] = end_im[N_CHAIN - 1:N_CHAIN, :]
    else:
        stre_ref[...] = end_re
        stim_ref[...] = end_im

    ys = []
    for h in range(2):
        c0 = h * N_FLAT
        h_re = hbuf[:, c0:c0 + HALF_FLAT].astype(BF16)
        h_im = hbuf[:, c0 + HALF_FLAT:c0 + N_FLAT].astype(BF16)
        ys.append(jnp.dot(h_re, wcre[h], preferred_element_type=F32)
                  - jnp.dot(h_im, wcim[h], preferred_element_type=F32))
    y = jnp.concatenate(ys, axis=1) + dsk[...] * u_s
    g = _gelu_tanh(y).astype(BF16)
    glu = jnp.dot(g, wglu[...], preferred_element_type=F32)
    o_s = glu[:, :D_MODEL] * _sigmoid(glu[:, D_MODEL:])

    xpbuf[HIST_ROWS:HIST_ROWS + TM, :] = u_p
    tail = u_p[TM - HIST_ROWS:, :]
    if is_prompt:
        @pl.when(tile_in_stream == 0)
        def _():
            pcarry[...] = jnp.zeros_like(pcarry)

        first_chain = (lax.broadcasted_iota(jnp.int32, (HIST_ROWS, POOL_WIDTH), 0) % N_CHAIN) == 0
        xpbuf[0:HIST_ROWS, :] = jnp.where(first_chain, pcarry[...], pltpu.roll(tail, 1, 0))
        pcarry[...] = pltpu.roll(tail, HIST_ROWS - (N_CHAIN - 1), 0)
        hist_ref[0] = pcarry[...]
        row = lax.broadcasted_iota(jnp.int32, (TM, 1), 0)
        pos1 = tile_in_stream * TM + CHAIN_LEN * (row % N_CHAIN) + row // N_CHAIN + 1
    else:
        xpbuf[0:HIST_ROWS, :] = cachet[0]
        hist_ref[0] = tail
        row = lax.broadcasted_iota(jnp.int32, (TM, 1), 0)
        pos1 = PAST_LEN + row // N_CHAIN + 1

    o_ps = []
    for gi, w in enumerate(POOL_WINDOWS):
        c0 = gi * POOL_GROUP_IN
        acc = xpbuf[HIST_ROWS:HIST_ROWS + TM, c0:c0 + POOL_GROUP_IN]
        for k in range(1, w):
            acc = acc + xpbuf[HIST_ROWS - N_CHAIN * k:HIST_ROWS - N_CHAIN * k + TM, c0:c0 + POOL_GROUP_IN]
        cnt = jnp.minimum(w, pos1).astype(F32)
        pooled = acc / cnt
        z = (pooled - u_p[:, c0:c0 + POOL_GROUP_IN]).astype(BF16)
        o_ps.append(jnp.dot(z, poolw[gi], preferred_element_type=F32))
    o_p = jnp.concatenate(o_ps, axis=1) * pscale[...]

    merged = (_sigmoid(gate_s) * o_s + _sigmoid(gate_p) * o_p).astype(BF16)
    x2 = x + jnp.dot(merged, wout[...], preferred_element_type=F32)
    for cb in range(n_lane_blocks):
        res[cb] = x2[:, cb * LANES:(cb + 1) * LANES]
    for j in range(N_CHAIN):
        for cb in range(n_lane_blocks):
            x2_ref[CHAIN_LEN * j:CHAIN_LEN * (j + 1), cb * LANES:(cb + 1) * LANES] = res[
                cb, pl.ds(j, CHAIN_LEN, stride=N_CHAIN), :]


def _const_spec(shape):
    nd = len(shape)
    return pl.BlockSpec(shape, lambda i, _nd=nd: (0,) * _nd)


def _mixer_weight_specs():
    return [
        _const_spec((1, D_MODEL)),
        _const_spec((D_MODEL, 3 * D_MODEL)),
        _const_spec((2, SSM_WIDTH // 2, N_FLAT)),
        _const_spec((2, HALF_FLAT, SSM_WIDTH // 2)),
        _const_spec((2, HALF_FLAT, SSM_WIDTH // 2)),
        _const_spec((1, SSM_WIDTH)),
        _const_spec((SSM_WIDTH, 2 * D_MODEL)),
        _const_spec((len(POOL_WINDOWS), POOL_GROUP_IN, POOL_GROUP_OUT)),
        _const_spec((1, D_MODEL)),
        _const_spec((D_MODEL, D_MODEL)),
        _const_spec((1, N_FLAT)),
        _const_spec((1, N_FLAT)),
        _const_spec((1, N_FLAT)),
        _const_spec((1, N_FLAT)),
    ]


def _mixer_common_scratch():
    return [
        pltpu.VMEM((D_MODEL // LANES, TM, LANES), F32),
        pltpu.VMEM((TM, 2 * N_FLAT), F32),
        pltpu.VMEM((D_MODEL // LANES, TM, LANES), F32),
        pltpu.VMEM((HIST_ROWS + TM, POOL_WIDTH), F32),
    ]


def _mixer_prompt(x, weights):
    n_streams, seq, _ = x.shape
    tiles_per_stream = seq // TM
    n_tiles = n_streams * tiles_per_stream
    x2d = x.reshape(n_streams * seq, D_MODEL)
    row_spec = pl.BlockSpec((TM, D_MODEL), lambda i: (i, 0))
    stream_spec = lambda shape: pl.BlockSpec(shape, lambda i: (i // tiles_per_stream, 0, 0))
    return pl.pallas_call(
        functools.partial(_mixer_kernel, True, tiles_per_stream),
        grid=(n_tiles,),
        in_specs=[row_spec] + _mixer_weight_specs(),
        out_specs=[row_spec, stream_spec((1, 1, N_FLAT)), stream_spec((1, 1, N_FLAT)),
                   stream_spec((1, HIST_ROWS, POOL_WIDTH))],
        out_shape=[jax.ShapeDtypeStruct((n_streams * seq, D_MODEL), F32),
                   jax.ShapeDtypeStruct((n_streams, 1, N_FLAT), F32),
                   jax.ShapeDtypeStruct((n_streams, 1, N_FLAT), F32),
                   jax.ShapeDtypeStruct((n_streams, HIST_ROWS, POOL_WIDTH), F32)],
        scratch_shapes=_mixer_common_scratch() + [
            pltpu.VMEM((N_CHAIN, N_FLAT), F32), pltpu.VMEM((N_CHAIN, N_FLAT), F32),
            pltpu.VMEM((N_CHAIN, N_FLAT), F32), pltpu.VMEM((N_CHAIN, N_FLAT), F32),
            pltpu.VMEM((1, N_FLAT), F32), pltpu.VMEM((1, N_FLAT), F32),
            pltpu.VMEM((HIST_ROWS, POOL_WIDTH), F32),
        ],
        compiler_params=pltpu.CompilerParams(dimension_semantics=("arbitrary",), vmem_limit_bytes=VMEM_LIMIT),
        name="mixer_prompt",
    )(x2d, *weights)


def _mixer_sample(x, h0_re, h0_im, cache_t, weights):
    n_streams, seq, _ = x.shape
    assert seq == CHAIN_LEN and n_streams % N_CHAIN == 0
    n_tiles = n_streams // N_CHAIN
    x2d = x.reshape(n_streams * seq, D_MODEL)
    row_spec = pl.BlockSpec((TM, D_MODEL), lambda i: (i, 0))
    st_spec = pl.BlockSpec((N_CHAIN, N_FLAT), lambda i: (i, 0))
    hist_spec = pl.BlockSpec((1, HIST_ROWS, POOL_WIDTH), lambda i: (i, 0, 0))
    return pl.pallas_call(
        functools.partial(_mixer_kernel, False, 1),
        grid=(n_tiles,),
        in_specs=[row_spec, st_spec, st_spec, hist_spec] + _mixer_weight_specs(),
        out_specs=[row_spec, st_spec, st_spec, hist_spec],
        out_shape=[jax.ShapeDtypeStruct((n_streams * seq, D_MODEL), F32),
                   jax.ShapeDtypeStruct((n_streams, N_FLAT), F32),
                   jax.ShapeDtypeStruct((n_streams, N_FLAT), F32),
                   jax.ShapeDtypeStruct((n_tiles, HIST_ROWS, POOL_WIDTH), F32)],
        scratch_shapes=_mixer_common_scratch(),
        compiler_params=pltpu.CompilerParams(dimension_semantics=("arbitrary",), vmem_limit_bytes=VMEM_LIMIT),
        name="mixer_sample",
    )(x2d, h0_re, h0_im, cache_t, *weights)


def _to_chunks(lo, hi):
    n = lo.shape[0] // CHUNK
    half = D_MODEL // 2
    both = jnp.concatenate([lo.reshape(n, CHUNK, half), hi.reshape(n, CHUNK, half)], axis=1)
    return both.astype(BF16)


def _from_chunks(blk):
    n = blk.shape[0]
    half = D_MODEL // 2
    f = blk.astype(F32)
    lo = f[:, :CHUNK, :].reshape(n * CHUNK, half).astype(BF16)
    hi = f[:, CHUNK:, :].reshape(n * CHUNK, half).astype(BF16)
    return lo, hi


def _route_kernel(n_prompt_tiles, x2p_ref, x2s_ref, g_ref, wr_ref, ltri_ref, utri_ref,
                  xs_ref, route_ref, nch_ref):
    i = pl.program_id(0)
    x = jnp.where(i < n_prompt_tiles, x2p_ref[...], x2s_ref[...])
    xn = _rms(x, g_ref[...]).astype(BF16)
    logits = jnp.dot(xn, wr_ref[...], preferred_element_type=F32)
    lane = lax.broadcasted_iota(jnp.int32, (TM, LANES), 1)
    big = jnp.int32(1 << 20)
    neg = jnp.float32(-jnp.inf)

    gmask = lane < N_EXPERT_GROUPS
    m = jnp.max(jnp.where(gmask, logits, neg), axis=1, keepdims=True)
    grp = jnp.min(jnp.where(gmask & (logits == m), lane, big), axis=1, keepdims=True)
    wg = 1.0 / jnp.sum(jnp.where(gmask, jnp.exp(logits - m), 0.0), axis=1, keepdims=True)

    eid = lane - N_EXPERT_GROUPS
    emask = (eid >= 0) & (eid < N_EXPERTS) & ((eid >> 3) == grp)
    v1 = jnp.max(jnp.where(emask, logits, neg), axis=1, keepdims=True)
    i1 = jnp.min(jnp.where(emask & (logits == v1), lane, big), axis=1, keepdims=True)
    emask2 = emask & (lane != i1)
    v2 = jnp.max(jnp.where(emask2, logits, neg), axis=1, keepdims=True)
    i2 = jnp.min(jnp.where(emask2 & (logits == v2), lane, big), axis=1, keepdims=True)
    e21 = jnp.exp(v2 - v1)
    w1 = wg / (1.0 + e21)
    w2 = wg * e21 / (1.0 + e21)

    a1 = lane == (i1 - N_EXPERT_GROUPS)
    a2 = lane == (i2 - N_EXPERT_GROUPS)
    a = (a1 | a2).astype(F32)
    before = jnp.dot(ltri_ref[...], a.astype(BF16), preferred_element_type=F32)
    cnt = jnp.sum(a, axis=0, keepdims=True)
    nch = jnp.floor((cnt + (CHUNK - 1)) * (1.0 / CHUNK))
    nch16 = jnp.broadcast_to(nch, (16, LANES))
    start = jnp.dot(nch16.astype(BF16), utri_ref[...], preferred_element_type=F32)
    slot = before + CHUNK * start[0:1, :]
    d1 = jnp.sum(jnp.where(a1, slot, 0.0), axis=1, keepdims=True)
    d2 = jnp.sum(jnp.where(a2, slot, 0.0), axis=1, keepdims=True)

    route = jnp.where(lane == 0, d1, jnp.where(lane == 1, d2, jnp.where(lane == 2, w1,
                      jnp.where(lane == 3, w2, 0.0))))
    route_ref[...] = route
    nch_ref[0] = nch16[0:8, :].astype(jnp.int32)

    dt = jnp.transpose(jnp.where(lane < 2, route, 0.0)).astype(jnp.int32)
    d1row = dt[0:1, :]
    d2row = dt[1:2, :]
    dest = lax.broadcasted_iota(jnp.int32, (CAP_ROWS, TM), 0)
    perm = ((dest == d1row) | (dest == d2row)).astype(F32).astype(BF16)
    half = D_MODEL // 2
    lo = jnp.dot(perm, xn[:, :half], preferred_element_type=F32)
    hi = jnp.dot(perm, xn[:, half:], preferred_element_type=F32)
    xs_ref[...] = _to_chunks(lo, hi)


def _route(x2p, x2s, g_ffn, w_router):
    n_prompt_tiles = x2p.shape[0] // TM
    n_tiles = n_prompt_tiles + x2s.shape[0] // TM
    r = jnp.arange(TM)
    ltri = (r[None, :] < r[:, None]).astype(BF16)
    e = jnp.arange(LANES)
    utri = (e[:, None] < e[None, :]).astype(BF16)
    return pl.pallas_call(
        functools.partial(_route_kernel, n_prompt_tiles),
        grid=(n_tiles,),
        in_specs=[pl.BlockSpec((TM, D_MODEL), lambda i: (jnp.minimum(i, n_prompt_tiles - 1), 0)),
                  pl.BlockSpec((TM, D_MODEL), lambda i: (jnp.maximum(i - n_prompt_tiles, 0), 0)),
                  _const_spec((1, D_MODEL)), _const_spec((D_MODEL, LANES)),
                  _const_spec((TM, TM)), _const_spec((LANES, LANES))],
        out_specs=[pl.BlockSpec((CAP_CHUNKS, 2 * CHUNK, D_MODEL // 2), lambda i: (i, 0, 0)),
                   pl.BlockSpec((TM, LANES), lambda i: (i, 0)),
                   pl.BlockSpec((1, 8, LANES), lambda i: (i, 0, 0))],
        out_shape=[jax.ShapeDtypeStruct((n_tiles * CAP_CHUNKS, 2 * CHUNK, D_MODEL // 2), BF16),
                   jax.ShapeDtypeStruct((n_tiles * TM, LANES), F32),
                   jax.ShapeDtypeStruct((n_tiles, 8, LANES), jnp.int32)],
        compiler_params=pltpu.CompilerParams(dimension_semantics=("arbitrary",), vmem_limit_bytes=VMEM_LIMIT),
        name="route_sort",
    )(x2p, x2s, g_ffn, w_router, ltri, utri)


def _chunk_tables(nch, n_expert_tiles):
    n_tiles = nch.shape[0]
    start = jnp.cumsum(nch, axis=1) - nch
    end = start + nch
    per_expert = jnp.sum(nch, axis=0)
    tiles_e = (per_expert + TILE_CHUNKS - 1) // TILE_CHUNKS
    cum_tiles = jnp.cumsum(tiles_e)
    base_e = (cum_tiles - tiles_e) * TILE_CHUNKS
    run_pos = base_e[None, :] + jnp.cumsum(nch, axis=0) - nch
    c = jnp.arange(CAP_CHUNKS, dtype=jnp.int32)
    e_of = jnp.sum((c[None, :, None] >= end[:, None, :]).astype(jnp.int32), axis=-1)
    valid = e_of < N_EXPERTS
    e_cl = jnp.minimum(e_of, N_EXPERTS - 1)
    pos = jnp.take_along_axis(run_pos, e_cl, axis=1) + c[None, :] - jnp.take_along_axis(start, e_cl, axis=1)
    slot_pos = jnp.where(valid, pos, 0).astype(jnp.int32)
    n_sorted = n_expert_tiles * TILE_CHUNKS
    src_id = (jnp.arange(n_tiles, dtype=jnp.int32)[:, None] * CAP_CHUNKS + c[None, :])
    zero_chunk = CAP_CHUNKS - 1
    src = jnp.full((n_sorted,), zero_chunk, jnp.int32).at[
        jnp.where(valid, pos, n_sorted).reshape(-1)].set(src_id.reshape(-1), mode="drop")
    n_active = cum_tiles[-1].astype(jnp.int32)
    t = jnp.arange(n_expert_tiles, dtype=jnp.int32)
    tile_expert = jnp.sum((t[:, None] >= cum_tiles[None, :]).astype(jnp.int32), axis=1)
    last_expert = jnp.sum(((n_active - 1) >= cum_tiles).astype(jnp.int32))
    tile_expert = jnp.where(t < n_active, tile_expert, last_expert).astype(jnp.int32)
    return tile_expert, src, n_active.reshape(1), slot_pos.reshape(-1)


def _expert_kernel(te_ref, src_ref, nact_ref, xs_hbm, wg_ref, wu_ref, wd_ref, ys_ref,
                   xbuf, sem, wg16, wu16, wd16):
    j = pl.program_id(0)
    n_active = nact_ref[0]

    def chunk_copy(tile, slot, c):
        return pltpu.make_async_copy(xs_hbm.at[src_ref[tile * TILE_CHUNKS + c]], xbuf.at[slot, c],
                                     sem.at[slot])

    def fetch(tile, slot):
        for c in range(TILE_CHUNKS):
            chunk_copy(tile, slot, c).start()

    def wait(tile, slot):
        for c in range(TILE_CHUNKS):
            chunk_copy(tile, slot, c).wait()

    @pl.when(j == 0)
    def _():
        fetch(0, 0)

    @pl.when(j + 1 < n_active)
    def _():
        fetch(j + 1, (j + 1) % 2)

    @pl.when(j < n_active)
    def _():
        changed = jnp.logical_or(j == 0, te_ref[j] != te_ref[jnp.maximum(j - 1, 0)])

        @pl.when(changed)
        def _():
            wg16[...] = wg_ref[0].astype(BF16)
            wu16[...] = wu_ref[0].astype(BF16)
            wd16[...] = wd_ref[0].astype(BF16)

        slot = j % 2
        wait(j, slot)
        lo, hi = _from_chunks(xbuf[slot])
        x = jnp.concatenate([lo, hi], axis=1)
        gate = jnp.dot(x, wg16[...], preferred_element_type=F32)
        up = jnp.dot(x, wu16[...], preferred_element_type=F32)
        hmid = (gate * _sigmoid(gate) * up).astype(BF16)
        y = jnp.dot(hmid, wd16[...], preferred_element_type=F32)
        half = D_MODEL // 2
        ys_ref[...] = _to_chunks(y[:, :half], y[:, half:])

    @pl.when(j >= n_active)
    def _():
        ys_ref[...] = jnp.zeros_like(ys_ref)


def _experts(tile_expert, src, n_active, xs, w_gate, w_up, w_down, n_expert_tiles):
    chunk_shape = (2 * CHUNK, D_MODEL // 2)
    grid_spec = pltpu.PrefetchScalarGridSpec(
        num_scalar_prefetch=3,
        grid=(n_expert_tiles,),
        in_specs=[pl.BlockSpec(memory_space=pl.ANY),
                  pl.BlockSpec((1, D_MODEL, D_EXPERT), lambda j, te, s, n: (te[j], 0, 0)),
                  pl.BlockSpec((1, D_MODEL, D_EXPERT), lambda j, te, s, n: (te[j], 0, 0)),
                  pl.BlockSpec((1, D_EXPERT, D_MODEL), lambda j, te, s, n: (te[j], 0, 0))],
        out_specs=pl.BlockSpec((TILE_CHUNKS,) + chunk_shape, lambda j, te, s, n: (j, 0, 0)),
        scratch_shapes=[pltpu.VMEM((2, TILE_CHUNKS) + chunk_shape, BF16),
                        pltpu.SemaphoreType.DMA((2,)),
                        pltpu.VMEM((D_MODEL, D_EXPERT), BF16),
                        pltpu.VMEM((D_MODEL, D_EXPERT), BF16),
                        pltpu.VMEM((D_EXPERT, D_MODEL), BF16)],
    )
    return pl.pallas_call(
        _expert_kernel,
        grid_spec=grid_spec,
        out_shape=jax.ShapeDtypeStruct((n_expert_tiles * TILE_CHUNKS,) + chunk_shape, BF16),
        compiler_params=pltpu.CompilerParams(dimension_semantics=("arbitrary",), vmem_limit_bytes=VMEM_LIMIT),
        name="expert_ffn",
    )(tile_expert, src, n_active, xs, w_gate, w_up, w_down)


def _combine_kernel(n_prompt_tiles, n_tiles, pos_ref, x2p_ref, x2s_ref, route_ref, gfin_ref, ys_hbm,
                    outp_ref, outs_ref, ybuf, sem):
    i = pl.program_id(0)

    def chunk_copy(tile, slot, c):
        return pltpu.make_async_copy(ys_hbm.at[pos_ref[tile * CAP_CHUNKS + c]], ybuf.at[slot, c],
                                     sem.at[slot])

    def fetch(tile, slot):
        for c in range(CAP_CHUNKS):
            chunk_copy(tile, slot, c).start()

    def wait(tile, slot):
        for c in range(CAP_CHUNKS):
            chunk_copy(tile, slot, c).wait()

    @pl.when(i == 0)
    def _():
        fetch(0, 0)

    @pl.when(i + 1 < n_tiles)
    def _():
        fetch(i + 1, (i + 1) % 2)

    slot = i % 2
    wait(i, slot)
    lo, hi = _from_chunks(ybuf[slot])
    route = route_ref[...]
    d1 = route[:, 0:1].astype(jnp.int32)
    d2 = route[:, 1:2].astype(jnp.int32)
    w1 = route[:, 2:3]
    w2 = route[:, 3:4]
    dest = lax.broadcasted_iota(jnp.int32, (TM, CAP_ROWS), 1)
    sel1 = (dest == d1).astype(F32).astype(BF16)
    sel2 = (dest == d2).astype(F32).astype(BF16)
    m1 = jnp.concatenate([jnp.dot(sel1, lo, preferred_element_type=F32),
                          jnp.dot(sel1, hi, preferred_element_type=F32)], axis=1)
    m2 = jnp.concatenate([jnp.dot(sel2, lo, preferred_element_type=F32),
                          jnp.dot(sel2, hi, preferred_element_type=F32)], axis=1)
    x = jnp.where(i < n_prompt_tiles, x2p_ref[...], x2s_ref[...])
    out = _rms(x + (w1 * m1 + w2 * m2), gfin_ref[...])

    @pl.when(i < n_prompt_tiles)
    def _():
        outp_ref[...] = out

    @pl.when(i >= n_prompt_tiles)
    def _():
        outs_ref[...] = out


def _combine(slot_pos, x2p, x2s, route, g_final, ys):
    n_prompt_tiles = x2p.shape[0] // TM
    n_tiles = n_prompt_tiles + x2s.shape[0] // TM
    chunk_shape = (2 * CHUNK, D_MODEL // 2)
    p_spec = pl.BlockSpec((TM, D_MODEL), lambda i, p: (jnp.minimum(i, n_prompt_tiles - 1), 0))
    s_spec = pl.BlockSpec((TM, D_MODEL), lambda i, p: (jnp.maximum(i - n_prompt_tiles, 0), 0))
    grid_spec = pltpu.PrefetchScalarGridSpec(
        num_scalar_prefetch=1,
        grid=(n_tiles,),
        in_specs=[p_spec, s_spec,
                  pl.BlockSpec((TM, LANES), lambda i, p: (i, 0)),
                  pl.BlockSpec((1, D_MODEL), lambda i, p: (0, 0)),
                  pl.BlockSpec(memory_space=pl.ANY)],
        out_specs=[p_spec, s_spec],
        scratch_shapes=[pltpu.VMEM((2, CAP_CHUNKS) + chunk_shape, BF16),
                        pltpu.SemaphoreType.DMA((2,))],
    )
    return pl.pallas_call(
        functools.partial(_combine_kernel, n_prompt_tiles, n_tiles),
        grid_spec=grid_spec,
        out_shape=[jax.ShapeDtypeStruct(x2p.shape, F32), jax.ShapeDtypeStruct(x2s.shape, F32)],
        compiler_params=pltpu.CompilerParams(dimension_semantics=("arbitrary",), vmem_limit_bytes=VMEM_LIMIT),
        name="combine_norm",
    )(slot_pos, x2p, x2s, route, g_final, ys)


def _block_diag_in(bb):
    t = bb.reshape(2, 16, SSM_STATE, SSM_GROUP).transpose(0, 1, 3, 2)
    eye = jnp.eye(16, dtype=bb.dtype)
    blk = t[:, :, :, None, :] * eye[None, :, None, :, None]
    return blk.reshape(2, 16 * SSM_GROUP, 16 * SSM_STATE)


def _block_diag_out(c):
    t = c.reshape(2, 16, SSM_GROUP, SSM_STATE).transpose(0, 1, 3, 2)
    eye = jnp.eye(16, dtype=c.dtype)
    blk = t[:, :, :, None, :] * eye[None, :, None, :, None]
    return blk.reshape(2, 16 * SSM_STATE, 16 * SSM_GROUP)


def kernel(x_prompt, x_sample, state_ssm_re, state_ssm_im, cache_pool, g_mix, w_in, ssm_a_re, ssm_a_im,
           ssm_log_dt, ssm_b_re, ssm_b_im, ssm_c_re, ssm_c_im, ssm_d, w_glu_a, w_glu_b, pool_w, pool_scale,
           w_out, g_ffn, w_router_group, w_router_expert, w_exp_gate, w_exp_up, w_exp_down, g_final):
    li = 0
    n_pb, seq_p, _ = x_prompt.shape
    n_sb, seq_s, _ = x_sample.shape

    a_re, a_im, a32_re, a32_im, bb_re, bb_im = _discretise(
        ssm_a_re[li], ssm_a_im[li], ssm_log_dt[li], ssm_b_re[li], ssm_b_im[li])
    row = lambda v: v.reshape(1, N_FLAT)
    wb = jnp.concatenate([_block_diag_in(bb_re), _block_diag_in(bb_im)], axis=2).astype(BF16)
    weights = [
        g_mix[li].reshape(1, D_MODEL),
        w_in[li].astype(BF16),
        wb,
        _block_diag_out(ssm_c_re[li]).astype(BF16),
        _block_diag_out(ssm_c_im[li]).astype(BF16),
        ssm_d[li].reshape(1, SSM_WIDTH),
        jnp.concatenate([w_glu_a[li], w_glu_b[li]], axis=1).astype(BF16),
        pool_w[li].astype(BF16),
        pool_scale[li].reshape(1, D_MODEL),
        w_out[li].astype(BF16),
        row(a_re), row(a_im), row(a32_re), row(a32_im),
    ]

    x2p, stp_re, stp_im, histp = _mixer_prompt(x_prompt, weights)

    n_stiles = n_sb // N_CHAIN
    cache16 = jnp.pad(cache_pool[li], ((0, 0), (1, 0), (0, 0)))
    cache_t = cache16.reshape(n_stiles, N_CHAIN, 16, POOL_WIDTH).transpose(0, 2, 1, 3).reshape(
        n_stiles, HIST_ROWS, POOL_WIDTH)
    x2s, sts_re, sts_im, hists = _mixer_sample(
        x_sample, state_ssm_re[li].reshape(n_sb, N_FLAT), state_ssm_im[li].reshape(n_sb, N_FLAT),
        cache_t, weights)

    w_router = jnp.concatenate(
        [w_router_group[li], w_router_expert[li].reshape(D_MODEL, N_EXPERTS),
         jnp.zeros((D_MODEL, LANES - N_EXPERT_GROUPS - N_EXPERTS), F32)], axis=1).astype(BF16)
    xs, route, nch = _route(x2p, x2s, g_ffn[li].reshape(1, D_MODEL), w_router)

    n_tiles = (x2p.shape[0] + x2s.shape[0]) // TM
    max_chunks = n_tiles * (2 * TM // CHUNK + N_EXPERTS)
    n_expert_tiles = max_chunks // TILE_CHUNKS + N_EXPERTS
    tile_expert, src, n_active, slot_pos = _chunk_tables(nch[:, 0, :N_EXPERTS], n_expert_tiles)
    ys = _experts(tile_expert, src, n_active, xs, w_exp_gate[li], w_exp_up[li], w_exp_down[li],
                  n_expert_tiles)
    yp, ysm = _combine(slot_pos, x2p, x2s, route, g_final.reshape(1, D_MODEL), ys)

    sd = state_ssm_re.dtype
    cd = cache_pool.dtype
    y_prompt = yp.reshape(n_pb, seq_p, D_MODEL)
    y_sample = ysm.reshape(n_sb, seq_s, D_MODEL)
    re_p = stp_re.reshape(1, n_pb, SSM_GROUPS, SSM_STATE).astype(sd)
    im_p = stp_im.reshape(1, n_pb, SSM_GROUPS, SSM_STATE).astype(sd)
    hist_p = histp[:, ::N_CHAIN, :][:, 1:, :].reshape(1, n_pb, POOL_HIST, POOL_WIDTH).astype(cd)
    re_s = sts_re.reshape(1, n_sb, SSM_GROUPS, SSM_STATE).astype(sd)
    im_s = sts_im.reshape(1, n_sb, SSM_GROUPS, SSM_STATE).astype(sd)
    hist_s = hists.reshape(n_stiles, 16, N_CHAIN, POOL_WIDTH).transpose(0, 2, 1, 3).reshape(
        n_sb, 16, POOL_WIDTH)[:, 1:, :].reshape(1, n_sb, POOL_HIST, POOL_WIDTH).astype(cd)
    return (y_prompt, y_sample, re_p, im_p, hist_p, re_s, im_s, hist_s)
```

```python
import functools
import math

import jax
import jax.numpy as jnp
from jax import lax
from jax.experimental import pallas as pl
from jax.experimental.pallas import tpu as pltpu

F32 = jnp.float32
BF16 = jnp.bfloat16

D_MODEL = 1024
SSM_WIDTH = 512
SSM_GROUPS = 32
SSM_GROUP = 16
SSM_STATE = 64
N_FLAT = SSM_GROUPS * SSM_STATE
HALF_FLAT = N_FLAT // 2
POOL_WIDTH = 512
POOL_WINDOWS = (2, 4, 8, 16)
POOL_GROUP_IN = 128
POOL_GROUP_OUT = 256
POOL_HIST = 15
N_EXPERTS = 32
EXPERTS_PER_GROUP = 8
N_EXPERT_GROUPS = 4
D_EXPERT = 512
EPS = 1e-6
PAST_LEN = 1024

TM = 256
N_CHAIN = 8
CHAIN_LEN = TM // N_CHAIN
HIST_ROWS = 16 * N_CHAIN

CHUNK = 8
TILE_CHUNKS = TM // CHUNK
CAP_CHUNKS = 96
CAP_ROWS = CAP_CHUNKS * CHUNK
MAX_USED_CHUNKS = 2 * TM // CHUNK + N_EXPERTS * (CHUNK - 1) // CHUNK
N_SPARE_SLOTS = CAP_CHUNKS - MAX_USED_CHUNKS - 1
ZERO_CHUNK = CAP_CHUNKS - 1
LANES = 128

VMEM_LIMIT = 52 * 1024 * 1024


def _rms(x, g):
    r = lax.rsqrt(jnp.mean(x * x, axis=-1, keepdims=True) + EPS)
    return x * r * g


def _sigmoid(x):
    return 0.5 * jnp.tanh(0.5 * x) + 0.5


def _gelu_tanh(x):
    c = math.sqrt(2.0 / math.pi)
    return x * (0.5 * (1.0 + jnp.tanh(c * (x + 0.044715 * (x * x * x)))))


def _disc_kernel(lre_ref, lim_ref, ldt_ref, bre_ref, bim_ref,
                 are_ref, aim_ref, a32re_ref, a32im_ref, bbre_ref, bbim_ref):
    lam_re = jnp.minimum(lre_ref[...], -1e-4)
    lam_im = lim_ref[...]
    dt = jnp.exp(ldt_ref[...])
    mag = jnp.exp(lam_re * dt)
    ang = lam_im * dt
    a_re = mag * jnp.cos(ang)
    a_im = mag * jnp.sin(ang)
    num_re = a_re - 1.0
    num_im = a_im
    den = lam_re * lam_re + lam_im * lam_im
    k_re = (num_re * lam_re + num_im * lam_im) / den
    k_im = (num_im * lam_re - num_re * lam_im) / den
    br = bre_ref[...]
    bi = bim_ref[...]
    bbre_ref[...] = k_re * br - k_im * bi
    bbim_ref[...] = k_re * bi + k_im * br
    are_ref[...] = a_re
    aim_ref[...] = a_im
    pr, pi = a_re, a_im
    for _ in range(int(math.log2(CHAIN_LEN))):
        pr, pi = pr * pr - pi * pi, 2.0 * pr * pi
    a32re_ref[...] = pr
    a32im_ref[...] = pi


def _discretise(a_re, a_im, log_dt, b_re, b_im):
    col = lambda v: v.reshape(N_FLAT, 1)
    ldt = jnp.broadcast_to(log_dt[:, None], (SSM_GROUPS, SSM_STATE))
    outs = pl.pallas_call(
        _disc_kernel,
        out_shape=[jax.ShapeDtypeStruct((N_FLAT, 1), F32)] * 4
        + [jax.ShapeDtypeStruct((N_FLAT, SSM_GROUP), F32)] * 2,
        name="s5_discretise",
    )(col(a_re), col(a_im), col(ldt), b_re.reshape(N_FLAT, SSM_GROUP), b_im.reshape(N_FLAT, SSM_GROUP))
    return outs


def _scan_half(hbuf, h, ar, ai, init_re, init_im, store):
    cre = h * N_FLAT
    cim = cre + HALF_FLAT
    hr, hi = init_re, init_im
    for t in range(CHAIN_LEN):
        rows = pl.ds(N_CHAIN * t, N_CHAIN)
        br = hbuf[rows, cre:cre + HALF_FLAT]
        bi = hbuf[rows, cim:cim + HALF_FLAT]
        nr = ar * hr - ai * hi + br
        ni = ar * hi + ai * hr + bi
        if store:
            hbuf[rows, cre:cre + HALF_FLAT] = nr
            hbuf[rows, cim:cim + HALF_FLAT] = ni
        hr, hi = nr, ni
    return hr, hi


def _mixer_kernel(is_prompt, tiles_per_stream, *refs):
    if is_prompt:
        (x_ref, gmix, win, wb, wcre, wcim, dsk, wglu, poolw, pscale, wout, are, aim, a32re, a32im,
         x2_ref, stre_ref, stim_ref, hist_ref,
         xperm, hbuf, res, xpbuf, fre, fim, hre, him, cre_s, cim_s, pcarry) = refs
    else:
        (x_ref, h0re, h0im, cachet, gmix, win, wb, wcre, wcim, dsk, wglu, poolw, pscale, wout, are, aim,
         a32re, a32im,
         x2_ref, stre_ref, stim_ref, hist_ref,
         xperm, hbuf, res, xpbuf) = refs

    tile_in_stream = pl.program_id(0) % tiles_per_stream

    n_lane_blocks = D_MODEL // LANES
    for j in range(N_CHAIN):
        for cb in range(n_lane_blocks):
            xperm[cb, pl.ds(j, CHAIN_LEN, stride=N_CHAIN), :] = x_ref[
                CHAIN_LEN * j:CHAIN_LEN * (j + 1), cb * LANES:(cb + 1) * LANES]
    x = jnp.concatenate([xperm[cb] for cb in range(n_lane_blocks)], axis=1)
    xn = _rms(x, gmix[...]).astype(BF16)
    proj = jnp.dot(xn, win[...], preferred_element_type=F32)
    u_s = proj[:, :SSM_WIDTH]
    u_p = proj[:, SSM_WIDTH:SSM_WIDTH + POOL_WIDTH]
    gate_s = proj[:, SSM_WIDTH + POOL_WIDTH:SSM_WIDTH + POOL_WIDTH + D_MODEL]
    gate_p = proj[:, SSM_WIDTH + POOL_WIDTH + D_MODEL:]

    ub = u_s.astype(BF16)
    half_w = SSM_WIDTH // 2
    for h in range(2):
        hbuf[:, h * N_FLAT:(h + 1) * N_FLAT] = jnp.dot(
            ub[:, h * half_w:(h + 1) * half_w], wb[h], preferred_element_type=F32)

    ar_full = are[...]
    ai_full = aim[...]
    if is_prompt:
        @pl.when(tile_in_stream == 0)
        def _():
            cre_s[...] = jnp.zeros_like(cre_s)
            cim_s[...] = jnp.zeros_like(cim_s)

        zeros = jnp.zeros((N_CHAIN, HALF_FLAT), F32)
        for h in range(2):
            f0 = h * HALF_FLAT
            ar = jnp.broadcast_to(ar_full[:, f0:f0 + HALF_FLAT], (N_CHAIN, HALF_FLAT))
            ai = jnp.broadcast_to(ai_full[:, f0:f0 + HALF_FLAT], (N_CHAIN, HALF_FLAT))
            lr, li = _scan_half(hbuf, h, ar, ai, zeros, zeros, store=False)
            fre[:, f0:f0 + HALF_FLAT] = lr
            fim[:, f0:f0 + HALF_FLAT] = li
        hre[0:1, :] = cre_s[...]
        him[0:1, :] = cim_s[...]
        p_re = a32re[...]
        p_im = a32im[...]
        for j in range(N_CHAIN - 1):
            sr = hre[j:j + 1, :]
            si = him[j:j + 1, :]
            hre[j + 1:j + 2, :] = fre[j:j + 1, :] + p_re * sr - p_im * si
            him[j + 1:j + 2, :] = fim[j:j + 1, :] + p_re * si + p_im * sr
        init_re = hre[...]
        init_im = him[...]
    else:
        init_re = h0re[...]
        init_im = h0im[...]

    fin_re = []
    fin_im = []
    for h in range(2):
        f0 = h * HALF_FLAT
        ar = jnp.broadcast_to(ar_full[:, f0:f0 + HALF_FLAT], (N_CHAIN, HALF_FLAT))
        ai = jnp.broadcast_to(ai_full[:, f0:f0 + HALF_FLAT], (N_CHAIN, HALF_FLAT))
        er, ei = _scan_half(hbuf, h, ar, ai, init_re[:, f0:f0 + HALF_FLAT], init_im[:, f0:f0 + HALF_FLAT],
                            store=True)
        fin_re.append(er)
        fin_im.append(ei)
    end_re = jnp.concatenate(fin_re, axis=1)
    end_im = jnp.concatenate(fin_im, axis=1)
    if is_prompt:
        cre_s[...] = end_re[N_CHAIN - 1:N_CHAIN, :]
        cim_s[...] = end_im[N_CHAIN - 1:N_CHAIN, :]
        stre_ref[0] = end_re[N_CHAIN - 1:N_CHAIN, :]
        stim_ref[0] = end_im[N_CHAIN - 1:N_CHAIN, :]
    else:
        stre_ref[...] = end_re
        stim_ref[...] = end_im

    ys = []
    for h in range(2):
        c0 = h * N_FLAT
        h_re = hbuf[:, c0:c0 + HALF_FLAT].astype(BF16)
        h_im = hbuf[:, c0 + HALF_FLAT:c0 + N_FLAT].astype(BF16)
        ys.append(jnp.dot(h_re, wcre[h], preferred_element_type=F32)
                  - jnp.dot(h_im, wcim[h], preferred_element_type=F32))
    y = jnp.concatenate(ys, axis=1) + dsk[...] * u_s
    g = _gelu_tanh(y).astype(BF16)
    glu = jnp.dot(g, wglu[...], preferred_element_type=F32)
    o_s = glu[:, :D_MODEL] * _sigmoid(glu[:, D_MODEL:])

    xpbuf[HIST_ROWS:HIST_ROWS + TM, :] = u_p
    tail = u_p[TM - HIST_ROWS:, :]
    if is_prompt:
        @pl.when(tile_in_stream == 0)
        def _():
            pcarry[...] = jnp.zeros_like(pcarry)

        first_chain = (lax.broadcasted_iota(jnp.int32, (HIST_ROWS, POOL_WIDTH), 0) % N_CHAIN) == 0
        xpbuf[0:HIST_ROWS, :] = jnp.where(first_chain, pcarry[...], pltpu.roll(tail, 1, 0))
        pcarry[...] = pltpu.roll(tail, HIST_ROWS - (N_CHAIN - 1), 0)
        hist_ref[0] = pcarry[...]
        row = lax.broadcasted_iota(jnp.int32, (TM, 1), 0)
        pos1 = tile_in_stream * TM + CHAIN_LEN * (row % N_CHAIN) + row // N_CHAIN + 1
    else:
        xpbuf[0:HIST_ROWS, :] = cachet[0]
        hist_ref[0] = tail
        row = lax.broadcasted_iota(jnp.int32, (TM, 1), 0)
        pos1 = PAST_LEN + row // N_CHAIN + 1

    o_ps = []
    for gi, w in enumerate(POOL_WINDOWS):
        c0 = gi * POOL_GROUP_IN
        acc = xpbuf[HIST_ROWS:HIST_ROWS + TM, c0:c0 + POOL_GROUP_IN]
        for k in range(1, w):
            acc = acc + xpbuf[HIST_ROWS - N_CHAIN * k:HIST_ROWS - N_CHAIN * k + TM, c0:c0 + POOL_GROUP_IN]
        cnt = jnp.minimum(w, pos1).astype(F32)
        pooled = acc / cnt
        z = (pooled - u_p[:, c0:c0 + POOL_GROUP_IN]).astype(BF16)
        o_ps.append(jnp.dot(z, poolw[gi], preferred_element_type=F32))
    o_p = jnp.concatenate(o_ps, axis=1) * pscale[...]

    merged = (_sigmoid(gate_s) * o_s + _sigmoid(gate_p) * o_p).astype(BF16)
    x2 = x + jnp.dot(merged, wout[...], preferred_element_type=F32)
    for cb in range(n_lane_blocks):
        res[cb] = x2[:, cb * LANES:(cb + 1) * LANES]
    for j in range(N_CHAIN):
        for cb in range(n_lane_blocks):
            x2_ref[CHAIN_LEN * j:CHAIN_LEN * (j + 1), cb * LANES:(cb + 1) * LANES] = res[
                cb, pl.ds(j, CHAIN_LEN, stride=N_CHAIN), :]


def _const_spec(shape):
    nd = len(shape)
    return pl.BlockSpec(shape, lambda i, _nd=nd: (0,) * _nd)


def _mixer_weight_specs():
    return [
        _const_spec((1, D_MODEL)),
        _const_spec((D_MODEL, 3 * D_MODEL)),
        _const_spec((2, SSM_WIDTH // 2, N_FLAT)),
        _const_spec((2, HALF_FLAT, SSM_WIDTH // 2)),
        _const_spec((2, HALF_FLAT, SSM_WIDTH // 2)),
        _const_spec((1, SSM_WIDTH)),
        _const_spec((SSM_WIDTH, 2 * D_MODEL)),
        _const_spec((len(POOL_WINDOWS), POOL_GROUP_IN, POOL_GROUP_OUT)),
        _const_spec((1, D_MODEL)),
        _const_spec((D_MODEL, D_MODEL)),
        _const_spec((1, N_FLAT)),
        _const_spec((1, N_FLAT)),
        _const_spec((1, N_FLAT)),
        _const_spec((1, N_FLAT)),
    ]


def _mixer_common_scratch():
    return [
        pltpu.VMEM((D_MODEL // LANES, TM, LANES), F32),
        pltpu.VMEM((TM, 2 * N_FLAT), F32),
        pltpu.VMEM((D_MODEL // LANES, TM, LANES), F32),
        pltpu.VMEM((HIST_ROWS + TM, POOL_WIDTH), F32),
    ]


def _mixer_prompt(x, weights):
    n_streams, seq, _ = x.shape
    tiles_per_stream = seq // TM
    n_tiles = n_streams * tiles_per_stream
    x2d = x.reshape(n_streams * seq, D_MODEL)
    row_spec = pl.BlockSpec((TM, D_MODEL), lambda i: (i, 0))
    stream_spec = lambda shape: pl.BlockSpec(shape, lambda i: (i // tiles_per_stream, 0, 0))
    return pl.pallas_call(
        functools.partial(_mixer_kernel, True, tiles_per_stream),
        grid=(n_tiles,),
        in_specs=[row_spec] + _mixer_weight_specs(),
        out_specs=[row_spec, stream_spec((1, 1, N_FLAT)), stream_spec((1, 1, N_FLAT)),
                   stream_spec((1, HIST_ROWS, POOL_WIDTH))],
        out_shape=[jax.ShapeDtypeStruct((n_streams * seq, D_MODEL), F32),
                   jax.ShapeDtypeStruct((n_streams, 1, N_FLAT), F32),
                   jax.ShapeDtypeStruct((n_streams, 1, N_FLAT), F32),
                   jax.ShapeDtypeStruct((n_streams, HIST_ROWS, POOL_WIDTH), F32)],
        scratch_shapes=_mixer_common_scratch() + [
            pltpu.VMEM((N_CHAIN, N_FLAT), F32), pltpu.VMEM((N_CHAIN, N_FLAT), F32),
            pltpu.VMEM((N_CHAIN, N_FLAT), F32), pltpu.VMEM((N_CHAIN, N_FLAT), F32),
            pltpu.VMEM((1, N_FLAT), F32), pltpu.VMEM((1, N_FLAT), F32),
            pltpu.VMEM((HIST_ROWS, POOL_WIDTH), F32),
        ],
        compiler_params=pltpu.CompilerParams(dimension_semantics=("arbitrary",), vmem_limit_bytes=VMEM_LIMIT),
        name="mixer_prompt",
    )(x2d, *weights)


def _mixer_sample(x, h0_re, h0_im, cache_t, weights):
    n_streams, seq, _ = x.shape
    assert seq == CHAIN_LEN and n_streams % N_CHAIN == 0
    n_tiles = n_streams // N_CHAIN
    x2d = x.reshape(n_streams * seq, D_MODEL)
    row_spec = pl.BlockSpec((TM, D_MODEL), lambda i: (i, 0))
    st_spec = pl.BlockSpec((N_CHAIN, N_FLAT), lambda i: (i, 0))
    hist_spec = pl.BlockSpec((1, HIST_ROWS, POOL_WIDTH), lambda i: (i, 0, 0))
    return pl.pallas_call(
        functools.partial(_mixer_kernel, False, 1),
        grid=(n_tiles,),
        in_specs=[row_spec, st_spec, st_spec, hist_spec] + _mixer_weight_specs(),
        out_specs=[row_spec, st_spec, st_spec, hist_spec],
        out_shape=[jax.ShapeDtypeStruct((n_streams * seq, D_MODEL), F32),
                   jax.ShapeDtypeStruct((n_streams, N_FLAT), F32),
                   jax.ShapeDtypeStruct((n_streams, N_FLAT), F32),
                   jax.ShapeDtypeStruct((n_tiles, HIST_ROWS, POOL_WIDTH), F32)],
        scratch_shapes=_mixer_common_scratch(),
        compiler_params=pltpu.CompilerParams(dimension_semantics=("arbitrary",), vmem_limit_bytes=VMEM_LIMIT),
        name="mixer_sample",
    )(x2d, h0_re, h0_im, cache_t, *weights)


def _to_chunks(lo, hi):
    n = lo.shape[0] // CHUNK
    half = D_MODEL // 2
    both = jnp.concatenate([lo.reshape(n, CHUNK, half), hi.reshape(n, CHUNK, half)], axis=1)
    return both.astype(BF16)


def _from_chunks(blk):
    n = blk.shape[0]
    half = D_MODEL // 2
    f = blk.astype(F32)
    lo = f[:, :CHUNK, :].reshape(n * CHUNK, half).astype(BF16)
    hi = f[:, CHUNK:, :].reshape(n * CHUNK, half).astype(BF16)
    return lo, hi


def _route_kernel(n_prompt_tiles, x2p_ref, x2s_ref, g_ref, wr_ref, ltri_ref, utri_ref,
                  xs_ref, route_ref, nch_ref):
    i = pl.program_id(0)
    x = jnp.where(i < n_prompt_tiles, x2p_ref[...], x2s_ref[...])
    xn = _rms(x, g_ref[...]).astype(BF16)
    logits = jnp.dot(xn, wr_ref[...], preferred_element_type=F32)
    lane = lax.broadcasted_iota(jnp.int32, (TM, LANES), 1)
    lane_f = lane.astype(F32)
    big = jnp.float32(1 << 20)
    neg = jnp.float32(-jnp.inf)

    gmask = lane < N_EXPERT_GROUPS
    m = jnp.max(jnp.where(gmask, logits, neg), axis=1, keepdims=True)
    grp = jnp.min(jnp.where(gmask & (logits == m), lane_f, big), axis=1, keepdims=True)
    wg = 1.0 / jnp.sum(jnp.where(gmask, jnp.exp(logits - m), 0.0), axis=1, keepdims=True)

    eid = lane - N_EXPERT_GROUPS
    lane_grp = (eid >> 3).astype(F32)
    emask = (eid >= 0) & (eid < N_EXPERTS) & (lane_grp == grp)
    v1 = jnp.max(jnp.where(emask, logits, neg), axis=1, keepdims=True)
    i1 = jnp.min(jnp.where(emask & (logits == v1), lane_f, big), axis=1, keepdims=True)
    emask2 = emask & (lane_f != i1)
    v2 = jnp.max(jnp.where(emask2, logits, neg), axis=1, keepdims=True)
    i2 = jnp.min(jnp.where(emask2 & (logits == v2), lane_f, big), axis=1, keepdims=True)
    e21 = jnp.exp(v2 - v1)
    w1 = wg / (1.0 + e21)
    w2 = wg * e21 / (1.0 + e21)

    a1 = lane_f == (i1 - N_EXPERT_GROUPS)
    a2 = lane_f == (i2 - N_EXPERT_GROUPS)
    a = (a1 | a2).astype(F32)
    before = jnp.dot(ltri_ref[...], a.astype(BF16), preferred_element_type=F32)
    cnt = jnp.sum(a, axis=0, keepdims=True)
    nch = jnp.floor((cnt + (CHUNK - 1)) * (1.0 / CHUNK))
    nch16 = jnp.broadcast_to(nch, (16, LANES))
    start = jnp.dot(nch16.astype(BF16), utri_ref[...], preferred_element_type=F32)
    slot = before + CHUNK * start[0:1, :]
    d1 = jnp.sum(jnp.where(a1, slot, 0.0), axis=1, keepdims=True)
    d2 = jnp.sum(jnp.where(a2, slot, 0.0), axis=1, keepdims=True)

    route = jnp.where(lane == 0, d1, jnp.where(lane == 1, d2, jnp.where(lane == 2, w1,
                      jnp.where(lane == 3, w2, 0.0))))
    route_ref[...] = route
    nch_ref[0] = nch16[0:8, :].astype(jnp.int32)

    dt = jnp.transpose(jnp.where(lane < 2, route, 0.0)).astype(jnp.int32)
    d1row = dt[0:1, :]
    d2row = dt[1:2, :]
    dest = lax.broadcasted_iota(jnp.int32, (CAP_ROWS, TM), 0)
    perm = ((dest == d1row) | (dest == d2row)).astype(F32).astype(BF16)
    half = D_MODEL // 2
    lo = jnp.dot(perm, xn[:, :half], preferred_element_type=F32)
    hi = jnp.dot(perm, xn[:, half:], preferred_element_type=F32)
    xs_ref[...] = _to_chunks(lo, hi)


def _route(x2p, x2s, g_ffn, w_router):
    n_prompt_tiles = x2p.shape[0] // TM
    n_tiles = n_prompt_tiles + x2s.shape[0] // TM
    r = jnp.arange(TM)
    ltri = (r[None, :] < r[:, None]).astype(BF16)
    e = jnp.arange(LANES)
    utri = (e[:, None] < e[None, :]).astype(BF16)
    return pl.pallas_call(
        functools.partial(_route_kernel, n_prompt_tiles),
        grid=(n_tiles,),
        in_specs=[pl.BlockSpec((TM, D_MODEL), lambda i: (jnp.minimum(i, n_prompt_tiles - 1), 0)),
                  pl.BlockSpec((TM, D_MODEL), lambda i: (jnp.maximum(i - n_prompt_tiles, 0), 0)),
                  _const_spec((1, D_MODEL)), _const_spec((D_MODEL, LANES)),
                  _const_spec((TM, TM)), _const_spec((LANES, LANES))],
        out_specs=[pl.BlockSpec((CAP_CHUNKS, 2 * CHUNK, D_MODEL // 2), lambda i: (i, 0, 0)),
                   pl.BlockSpec((TM, LANES), lambda i: (i, 0)),
                   pl.BlockSpec((1, 8, LANES), lambda i: (i, 0, 0))],
        out_shape=[jax.ShapeDtypeStruct((n_tiles * CAP_CHUNKS, 2 * CHUNK, D_MODEL // 2), BF16),
                   jax.ShapeDtypeStruct((n_tiles * TM, LANES), F32),
                   jax.ShapeDtypeStruct((n_tiles, 8, LANES), jnp.int32)],
        compiler_params=pltpu.CompilerParams(dimension_semantics=("arbitrary",), vmem_limit_bytes=VMEM_LIMIT),
        name="route_sort",
    )(x2p, x2s, g_ffn, w_router, ltri, utri)


def _chunk_tables(nch, n_expert_tiles):
    n_tiles = nch.shape[0]
    i32 = jnp.int32
    start = jnp.cumsum(nch, axis=1) - nch
    off = jnp.cumsum(nch, axis=0) - nch
    per_expert = jnp.sum(nch, axis=0)
    tiles_e = (per_expert + TILE_CHUNKS - 1) // TILE_CHUNKS
    cum_tiles = jnp.cumsum(tiles_e)
    first_tile = cum_tiles - tiles_e
    n_active = cum_tiles[-1].astype(i32)
    t = jnp.arange(n_expert_tiles, dtype=i32)
    te = jnp.minimum(jnp.sum((t[:, None] >= cum_tiles[None, :]).astype(i32), axis=1), N_EXPERTS - 1)
    onehot = te[:, None] == jnp.arange(N_EXPERTS, dtype=i32)[None, :]
    pick = lambda tab: jnp.sum(jnp.where(onehot[:, :, None], tab.T[None, :, :], 0), axis=1)
    off_t, nch_t, start_t = pick(off), pick(nch), pick(start)
    k = t - jnp.sum(jnp.where(onehot, first_tile[None, :], 0), axis=1)
    q = (TILE_CHUNKS * k)[:, None] + jnp.arange(TILE_CHUNKS, dtype=i32)[None, :]
    in_run = (off_t[:, None, :] <= q[:, :, None]) & (q[:, :, None] < (off_t + nch_t)[:, None, :])
    run_src = (jnp.arange(n_tiles, dtype=i32) * CAP_CHUNKS)[None, :] + start_t - off_t
    src = jnp.sum(jnp.where(in_run, run_src[:, None, :], 0), axis=-1) + q
    valid = jnp.any(in_run, axis=-1) & (t < n_active)[:, None]
    src = jnp.where(valid, src, ZERO_CHUNK).astype(i32)
    last_expert = jnp.sum(jnp.where(t == n_active - 1, te, 0))
    tile_expert = jnp.where(t < n_active, te, last_expert).astype(i32)
    return tile_expert, src.reshape(-1), n_active.reshape(1)


def _spare_chunk(k):
    return (1 + k // N_SPARE_SLOTS) * CAP_CHUNKS + MAX_USED_CHUNKS + k % N_SPARE_SLOTS


def _expert_kernel(te_ref, src_ref, nact_ref, xs_hbm, wg_ref, wu_ref, wd_ref, ys_hbm,
                   xbuf, obuf, gsem, ssem, wg16, wu16, wd16):
    j = pl.program_id(0)
    n_active = nact_ref[0]

    def gather_copy(tile, slot, c):
        return pltpu.make_async_copy(xs_hbm.at[src_ref[tile * TILE_CHUNKS + c]], xbuf.at[slot, c],
                                     gsem.at[slot])

    def scatter_copy(tile, slot, c):
        s = src_ref[tile * TILE_CHUNKS + c]
        d = jnp.where(s == ZERO_CHUNK, jnp.where(slot == 0, _spare_chunk(c), _spare_chunk(TILE_CHUNKS + c)), s)
        return pltpu.make_async_copy(obuf.at[slot, c], ys_hbm.at[d], ssem.at[slot])

    def start_all(copy, tile, slot):
        for c in range(TILE_CHUNKS):
            copy(tile, slot, c).start()

    def wait_all(copy, tile, slot):
        for c in range(TILE_CHUNKS):
            copy(tile, slot, c).wait()

    @pl.when(j == 0)
    def _():
        start_all(gather_copy, 0, 0)

    @pl.when(j + 1 < n_active)
    def _():
        start_all(gather_copy, j + 1, (j + 1) % 2)

    @pl.when(j < n_active)
    def _():
        changed = jnp.logical_or(j == 0, te_ref[j] != te_ref[jnp.maximum(j - 1, 0)])

        @pl.when(changed)
        def _():
            wg16[...] = wg_ref[0].astype(BF16)
            wu16[...] = wu_ref[0].astype(BF16)
            wd16[...] = wd_ref[0].astype(BF16)

        slot = j % 2
        wait_all(gather_copy, j, slot)
        lo, hi = _from_chunks(xbuf[slot])
        x = jnp.concatenate([lo, hi], axis=1)
        gate = jnp.dot(x, wg16[...], preferred_element_type=F32)
        up = jnp.dot(x, wu16[...], preferred_element_type=F32)
        hmid = (gate * _sigmoid(gate) * up).astype(BF16)
        y = jnp.dot(hmid, wd16[...], preferred_element_type=F32)
        half = D_MODEL // 2
        obuf[slot] = _to_chunks(y[:, :half], y[:, half:])
        start_all(scatter_copy, j, slot)

        @pl.when(j >= 1)
        def _():
            wait_all(scatter_copy, j - 1, 1 - slot)

        @pl.when(j == n_active - 1)
        def _():
            wait_all(scatter_copy, j, slot)


def _experts(tile_expert, src, n_active, xs, w_gate, w_up, w_down, n_expert_tiles):
    chunk_shape = (2 * CHUNK, D_MODEL // 2)
    assert _spare_chunk(2 * TILE_CHUNKS - 1) < xs.shape[0]
    grid_spec = pltpu.PrefetchScalarGridSpec(
        num_scalar_prefetch=3,
        grid=(n_expert_tiles,),
        in_specs=[pl.BlockSpec(memory_space=pl.ANY),
                  pl.BlockSpec((1, D_MODEL, D_EXPERT), lambda j, te, s, n: (te[j], 0, 0)),
                  pl.BlockSpec((1, D_MODEL, D_EXPERT), lambda j, te, s, n: (te[j], 0, 0)),
                  pl.BlockSpec((1, D_EXPERT, D_MODEL), lambda j, te, s, n: (te[j], 0, 0))],
        out_specs=pl.BlockSpec(memory_space=pl.ANY),
        scratch_shapes=[pltpu.VMEM((2, TILE_CHUNKS) + chunk_shape, BF16),
                        pltpu.VMEM((2, TILE_CHUNKS) + chunk_shape, BF16),
                        pltpu.SemaphoreType.DMA((2,)),
                        pltpu.SemaphoreType.DMA((2,)),
                        pltpu.VMEM((D_MODEL, D_EXPERT), BF16),
                        pltpu.VMEM((D_MODEL, D_EXPERT), BF16),
                        pltpu.VMEM((D_EXPERT, D_MODEL), BF16)],
    )
    return pl.pallas_call(
        _expert_kernel,
        grid_spec=grid_spec,
        out_shape=jax.ShapeDtypeStruct(xs.shape, xs.dtype),
        input_output_aliases={3: 0},
        compiler_params=pltpu.CompilerParams(dimension_semantics=("arbitrary",), vmem_limit_bytes=VMEM_LIMIT),
        name="expert_ffn",
    )(tile_expert, src, n_active, xs, w_gate, w_up, w_down)


def _combine_kernel(n_prompt_tiles, x2p_ref, x2s_ref, route_ref, gfin_ref, ys_ref, outp_ref, outs_ref):
    i = pl.program_id(0)
    lo, hi = _from_chunks(ys_ref[...])
    route = route_ref[...]
    d1 = route[:, 0:1].astype(jnp.int32)
    d2 = route[:, 1:2].astype(jnp.int32)
    w1 = route[:, 2:3]
    w2 = route[:, 3:4]
    dest = lax.broadcasted_iota(jnp.int32, (TM, CAP_ROWS), 1)
    sel1 = (dest == d1).astype(F32).astype(BF16)
    sel2 = (dest == d2).astype(F32).astype(BF16)
    m1 = jnp.concatenate([jnp.dot(sel1, lo, preferred_element_type=F32),
                          jnp.dot(sel1, hi, preferred_element_type=F32)], axis=1)
    m2 = jnp.concatenate([jnp.dot(sel2, lo, preferred_element_type=F32),
                          jnp.dot(sel2, hi, preferred_element_type=F32)], axis=1)
    x = jnp.where(i < n_prompt_tiles, x2p_ref[...], x2s_ref[...])
    out = _rms(x + (w1 * m1 + w2 * m2), gfin_ref[...])

    @pl.when(i < n_prompt_tiles)
    def _():
        outp_ref[...] = out

    @pl.when(i >= n_prompt_tiles)
    def _():
        outs_ref[...] = out


def _combine(x2p, x2s, route, g_final, ys):
    n_prompt_tiles = x2p.shape[0] // TM
    n_tiles = n_prompt_tiles + x2s.shape[0] // TM
    p_spec = pl.BlockSpec((TM, D_MODEL), lambda i: (jnp.minimum(i, n_prompt_tiles - 1), 0))
    s_spec = pl.BlockSpec((TM, D_MODEL), lambda i: (jnp.maximum(i - n_prompt_tiles, 0), 0))
    return pl.pallas_call(
        functools.partial(_combine_kernel, n_prompt_tiles),
        grid=(n_tiles,),
        in_specs=[p_spec, s_spec,
                  pl.BlockSpec((TM, LANES), lambda i: (i, 0)),
                  _const_spec((1, D_MODEL)),
                  pl.BlockSpec((CAP_CHUNKS, 2 * CHUNK, D_MODEL // 2), lambda i: (i, 0, 0))],
        out_specs=[p_spec, s_spec],
        out_shape=[jax.ShapeDtypeStruct(x2p.shape, F32), jax.ShapeDtypeStruct(x2s.shape, F32)],
        compiler_params=pltpu.CompilerParams(dimension_semantics=("arbitrary",), vmem_limit_bytes=VMEM_LIMIT),
        name="combine_norm",
    )(x2p, x2s, route, g_final, ys)


def _block_diag_in(bb):
    t = bb.reshape(2, 16, SSM_STATE, SSM_GROUP).transpose(0, 1, 3, 2)
    eye = jnp.eye(16, dtype=bb.dtype)
    blk = t[:, :, :, None, :] * eye[None, :, None, :, None]
    return blk.reshape(2, 16 * SSM_GROUP, 16 * SSM_STATE)


def _block_diag_out(c):
    t = c.reshape(2, 16, SSM_GROUP, SSM_STATE).transpose(0, 1, 3, 2)
    eye = jnp.eye(16, dtype=c.dtype)
    blk = t[:, :, :, None, :] * eye[None, :, None, :, None]
    return blk.reshape(2, 16 * SSM_STATE, 16 * SSM_GROUP)


def kernel(x_prompt, x_sample, state_ssm_re, state_ssm_im, cache_pool, g_mix, w_in, ssm_a_re, ssm_a_im,
           ssm_log_dt, ssm_b_re, ssm_b_im, ssm_c_re, ssm_c_im, ssm_d, w_glu_a, w_glu_b, pool_w, pool_scale,
           w_out, g_ffn, w_router_group, w_router_expert, w_exp_gate, w_exp_up, w_exp_down, g_final):
    li = 0
    n_pb, seq_p, _ = x_prompt.shape
    n_sb, seq_s, _ = x_sample.shape

    a_re, a_im, a32_re, a32_im, bb_re, bb_im = _discretise(
        ssm_a_re[li], ssm_a_im[li], ssm_log_dt[li], ssm_b_re[li], ssm_b_im[li])
    row = lambda v: v.reshape(1, N_FLAT)
    wb = jnp.concatenate([_block_diag_in(bb_re), _block_diag_in(bb_im)], axis=2).astype(BF16)
    weights = [
        g_mix[li].reshape(1, D_MODEL),
        w_in[li].astype(BF16),
        wb,
        _block_diag_out(ssm_c_re[li]).astype(BF16),
        _block_diag_out(ssm_c_im[li]).astype(BF16),
        ssm_d[li].reshape(1, SSM_WIDTH),
        jnp.concatenate([w_glu_a[li], w_glu_b[li]], axis=1).astype(BF16),
        pool_w[li].astype(BF16),
        pool_scale[li].reshape(1, D_MODEL),
        w_out[li].astype(BF16),
        row(a_re), row(a_im), row(a32_re), row(a32_im),
    ]

    x2p, stp_re, stp_im, histp = _mixer_prompt(x_prompt, weights)

    n_stiles = n_sb // N_CHAIN
    cache16 = jnp.pad(cache_pool[li], ((0, 0), (1, 0), (0, 0)))
    cache_t = cache16.reshape(n_stiles, N_CHAIN, 16, POOL_WIDTH).transpose(0, 2, 1, 3).reshape(
        n_stiles, HIST_ROWS, POOL_WIDTH)
    x2s, sts_re, sts_im, hists = _mixer_sample(
        x_sample, state_ssm_re[li].reshape(n_sb, N_FLAT), state_ssm_im[li].reshape(n_sb, N_FLAT),
        cache_t, weights)

    w_router = jnp.concatenate(
        [w_router_group[li], w_router_expert[li].reshape(D_MODEL, N_EXPERTS),
         jnp.zeros((D_MODEL, LANES - N_EXPERT_GROUPS - N_EXPERTS), F32)], axis=1).astype(BF16)
    xs, route, nch = _route(x2p, x2s, g_ffn[li].reshape(1, D_MODEL), w_router)

    n_tiles = (x2p.shape[0] + x2s.shape[0]) // TM
    max_chunks = n_tiles * (2 * TM // CHUNK + N_EXPERTS)
    n_expert_tiles = max_chunks // TILE_CHUNKS + N_EXPERTS
    tile_expert, src, n_active = _chunk_tables(nch[:, 0, :N_EXPERTS], n_expert_tiles)
    ys = _experts(tile_expert, src, n_active, xs, w_exp_gate[li], w_exp_up[li], w_exp_down[li],
                  n_expert_tiles)
    yp, ysm = _combine(x2p, x2s, route, g_final.reshape(1, D_MODEL), ys)

    sd = state_ssm_re.dtype
    cd = cache_pool.dtype
    y_prompt = yp.reshape(n_pb, seq_p, D_MODEL)
    y_sample = ysm.reshape(n_sb, seq_s, D_MODEL)
    re_p = stp_re.reshape(1, n_pb, SSM_GROUPS, SSM_STATE).astype(sd)
    im_p = stp_im.reshape(1, n_pb, SSM_GROUPS, SSM_STATE).astype(sd)
    hist_p = histp[:, ::N_CHAIN, :][:, 1:, :].reshape(1, n_pb, POOL_HIST, POOL_WIDTH).astype(cd)
    re_s = sts_re.reshape(1, n_sb, SSM_GROUPS, SSM_STATE).astype(sd)
    im_s = sts_im.reshape(1, n_sb, SSM_GROUPS, SSM_STATE).astype(sd)
    hist_s = hists.reshape(n_stiles, 16, N_CHAIN, POOL_WIDTH).transpose(0, 2, 1, 3).reshape(
        n_sb, 16, POOL_WIDTH)[:, 1:, :].reshape(1, n_sb, POOL_HIST, POOL_WIDTH).astype(cd)
    return (y_prompt, y_sample, re_p, im_p, hist_p, re_s, im_s, hist_s)
```

```python
import functools
import math

import jax
import jax.numpy as jnp
from jax import lax
from jax.experimental import pallas as pl
from jax.experimental.pallas import tpu as pltpu

F32 = jnp.float32
BF16 = jnp.bfloat16

D_MODEL = 1024
SSM_WIDTH = 512
SSM_GROUPS = 32
SSM_GROUP = 16
SSM_STATE = 64
N_FLAT = SSM_GROUPS * SSM_STATE
HALF_FLAT = N_FLAT // 2
POOL_WIDTH = 512
POOL_WINDOWS = (2, 4, 8, 16)
POOL_GROUP_IN = 128
POOL_GROUP_OUT = 256
POOL_HIST = 15
N_EXPERTS = 32
EXPERTS_PER_GROUP = 8
N_EXPERT_GROUPS = 4
D_EXPERT = 512
EPS = 1e-6
PAST_LEN = 1024

TM = 256
N_CHAIN = 8
CHAIN_LEN = TM // N_CHAIN
HIST_ROWS = 16 * N_CHAIN

CHUNK = 8
TILE_CHUNKS = TM // CHUNK
CAP_CHUNKS = 96
CAP_ROWS = CAP_CHUNKS * CHUNK
MAX_USED_CHUNKS = 2 * TM // CHUNK + N_EXPERTS * (CHUNK - 1) // CHUNK
N_SPARE_SLOTS = CAP_CHUNKS - MAX_USED_CHUNKS - 1
ZERO_CHUNK = CAP_CHUNKS - 1
LANES = 128

VMEM_LIMIT = 52 * 1024 * 1024


def _rms(x, g):
    r = lax.rsqrt(jnp.mean(x * x, axis=-1, keepdims=True) + EPS)
    return x * r * g


def _sigmoid(x):
    return 0.5 * jnp.tanh(0.5 * x) + 0.5


def _gelu_tanh(x):
    c = math.sqrt(2.0 / math.pi)
    return x * (0.5 * (1.0 + jnp.tanh(c * (x + 0.044715 * (x * x * x)))))


def _disc_kernel(lre_ref, lim_ref, ldt_ref, bre_ref, bim_ref,
                 are_ref, aim_ref, a32re_ref, a32im_ref, bbre_ref, bbim_ref):
    lam_re = jnp.minimum(lre_ref[...], -1e-4)
    lam_im = lim_ref[...]
    dt = jnp.exp(ldt_ref[...])
    mag = jnp.exp(lam_re * dt)
    ang = lam_im * dt
    a_re = mag * jnp.cos(ang)
    a_im = mag * jnp.sin(ang)
    num_re = a_re - 1.0
    num_im = a_im
    den = lam_re * lam_re + lam_im * lam_im
    k_re = (num_re * lam_re + num_im * lam_im) / den
    k_im = (num_im * lam_re - num_re * lam_im) / den
    br = bre_ref[...]
    bi = bim_ref[...]
    bbre_ref[...] = k_re * br - k_im * bi
    bbim_ref[...] = k_re * bi + k_im * br
    are_ref[...] = a_re
    aim_ref[...] = a_im
    pr, pi = a_re, a_im
    for _ in range(int(math.log2(CHAIN_LEN))):
        pr, pi = pr * pr - pi * pi, 2.0 * pr * pi
    a32re_ref[...] = pr
    a32im_ref[...] = pi


def _discretise(a_re, a_im, log_dt, b_re, b_im):
    col = lambda v: v.reshape(N_FLAT, 1)
    ldt = jnp.broadcast_to(log_dt[:, None], (SSM_GROUPS, SSM_STATE))
    outs = pl.pallas_call(
        _disc_kernel,
        out_shape=[jax.ShapeDtypeStruct((N_FLAT, 1), F32)] * 4
        + [jax.ShapeDtypeStruct((N_FLAT, SSM_GROUP), F32)] * 2,
        name="s5_discretise",
    )(col(a_re), col(a_im), col(ldt), b_re.reshape(N_FLAT, SSM_GROUP), b_im.reshape(N_FLAT, SSM_GROUP))
    return outs


def _scan_half(hbuf, h, ar, ai, init_re, init_im, store):
    cre = h * N_FLAT
    cim = cre + HALF_FLAT
    hr, hi = init_re, init_im
    for t in range(CHAIN_LEN):
        rows = pl.ds(N_CHAIN * t, N_CHAIN)
        br = hbuf[rows, cre:cre + HALF_FLAT]
        bi = hbuf[rows, cim:cim + HALF_FLAT]
        nr = ar * hr - ai * hi + br
        ni = ar * hi + ai * hr + bi
        if store:
            hbuf[rows, cre:cre + HALF_FLAT] = nr
            hbuf[rows, cim:cim + HALF_FLAT] = ni
        hr, hi = nr, ni
    return hr, hi


N_MIXER_WEIGHTS = 14
N_PAR = 2


def _mixer_kernel(is_prompt, tiles_per_stream, *refs):
    n_in = 1 if is_prompt else 4
    ins = refs[:n_in]
    (gmix, win, wb, wcre, wcim, dsk, wglu, poolw, pscale, wout, are, aim, a32re, a32im) = refs[
        n_in:n_in + N_MIXER_WEIGHTS]
    x2_ref, stre_ref, stim_ref, hist_ref = refs[n_in + N_MIXER_WEIGHTS:n_in + N_MIXER_WEIGHTS + 4]
    scratch = refs[n_in + N_MIXER_WEIGHTS + 4:]
    per = len(scratch) // N_PAR
    lanes = [scratch[p * per:(p + 1) * per] for p in range(N_PAR)]
    x_ref = ins[0]

    tile_in_stream = pl.program_id(0) % tiles_per_stream
    n_lane_blocks = D_MODEL // LANES
    half_w = SSM_WIDTH // 2
    ar_full = are[...]
    ai_full = aim[...]

    if is_prompt:
        @pl.when(pl.program_id(0) == 0)
        def _():
            for p in range(N_PAR):
                cre_s, cim_s, pcarry = lanes[p][8], lanes[p][9], lanes[p][10]
                cre_s[...] = jnp.zeros_like(cre_s)
                cim_s[...] = jnp.zeros_like(cim_s)
                pcarry[...] = jnp.zeros_like(pcarry)

    def a_half(h):
        f0 = h * HALF_FLAT
        return (jnp.broadcast_to(ar_full[:, f0:f0 + HALF_FLAT], (N_CHAIN, HALF_FLAT)),
                jnp.broadcast_to(ai_full[:, f0:f0 + HALF_FLAT], (N_CHAIN, HALF_FLAT)))

    def project(p):
        xperm, hbuf = lanes[p][0], lanes[p][1]
        for j in range(N_CHAIN):
            for cb in range(n_lane_blocks):
                xperm[cb, pl.ds(j, CHAIN_LEN, stride=N_CHAIN), :] = x_ref[
                    p, CHAIN_LEN * j:CHAIN_LEN * (j + 1), cb * LANES:(cb + 1) * LANES]
        x = jnp.concatenate([xperm[cb] for cb in range(n_lane_blocks)], axis=1)
        xn = _rms(x, gmix[...]).astype(BF16)
        proj = jnp.dot(xn, win[...], preferred_element_type=F32)
        ub = proj[:, :SSM_WIDTH].astype(BF16)
        for h in range(2):
            hbuf[:, h * N_FLAT:(h + 1) * N_FLAT] = jnp.dot(
                ub[:, h * half_w:(h + 1) * half_w], wb[h], preferred_element_type=F32)
        return x, proj

    def recur(p):
        hbuf = lanes[p][1]
        if is_prompt:
            fre, fim, hre, him, cre_s, cim_s = lanes[p][4:10]
            zeros = jnp.zeros((N_CHAIN, HALF_FLAT), F32)
            for h in range(2):
                f0 = h * HALF_FLAT
                ar, ai = a_half(h)
                lr, li = _scan_half(hbuf, h, ar, ai, zeros, zeros, store=False)
                fre[:, f0:f0 + HALF_FLAT] = lr
                fim[:, f0:f0 + HALF_FLAT] = li
            fresh = tile_in_stream == 0
            hre[0:1, :] = jnp.where(fresh, 0.0, cre_s[...])
            him[0:1, :] = jnp.where(fresh, 0.0, cim_s[...])
            p_re = a32re[...]
            p_im = a32im[...]
            for j in range(N_CHAIN - 1):
                sr = hre[j:j + 1, :]
                si = him[j:j + 1, :]
                hre[j + 1:j + 2, :] = fre[j:j + 1, :] + p_re * sr - p_im * si
                him[j + 1:j + 2, :] = fim[j:j + 1, :] + p_re * si + p_im * sr
            init_re = hre[...]
            init_im = him[...]
        else:
            init_re = ins[1][p]
            init_im = ins[2][p]
        fin_re = []
        fin_im = []
        for h in range(2):
            f0 = h * HALF_FLAT
            ar, ai = a_half(h)
            er, ei = _scan_half(hbuf, h, ar, ai, init_re[:, f0:f0 + HALF_FLAT],
                                init_im[:, f0:f0 + HALF_FLAT], store=True)
            fin_re.append(er)
            fin_im.append(ei)
        end_re = jnp.concatenate(fin_re, axis=1)
        end_im = jnp.concatenate(fin_im, axis=1)
        if is_prompt:
            cre_s[...] = end_re[N_CHAIN - 1:N_CHAIN, :]
            cim_s[...] = end_im[N_CHAIN - 1:N_CHAIN, :]
            stre_ref[p] = end_re[N_CHAIN - 1:N_CHAIN, :]
            stim_ref[p] = end_im[N_CHAIN - 1:N_CHAIN, :]
        else:
            stre_ref[p] = end_re
            stim_ref[p] = end_im

    def finish(p, x, proj):
        hbuf, res, xpbuf = lanes[p][1], lanes[p][2], lanes[p][3]
        u_s = proj[:, :SSM_WIDTH]
        u_p = proj[:, SSM_WIDTH:SSM_WIDTH + POOL_WIDTH]
        gate_s = proj[:, SSM_WIDTH + POOL_WIDTH:SSM_WIDTH + POOL_WIDTH + D_MODEL]
        gate_p = proj[:, SSM_WIDTH + POOL_WIDTH + D_MODEL:]
        ys = []
        for h in range(2):
            c0 = h * N_FLAT
            h_re = hbuf[:, c0:c0 + HALF_FLAT].astype(BF16)
            h_im = hbuf[:, c0 + HALF_FLAT:c0 + N_FLAT].astype(BF16)
            ys.append(jnp.dot(h_re, wcre[h], preferred_element_type=F32)
                      - jnp.dot(h_im, wcim[h], preferred_element_type=F32))
        y = jnp.concatenate(ys, axis=1) + dsk[...] * u_s
        g = _gelu_tanh(y).astype(BF16)
        glu = jnp.dot(g, wglu[...], preferred_element_type=F32)
        o_s = glu[:, :D_MODEL] * _sigmoid(glu[:, D_MODEL:])

        xpbuf[HIST_ROWS:HIST_ROWS + TM, :] = u_p
        tail = u_p[TM - HIST_ROWS:, :]
        row = lax.broadcasted_iota(jnp.int32, (TM, 1), 0)
        if is_prompt:
            pcarry = lanes[p][10]
            first_chain = (lax.broadcasted_iota(jnp.int32, (HIST_ROWS, POOL_WIDTH), 0) % N_CHAIN) == 0
            carried = jnp.where(tile_in_stream == 0, 0.0, pcarry[...])
            xpbuf[0:HIST_ROWS, :] = jnp.where(first_chain, carried, pltpu.roll(tail, 1, 0))
            new_carry = pltpu.roll(tail, HIST_ROWS - (N_CHAIN - 1), 0)
            pcarry[...] = new_carry
            hist_ref[p] = new_carry
            pos1 = tile_in_stream * TM + CHAIN_LEN * (row % N_CHAIN) + row // N_CHAIN + 1
        else:
            xpbuf[0:HIST_ROWS, :] = ins[3][p]
            hist_ref[p] = tail
            pos1 = PAST_LEN + row // N_CHAIN + 1

        o_ps = []
        for gi, w in enumerate(POOL_WINDOWS):
            c0 = gi * POOL_GROUP_IN
            acc = xpbuf[HIST_ROWS:HIST_ROWS + TM, c0:c0 + POOL_GROUP_IN]
            for k in range(1, w):
                acc = acc + xpbuf[HIST_ROWS - N_CHAIN * k:HIST_ROWS - N_CHAIN * k + TM, c0:c0 + POOL_GROUP_IN]
            cnt = jnp.minimum(w, pos1).astype(F32)
            pooled = acc / cnt
            z = (pooled - u_p[:, c0:c0 + POOL_GROUP_IN]).astype(BF16)
            o_ps.append(jnp.dot(z, poolw[gi], preferred_element_type=F32))
        o_p = jnp.concatenate(o_ps, axis=1) * pscale[...]

        merged = (_sigmoid(gate_s) * o_s + _sigmoid(gate_p) * o_p).astype(BF16)
        x2 = x + jnp.dot(merged, wout[...], preferred_element_type=F32)
        for cb in range(n_lane_blocks):
            res[cb] = x2[:, cb * LANES:(cb + 1) * LANES]
        for j in range(N_CHAIN):
            for cb in range(n_lane_blocks):
                x2_ref[p, CHAIN_LEN * j:CHAIN_LEN * (j + 1), cb * LANES:(cb + 1) * LANES] = res[
                    cb, pl.ds(j, CHAIN_LEN, stride=N_CHAIN), :]

    projected = [project(p) for p in range(N_PAR)]
    for p in range(N_PAR):
        recur(p)
    for p in range(N_PAR):
        finish(p, *projected[p])


def _const_spec(shape):
    nd = len(shape)
    return pl.BlockSpec(shape, lambda i, _nd=nd: (0,) * _nd)


def _mixer_weight_specs():
    return [
        _const_spec((1, D_MODEL)),
        _const_spec((D_MODEL, 3 * D_MODEL)),
        _const_spec((2, SSM_WIDTH // 2, N_FLAT)),
        _const_spec((2, HALF_FLAT, SSM_WIDTH // 2)),
        _const_spec((2, HALF_FLAT, SSM_WIDTH // 2)),
        _const_spec((1, SSM_WIDTH)),
        _const_spec((SSM_WIDTH, 2 * D_MODEL)),
        _const_spec((len(POOL_WINDOWS), POOL_GROUP_IN, POOL_GROUP_OUT)),
        _const_spec((1, D_MODEL)),
        _const_spec((D_MODEL, D_MODEL)),
        _const_spec((1, N_FLAT)),
        _const_spec((1, N_FLAT)),
        _const_spec((1, N_FLAT)),
        _const_spec((1, N_FLAT)),
    ]


def _mixer_common_scratch():
    return [
        pltpu.VMEM((D_MODEL // LANES, TM, LANES), F32),
        pltpu.VMEM((TM, 2 * N_FLAT), F32),
        pltpu.VMEM((D_MODEL // LANES, TM, LANES), F32),
        pltpu.VMEM((HIST_ROWS + TM, POOL_WIDTH), F32),
    ]


def _mixer_prompt(x, weights):
    n_streams, seq, _ = x.shape
    assert n_streams % N_PAR == 0 and seq % TM == 0
    tiles_per_stream = seq // TM
    blk = lambda shape: pl.BlockSpec(shape, lambda i: (i // tiles_per_stream, 0, 0))
    row_spec = pl.BlockSpec((N_PAR, TM, D_MODEL), lambda i: (i // tiles_per_stream, i % tiles_per_stream, 0))
    lane_scratch = _mixer_common_scratch() + [
        pltpu.VMEM((N_CHAIN, N_FLAT), F32), pltpu.VMEM((N_CHAIN, N_FLAT), F32),
        pltpu.VMEM((N_CHAIN, N_FLAT), F32), pltpu.VMEM((N_CHAIN, N_FLAT), F32),
        pltpu.VMEM((1, N_FLAT), F32), pltpu.VMEM((1, N_FLAT), F32),
        pltpu.VMEM((HIST_ROWS, POOL_WIDTH), F32),
    ]
    return pl.pallas_call(
        functools.partial(_mixer_kernel, True, tiles_per_stream),
        grid=(n_streams // N_PAR * tiles_per_stream,),
        in_specs=[row_spec] + _mixer_weight_specs(),
        out_specs=[row_spec, blk((N_PAR, 1, N_FLAT)), blk((N_PAR, 1, N_FLAT)),
                   blk((N_PAR, HIST_ROWS, POOL_WIDTH))],
        out_shape=[jax.ShapeDtypeStruct((n_streams, seq, D_MODEL), F32),
                   jax.ShapeDtypeStruct((n_streams, 1, N_FLAT), F32),
                   jax.ShapeDtypeStruct((n_streams, 1, N_FLAT), F32),
                   jax.ShapeDtypeStruct((n_streams, HIST_ROWS, POOL_WIDTH), F32)],
        scratch_shapes=lane_scratch * N_PAR,
        compiler_params=pltpu.CompilerParams(dimension_semantics=("arbitrary",), vmem_limit_bytes=VMEM_LIMIT),
        name="mixer_prompt",
    )(x, *weights)


def _mixer_sample(x, h0_re, h0_im, cache_t, weights):
    n_streams, seq, _ = x.shape
    assert seq == CHAIN_LEN and n_streams % (N_CHAIN * N_PAR) == 0
    n_tiles = n_streams // N_CHAIN
    blk = lambda shape: pl.BlockSpec(shape, lambda i: (i, 0, 0))
    row_spec = blk((N_PAR, TM, D_MODEL))
    st_spec = blk((N_PAR, N_CHAIN, N_FLAT))
    hist_spec = blk((N_PAR, HIST_ROWS, POOL_WIDTH))
    tiles = lambda v: v.reshape((n_tiles, -1) + v.shape[-1:])
    return pl.pallas_call(
        functools.partial(_mixer_kernel, False, 1),
        grid=(n_tiles // N_PAR,),
        in_specs=[row_spec, st_spec, st_spec, hist_spec] + _mixer_weight_specs(),
        out_specs=[row_spec, st_spec, st_spec, hist_spec],
        out_shape=[jax.ShapeDtypeStruct((n_tiles, TM, D_MODEL), F32),
                   jax.ShapeDtypeStruct((n_tiles, N_CHAIN, N_FLAT), F32),
                   jax.ShapeDtypeStruct((n_tiles, N_CHAIN, N_FLAT), F32),
                   jax.ShapeDtypeStruct((n_tiles, HIST_ROWS, POOL_WIDTH), F32)],
        scratch_shapes=_mixer_common_scratch() * N_PAR,
        compiler_params=pltpu.CompilerParams(dimension_semantics=("arbitrary",), vmem_limit_bytes=VMEM_LIMIT),
        name="mixer_sample",
    )(tiles(x), tiles(h0_re), tiles(h0_im), cache_t, *weights)


def _to_chunks(lo, hi):
    n = lo.shape[0] // CHUNK
    half = D_MODEL // 2
    both = jnp.concatenate([lo.reshape(n, CHUNK, half), hi.reshape(n, CHUNK, half)], axis=1)
    return both.astype(BF16)


def _from_chunks(blk):
    n = blk.shape[0]
    half = D_MODEL // 2
    f = blk.astype(F32)
    lo = f[:, :CHUNK, :].reshape(n * CHUNK, half).astype(BF16)
    hi = f[:, CHUNK:, :].reshape(n * CHUNK, half).astype(BF16)
    return lo, hi


def _route_kernel(n_prompt_tiles, x2p_ref, x2s_ref, g_ref, wr_ref, ltri_ref, utri_ref,
                  xs_ref, route_ref, nch_ref):
    i = pl.program_id(0)
    x = jnp.where(i < n_prompt_tiles, x2p_ref[...], x2s_ref[...])
    xn = _rms(x, g_ref[...]).astype(BF16)
    logits = jnp.dot(xn, wr_ref[...], preferred_element_type=F32)
    lane = lax.broadcasted_iota(jnp.int32, (TM, LANES), 1)
    lane_f = lane.astype(F32)
    big = jnp.float32(1 << 20)
    neg = jnp.float32(-jnp.inf)

    gmask = lane < N_EXPERT_GROUPS
    m = jnp.max(jnp.where(gmask, logits, neg), axis=1, keepdims=True)
    grp = jnp.min(jnp.where(gmask & (logits == m), lane_f, big), axis=1, keepdims=True)
    wg = 1.0 / jnp.sum(jnp.where(gmask, jnp.exp(logits - m), 0.0), axis=1, keepdims=True)

    eid = lane - N_EXPERT_GROUPS
    lane_grp = (eid >> 3).astype(F32)
    emask = (eid >= 0) & (eid < N_EXPERTS) & (lane_grp == grp)
    v1 = jnp.max(jnp.where(emask, logits, neg), axis=1, keepdims=True)
    i1 = jnp.min(jnp.where(emask & (logits == v1), lane_f, big), axis=1, keepdims=True)
    emask2 = emask & (lane_f != i1)
    v2 = jnp.max(jnp.where(emask2, logits, neg), axis=1, keepdims=True)
    i2 = jnp.min(jnp.where(emask2 & (logits == v2), lane_f, big), axis=1, keepdims=True)
    e21 = jnp.exp(v2 - v1)
    w1 = wg / (1.0 + e21)
    w2 = wg * e21 / (1.0 + e21)

    a1 = lane_f == (i1 - N_EXPERT_GROUPS)
    a2 = lane_f == (i2 - N_EXPERT_GROUPS)
    a = (a1 | a2).astype(F32)
    before = jnp.dot(ltri_ref[...], a.astype(BF16), preferred_element_type=F32)
    cnt = jnp.sum(a, axis=0, keepdims=True)
    nch = jnp.floor((cnt + (CHUNK - 1)) * (1.0 / CHUNK))
    nch16 = jnp.broadcast_to(nch, (16, LANES))
    start = jnp.dot(nch16.astype(BF16), utri_ref[...], preferred_element_type=F32)
    slot = before + CHUNK * start[0:1, :]
    d1 = jnp.sum(jnp.where(a1, slot, 0.0), axis=1, keepdims=True)
    d2 = jnp.sum(jnp.where(a2, slot, 0.0), axis=1, keepdims=True)

    route = jnp.where(lane == 0, d1, jnp.where(lane == 1, d2, jnp.where(lane == 2, w1,
                      jnp.where(lane == 3, w2, 0.0))))
    route_ref[...] = route
    nch_ref[0] = nch16[0:8, :].astype(jnp.int32)

    dt = jnp.transpose(jnp.where(lane < 2, route, 0.0)).astype(jnp.int32)
    d1row = dt[0:1, :]
    d2row = dt[1:2, :]
    dest = lax.broadcasted_iota(jnp.int32, (CAP_ROWS, TM), 0)
    perm = ((dest == d1row) | (dest == d2row)).astype(F32).astype(BF16)
    half = D_MODEL // 2
    lo = jnp.dot(perm, xn[:, :half], preferred_element_type=F32)
    hi = jnp.dot(perm, xn[:, half:], preferred_element_type=F32)
    xs_ref[...] = _to_chunks(lo, hi)


def _route(x2p, x2s, g_ffn, w_router):
    n_prompt_tiles = x2p.shape[0] // TM
    n_tiles = n_prompt_tiles + x2s.shape[0] // TM
    r = jnp.arange(TM)
    ltri = (r[None, :] < r[:, None]).astype(BF16)
    e = jnp.arange(LANES)
    utri = (e[:, None] < e[None, :]).astype(BF16)
    return pl.pallas_call(
        functools.partial(_route_kernel, n_prompt_tiles),
        grid=(n_tiles,),
        in_specs=[pl.BlockSpec((TM, D_MODEL), lambda i: (jnp.minimum(i, n_prompt_tiles - 1), 0)),
                  pl.BlockSpec((TM, D_MODEL), lambda i: (jnp.maximum(i - n_prompt_tiles, 0), 0)),
                  _const_spec((1, D_MODEL)), _const_spec((D_MODEL, LANES)),
                  _const_spec((TM, TM)), _const_spec((LANES, LANES))],
        out_specs=[pl.BlockSpec((CAP_CHUNKS, 2 * CHUNK, D_MODEL // 2), lambda i: (i, 0, 0)),
                   pl.BlockSpec((TM, LANES), lambda i: (i, 0)),
                   pl.BlockSpec((1, 8, LANES), lambda i: (i, 0, 0))],
        out_shape=[jax.ShapeDtypeStruct((n_tiles * CAP_CHUNKS, 2 * CHUNK, D_MODEL // 2), BF16),
                   jax.ShapeDtypeStruct((n_tiles * TM, LANES), F32),
                   jax.ShapeDtypeStruct((n_tiles, 8, LANES), jnp.int32)],
        compiler_params=pltpu.CompilerParams(dimension_semantics=("arbitrary",), vmem_limit_bytes=VMEM_LIMIT),
        name="route_sort",
    )(x2p, x2s, g_ffn, w_router, ltri, utri)


def _chunk_tables(nch, n_expert_tiles):
    n_tiles = nch.shape[0]
    i32 = jnp.int32
    start = jnp.cumsum(nch, axis=1) - nch
    off = jnp.cumsum(nch, axis=0) - nch
    per_expert = jnp.sum(nch, axis=0)
    tiles_e = (per_expert + TILE_CHUNKS - 1) // TILE_CHUNKS
    cum_tiles = jnp.cumsum(tiles_e)
    first_tile = cum_tiles - tiles_e
    n_active = cum_tiles[-1].astype(i32)
    t = jnp.arange(n_expert_tiles, dtype=i32)
    te = jnp.minimum(jnp.sum((t[:, None] >= cum_tiles[None, :]).astype(i32), axis=1), N_EXPERTS - 1)
    onehot = te[:, None] == jnp.arange(N_EXPERTS, dtype=i32)[None, :]
    pick = lambda tab: jnp.sum(jnp.where(onehot[:, :, None], tab.T[None, :, :], 0), axis=1)
    off_t, nch_t, start_t = pick(off), pick(nch), pick(start)
    k = t - jnp.sum(jnp.where(onehot, first_tile[None, :], 0), axis=1)
    q = (TILE_CHUNKS * k)[:, None] + jnp.arange(TILE_CHUNKS, dtype=i32)[None, :]
    in_run = (off_t[:, None, :] <= q[:, :, None]) & (q[:, :, None] < (off_t + nch_t)[:, None, :])
    run_src = (jnp.arange(n_tiles, dtype=i32) * CAP_CHUNKS)[None, :] + start_t - off_t
    src = jnp.sum(jnp.where(in_run, run_src[:, None, :], 0), axis=-1) + q
    valid = jnp.any(in_run, axis=-1) & (t < n_active)[:, None]
    src = jnp.where(valid, src, ZERO_CHUNK).astype(i32)
    last_expert = jnp.sum(jnp.where(t == n_active - 1, te, 0))
    tile_expert = jnp.where(t < n_active, te, last_expert).astype(i32)
    return tile_expert, src.reshape(-1), n_active.reshape(1)


def _spare_chunk(k):
    return (1 + k // N_SPARE_SLOTS) * CAP_CHUNKS + MAX_USED_CHUNKS + k % N_SPARE_SLOTS


def _expert_kernel(te_ref, src_ref, nact_ref, xs_hbm, wg_ref, wu_ref, wd_ref, ys_hbm,
                   xbuf, obuf, gsem, ssem, wg16, wu16, wd16):
    j = pl.program_id(0)
    n_active = nact_ref[0]

    def gather_copy(tile, slot, c):
        return pltpu.make_async_copy(xs_hbm.at[src_ref[tile * TILE_CHUNKS + c]], xbuf.at[slot, c],
                                     gsem.at[slot])

    def scatter_copy(tile, slot, c):
        s = src_ref[tile * TILE_CHUNKS + c]
        d = jnp.where(s == ZERO_CHUNK, jnp.where(slot == 0, _spare_chunk(c), _spare_chunk(TILE_CHUNKS + c)), s)
        return pltpu.make_async_copy(obuf.at[slot, c], ys_hbm.at[d], ssem.at[slot])

    def start_all(copy, tile, slot):
        for c in range(TILE_CHUNKS):
            copy(tile, slot, c).start()

    def wait_all(copy, tile, slot):
        for c in range(TILE_CHUNKS):
            copy(tile, slot, c).wait()

    @pl.when(j == 0)
    def _():
        start_all(gather_copy, 0, 0)

    @pl.when(j + 1 < n_active)
    def _():
        start_all(gather_copy, j + 1, (j + 1) % 2)

    @pl.when(j < n_active)
    def _():
        changed = jnp.logical_or(j == 0, te_ref[j] != te_ref[jnp.maximum(j - 1, 0)])

        @pl.when(changed)
        def _():
            wg16[...] = wg_ref[0].astype(BF16)
            wu16[...] = wu_ref[0].astype(BF16)
            wd16[...] = wd_ref[0].astype(BF16)

        slot = j % 2
        wait_all(gather_copy, j, slot)
        lo, hi = _from_chunks(xbuf[slot])
        x = jnp.concatenate([lo, hi], axis=1)
        gate = jnp.dot(x, wg16[...], preferred_element_type=F32)
        up = jnp.dot(x, wu16[...], preferred_element_type=F32)
        hmid = (gate * _sigmoid(gate) * up).astype(BF16)
        y = jnp.dot(hmid, wd16[...], preferred_element_type=F32)
        half = D_MODEL // 2
        obuf[slot] = _to_chunks(y[:, :half], y[:, half:])
        start_all(scatter_copy, j, slot)

        @pl.when(j >= 1)
        def _():
            wait_all(scatter_copy, j - 1, 1 - slot)

        @pl.when(j == n_active - 1)
        def _():
            wait_all(scatter_copy, j, slot)


def _experts(tile_expert, src, n_active, xs, w_gate, w_up, w_down, n_expert_tiles):
    chunk_shape = (2 * CHUNK, D_MODEL // 2)
    assert _spare_chunk(2 * TILE_CHUNKS - 1) < xs.shape[0]
    grid_spec = pltpu.PrefetchScalarGridSpec(
        num_scalar_prefetch=3,
        grid=(n_expert_tiles,),
        in_specs=[pl.BlockSpec(memory_space=pl.ANY),
                  pl.BlockSpec((1, D_MODEL, D_EXPERT), lambda j, te, s, n: (te[j], 0, 0)),
                  pl.BlockSpec((1, D_MODEL, D_EXPERT), lambda j, te, s, n: (te[j], 0, 0)),
                  pl.BlockSpec((1, D_EXPERT, D_MODEL), lambda j, te, s, n: (te[j], 0, 0))],
        out_specs=pl.BlockSpec(memory_space=pl.ANY),
        scratch_shapes=[pltpu.VMEM((2, TILE_CHUNKS) + chunk_shape, BF16),
                        pltpu.VMEM((2, TILE_CHUNKS) + chunk_shape, BF16),
                        pltpu.SemaphoreType.DMA((2,)),
                        pltpu.SemaphoreType.DMA((2,)),
                        pltpu.VMEM((D_MODEL, D_EXPERT), BF16),
                        pltpu.VMEM((D_MODEL, D_EXPERT), BF16),
                        pltpu.VMEM((D_EXPERT, D_MODEL), BF16)],
    )
    return pl.pallas_call(
        _expert_kernel,
        grid_spec=grid_spec,
        out_shape=jax.ShapeDtypeStruct(xs.shape, xs.dtype),
        input_output_aliases={3: 0},
        compiler_params=pltpu.CompilerParams(dimension_semantics=("arbitrary",), vmem_limit_bytes=VMEM_LIMIT),
        name="expert_ffn",
    )(tile_expert, src, n_active, xs, w_gate, w_up, w_down)


def _combine_kernel(n_prompt_tiles, x2p_ref, x2s_ref, route_ref, gfin_ref, ys_ref, outp_ref, outs_ref):
    i = pl.program_id(0)
    lo, hi = _from_chunks(ys_ref[...])
    route = route_ref[...]
    d1 = route[:, 0:1].astype(jnp.int32)
    d2 = route[:, 1:2].astype(jnp.int32)
    w1 = route[:, 2:3]
    w2 = route[:, 3:4]
    dest = lax.broadcasted_iota(jnp.int32, (TM, CAP_ROWS), 1)
    sel1 = (dest == d1).astype(F32).astype(BF16)
    sel2 = (dest == d2).astype(F32).astype(BF16)
    m1 = jnp.concatenate([jnp.dot(sel1, lo, preferred_element_type=F32),
                          jnp.dot(sel1, hi, preferred_element_type=F32)], axis=1)
    m2 = jnp.concatenate([jnp.dot(sel2, lo, preferred_element_type=F32),
                          jnp.dot(sel2, hi, preferred_element_type=F32)], axis=1)
    x = jnp.where(i < n_prompt_tiles, x2p_ref[...], x2s_ref[...])
    out = _rms(x + (w1 * m1 + w2 * m2), gfin_ref[...])

    @pl.when(i < n_prompt_tiles)
    def _():
        outp_ref[...] = out

    @pl.when(i >= n_prompt_tiles)
    def _():
        outs_ref[...] = out


def _combine(x2p, x2s, route, g_final, ys):
    n_prompt_tiles = x2p.shape[0] // TM
    n_tiles = n_prompt_tiles + x2s.shape[0] // TM
    p_spec = pl.BlockSpec((TM, D_MODEL), lambda i: (jnp.minimum(i, n_prompt_tiles - 1), 0))
    s_spec = pl.BlockSpec((TM, D_MODEL), lambda i: (jnp.maximum(i - n_prompt_tiles, 0), 0))
    return pl.pallas_call(
        functools.partial(_combine_kernel, n_prompt_tiles),
        grid=(n_tiles,),
        in_specs=[p_spec, s_spec,
                  pl.BlockSpec((TM, LANES), lambda i: (i, 0)),
                  _const_spec((1, D_MODEL)),
                  pl.BlockSpec((CAP_CHUNKS, 2 * CHUNK, D_MODEL // 2), lambda i: (i, 0, 0))],
        out_specs=[p_spec, s_spec],
        out_shape=[jax.ShapeDtypeStruct(x2p.shape, F32), jax.ShapeDtypeStruct(x2s.shape, F32)],
        compiler_params=pltpu.CompilerParams(dimension_semantics=("arbitrary",), vmem_limit_bytes=VMEM_LIMIT),
        name="combine_norm",
    )(x2p, x2s, route, g_final, ys)


def _block_diag_in(bb):
    t = bb.reshape(2, 16, SSM_STATE, SSM_GROUP).transpose(0, 1, 3, 2)
    eye = jnp.eye(16, dtype=bb.dtype)
    blk = t[:, :, :, None, :] * eye[None, :, None, :, None]
    return blk.reshape(2, 16 * SSM_GROUP, 16 * SSM_STATE)


def _block_diag_out(c):
    t = c.reshape(2, 16, SSM_GROUP, SSM_STATE).transpose(0, 1, 3, 2)
    eye = jnp.eye(16, dtype=c.dtype)
    blk = t[:, :, :, None, :] * eye[None, :, None, :, None]
    return blk.reshape(2, 16 * SSM_STATE, 16 * SSM_GROUP)


def kernel(x_prompt, x_sample, state_ssm_re, state_ssm_im, cache_pool, g_mix, w_in, ssm_a_re, ssm_a_im,
           ssm_log_dt, ssm_b_re, ssm_b_im, ssm_c_re, ssm_c_im, ssm_d, w_glu_a, w_glu_b, pool_w, pool_scale,
           w_out, g_ffn, w_router_group, w_router_expert, w_exp_gate, w_exp_up, w_exp_down, g_final):
    li = 0
    n_pb, seq_p, _ = x_prompt.shape
    n_sb, seq_s, _ = x_sample.shape

    a_re, a_im, a32_re, a32_im, bb_re, bb_im = _discretise(
        ssm_a_re[li], ssm_a_im[li], ssm_log_dt[li], ssm_b_re[li], ssm_b_im[li])
    row = lambda v: v.reshape(1, N_FLAT)
    wb = jnp.concatenate([_block_diag_in(bb_re), _block_diag_in(bb_im)], axis=2).astype(BF16)
    weights = [
        g_mix[li].reshape(1, D_MODEL),
        w_in[li].astype(BF16),
        wb,
        _block_diag_out(ssm_c_re[li]).astype(BF16),
        _block_diag_out(ssm_c_im[li]).astype(BF16),
        ssm_d[li].reshape(1, SSM_WIDTH),
        jnp.concatenate([w_glu_a[li], w_glu_b[li]], axis=1).astype(BF16),
        pool_w[li].astype(BF16),
        pool_scale[li].reshape(1, D_MODEL),
        w_out[li].astype(BF16),
        row(a_re), row(a_im), row(a32_re), row(a32_im),
    ]

    x2p, stp_re, stp_im, histp = _mixer_prompt(x_prompt, weights)
    x2p = x2p.reshape(n_pb * seq_p, D_MODEL)

    n_stiles = n_sb // N_CHAIN
    cache16 = jnp.pad(cache_pool[li], ((0, 0), (1, 0), (0, 0)))
    cache_t = cache16.reshape(n_stiles, N_CHAIN, 16, POOL_WIDTH).transpose(0, 2, 1, 3).reshape(
        n_stiles, HIST_ROWS, POOL_WIDTH)
    x2s, sts_re, sts_im, hists = _mixer_sample(
        x_sample, state_ssm_re[li].reshape(n_sb, N_FLAT), state_ssm_im[li].reshape(n_sb, N_FLAT),
        cache_t, weights)
    x2s = x2s.reshape(n_sb * seq_s, D_MODEL)

    w_router = jnp.concatenate(
        [w_router_group[li], w_router_expert[li].reshape(D_MODEL, N_EXPERTS),
         jnp.zeros((D_MODEL, LANES - N_EXPERT_GROUPS - N_EXPERTS), F32)], axis=1).astype(BF16)
    xs, route, nch = _route(x2p, x2s, g_ffn[li].reshape(1, D_MODEL), w_router)

    n_tiles = (x2p.shape[0] + x2s.shape[0]) // TM
    max_chunks = n_tiles * (2 * TM // CHUNK + N_EXPERTS)
    n_expert_tiles = max_chunks // TILE_CHUNKS + N_EXPERTS
    tile_expert, src, n_active = _chunk_tables(nch[:, 0, :N_EXPERTS], n_expert_tiles)
    ys = _experts(tile_expert, src, n_active, xs, w_exp_gate[li], w_exp_up[li], w_exp_down[li],
                  n_expert_tiles)
    yp, ysm = _combine(x2p, x2s, route, g_final.reshape(1, D_MODEL), ys)

    sd = state_ssm_re.dtype
    cd = cache_pool.dtype
    y_prompt = yp.reshape(n_pb, seq_p, D_MODEL)
    y_sample = ysm.reshape(n_sb, seq_s, D_MODEL)
    re_p = stp_re.reshape(1, n_pb, SSM_GROUPS, SSM_STATE).astype(sd)
    im_p = stp_im.reshape(1, n_pb, SSM_GROUPS, SSM_STATE).astype(sd)
    hist_p = histp[:, ::N_CHAIN, :][:, 1:, :].reshape(1, n_pb, POOL_HIST, POOL_WIDTH).astype(cd)
    re_s = sts_re.reshape(1, n_sb, SSM_GROUPS, SSM_STATE).astype(sd)
    im_s = sts_im.reshape(1, n_sb, SSM_GROUPS, SSM_STATE).astype(sd)
    hist_s = hists.reshape(n_stiles, 16, N_CHAIN, POOL_WIDTH).transpose(0, 2, 1, 3).reshape(
        n_sb, 16, POOL_WIDTH)[:, 1:, :].reshape(1, n_sb, POOL_HIST, POOL_WIDTH).astype(cd)
    return (y_prompt, y_sample, re_p, im_p, hist_p, re_s, im_s, hist_s)
```

```python
import functools
import math

import jax
import jax.numpy as jnp
from jax import lax
from jax.experimental import pallas as pl
from jax.experimental.pallas import tpu as pltpu

F32 = jnp.float32
BF16 = jnp.bfloat16

D_MODEL = 1024
SSM_WIDTH = 512
SSM_GROUPS = 32
SSM_GROUP = 16
SSM_STATE = 64
N_FLAT = SSM_GROUPS * SSM_STATE
HALF_FLAT = N_FLAT // 2
POOL_WIDTH = 512
POOL_WINDOWS = (2, 4, 8, 16)
POOL_GROUP_IN = 128
POOL_GROUP_OUT = 256
POOL_HIST = 15
N_EXPERTS = 32
EXPERTS_PER_GROUP = 8
N_EXPERT_GROUPS = 4
D_EXPERT = 512
EPS = 1e-6
PAST_LEN = 1024

TM = 256
N_CHAIN = 8
CHAIN_LEN = TM // N_CHAIN
HIST_ROWS = 16 * N_CHAIN

CHUNK = 8
TILE_CHUNKS = TM // CHUNK
CAP_CHUNKS = 96
CAP_ROWS = CAP_CHUNKS * CHUNK
MAX_USED_CHUNKS = 2 * TM // CHUNK + N_EXPERTS * (CHUNK - 1) // CHUNK
N_SPARE_SLOTS = CAP_CHUNKS - MAX_USED_CHUNKS - 1
ZERO_CHUNK = CAP_CHUNKS - 1
LANES = 128

VMEM_LIMIT = 52 * 1024 * 1024


def _rms(x, g):
    r = lax.rsqrt(jnp.mean(x * x, axis=-1, keepdims=True) + EPS)
    return x * r * g


def _sigmoid(x):
    return 0.5 * jnp.tanh(0.5 * x) + 0.5


def _gelu_tanh(x):
    c = math.sqrt(2.0 / math.pi)
    return x * (0.5 * (1.0 + jnp.tanh(c * (x + 0.044715 * (x * x * x)))))


def _disc_kernel(lre_ref, lim_ref, ldt_ref, bre_ref, bim_ref,
                 are_ref, aim_ref, a32re_ref, a32im_ref, bbre_ref, bbim_ref):
    lam_re = jnp.minimum(lre_ref[...], -1e-4)
    lam_im = lim_ref[...]
    dt = jnp.exp(ldt_ref[...])
    mag = jnp.exp(lam_re * dt)
    ang = lam_im * dt
    a_re = mag * jnp.cos(ang)
    a_im = mag * jnp.sin(ang)
    num_re = a_re - 1.0
    num_im = a_im
    den = lam_re * lam_re + lam_im * lam_im
    k_re = (num_re * lam_re + num_im * lam_im) / den
    k_im = (num_im * lam_re - num_re * lam_im) / den
    br = bre_ref[...]
    bi = bim_ref[...]
    bbre_ref[...] = k_re * br - k_im * bi
    bbim_ref[...] = k_re * bi + k_im * br
    are_ref[...] = a_re
    aim_ref[...] = a_im
    pr, pi = a_re, a_im
    for _ in range(int(math.log2(CHAIN_LEN))):
        pr, pi = pr * pr - pi * pi, 2.0 * pr * pi
    a32re_ref[...] = pr
    a32im_ref[...] = pi


def _discretise(a_re, a_im, log_dt, b_re, b_im):
    col = lambda v: v.reshape(N_FLAT, 1)
    ldt = jnp.broadcast_to(log_dt[:, None], (SSM_GROUPS, SSM_STATE))
    outs = pl.pallas_call(
        _disc_kernel,
        out_shape=[jax.ShapeDtypeStruct((N_FLAT, 1), F32)] * 4
        + [jax.ShapeDtypeStruct((N_FLAT, SSM_GROUP), F32)] * 2,
        name="s5_discretise",
    )(col(a_re), col(a_im), col(ldt), b_re.reshape(N_FLAT, SSM_GROUP), b_im.reshape(N_FLAT, SSM_GROUP))
    return outs


def _scan_half(hbuf, h, ar, ai, init_re, init_im, store):
    cre = h * N_FLAT
    cim = cre + HALF_FLAT
    hr, hi = init_re, init_im
    for t in range(CHAIN_LEN):
        rows = pl.ds(N_CHAIN * t, N_CHAIN)
        br = hbuf[rows, cre:cre + HALF_FLAT]
        bi = hbuf[rows, cim:cim + HALF_FLAT]
        nr = ar * hr - ai * hi + br
        ni = ar * hi + ai * hr + bi
        if store:
            hbuf[rows, cre:cre + HALF_FLAT] = nr
            hbuf[rows, cim:cim + HALF_FLAT] = ni
        hr, hi = nr, ni
    return hr, hi


N_MIXER_WEIGHTS = 14
N_PAR = 2


def _mixer_kernel(is_prompt, tiles_per_stream, *refs):
    n_in = 1 if is_prompt else 4
    ins = refs[:n_in]
    (gmix, win, wb, wcre, wcim, dsk, wglu, poolw, pscale, wout, are, aim, a32re, a32im) = refs[
        n_in:n_in + N_MIXER_WEIGHTS]
    x2_ref, stre_ref, stim_ref, hist_ref = refs[n_in + N_MIXER_WEIGHTS:n_in + N_MIXER_WEIGHTS + 4]
    scratch = refs[n_in + N_MIXER_WEIGHTS + 4:]
    per = len(scratch) // N_PAR
    lanes = [scratch[p * per:(p + 1) * per] for p in range(N_PAR)]
    x_ref = ins[0]

    tile_in_stream = pl.program_id(0) % tiles_per_stream
    n_lane_blocks = D_MODEL // LANES
    half_w = SSM_WIDTH // 2
    ar_full = are[...]
    ai_full = aim[...]

    if is_prompt:
        @pl.when(pl.program_id(0) == 0)
        def _():
            for p in range(N_PAR):
                cre_s, cim_s, pcarry = lanes[p][8], lanes[p][9], lanes[p][10]
                cre_s[...] = jnp.zeros_like(cre_s)
                cim_s[...] = jnp.zeros_like(cim_s)
                pcarry[...] = jnp.zeros_like(pcarry)

    def a_half(h):
        f0 = h * HALF_FLAT
        return (jnp.broadcast_to(ar_full[:, f0:f0 + HALF_FLAT], (N_CHAIN, HALF_FLAT)),
                jnp.broadcast_to(ai_full[:, f0:f0 + HALF_FLAT], (N_CHAIN, HALF_FLAT)))

    def project(p):
        xperm, hbuf = lanes[p][0], lanes[p][1]
        for j in range(N_CHAIN):
            for cb in range(n_lane_blocks):
                xperm[cb, pl.ds(j, CHAIN_LEN, stride=N_CHAIN), :] = x_ref[
                    p, CHAIN_LEN * j:CHAIN_LEN * (j + 1), cb * LANES:(cb + 1) * LANES]
        x = jnp.concatenate([xperm[cb] for cb in range(n_lane_blocks)], axis=1)
        xn = _rms(x, gmix[...]).astype(BF16)
        proj = jnp.dot(xn, win[...], preferred_element_type=F32)
        ub = proj[:, :SSM_WIDTH].astype(BF16)
        for h in range(2):
            hbuf[:, h * N_FLAT:(h + 1) * N_FLAT] = jnp.dot(
                ub[:, h * half_w:(h + 1) * half_w], wb[h], preferred_element_type=F32)
        return x, proj

    def recur(p):
        hbuf = lanes[p][1]
        if is_prompt:
            fre, fim, hre, him, cre_s, cim_s = lanes[p][4:10]
            zeros = jnp.zeros((N_CHAIN, HALF_FLAT), F32)
            for h in range(2):
                f0 = h * HALF_FLAT
                ar, ai = a_half(h)
                lr, li = _scan_half(hbuf, h, ar, ai, zeros, zeros, store=False)
                fre[:, f0:f0 + HALF_FLAT] = lr
                fim[:, f0:f0 + HALF_FLAT] = li
            fresh = tile_in_stream == 0
            hre[0:1, :] = jnp.where(fresh, 0.0, cre_s[...])
            him[0:1, :] = jnp.where(fresh, 0.0, cim_s[...])
            p_re = a32re[...]
            p_im = a32im[...]
            for j in range(N_CHAIN - 1):
                sr = hre[j:j + 1, :]
                si = him[j:j + 1, :]
                hre[j + 1:j + 2, :] = fre[j:j + 1, :] + p_re * sr - p_im * si
                him[j + 1:j + 2, :] = fim[j:j + 1, :] + p_re * si + p_im * sr
            init_re = hre[...]
            init_im = him[...]
        else:
            init_re = ins[1][p]
            init_im = ins[2][p]
        fin_re = []
        fin_im = []
        for h in range(2):
            f0 = h * HALF_FLAT
            ar, ai = a_half(h)
            er, ei = _scan_half(hbuf, h, ar, ai, init_re[:, f0:f0 + HALF_FLAT],
                                init_im[:, f0:f0 + HALF_FLAT], store=True)
            fin_re.append(er)
            fin_im.append(ei)
        end_re = jnp.concatenate(fin_re, axis=1)
        end_im = jnp.concatenate(fin_im, axis=1)
        if is_prompt:
            cre_s[...] = end_re[N_CHAIN - 1:N_CHAIN, :]
            cim_s[...] = end_im[N_CHAIN - 1:N_CHAIN, :]
            stre_ref[p] = end_re[N_CHAIN - 1:N_CHAIN, :]
            stim_ref[p] = end_im[N_CHAIN - 1:N_CHAIN, :]
        else:
            stre_ref[p] = end_re
            stim_ref[p] = end_im

    def finish(p, x, proj):
        hbuf, res, xpbuf = lanes[p][1], lanes[p][2], lanes[p][3]
        u_s = proj[:, :SSM_WIDTH]
        u_p = proj[:, SSM_WIDTH:SSM_WIDTH + POOL_WIDTH]
        gate_s = proj[:, SSM_WIDTH + POOL_WIDTH:SSM_WIDTH + POOL_WIDTH + D_MODEL]
        gate_p = proj[:, SSM_WIDTH + POOL_WIDTH + D_MODEL:]
        ys = []
        for h in range(2):
            c0 = h * N_FLAT
            h_re = hbuf[:, c0:c0 + HALF_FLAT].astype(BF16)
            h_im = hbuf[:, c0 + HALF_FLAT:c0 + N_FLAT].astype(BF16)
            ys.append(jnp.dot(h_re, wcre[h], preferred_element_type=F32)
                      - jnp.dot(h_im, wcim[h], preferred_element_type=F32))
        y = jnp.concatenate(ys, axis=1) + dsk[...] * u_s
        g = _gelu_tanh(y).astype(BF16)
        glu = jnp.dot(g, wglu[...], preferred_element_type=F32)
        o_s = glu[:, :D_MODEL] * _sigmoid(glu[:, D_MODEL:])

        xpbuf[HIST_ROWS:HIST_ROWS + TM, :] = u_p
        tail = u_p[TM - HIST_ROWS:, :]
        row = lax.broadcasted_iota(jnp.int32, (TM, 1), 0)
        if is_prompt:
            pcarry = lanes[p][10]
            first_chain = (lax.broadcasted_iota(jnp.int32, (HIST_ROWS, POOL_WIDTH), 0) % N_CHAIN) == 0
            carried = jnp.where(tile_in_stream == 0, 0.0, pcarry[...])
            xpbuf[0:HIST_ROWS, :] = jnp.where(first_chain, carried, pltpu.roll(tail, 1, 0))
            new_carry = pltpu.roll(tail, HIST_ROWS - (N_CHAIN - 1), 0)
            pcarry[...] = new_carry
            hist_ref[p] = new_carry
            pos1 = tile_in_stream * TM + CHAIN_LEN * (row % N_CHAIN) + row // N_CHAIN + 1
        else:
            xpbuf[0:HIST_ROWS, :] = ins[3][p]
            hist_ref[p] = tail
            pos1 = PAST_LEN + row // N_CHAIN + 1

        o_ps = []
        for gi, w in enumerate(POOL_WINDOWS):
            c0 = gi * POOL_GROUP_IN
            acc = xpbuf[HIST_ROWS:HIST_ROWS + TM, c0:c0 + POOL_GROUP_IN]
            for k in range(1, w):
                acc = acc + xpbuf[HIST_ROWS - N_CHAIN * k:HIST_ROWS - N_CHAIN * k + TM, c0:c0 + POOL_GROUP_IN]
            cnt = jnp.minimum(w, pos1).astype(F32)
            pooled = acc / cnt
            z = (pooled - u_p[:, c0:c0 + POOL_GROUP_IN]).astype(BF16)
            o_ps.append(jnp.dot(z, poolw[gi], preferred_element_type=F32))
        o_p = jnp.concatenate(o_ps, axis=1) * pscale[...]

        merged = (_sigmoid(gate_s) * o_s + _sigmoid(gate_p) * o_p).astype(BF16)
        x2 = x + jnp.dot(merged, wout[...], preferred_element_type=F32)
        for cb in range(n_lane_blocks):
            res[cb] = x2[:, cb * LANES:(cb + 1) * LANES]
        for j in range(N_CHAIN):
            for cb in range(n_lane_blocks):
                x2_ref[p, CHAIN_LEN * j:CHAIN_LEN * (j + 1), cb * LANES:(cb + 1) * LANES] = res[
                    cb, pl.ds(j, CHAIN_LEN, stride=N_CHAIN), :]

    projected = [project(p) for p in range(N_PAR)]
    for p in range(N_PAR):
        recur(p)
    for p in range(N_PAR):
        finish(p, *projected[p])


def _const_spec(shape):
    nd = len(shape)
    return pl.BlockSpec(shape, lambda i, _nd=nd: (0,) * _nd)


def _mixer_weight_specs():
    return [
        _const_spec((1, D_MODEL)),
        _const_spec((D_MODEL, 3 * D_MODEL)),
        _const_spec((2, SSM_WIDTH // 2, N_FLAT)),
        _const_spec((2, HALF_FLAT, SSM_WIDTH // 2)),
        _const_spec((2, HALF_FLAT, SSM_WIDTH // 2)),
        _const_spec((1, SSM_WIDTH)),
        _const_spec((SSM_WIDTH, 2 * D_MODEL)),
        _const_spec((len(POOL_WINDOWS), POOL_GROUP_IN, POOL_GROUP_OUT)),
        _const_spec((1, D_MODEL)),
        _const_spec((D_MODEL, D_MODEL)),
        _const_spec((1, N_FLAT)),
        _const_spec((1, N_FLAT)),
        _const_spec((1, N_FLAT)),
        _const_spec((1, N_FLAT)),
    ]


def _mixer_common_scratch():
    return [
        pltpu.VMEM((D_MODEL // LANES, TM, LANES), F32),
        pltpu.VMEM((TM, 2 * N_FLAT), F32),
        pltpu.VMEM((D_MODEL // LANES, TM, LANES), F32),
        pltpu.VMEM((HIST_ROWS + TM, POOL_WIDTH), F32),
    ]


def _mixer_prompt(x, weights):
    n_streams, seq, _ = x.shape
    assert n_streams % N_PAR == 0 and seq % TM == 0
    tiles_per_stream = seq // TM
    blk = lambda shape: pl.BlockSpec(shape, lambda i: (i // tiles_per_stream, 0, 0))
    row_spec = pl.BlockSpec((N_PAR, TM, D_MODEL), lambda i: (i // tiles_per_stream, i % tiles_per_stream, 0))
    lane_scratch = _mixer_common_scratch() + [
        pltpu.VMEM((N_CHAIN, N_FLAT), F32), pltpu.VMEM((N_CHAIN, N_FLAT), F32),
        pltpu.VMEM((N_CHAIN, N_FLAT), F32), pltpu.VMEM((N_CHAIN, N_FLAT), F32),
        pltpu.VMEM((1, N_FLAT), F32), pltpu.VMEM((1, N_FLAT), F32),
        pltpu.VMEM((HIST_ROWS, POOL_WIDTH), F32),
    ]
    return pl.pallas_call(
        functools.partial(_mixer_kernel, True, tiles_per_stream),
        grid=(n_streams // N_PAR * tiles_per_stream,),
        in_specs=[row_spec] + _mixer_weight_specs(),
        out_specs=[row_spec, blk((N_PAR, 1, N_FLAT)), blk((N_PAR, 1, N_FLAT)),
                   blk((N_PAR, HIST_ROWS, POOL_WIDTH))],
        out_shape=[jax.ShapeDtypeStruct((n_streams, seq, D_MODEL), F32),
                   jax.ShapeDtypeStruct((n_streams, 1, N_FLAT), F32),
                   jax.ShapeDtypeStruct((n_streams, 1, N_FLAT), F32),
                   jax.ShapeDtypeStruct((n_streams, HIST_ROWS, POOL_WIDTH), F32)],
        scratch_shapes=lane_scratch * N_PAR,
        compiler_params=pltpu.CompilerParams(dimension_semantics=("arbitrary",), vmem_limit_bytes=VMEM_LIMIT),
        name="mixer_prompt",
    )(x, *weights)


def _mixer_sample(x, h0_re, h0_im, cache_t, weights):
    n_streams, seq, _ = x.shape
    assert seq == CHAIN_LEN and n_streams % (N_CHAIN * N_PAR) == 0
    n_tiles = n_streams // N_CHAIN
    blk = lambda shape: pl.BlockSpec(shape, lambda i: (i, 0, 0))
    row_spec = blk((N_PAR, TM, D_MODEL))
    st_spec = blk((N_PAR, N_CHAIN, N_FLAT))
    hist_spec = blk((N_PAR, HIST_ROWS, POOL_WIDTH))
    tiles = lambda v: v.reshape((n_tiles, -1) + v.shape[-1:])
    return pl.pallas_call(
        functools.partial(_mixer_kernel, False, 1),
        grid=(n_tiles // N_PAR,),
        in_specs=[row_spec, st_spec, st_spec, hist_spec] + _mixer_weight_specs(),
        out_specs=[row_spec, st_spec, st_spec, hist_spec],
        out_shape=[jax.ShapeDtypeStruct((n_tiles, TM, D_MODEL), F32),
                   jax.ShapeDtypeStruct((n_tiles, N_CHAIN, N_FLAT), F32),
                   jax.ShapeDtypeStruct((n_tiles, N_CHAIN, N_FLAT), F32),
                   jax.ShapeDtypeStruct((n_tiles, HIST_ROWS, POOL_WIDTH), F32)],
        scratch_shapes=_mixer_common_scratch() * N_PAR,
        compiler_params=pltpu.CompilerParams(dimension_semantics=("arbitrary",), vmem_limit_bytes=VMEM_LIMIT),
        name="mixer_sample",
    )(tiles(x), tiles(h0_re), tiles(h0_im), cache_t, *weights)


def _to_chunks(lo, hi):
    n = lo.shape[0] // CHUNK
    half = D_MODEL // 2
    both = jnp.concatenate([lo.reshape(n, CHUNK, half), hi.reshape(n, CHUNK, half)], axis=1)
    return both.astype(BF16)


def _from_chunks(blk):
    n = blk.shape[0]
    half = D_MODEL // 2
    f = blk.astype(F32)
    lo = f[:, :CHUNK, :].reshape(n * CHUNK, half).astype(BF16)
    hi = f[:, CHUNK:, :].reshape(n * CHUNK, half).astype(BF16)
    return lo, hi


def _route_kernel(n_prompt_tiles, x2p_ref, x2s_ref, g_ref, wr_ref, ltri_ref, utri_ref,
                  xs_ref, route_ref, nch_ref):
    i = pl.program_id(0)
    x = jnp.where(i < n_prompt_tiles, x2p_ref[...], x2s_ref[...])
    xn = _rms(x, g_ref[...]).astype(BF16)
    logits = jnp.dot(xn, wr_ref[...], preferred_element_type=F32)
    lane = lax.broadcasted_iota(jnp.int32, (TM, LANES), 1)
    lane_f = lane.astype(F32)
    big = jnp.float32(1 << 20)
    neg = jnp.float32(-jnp.inf)

    gmask = lane < N_EXPERT_GROUPS
    m = jnp.max(jnp.where(gmask, logits, neg), axis=1, keepdims=True)
    grp = jnp.min(jnp.where(gmask & (logits == m), lane_f, big), axis=1, keepdims=True)
    wg = 1.0 / jnp.sum(jnp.where(gmask, jnp.exp(logits - m), 0.0), axis=1, keepdims=True)

    eid = lane - N_EXPERT_GROUPS
    lane_grp = (eid >> 3).astype(F32)
    emask = (eid >= 0) & (eid < N_EXPERTS) & (lane_grp == grp)
    v1 = jnp.max(jnp.where(emask, logits, neg), axis=1, keepdims=True)
    i1 = jnp.min(jnp.where(emask & (logits == v1), lane_f, big), axis=1, keepdims=True)
    emask2 = emask & (lane_f != i1)
    v2 = jnp.max(jnp.where(emask2, logits, neg), axis=1, keepdims=True)
    i2 = jnp.min(jnp.where(emask2 & (logits == v2), lane_f, big), axis=1, keepdims=True)
    e21 = jnp.exp(v2 - v1)
    w1 = wg / (1.0 + e21)
    w2 = wg * e21 / (1.0 + e21)

    a1 = lane_f == (i1 - N_EXPERT_GROUPS)
    a2 = lane_f == (i2 - N_EXPERT_GROUPS)
    a = (a1 | a2).astype(F32)
    before = jnp.dot(ltri_ref[...], a.astype(BF16), preferred_element_type=F32)
    cnt = jnp.sum(a, axis=0, keepdims=True)
    nch = jnp.floor((cnt + (CHUNK - 1)) * (1.0 / CHUNK))
    nch16 = jnp.broadcast_to(nch, (16, LANES))
    start = jnp.dot(nch16.astype(BF16), utri_ref[...], preferred_element_type=F32)
    slot = before + CHUNK * start[0:1, :]
    d1 = jnp.sum(jnp.where(a1, slot, 0.0), axis=1, keepdims=True)
    d2 = jnp.sum(jnp.where(a2, slot, 0.0), axis=1, keepdims=True)

    route = jnp.where(lane == 0, d1, jnp.where(lane == 1, d2, jnp.where(lane == 2, w1,
                      jnp.where(lane == 3, w2, 0.0))))
    route_ref[...] = route
    nch_ref[0] = nch16[0:8, :].astype(jnp.int32)

    dt = jnp.transpose(jnp.where(lane < 2, route, 0.0)).astype(jnp.int32)
    d1row = dt[0:1, :]
    d2row = dt[1:2, :]
    dest = lax.broadcasted_iota(jnp.int32, (CAP_ROWS, TM), 0)
    perm = ((dest == d1row) | (dest == d2row)).astype(F32).astype(BF16)
    half = D_MODEL // 2
    lo = jnp.dot(perm, xn[:, :half], preferred_element_type=F32)
    hi = jnp.dot(perm, xn[:, half:], preferred_element_type=F32)
    xs_ref[...] = _to_chunks(lo, hi)


def _route(x2p, x2s, g_ffn, w_router):
    n_prompt_tiles = x2p.shape[0] // TM
    n_tiles = n_prompt_tiles + x2s.shape[0] // TM
    r = jnp.arange(TM)
    ltri = (r[None, :] < r[:, None]).astype(BF16)
    e = jnp.arange(LANES)
    utri = (e[:, None] < e[None, :]).astype(BF16)
    return pl.pallas_call(
        functools.partial(_route_kernel, n_prompt_tiles),
        grid=(n_tiles,),
        in_specs=[pl.BlockSpec((TM, D_MODEL), lambda i: (jnp.minimum(i, n_prompt_tiles - 1), 0)),
                  pl.BlockSpec((TM, D_MODEL), lambda i: (jnp.maximum(i - n_prompt_tiles, 0), 0)),
                  _const_spec((1, D_MODEL)), _const_spec((D_MODEL, LANES)),
                  _const_spec((TM, TM)), _const_spec((LANES, LANES))],
        out_specs=[pl.BlockSpec((CAP_CHUNKS, 2 * CHUNK, D_MODEL // 2), lambda i: (i, 0, 0)),
                   pl.BlockSpec((TM, LANES), lambda i: (i, 0)),
                   pl.BlockSpec((1, 8, LANES), lambda i: (i, 0, 0))],
        out_shape=[jax.ShapeDtypeStruct((n_tiles * CAP_CHUNKS, 2 * CHUNK, D_MODEL // 2), BF16),
                   jax.ShapeDtypeStruct((n_tiles * TM, LANES), F32),
                   jax.ShapeDtypeStruct((n_tiles, 8, LANES), jnp.int32)],
        compiler_params=pltpu.CompilerParams(dimension_semantics=("arbitrary",), vmem_limit_bytes=VMEM_LIMIT),
        name="route_sort",
    )(x2p, x2s, g_ffn, w_router, ltri, utri)


def _spare_chunk(k):
    return (1 + k // N_SPARE_SLOTS) * CAP_CHUNKS + MAX_USED_CHUNKS + k % N_SPARE_SLOTS


def _chunk_tables(nch, n_expert_tiles):
    n_tiles = nch.shape[0]
    i32 = jnp.int32
    start = jnp.cumsum(nch, axis=1) - nch
    off = jnp.cumsum(nch, axis=0) - nch
    per_expert = jnp.sum(nch, axis=0)
    tiles_e = (per_expert + TILE_CHUNKS - 1) // TILE_CHUNKS
    cum_tiles = jnp.cumsum(tiles_e)
    first_tile = cum_tiles - tiles_e
    n_active = cum_tiles[-1].astype(i32)
    t = jnp.arange(n_expert_tiles, dtype=i32)
    te = jnp.minimum(jnp.sum((t[:, None] >= cum_tiles[None, :]).astype(i32), axis=1), N_EXPERTS - 1)
    onehot = te[:, None] == jnp.arange(N_EXPERTS, dtype=i32)[None, :]
    pick = lambda tab: jnp.sum(jnp.where(onehot[:, :, None], tab.T[None, :, :], 0), axis=1)
    off_t, nch_t, start_t = pick(off), pick(nch), pick(start)
    k = t - jnp.sum(jnp.where(onehot, first_tile[None, :], 0), axis=1)
    q = (TILE_CHUNKS * k)[:, None] + jnp.arange(TILE_CHUNKS, dtype=i32)[None, :]
    in_run = (off_t[:, None, :] <= q[:, :, None]) & (q[:, :, None] < (off_t + nch_t)[:, None, :])
    run_src = (jnp.arange(n_tiles, dtype=i32) * CAP_CHUNKS)[None, :] + start_t - off_t
    src = jnp.sum(jnp.where(in_run, run_src[:, None, :], 0), axis=-1) + q
    valid = jnp.any(in_run, axis=-1) & (t < n_active)[:, None]
    src = jnp.where(valid, src, ZERO_CHUNK).astype(i32)
    spare = _spare_chunk((t % 2)[:, None] * TILE_CHUNKS + jnp.arange(TILE_CHUNKS, dtype=i32)[None, :])
    dst = jnp.where(valid, src, spare).astype(i32)
    return first_tile.astype(i32), tiles_e.astype(i32), src.reshape(-1), dst.reshape(-1), n_active.reshape(1)


def _expert_kernel(first_ref, ntile_ref, src_ref, dst_ref, nact_ref, xs_hbm, wg_ref, wu_ref, wd_ref, ys_hbm,
                   xbuf0, xbuf1, obuf0, obuf1, gsem, ssem, wg16, wu16, wd16):
    e = pl.program_id(0)
    n_active = nact_ref[0]
    xbufs = (xbuf0, xbuf1)
    obufs = (obuf0, obuf1)

    def gather_copy(tile, slot, c):
        return pltpu.make_async_copy(xs_hbm.at[src_ref[tile * TILE_CHUNKS + c]], xbufs[slot].at[c],
                                     gsem.at[slot])

    def scatter_copy(tile, slot, c):
        return pltpu.make_async_copy(obufs[slot].at[c], ys_hbm.at[dst_ref[tile * TILE_CHUNKS + c]],
                                     ssem.at[slot])

    def start_all(copy, tile, slot):
        for c in range(TILE_CHUNKS):
            copy(tile, slot, c).start()

    def wait_all(copy, tile, slot):
        for c in range(TILE_CHUNKS):
            copy(tile, slot, c).wait()

    @pl.when(e == 0)
    def _():
        start_all(gather_copy, 0, 0)

    wg16[...] = wg_ref[0].astype(BF16)
    wu16[...] = wu_ref[0].astype(BF16)
    wd16[...] = wd_ref[0].astype(BF16)

    def do_tile(tile, slot):
        wait_all(gather_copy, tile, slot)
        lo, hi = _from_chunks(xbufs[slot][...])
        start_all(gather_copy, tile + 1, 1 - slot)
        x = jnp.concatenate([lo, hi], axis=1)
        gate = jnp.dot(x, wg16[...], preferred_element_type=F32)
        up = jnp.dot(x, wu16[...], preferred_element_type=F32)
        hmid = (gate * _sigmoid(gate) * up).astype(BF16)
        y = jnp.dot(hmid, wd16[...], preferred_element_type=F32)
        half = D_MODEL // 2
        obufs[slot][...] = _to_chunks(y[:, :half], y[:, half:])
        start_all(scatter_copy, tile, slot)

        @pl.when(tile >= 1)
        def _():
            wait_all(scatter_copy, tile - 1, 1 - slot)

    def tile_body(k, carry):
        tile = first_ref[e] + k
        for slot in range(2):
            @pl.when(tile % 2 == slot)
            def _():
                do_tile(tile, slot)
        return carry

    lax.fori_loop(0, ntile_ref[e], tile_body, 0)

    @pl.when(e == pl.num_programs(0) - 1)
    def _():
        for slot in range(2):
            @pl.when(n_active % 2 == slot)
            def _():
                wait_all(scatter_copy, n_active - 1, 1 - slot)
                wait_all(gather_copy, n_active, slot)


def _experts(first_tile, tiles_e, src, dst, n_active, xs, w_gate, w_up, w_down):
    chunk_shape = (2 * CHUNK, D_MODEL // 2)
    assert _spare_chunk(2 * TILE_CHUNKS - 1) < xs.shape[0]
    tile_buf = pltpu.VMEM((TILE_CHUNKS,) + chunk_shape, BF16)
    grid_spec = pltpu.PrefetchScalarGridSpec(
        num_scalar_prefetch=5,
        grid=(N_EXPERTS,),
        in_specs=[pl.BlockSpec(memory_space=pl.ANY),
                  pl.BlockSpec((1, D_MODEL, D_EXPERT), lambda e, *_: (e, 0, 0)),
                  pl.BlockSpec((1, D_MODEL, D_EXPERT), lambda e, *_: (e, 0, 0)),
                  pl.BlockSpec((1, D_EXPERT, D_MODEL), lambda e, *_: (e, 0, 0))],
        out_specs=pl.BlockSpec(memory_space=pl.ANY),
        scratch_shapes=[tile_buf, tile_buf, tile_buf, tile_buf,
                        pltpu.SemaphoreType.DMA((2,)),
                        pltpu.SemaphoreType.DMA((2,)),
                        pltpu.VMEM((D_MODEL, D_EXPERT), BF16),
                        pltpu.VMEM((D_MODEL, D_EXPERT), BF16),
                        pltpu.VMEM((D_EXPERT, D_MODEL), BF16)],
    )
    return pl.pallas_call(
        _expert_kernel,
        grid_spec=grid_spec,
        out_shape=jax.ShapeDtypeStruct(xs.shape, xs.dtype),
        input_output_aliases={5: 0},
        compiler_params=pltpu.CompilerParams(dimension_semantics=("arbitrary",), vmem_limit_bytes=VMEM_LIMIT),
        name="expert_ffn",
    )(first_tile, tiles_e, src, dst, n_active, xs, w_gate, w_up, w_down)


def _combine_kernel(n_prompt_tiles, x2p_ref, x2s_ref, route_ref, gfin_ref, ys_ref, outp_ref, outs_ref):
    i = pl.program_id(0)
    lo, hi = _from_chunks(ys_ref[...])
    route = route_ref[...]
    d1 = route[:, 0:1].astype(jnp.int32)
    d2 = route[:, 1:2].astype(jnp.int32)
    w1 = route[:, 2:3]
    w2 = route[:, 3:4]
    dest = lax.broadcasted_iota(jnp.int32, (TM, CAP_ROWS), 1)
    sel1 = (dest == d1).astype(F32).astype(BF16)
    sel2 = (dest == d2).astype(F32).astype(BF16)
    m1 = jnp.concatenate([jnp.dot(sel1, lo, preferred_element_type=F32),
                          jnp.dot(sel1, hi, preferred_element_type=F32)], axis=1)
    m2 = jnp.concatenate([jnp.dot(sel2, lo, preferred_element_type=F32),
                          jnp.dot(sel2, hi, preferred_element_type=F32)], axis=1)
    x = jnp.where(i < n_prompt_tiles, x2p_ref[...], x2s_ref[...])
    out = _rms(x + (w1 * m1 + w2 * m2), gfin_ref[...])

    @pl.when(i < n_prompt_tiles)
    def _():
        outp_ref[...] = out

    @pl.when(i >= n_prompt_tiles)
    def _():
        outs_ref[...] = out


def _combine(x2p, x2s, route, g_final, ys):
    n_prompt_tiles = x2p.shape[0] // TM
    n_tiles = n_prompt_tiles + x2s.shape[0] // TM
    p_spec = pl.BlockSpec((TM, D_MODEL), lambda i: (jnp.minimum(i, n_prompt_tiles - 1), 0))
    s_spec = pl.BlockSpec((TM, D_MODEL), lambda i: (jnp.maximum(i - n_prompt_tiles, 0), 0))
    return pl.pallas_call(
        functools.partial(_combine_kernel, n_prompt_tiles),
        grid=(n_tiles,),
        in_specs=[p_spec, s_spec,
                  pl.BlockSpec((TM, LANES), lambda i: (i, 0)),
                  _const_spec((1, D_MODEL)),
                  pl.BlockSpec((CAP_CHUNKS, 2 * CHUNK, D_MODEL // 2), lambda i: (i, 0, 0))],
        out_specs=[p_spec, s_spec],
        out_shape=[jax.ShapeDtypeStruct(x2p.shape, F32), jax.ShapeDtypeStruct(x2s.shape, F32)],
        compiler_params=pltpu.CompilerParams(dimension_semantics=("arbitrary",), vmem_limit_bytes=VMEM_LIMIT),
        name="combine_norm",
    )(x2p, x2s, route, g_final, ys)


def _block_diag_in(bb):
    t = bb.reshape(2, 16, SSM_STATE, SSM_GROUP).transpose(0, 1, 3, 2)
    eye = jnp.eye(16, dtype=bb.dtype)
    blk = t[:, :, :, None, :] * eye[None, :, None, :, None]
    return blk.reshape(2, 16 * SSM_GROUP, 16 * SSM_STATE)


def _block_diag_out(c):
    t = c.reshape(2, 16, SSM_GROUP, SSM_STATE).transpose(0, 1, 3, 2)
    eye = jnp.eye(16, dtype=c.dtype)
    blk = t[:, :, :, None, :] * eye[None, :, None, :, None]
    return blk.reshape(2, 16 * SSM_STATE, 16 * SSM_GROUP)


def kernel(x_prompt, x_sample, state_ssm_re, state_ssm_im, cache_pool, g_mix, w_in, ssm_a_re, ssm_a_im,
           ssm_log_dt, ssm_b_re, ssm_b_im, ssm_c_re, ssm_c_im, ssm_d, w_glu_a, w_glu_b, pool_w, pool_scale,
           w_out, g_ffn, w_router_group, w_router_expert, w_exp_gate, w_exp_up, w_exp_down, g_final):
    li = 0
    n_pb, seq_p, _ = x_prompt.shape
    n_sb, seq_s, _ = x_sample.shape

    a_re, a_im, a32_re, a32_im, bb_re, bb_im = _discretise(
        ssm_a_re[li], ssm_a_im[li], ssm_log_dt[li], ssm_b_re[li], ssm_b_im[li])
    row = lambda v: v.reshape(1, N_FLAT)
    wb = jnp.concatenate([_block_diag_in(bb_re), _block_diag_in(bb_im)], axis=2).astype(BF16)
    weights = [
        g_mix[li].reshape(1, D_MODEL),
        w_in[li].astype(BF16),
        wb,
        _block_diag_out(ssm_c_re[li]).astype(BF16),
        _block_diag_out(ssm_c_im[li]).astype(BF16),
        ssm_d[li].reshape(1, SSM_WIDTH),
        jnp.concatenate([w_glu_a[li], w_glu_b[li]], axis=1).astype(BF16),
        pool_w[li].astype(BF16),
        pool_scale[li].reshape(1, D_MODEL),
        w_out[li].astype(BF16),
        row(a_re), row(a_im), row(a32_re), row(a32_im),
    ]

    x2p, stp_re, stp_im, histp = _mixer_prompt(x_prompt, weights)
    x2p = x2p.reshape(n_pb * seq_p, D_MODEL)

    n_stiles = n_sb // N_CHAIN
    cache16 = jnp.pad(cache_pool[li], ((0, 0), (1, 0), (0, 0)))
    cache_t = cache16.reshape(n_stiles, N_CHAIN, 16, POOL_WIDTH).transpose(0, 2, 1, 3).reshape(
        n_stiles, HIST_ROWS, POOL_WIDTH)
    x2s, sts_re, sts_im, hists = _mixer_sample(
        x_sample, state_ssm_re[li].reshape(n_sb, N_FLAT), state_ssm_im[li].reshape(n_sb, N_FLAT),
        cache_t, weights)
    x2s = x2s.reshape(n_sb * seq_s, D_MODEL)

    w_router = jnp.concatenate(
        [w_router_group[li], w_router_expert[li].reshape(D_MODEL, N_EXPERTS),
         jnp.zeros((D_MODEL, LANES - N_EXPERT_GROUPS - N_EXPERTS), F32)], axis=1).astype(BF16)
    xs, route, nch = _route(x2p, x2s, g_ffn[li].reshape(1, D_MODEL), w_router)

    n_tiles = (x2p.shape[0] + x2s.shape[0]) // TM
    max_chunks = n_tiles * (2 * TM // CHUNK + N_EXPERTS)
    n_expert_tiles = max_chunks // TILE_CHUNKS + N_EXPERTS
    first_tile, tiles_e, src, dst, n_active = _chunk_tables(nch[:, 0, :N_EXPERTS], n_expert_tiles)
    ys = _experts(first_tile, tiles_e, src, dst, n_active, xs, w_exp_gate[li], w_exp_up[li], w_exp_down[li])
    yp, ysm = _combine(x2p, x2s, route, g_final.reshape(1, D_MODEL), ys)

    sd = state_ssm_re.dtype
    cd = cache_pool.dtype
    y_prompt = yp.reshape(n_pb, seq_p, D_MODEL)
    y_sample = ysm.reshape(n_sb, seq_s, D_MODEL)
    re_p = stp_re.reshape(1, n_pb, SSM_GROUPS, SSM_STATE).astype(sd)
    im_p = stp_im.reshape(1, n_pb, SSM_GROUPS, SSM_STATE).astype(sd)
    hist_p = histp[:, ::N_CHAIN, :][:, 1:, :].reshape(1, n_pb, POOL_HIST, POOL_WIDTH).astype(cd)
    re_s = sts_re.reshape(1, n_sb, SSM_GROUPS, SSM_STATE).astype(sd)
    im_s = sts_im.reshape(1, n_sb, SSM_GROUPS, SSM_STATE).astype(sd)
    hist_s = hists.reshape(n_stiles, 16, N_CHAIN, POOL_WIDTH).transpose(0, 2, 1, 3).reshape(
        n_sb, 16, POOL_WIDTH)[:, 1:, :].reshape(1, n_sb, POOL_HIST, POOL_WIDTH).astype(cd)
    return (y_prompt, y_sample, re_p, im_p, hist_p, re_s, im_s, hist_s)
```

```python
import functools
import math

import jax
import jax.numpy as jnp
from jax import lax
from jax.experimental import pallas as pl
from jax.experimental.pallas import tpu as pltpu

F32 = jnp.float32
BF16 = jnp.bfloat16

D_MODEL = 1024
SSM_WIDTH = 512
SSM_GROUPS = 32
SSM_GROUP = 16
SSM_STATE = 64
N_FLAT = SSM_GROUPS * SSM_STATE
HALF_FLAT = N_FLAT // 2
POOL_WIDTH = 512
POOL_WINDOWS = (2, 4, 8, 16)
POOL_GROUP_IN = 128
POOL_GROUP_OUT = 256
POOL_HIST = 15
N_EXPERTS = 32
EXPERTS_PER_GROUP = 8
N_EXPERT_GROUPS = 4
D_EXPERT = 512
EPS = 1e-6
PAST_LEN = 1024

TM = 256
N_CHAIN = 8
CHAIN_LEN = TM // N_CHAIN
HIST_ROWS = 16 * N_CHAIN

CHUNK = 8
TILE_CHUNKS = TM // CHUNK
CAP_CHUNKS = 96
CAP_ROWS = CAP_CHUNKS * CHUNK
MAX_USED_CHUNKS = 2 * TM // CHUNK + N_EXPERTS * (CHUNK - 1) // CHUNK
N_SPARE_SLOTS = CAP_CHUNKS - MAX_USED_CHUNKS - 1
ZERO_CHUNK = CAP_CHUNKS - 1
LANES = 128

VMEM_LIMIT = 52 * 1024 * 1024


def _rms(x, g):
    r = lax.rsqrt(jnp.mean(x * x, axis=-1, keepdims=True) + EPS)
    return x * r * g


def _sigmoid(x):
    return 0.5 * jnp.tanh(0.5 * x) + 0.5


def _gelu_tanh(x):
    c = math.sqrt(2.0 / math.pi)
    return x * (0.5 * (1.0 + jnp.tanh(c * (x + 0.044715 * (x * x * x)))))


def _disc_kernel(lre_ref, lim_ref, ldt_ref, bre_ref, bim_ref,
                 are_ref, aim_ref, a32re_ref, a32im_ref, bbre_ref, bbim_ref):
    lam_re = jnp.minimum(lre_ref[...], -1e-4)
    lam_im = lim_ref[...]
    dt = jnp.exp(ldt_ref[...])
    mag = jnp.exp(lam_re * dt)
    ang = lam_im * dt
    a_re = mag * jnp.cos(ang)
    a_im = mag * jnp.sin(ang)
    num_re = a_re - 1.0
    num_im = a_im
    den = lam_re * lam_re + lam_im * lam_im
    k_re = (num_re * lam_re + num_im * lam_im) / den
    k_im = (num_im * lam_re - num_re * lam_im) / den
    br = bre_ref[...]
    bi = bim_ref[...]
    bbre_ref[...] = k_re * br - k_im * bi
    bbim_ref[...] = k_re * bi + k_im * br
    are_ref[...] = a_re
    aim_ref[...] = a_im
    pr, pi = a_re, a_im
    for _ in range(int(math.log2(CHAIN_LEN))):
        pr, pi = pr * pr - pi * pi, 2.0 * pr * pi
    a32re_ref[...] = pr
    a32im_ref[...] = pi


def _discretise(a_re, a_im, log_dt, b_re, b_im):
    col = lambda v: v.reshape(N_FLAT, 1)
    ldt = jnp.broadcast_to(log_dt[:, None], (SSM_GROUPS, SSM_STATE))
    outs = pl.pallas_call(
        _disc_kernel,
        out_shape=[jax.ShapeDtypeStruct((N_FLAT, 1), F32)] * 4
        + [jax.ShapeDtypeStruct((N_FLAT, SSM_GROUP), F32)] * 2,
        name="s5_discretise",
    )(col(a_re), col(a_im), col(ldt), b_re.reshape(N_FLAT, SSM_GROUP), b_im.reshape(N_FLAT, SSM_GROUP))
    return outs


def _scan_half(hbuf, h, ar, ai, init_re, init_im, store):
    cre = h * N_FLAT
    cim = cre + HALF_FLAT
    hr, hi = init_re, init_im
    for t in range(CHAIN_LEN):
        rows = pl.ds(N_CHAIN * t, N_CHAIN)
        br = hbuf[rows, cre:cre + HALF_FLAT]
        bi = hbuf[rows, cim:cim + HALF_FLAT]
        nr = ar * hr - ai * hi + br
        ni = ar * hi + ai * hr + bi
        if store:
            hbuf[rows, cre:cre + HALF_FLAT] = nr
            hbuf[rows, cim:cim + HALF_FLAT] = ni
        hr, hi = nr, ni
    return hr, hi


N_MIXER_WEIGHTS = 14
N_PAR = 2


def _mixer_kernel(is_prompt, tiles_per_stream, *refs):
    n_in = 1 if is_prompt else 4
    ins = refs[:n_in]
    (gmix, win, wb, wcre, wcim, dsk, wglu, poolw, pscale, wout, are, aim, a32re, a32im) = refs[
        n_in:n_in + N_MIXER_WEIGHTS]
    x2_ref, stre_ref, stim_ref, hist_ref = refs[n_in + N_MIXER_WEIGHTS:n_in + N_MIXER_WEIGHTS + 4]
    scratch = refs[n_in + N_MIXER_WEIGHTS + 4:]
    per = len(scratch) // N_PAR
    lanes = [scratch[p * per:(p + 1) * per] for p in range(N_PAR)]
    x_ref = ins[0]

    tile_in_stream = pl.program_id(0) % tiles_per_stream
    n_lane_blocks = D_MODEL // LANES
    half_w = SSM_WIDTH // 2
    ar_full = are[...]
    ai_full = aim[...]

    if is_prompt:
        @pl.when(pl.program_id(0) == 0)
        def _():
            for p in range(N_PAR):
                cre_s, cim_s, pcarry = lanes[p][8], lanes[p][9], lanes[p][10]
                cre_s[...] = jnp.zeros_like(cre_s)
                cim_s[...] = jnp.zeros_like(cim_s)
                pcarry[...] = jnp.zeros_like(pcarry)

    def a_half(h):
        f0 = h * HALF_FLAT
        return (jnp.broadcast_to(ar_full[:, f0:f0 + HALF_FLAT], (N_CHAIN, HALF_FLAT)),
                jnp.broadcast_to(ai_full[:, f0:f0 + HALF_FLAT], (N_CHAIN, HALF_FLAT)))

    def project(p):
        xperm, hbuf = lanes[p][0], lanes[p][1]
        for j in range(N_CHAIN):
            for cb in range(n_lane_blocks):
                xperm[cb, pl.ds(j, CHAIN_LEN, stride=N_CHAIN), :] = x_ref[
                    p, CHAIN_LEN * j:CHAIN_LEN * (j + 1), cb * LANES:(cb + 1) * LANES]
        x = jnp.concatenate([xperm[cb] for cb in range(n_lane_blocks)], axis=1)
        xn = _rms(x, gmix[...]).astype(BF16)
        proj = jnp.dot(xn, win[...], preferred_element_type=F32)
        ub = proj[:, :SSM_WIDTH].astype(BF16)
        for h in range(2):
            hbuf[:, h * N_FLAT:(h + 1) * N_FLAT] = jnp.dot(
                ub[:, h * half_w:(h + 1) * half_w], wb[h], preferred_element_type=F32)
        return x, proj

    def recur(p):
        hbuf = lanes[p][1]
        if is_prompt:
            fre, fim, hre, him, cre_s, cim_s = lanes[p][4:10]
            zeros = jnp.zeros((N_CHAIN, HALF_FLAT), F32)
            for h in range(2):
                f0 = h * HALF_FLAT
                ar, ai = a_half(h)
                lr, li = _scan_half(hbuf, h, ar, ai, zeros, zeros, store=False)
                fre[:, f0:f0 + HALF_FLAT] = lr
                fim[:, f0:f0 + HALF_FLAT] = li
            fresh = tile_in_stream == 0
            hre[0:1, :] = jnp.where(fresh, 0.0, cre_s[...])
            him[0:1, :] = jnp.where(fresh, 0.0, cim_s[...])
            p_re = a32re[...]
            p_im = a32im[...]
            for j in range(N_CHAIN - 1):
                sr = hre[j:j + 1, :]
                si = him[j:j + 1, :]
                hre[j + 1:j + 2, :] = fre[j:j + 1, :] + p_re * sr - p_im * si
                him[j + 1:j + 2, :] = fim[j:j + 1, :] + p_re * si + p_im * sr
            init_re = hre[...]
            init_im = him[...]
        else:
            init_re = ins[1][p]
            init_im = ins[2][p]
        fin_re = []
        fin_im = []
        for h in range(2):
            f0 = h * HALF_FLAT
            ar, ai = a_half(h)
            er, ei = _scan_half(hbuf, h, ar, ai, init_re[:, f0:f0 + HALF_FLAT],
                                init_im[:, f0:f0 + HALF_FLAT], store=True)
            fin_re.append(er)
            fin_im.append(ei)
        end_re = jnp.concatenate(fin_re, axis=1)
        end_im = jnp.concatenate(fin_im, axis=1)
        if is_prompt:
            cre_s[...] = end_re[N_CHAIN - 1:N_CHAIN, :]
            cim_s[...] = end_im[N_CHAIN - 1:N_CHAIN, :]
            stre_ref[p] = end_re[N_CHAIN - 1:N_CHAIN, :]
            stim_ref[p] = end_im[N_CHAIN - 1:N_CHAIN, :]
        else:
            stre_ref[p] = end_re
            stim_ref[p] = end_im

    def finish(p, x, proj):
        hbuf, res, xpbuf = lanes[p][1], lanes[p][2], lanes[p][3]
        u_s = proj[:, :SSM_WIDTH]
        u_p = proj[:, SSM_WIDTH:SSM_WIDTH + POOL_WIDTH]
        gate_s = proj[:, SSM_WIDTH + POOL_WIDTH:SSM_WIDTH + POOL_WIDTH + D_MODEL]
        gate_p = proj[:, SSM_WIDTH + POOL_WIDTH + D_MODEL:]
        ys = []
        for h in range(2):
            c0 = h * N_FLAT
            h_re = hbuf[:, c0:c0 + HALF_FLAT].astype(BF16)
            h_im = hbuf[:, c0 + HALF_FLAT:c0 + N_FLAT].astype(BF16)
            ys.append(jnp.dot(h_re, wcre[h], preferred_element_type=F32)
                      - jnp.dot(h_im, wcim[h], preferred_element_type=F32))
        y = jnp.concatenate(ys, axis=1) + dsk[...] * u_s
        g = _gelu_tanh(y).astype(BF16)
        glu = jnp.dot(g, wglu[...], preferred_element_type=F32)
        o_s = glu[:, :D_MODEL] * _sigmoid(glu[:, D_MODEL:])

        xpbuf[HIST_ROWS:HIST_ROWS + TM, :] = u_p
        tail = u_p[TM - HIST_ROWS:, :]
        row = lax.broadcasted_iota(jnp.int32, (TM, 1), 0)
        if is_prompt:
            pcarry = lanes[p][10]
            first_chain = (lax.broadcasted_iota(jnp.int32, (HIST_ROWS, POOL_WIDTH), 0) % N_CHAIN) == 0
            carried = jnp.where(tile_in_stream == 0, 0.0, pcarry[...])
            xpbuf[0:HIST_ROWS, :] = jnp.where(first_chain, carried, pltpu.roll(tail, 1, 0))
            new_carry = pltpu.roll(tail, HIST_ROWS - (N_CHAIN - 1), 0)
            pcarry[...] = new_carry
            hist_ref[p] = new_carry
            pos1 = tile_in_stream * TM + CHAIN_LEN * (row % N_CHAIN) + row // N_CHAIN + 1
        else:
            xpbuf[0:HIST_ROWS, :] = ins[3][p]
            hist_ref[p] = tail
            pos1 = PAST_LEN + row // N_CHAIN + 1

        o_ps = []
        for gi, w in enumerate(POOL_WINDOWS):
            c0 = gi * POOL_GROUP_IN
            acc = xpbuf[HIST_ROWS:HIST_ROWS + TM, c0:c0 + POOL_GROUP_IN]
            for k in range(1, w):
                acc = acc + xpbuf[HIST_ROWS - N_CHAIN * k:HIST_ROWS - N_CHAIN * k + TM, c0:c0 + POOL_GROUP_IN]
            cnt = jnp.minimum(w, pos1).astype(F32)
            pooled = acc / cnt
            z = (pooled - u_p[:, c0:c0 + POOL_GROUP_IN]).astype(BF16)
            o_ps.append(jnp.dot(z, poolw[gi], preferred_element_type=F32))
        o_p = jnp.concatenate(o_ps, axis=1) * pscale[...]

        merged = (_sigmoid(gate_s) * o_s + _sigmoid(gate_p) * o_p).astype(BF16)
        x2 = x + jnp.dot(merged, wout[...], preferred_element_type=F32)
        for cb in range(n_lane_blocks):
            res[cb] = x2[:, cb * LANES:(cb + 1) * LANES]
        for j in range(N_CHAIN):
            for cb in range(n_lane_blocks):
                x2_ref[p, CHAIN_LEN * j:CHAIN_LEN * (j + 1), cb * LANES:(cb + 1) * LANES] = res[
                    cb, pl.ds(j, CHAIN_LEN, stride=N_CHAIN), :]

    projected = [project(p) for p in range(N_PAR)]
    for p in range(N_PAR):
        recur(p)
    for p in range(N_PAR):
        finish(p, *projected[p])


def _const_spec(shape):
    nd = len(shape)
    return pl.BlockSpec(shape, lambda i, _nd=nd: (0,) * _nd)


def _mixer_weight_specs():
    return [
        _const_spec((1, D_MODEL)),
        _const_spec((D_MODEL, 3 * D_MODEL)),
        _const_spec((2, SSM_WIDTH // 2, N_FLAT)),
        _const_spec((2, HALF_FLAT, SSM_WIDTH // 2)),
        _const_spec((2, HALF_FLAT, SSM_WIDTH // 2)),
        _const_spec((1, SSM_WIDTH)),
        _const_spec((SSM_WIDTH, 2 * D_MODEL)),
        _const_spec((len(POOL_WINDOWS), POOL_GROUP_IN, POOL_GROUP_OUT)),
        _const_spec((1, D_MODEL)),
        _const_spec((D_MODEL, D_MODEL)),
        _const_spec((1, N_FLAT)),
        _const_spec((1, N_FLAT)),
        _const_spec((1, N_FLAT)),
        _const_spec((1, N_FLAT)),
    ]


def _mixer_common_scratch():
    return [
        pltpu.VMEM((D_MODEL // LANES, TM, LANES), F32),
        pltpu.VMEM((TM, 2 * N_FLAT), F32),
        pltpu.VMEM((D_MODEL // LANES, TM, LANES), F32),
        pltpu.VMEM((HIST_ROWS + TM, POOL_WIDTH), F32),
    ]


def _mixer_prompt(x, weights):
    n_streams, seq, _ = x.shape
    assert n_streams % N_PAR == 0 and seq % TM == 0
    tiles_per_stream = seq // TM
    blk = lambda shape: pl.BlockSpec(shape, lambda i: (i // tiles_per_stream, 0, 0))
    row_spec = pl.BlockSpec((N_PAR, TM, D_MODEL), lambda i: (i // tiles_per_stream, i % tiles_per_stream, 0))
    lane_scratch = _mixer_common_scratch() + [
        pltpu.VMEM((N_CHAIN, N_FLAT), F32), pltpu.VMEM((N_CHAIN, N_FLAT), F32),
        pltpu.VMEM((N_CHAIN, N_FLAT), F32), pltpu.VMEM((N_CHAIN, N_FLAT), F32),
        pltpu.VMEM((1, N_FLAT), F32), pltpu.VMEM((1, N_FLAT), F32),
        pltpu.VMEM((HIST_ROWS, POOL_WIDTH), F32),
    ]
    return pl.pallas_call(
        functools.partial(_mixer_kernel, True, tiles_per_stream),
        grid=(n_streams // N_PAR * tiles_per_stream,),
        in_specs=[row_spec] + _mixer_weight_specs(),
        out_specs=[row_spec, blk((N_PAR, 1, N_FLAT)), blk((N_PAR, 1, N_FLAT)),
                   blk((N_PAR, HIST_ROWS, POOL_WIDTH))],
        out_shape=[jax.ShapeDtypeStruct((n_streams, seq, D_MODEL), F32),
                   jax.ShapeDtypeStruct((n_streams, 1, N_FLAT), F32),
                   jax.ShapeDtypeStruct((n_streams, 1, N_FLAT), F32),
                   jax.ShapeDtypeStruct((n_streams, HIST_ROWS, POOL_WIDTH), F32)],
        scratch_shapes=lane_scratch * N_PAR,
        compiler_params=pltpu.CompilerParams(dimension_semantics=("arbitrary",), vmem_limit_bytes=VMEM_LIMIT),
        name="mixer_prompt",
    )(x, *weights)


def _mixer_sample(x, h0_re, h0_im, cache_t, weights):
    n_streams, seq, _ = x.shape
    assert seq == CHAIN_LEN and n_streams % (N_CHAIN * N_PAR) == 0
    n_tiles = n_streams // N_CHAIN
    blk = lambda shape: pl.BlockSpec(shape, lambda i: (i, 0, 0))
    row_spec = blk((N_PAR, TM, D_MODEL))
    st_spec = blk((N_PAR, N_CHAIN, N_FLAT))
    hist_spec = blk((N_PAR, HIST_ROWS, POOL_WIDTH))
    tiles = lambda v: v.reshape((n_tiles, -1) + v.shape[-1:])
    return pl.pallas_call(
        functools.partial(_mixer_kernel, False, 1),
        grid=(n_tiles // N_PAR,),
        in_specs=[row_spec, st_spec, st_spec, hist_spec] + _mixer_weight_specs(),
        out_specs=[row_spec, st_spec, st_spec, hist_spec],
        out_shape=[jax.ShapeDtypeStruct((n_tiles, TM, D_MODEL), F32),
                   jax.ShapeDtypeStruct((n_tiles, N_CHAIN, N_FLAT), F32),
                   jax.ShapeDtypeStruct((n_tiles, N_CHAIN, N_FLAT), F32),
                   jax.ShapeDtypeStruct((n_tiles, HIST_ROWS, POOL_WIDTH), F32)],
        scratch_shapes=_mixer_common_scratch() * N_PAR,
        compiler_params=pltpu.CompilerParams(dimension_semantics=("arbitrary",), vmem_limit_bytes=VMEM_LIMIT),
        name="mixer_sample",
    )(tiles(x), tiles(h0_re), tiles(h0_im), cache_t, *weights)


def _to_chunks(lo, hi):
    n = lo.shape[0] // CHUNK
    half = D_MODEL // 2
    both = jnp.concatenate([lo.reshape(n, CHUNK, half), hi.reshape(n, CHUNK, half)], axis=1)
    return both.astype(BF16)


def _from_chunks(blk):
    n = blk.shape[0]
    half = D_MODEL // 2
    f = blk.astype(F32)
    lo = f[:, :CHUNK, :].reshape(n * CHUNK, half).astype(BF16)
    hi = f[:, CHUNK:, :].reshape(n * CHUNK, half).astype(BF16)
    return lo, hi


def _route_kernel(n_prompt_tiles, x2p_ref, x2s_ref, g_ref, wr_ref, ltri_ref, utri_ref,
                  xs_ref, route_ref, nch_ref):
    i = pl.program_id(0)
    x = jnp.where(i < n_prompt_tiles, x2p_ref[...], x2s_ref[...])
    xn = _rms(x, g_ref[...]).astype(BF16)
    logits = jnp.dot(xn, wr_ref[...], preferred_element_type=F32)
    lane = lax.broadcasted_iota(jnp.int32, (TM, LANES), 1)
    lane_f = lane.astype(F32)
    big = jnp.float32(1 << 20)
    neg = jnp.float32(-jnp.inf)

    gmask = lane < N_EXPERT_GROUPS
    m = jnp.max(jnp.where(gmask, logits, neg), axis=1, keepdims=True)
    grp = jnp.min(jnp.where(gmask & (logits == m), lane_f, big), axis=1, keepdims=True)
    wg = 1.0 / jnp.sum(jnp.where(gmask, jnp.exp(logits - m), 0.0), axis=1, keepdims=True)

    eid = lane - N_EXPERT_GROUPS
    lane_grp = (eid >> 3).astype(F32)
    emask = (eid >= 0) & (eid < N_EXPERTS) & (lane_grp == grp)
    v1 = jnp.max(jnp.where(emask, logits, neg), axis=1, keepdims=True)
    i1 = jnp.min(jnp.where(emask & (logits == v1), lane_f, big), axis=1, keepdims=True)
    emask2 = emask & (lane_f != i1)
    v2 = jnp.max(jnp.where(emask2, logits, neg), axis=1, keepdims=True)
    i2 = jnp.min(jnp.where(emask2 & (logits == v2), lane_f, big), axis=1, keepdims=True)
    e21 = jnp.exp(v2 - v1)
    w1 = wg / (1.0 + e21)
    w2 = wg * e21 / (1.0 + e21)

    a1 = lane_f == (i1 - N_EXPERT_GROUPS)
    a2 = lane_f == (i2 - N_EXPERT_GROUPS)
    a = (a1 | a2).astype(F32)
    before = jnp.dot(ltri_ref[...], a.astype(BF16), preferred_element_type=F32)
    cnt = jnp.sum(a, axis=0, keepdims=True)
    nch = jnp.floor((cnt + (CHUNK - 1)) * (1.0 / CHUNK))
    nch16 = jnp.broadcast_to(nch, (16, LANES))
    start = jnp.dot(nch16.astype(BF16), utri_ref[...], preferred_element_type=F32)
    slot = before + CHUNK * start[0:1, :]
    d1 = jnp.sum(jnp.where(a1, slot, 0.0), axis=1, keepdims=True)
    d2 = jnp.sum(jnp.where(a2, slot, 0.0), axis=1, keepdims=True)

    route = jnp.where(lane == 0, d1, jnp.where(lane == 1, d2, jnp.where(lane == 2, w1,
                      jnp.where(lane == 3, w2, 0.0))))
    route_ref[...] = route
    nch_ref[0] = nch16[0:8, :].astype(jnp.int32)

    dt = jnp.transpose(jnp.where(lane < 2, route, 0.0)).astype(jnp.int32)
    d1row = dt[0:1, :]
    d2row = dt[1:2, :]
    dest = lax.broadcasted_iota(jnp.int32, (CAP_ROWS, TM), 0)
    perm = ((dest == d1row) | (dest == d2row)).astype(F32).astype(BF16)
    half = D_MODEL // 2
    lo = jnp.dot(perm, xn[:, :half], preferred_element_type=F32)
    hi = jnp.dot(perm, xn[:, half:], preferred_element_type=F32)
    xs_ref[...] = _to_chunks(lo, hi)


def _route(x2p, x2s, g_ffn, w_router):
    n_prompt_tiles = x2p.shape[0] // TM
    n_tiles = n_prompt_tiles + x2s.shape[0] // TM
    r = jnp.arange(TM)
    ltri = (r[None, :] < r[:, None]).astype(BF16)
    e = jnp.arange(LANES)
    utri = (e[:, None] < e[None, :]).astype(BF16)
    return pl.pallas_call(
        functools.partial(_route_kernel, n_prompt_tiles),
        grid=(n_tiles,),
        in_specs=[pl.BlockSpec((TM, D_MODEL), lambda i: (jnp.minimum(i, n_prompt_tiles - 1), 0)),
                  pl.BlockSpec((TM, D_MODEL), lambda i: (jnp.maximum(i - n_prompt_tiles, 0), 0)),
                  _const_spec((1, D_MODEL)), _const_spec((D_MODEL, LANES)),
                  _const_spec((TM, TM)), _const_spec((LANES, LANES))],
        out_specs=[pl.BlockSpec((CAP_CHUNKS, 2 * CHUNK, D_MODEL // 2), lambda i: (i, 0, 0)),
                   pl.BlockSpec((TM, LANES), lambda i: (i, 0)),
                   pl.BlockSpec((1, 8, LANES), lambda i: (i, 0, 0))],
        out_shape=[jax.ShapeDtypeStruct((n_tiles * CAP_CHUNKS, 2 * CHUNK, D_MODEL // 2), BF16),
                   jax.ShapeDtypeStruct((n_tiles * TM, LANES), F32),
                   jax.ShapeDtypeStruct((n_tiles, 8, LANES), jnp.int32)],
        compiler_params=pltpu.CompilerParams(dimension_semantics=("arbitrary",), vmem_limit_bytes=VMEM_LIMIT),
        name="route_sort",
    )(x2p, x2s, g_ffn, w_router, ltri, utri)


def _spare_chunk(k):
    return (1 + k // N_SPARE_SLOTS) * CAP_CHUNKS + MAX_USED_CHUNKS + k % N_SPARE_SLOTS


def _chunk_tables(nch, n_expert_tiles):
    n_tiles = nch.shape[0]
    i32 = jnp.int32
    start = jnp.cumsum(nch, axis=1) - nch
    off = jnp.cumsum(nch, axis=0) - nch
    per_expert = jnp.sum(nch, axis=0)
    tiles_e = (per_expert + TILE_CHUNKS - 1) // TILE_CHUNKS
    cum_tiles = jnp.cumsum(tiles_e)
    first_tile = cum_tiles - tiles_e
    n_active = cum_tiles[-1].astype(i32)
    t = jnp.arange(n_expert_tiles, dtype=i32)
    te = jnp.minimum(jnp.sum((t[:, None] >= cum_tiles[None, :]).astype(i32), axis=1), N_EXPERTS - 1)
    onehot = te[:, None] == jnp.arange(N_EXPERTS, dtype=i32)[None, :]
    pick = lambda tab: jnp.sum(jnp.where(onehot[:, :, None], tab.T[None, :, :], 0), axis=1)
    off_t, nch_t, start_t = pick(off), pick(nch), pick(start)
    k = t - jnp.sum(jnp.where(onehot, first_tile[None, :], 0), axis=1)
    q = (TILE_CHUNKS * k)[:, None] + jnp.arange(TILE_CHUNKS, dtype=i32)[None, :]
    in_run = (off_t[:, None, :] <= q[:, :, None]) & (q[:, :, None] < (off_t + nch_t)[:, None, :])
    run_src = (jnp.arange(n_tiles, dtype=i32) * CAP_CHUNKS)[None, :] + start_t - off_t
    src = jnp.sum(jnp.where(in_run, run_src[:, None, :], 0), axis=-1) + q
    valid = jnp.any(in_run, axis=-1) & (t < n_active)[:, None]
    src = jnp.where(valid, src, ZERO_CHUNK).astype(i32)
    spare = _spare_chunk((t % 2)[:, None] * TILE_CHUNKS + jnp.arange(TILE_CHUNKS, dtype=i32)[None, :])
    dst = jnp.where(valid, src, spare).astype(i32)
    dst = jnp.concatenate([spare[1:2], dst], axis=0)
    return first_tile.astype(i32), tiles_e.astype(i32), src.reshape(-1), dst.reshape(-1), n_active.reshape(1)


def _expert_kernel(first_ref, ntile_ref, src_ref, dst_ref, nact_ref, xs_hbm, wg_ref, wu_ref, wd_ref, ys_hbm,
                   xbuf0, xbuf1, obuf0, obuf1, gsem, ssem, wg16, wu16, wd16):
    e = pl.program_id(0)
    n_active = nact_ref[0]
    xbufs = (xbuf0, xbuf1)
    obufs = (obuf0, obuf1)

    def gather_copy(tile, slot, c):
        return pltpu.make_async_copy(xs_hbm.at[src_ref[tile * TILE_CHUNKS + c]], xbufs[slot].at[c],
                                     gsem.at[slot])

    def scatter_copy(tile, slot, c):
        return pltpu.make_async_copy(obufs[slot].at[c], ys_hbm.at[dst_ref[(tile + 1) * TILE_CHUNKS + c]],
                                     ssem.at[slot])

    def start_all(copy, tile, slot):
        for c in range(TILE_CHUNKS):
            copy(tile, slot, c).start()

    def wait_all(copy, tile, slot):
        for c in range(TILE_CHUNKS):
            copy(tile, slot, c).wait()

    @pl.when(e == 0)
    def _():
        start_all(gather_copy, 0, 0)
        obuf1[...] = jnp.zeros_like(obuf1)

    wg16[...] = wg_ref[0].astype(BF16)
    wu16[...] = wu_ref[0].astype(BF16)
    wd16[...] = wd_ref[0].astype(BF16)

    def do_tile(tile, slot):
        wait_all(gather_copy, tile, slot)
        lo, hi = _from_chunks(xbufs[slot][...])
        x = jnp.concatenate([lo, hi], axis=1)
        gate = jnp.dot(x, wg16[...], preferred_element_type=F32)
        start_all(scatter_copy, tile - 1, 1 - slot)
        up = jnp.dot(x, wu16[...], preferred_element_type=F32)
        hmid = (gate * _sigmoid(gate) * up).astype(BF16)
        start_all(gather_copy, tile + 1, 1 - slot)
        y = jnp.dot(hmid, wd16[...], preferred_element_type=F32)
        half = D_MODEL // 2
        wait_all(scatter_copy, tile - 1, 1 - slot)
        obufs[slot][...] = _to_chunks(y[:, :half], y[:, half:])

    def tile_body(k, carry):
        tile = first_ref[e] + k
        for slot in range(2):
            @pl.when(tile % 2 == slot)
            def _():
                do_tile(tile, slot)
        return carry

    lax.fori_loop(0, ntile_ref[e], tile_body, 0)

    @pl.when(e == pl.num_programs(0) - 1)
    def _():
        for slot in range(2):
            @pl.when(n_active % 2 == slot)
            def _():
                start_all(scatter_copy, n_active - 1, 1 - slot)
                wait_all(scatter_copy, n_active - 1, 1 - slot)
                wait_all(gather_copy, n_active, slot)


def _experts(first_tile, tiles_e, src, dst, n_active, xs, w_gate, w_up, w_down):
    chunk_shape = (2 * CHUNK, D_MODEL // 2)
    assert _spare_chunk(2 * TILE_CHUNKS - 1) < xs.shape[0]
    tile_buf = pltpu.VMEM((TILE_CHUNKS,) + chunk_shape, BF16)
    grid_spec = pltpu.PrefetchScalarGridSpec(
        num_scalar_prefetch=5,
        grid=(N_EXPERTS,),
        in_specs=[pl.BlockSpec(memory_space=pl.ANY),
                  pl.BlockSpec((1, D_MODEL, D_EXPERT), lambda e, *_: (e, 0, 0)),
                  pl.BlockSpec((1, D_MODEL, D_EXPERT), lambda e, *_: (e, 0, 0)),
                  pl.BlockSpec((1, D_EXPERT, D_MODEL), lambda e, *_: (e, 0, 0))],
        out_specs=pl.BlockSpec(memory_space=pl.ANY),
        scratch_shapes=[tile_buf, tile_buf, tile_buf, tile_buf,
                        pltpu.SemaphoreType.DMA((2,)),
                        pltpu.SemaphoreType.DMA((2,)),
                        pltpu.VMEM((D_MODEL, D_EXPERT), BF16),
                        pltpu.VMEM((D_MODEL, D_EXPERT), BF16),
                        pltpu.VMEM((D_EXPERT, D_MODEL), BF16)],
    )
    return pl.pallas_call(
        _expert_kernel,
        grid_spec=grid_spec,
        out_shape=jax.ShapeDtypeStruct(xs.shape, xs.dtype),
        input_output_aliases={5: 0},
        compiler_params=pltpu.CompilerParams(dimension_semantics=("arbitrary",), vmem_limit_bytes=VMEM_LIMIT),
        name="expert_ffn",
    )(first_tile, tiles_e, src, dst, n_active, xs, w_gate, w_up, w_down)


def _combine_kernel(n_prompt_tiles, x2p_ref, x2s_ref, route_ref, gfin_ref, ys_ref, outp_ref, outs_ref):
    i = pl.program_id(0)
    lo, hi = _from_chunks(ys_ref[...])
    route = route_ref[...]
    d1 = route[:, 0:1].astype(jnp.int32)
    d2 = route[:, 1:2].astype(jnp.int32)
    w1 = route[:, 2:3]
    w2 = route[:, 3:4]
    dest = lax.broadcasted_iota(jnp.int32, (TM, CAP_ROWS), 1)
    sel1 = (dest == d1).astype(F32).astype(BF16)
    sel2 = (dest == d2).astype(F32).astype(BF16)
    m1 = jnp.concatenate([jnp.dot(sel1, lo, preferred_element_type=F32),
                          jnp.dot(sel1, hi, preferred_element_type=F32)], axis=1)
    m2 = jnp.concatenate([jnp.dot(sel2, lo, preferred_element_type=F32),
                          jnp.dot(sel2, hi, preferred_element_type=F32)], axis=1)
    x = jnp.where(i < n_prompt_tiles, x2p_ref[...], x2s_ref[...])
    out = _rms(x + (w1 * m1 + w2 * m2), gfin_ref[...])

    @pl.when(i < n_prompt_tiles)
    def _():
        outp_ref[...] = out

    @pl.when(i >= n_prompt_tiles)
    def _():
        outs_ref[...] = out


def _combine(x2p, x2s, route, g_final, ys):
    n_prompt_tiles = x2p.shape[0] // TM
    n_tiles = n_prompt_tiles + x2s.shape[0] // TM
    p_spec = pl.BlockSpec((TM, D_MODEL), lambda i: (jnp.minimum(i, n_prompt_tiles - 1), 0))
    s_spec = pl.BlockSpec((TM, D_MODEL), lambda i: (jnp.maximum(i - n_prompt_tiles, 0), 0))
    return pl.pallas_call(
        functools.partial(_combine_kernel, n_prompt_tiles),
        grid=(n_tiles,),
        in_specs=[p_spec, s_spec,
                  pl.BlockSpec((TM, LANES), lambda i: (i, 0)),
                  _const_spec((1, D_MODEL)),
                  pl.BlockSpec((CAP_CHUNKS, 2 * CHUNK, D_MODEL // 2), lambda i: (i, 0, 0))],
        out_specs=[p_spec, s_spec],
        out_shape=[jax.ShapeDtypeStruct(x2p.shape, F32), jax.ShapeDtypeStruct(x2s.shape, F32)],
        compiler_params=pltpu.CompilerParams(dimension_semantics=("arbitrary",), vmem_limit_bytes=VMEM_LIMIT),
        name="combine_norm",
    )(x2p, x2s, route, g_final, ys)


def _block_diag_in(bb):
    t = bb.reshape(2, 16, SSM_STATE, SSM_GROUP).transpose(0, 1, 3, 2)
    eye = jnp.eye(16, dtype=bb.dtype)
    blk = t[:, :, :, None, :] * eye[None, :, None, :, None]
    return blk.reshape(2, 16 * SSM_GROUP, 16 * SSM_STATE)


def _block_diag_out(c):
    t = c.reshape(2, 16, SSM_GROUP, SSM_STATE).transpose(0, 1, 3, 2)
    eye = jnp.eye(16, dtype=c.dtype)
    blk = t[:, :, :, None, :] * eye[None, :, None, :, None]
    return blk.reshape(2, 16 * SSM_STATE, 16 * SSM_GROUP)


def kernel(x_prompt, x_sample, state_ssm_re, state_ssm_im, cache_pool, g_mix, w_in, ssm_a_re, ssm_a_im,
           ssm_log_dt, ssm_b_re, ssm_b_im, ssm_c_re, ssm_c_im, ssm_d, w_glu_a, w_glu_b, pool_w, pool_scale,
           w_out, g_ffn, w_router_group, w_router_expert, w_exp_gate, w_exp_up, w_exp_down, g_final):
    li = 0
    n_pb, seq_p, _ = x_prompt.shape
    n_sb, seq_s, _ = x_sample.shape

    a_re, a_im, a32_re, a32_im, bb_re, bb_im = _discretise(
        ssm_a_re[li], ssm_a_im[li], ssm_log_dt[li], ssm_b_re[li], ssm_b_im[li])
    row = lambda v: v.reshape(1, N_FLAT)
    wb = jnp.concatenate([_block_diag_in(bb_re), _block_diag_in(bb_im)], axis=2).astype(BF16)
    weights = [
        g_mix[li].reshape(1, D_MODEL),
        w_in[li].astype(BF16),
        wb,
        _block_diag_out(ssm_c_re[li]).astype(BF16),
        _block_diag_out(ssm_c_im[li]).astype(BF16),
        ssm_d[li].reshape(1, SSM_WIDTH),
        jnp.concatenate([w_glu_a[li], w_glu_b[li]], axis=1).astype(BF16),
        pool_w[li].astype(BF16),
        pool_scale[li].reshape(1, D_MODEL),
        w_out[li].astype(BF16),
        row(a_re), row(a_im), row(a32_re), row(a32_im),
    ]

    x2p, stp_re, stp_im, histp = _mixer_prompt(x_prompt, weights)
    x2p = x2p.reshape(n_pb * seq_p, D_MODEL)

    n_stiles = n_sb // N_CHAIN
    cache16 = jnp.pad(cache_pool[li], ((0, 0), (1, 0), (0, 0)))
    cache_t = cache16.reshape(n_stiles, N_CHAIN, 16, POOL_WIDTH).transpose(0, 2, 1, 3).reshape(
        n_stiles, HIST_ROWS, POOL_WIDTH)
    x2s, sts_re, sts_im, hists = _mixer_sample(
        x_sample, state_ssm_re[li].reshape(n_sb, N_FLAT), state_ssm_im[li].reshape(n_sb, N_FLAT),
        cache_t, weights)
    x2s = x2s.reshape(n_sb * seq_s, D_MODEL)

    w_router = jnp.concatenate(
        [w_router_group[li], w_router_expert[li].reshape(D_MODEL, N_EXPERTS),
         jnp.zeros((D_MODEL, LANES - N_EXPERT_GROUPS - N_EXPERTS), F32)], axis=1).astype(BF16)
    xs, route, nch = _route(x2p, x2s, g_ffn[li].reshape(1, D_MODEL), w_router)

    n_tiles = (x2p.shape[0] + x2s.shape[0]) // TM
    max_chunks = n_tiles * (2 * TM // CHUNK + N_EXPERTS)
    n_expert_tiles = max_chunks // TILE_CHUNKS + N_EXPERTS
    first_tile, tiles_e, src, dst, n_active = _chunk_tables(nch[:, 0, :N_EXPERTS], n_expert_tiles)
    ys = _experts(first_tile, tiles_e, src, dst, n_active, xs, w_exp_gate[li], w_exp_up[li], w_exp_down[li])
    yp, ysm = _combine(x2p, x2s, route, g_final.reshape(1, D_MODEL), ys)

    sd = state_ssm_re.dtype
    cd = cache_pool.dtype
    y_prompt = yp.reshape(n_pb, seq_p, D_MODEL)
    y_sample = ysm.reshape(n_sb, seq_s, D_MODEL)
    re_p = stp_re.reshape(1, n_pb, SSM_GROUPS, SSM_STATE).astype(sd)
    im_p = stp_im.reshape(1, n_pb, SSM_GROUPS, SSM_STATE).astype(sd)
    hist_p = histp[:, ::N_CHAIN, :][:, 1:, :].reshape(1, n_pb, POOL_HIST, POOL_WIDTH).astype(cd)
    re_s = sts_re.reshape(1, n_sb, SSM_GROUPS, SSM_STATE).astype(sd)
    im_s = sts_im.reshape(1, n_sb, SSM_GROUPS, SSM_STATE).astype(sd)
    hist_s = hists.reshape(n_stiles, 16, N_CHAIN, POOL_WIDTH).transpose(0, 2, 1, 3).reshape(
        n_sb, 16, POOL_WIDTH)[:, 1:, :].reshape(1, n_sb, POOL_HIST, POOL_WIDTH).astype(cd)
    return (y_prompt, y_sample, re_p, im_p, hist_p, re_s, im_s, hist_s)
```

```python
import functools
import math

import jax
import jax.numpy as jnp
from jax import lax
from jax.experimental import pallas as pl
from jax.experimental.pallas import tpu as pltpu

F32 = jnp.float32
BF16 = jnp.bfloat16

D_MODEL = 1024
SSM_WIDTH = 512
SSM_GROUPS = 32
SSM_GROUP = 16
SSM_STATE = 64
N_FLAT = SSM_GROUPS * SSM_STATE
HALF_FLAT = N_FLAT // 2
POOL_WIDTH = 512
POOL_WINDOWS = (2, 4, 8, 16)
POOL_GROUP_IN = 128
POOL_GROUP_OUT = 256
POOL_HIST = 15
N_EXPERTS = 32
EXPERTS_PER_GROUP = 8
N_EXPERT_GROUPS = 4
D_EXPERT = 512
EPS = 1e-6
PAST_LEN = 1024

TM = 256
N_CHAIN = 8
CHAIN_LEN = TM // N_CHAIN
HIST_ROWS = 16 * N_CHAIN

CHUNK = 8
TILE_CHUNKS = TM // CHUNK
CAP_CHUNKS = 96
CAP_ROWS = CAP_CHUNKS * CHUNK
MAX_USED_CHUNKS = 2 * TM // CHUNK + N_EXPERTS * (CHUNK - 1) // CHUNK
N_SPARE_SLOTS = CAP_CHUNKS - MAX_USED_CHUNKS - 1
ZERO_CHUNK = CAP_CHUNKS - 1
LANES = 128

VMEM_LIMIT = 52 * 1024 * 1024


def _rms(x, g):
    r = lax.rsqrt(jnp.mean(x * x, axis=-1, keepdims=True) + EPS)
    return x * r * g


def _sigmoid(x):
    return 0.5 * jnp.tanh(0.5 * x) + 0.5


def _gelu_tanh(x):
    c = math.sqrt(2.0 / math.pi)
    return x * (0.5 * (1.0 + jnp.tanh(c * (x + 0.044715 * (x * x * x)))))


def _disc_kernel(lre_ref, lim_ref, ldt_ref, bre_ref, bim_ref,
                 are_ref, aim_ref, a32re_ref, a32im_ref, bbre_ref, bbim_ref):
    lam_re = jnp.minimum(lre_ref[...], -1e-4)
    lam_im = lim_ref[...]
    dt = jnp.exp(ldt_ref[...])
    mag = jnp.exp(lam_re * dt)
    ang = lam_im * dt
    a_re = mag * jnp.cos(ang)
    a_im = mag * jnp.sin(ang)
    num_re = a_re - 1.0
    num_im = a_im
    den = lam_re * lam_re + lam_im * lam_im
    k_re = (num_re * lam_re + num_im * lam_im) / den
    k_im = (num_im * lam_re - num_re * lam_im) / den
    br = bre_ref[...]
    bi = bim_ref[...]
    bbre_ref[...] = k_re * br - k_im * bi
    bbim_ref[...] = k_re * bi + k_im * br
    are_ref[...] = a_re
    aim_ref[...] = a_im
    pr, pi = a_re, a_im
    for _ in range(int(math.log2(CHAIN_LEN))):
        pr, pi = pr * pr - pi * pi, 2.0 * pr * pi
    a32re_ref[...] = pr
    a32im_ref[...] = pi


def _discretise(a_re, a_im, log_dt, b_re, b_im):
    col = lambda v: v.reshape(N_FLAT, 1)
    ldt = jnp.broadcast_to(log_dt[:, None], (SSM_GROUPS, SSM_STATE))
    outs = pl.pallas_call(
        _disc_kernel,
        out_shape=[jax.ShapeDtypeStruct((N_FLAT, 1), F32)] * 4
        + [jax.ShapeDtypeStruct((N_FLAT, SSM_GROUP), F32)] * 2,
        name="s5_discretise",
    )(col(a_re), col(a_im), col(ldt), b_re.reshape(N_FLAT, SSM_GROUP), b_im.reshape(N_FLAT, SSM_GROUP))
    return outs


def _scan_half(hbuf, h, ar, ai, init_re, init_im, store):
    cre = h * N_FLAT
    cim = cre + HALF_FLAT
    hr, hi = init_re, init_im
    for t in range(CHAIN_LEN):
        rows = pl.ds(N_CHAIN * t, N_CHAIN)
        br = hbuf[rows, cre:cre + HALF_FLAT]
        bi = hbuf[rows, cim:cim + HALF_FLAT]
        nr = ar * hr - ai * hi + br
        ni = ar * hi + ai * hr + bi
        if store:
            hbuf[rows, cre:cre + HALF_FLAT] = nr
            hbuf[rows, cim:cim + HALF_FLAT] = ni
        hr, hi = nr, ni
    return hr, hi


N_MIXER_WEIGHTS = 14
N_PAR = 2


def _mixer_kernel(is_prompt, tiles_per_stream, *refs):
    n_in = 1 if is_prompt else 4
    ins = refs[:n_in]
    (gmix, win, wb, wcre, wcim, dsk, wglu, poolw, pscale, wout, are, aim, a32re, a32im) = refs[
        n_in:n_in + N_MIXER_WEIGHTS]
    x2_ref, stre_ref, stim_ref, hist_ref = refs[n_in + N_MIXER_WEIGHTS:n_in + N_MIXER_WEIGHTS + 4]
    scratch = refs[n_in + N_MIXER_WEIGHTS + 4:]
    per = len(scratch) // N_PAR
    lanes = [scratch[p * per:(p + 1) * per] for p in range(N_PAR)]
    x_ref = ins[0]

    tile_in_stream = pl.program_id(0) % tiles_per_stream
    n_lane_blocks = D_MODEL // LANES
    half_w = SSM_WIDTH // 2
    ar_full = are[...]
    ai_full = aim[...]

    if is_prompt:
        @pl.when(pl.program_id(0) == 0)
        def _():
            for p in range(N_PAR):
                cre_s, cim_s, pcarry = lanes[p][8], lanes[p][9], lanes[p][10]
                cre_s[...] = jnp.zeros_like(cre_s)
                cim_s[...] = jnp.zeros_like(cim_s)
                pcarry[...] = jnp.zeros_like(pcarry)

    def a_half(h):
        f0 = h * HALF_FLAT
        return (jnp.broadcast_to(ar_full[:, f0:f0 + HALF_FLAT], (N_CHAIN, HALF_FLAT)),
                jnp.broadcast_to(ai_full[:, f0:f0 + HALF_FLAT], (N_CHAIN, HALF_FLAT)))

    def project(p):
        xperm, hbuf = lanes[p][0], lanes[p][1]
        for j in range(N_CHAIN):
            for cb in range(n_lane_blocks):
                xperm[cb, pl.ds(j, CHAIN_LEN, stride=N_CHAIN), :] = x_ref[
                    p, CHAIN_LEN * j:CHAIN_LEN * (j + 1), cb * LANES:(cb + 1) * LANES]
        x = jnp.concatenate([xperm[cb] for cb in range(n_lane_blocks)], axis=1)
        xn = _rms(x, gmix[...]).astype(BF16)
        proj = jnp.dot(xn, win[...], preferred_element_type=F32)
        ub = proj[:, :SSM_WIDTH].astype(BF16)
        for h in range(2):
            hbuf[:, h * N_FLAT:(h + 1) * N_FLAT] = jnp.dot(
                ub[:, h * half_w:(h + 1) * half_w], wb[h], preferred_element_type=F32)
        return x, proj

    def recur(p):
        hbuf = lanes[p][1]
        if is_prompt:
            fre, fim, hre, him, cre_s, cim_s = lanes[p][4:10]
            zeros = jnp.zeros((N_CHAIN, HALF_FLAT), F32)
            for h in range(2):
                f0 = h * HALF_FLAT
                ar, ai = a_half(h)
                lr, li = _scan_half(hbuf, h, ar, ai, zeros, zeros, store=False)
                fre[:, f0:f0 + HALF_FLAT] = lr
                fim[:, f0:f0 + HALF_FLAT] = li
            fresh = tile_in_stream == 0
            hre[0:1, :] = jnp.where(fresh, 0.0, cre_s[...])
            him[0:1, :] = jnp.where(fresh, 0.0, cim_s[...])
            p_re = a32re[...]
            p_im = a32im[...]
            for j in range(N_CHAIN - 1):
                sr = hre[j:j + 1, :]
                si = him[j:j + 1, :]
                hre[j + 1:j + 2, :] = fre[j:j + 1, :] + p_re * sr - p_im * si
                him[j + 1:j + 2, :] = fim[j:j + 1, :] + p_re * si + p_im * sr
            init_re = hre[...]
            init_im = him[...]
        else:
            init_re = ins[1][p]
            init_im = ins[2][p]
        fin_re = []
        fin_im = []
        for h in range(2):
            f0 = h * HALF_FLAT
            ar, ai = a_half(h)
            er, ei = _scan_half(hbuf, h, ar, ai, init_re[:, f0:f0 + HALF_FLAT],
                                init_im[:, f0:f0 + HALF_FLAT], store=True)
            fin_re.append(er)
            fin_im.append(ei)
        end_re = jnp.concatenate(fin_re, axis=1)
        end_im = jnp.concatenate(fin_im, axis=1)
        if is_prompt:
            cre_s[...] = end_re[N_CHAIN - 1:N_CHAIN, :]
            cim_s[...] = end_im[N_CHAIN - 1:N_CHAIN, :]
            stre_ref[p] = end_re[N_CHAIN - 1:N_CHAIN, :]
            stim_ref[p] = end_im[N_CHAIN - 1:N_CHAIN, :]
        else:
            stre_ref[p] = end_re
            stim_ref[p] = end_im

    def finish(p, x, proj):
        hbuf, res, xpbuf = lanes[p][1], lanes[p][2], lanes[p][3]
        u_s = proj[:, :SSM_WIDTH]
        u_p = proj[:, SSM_WIDTH:SSM_WIDTH + POOL_WIDTH]
        gate_s = proj[:, SSM_WIDTH + POOL_WIDTH:SSM_WIDTH + POOL_WIDTH + D_MODEL]
        gate_p = proj[:, SSM_WIDTH + POOL_WIDTH + D_MODEL:]
        ys = []
        for h in range(2):
            c0 = h * N_FLAT
            h_re = hbuf[:, c0:c0 + HALF_FLAT].astype(BF16)
            h_im = hbuf[:, c0 + HALF_FLAT:c0 + N_FLAT].astype(BF16)
            ys.append(jnp.dot(h_re, wcre[h], preferred_element_type=F32)
                      - jnp.dot(h_im, wcim[h], preferred_element_type=F32))
        y = jnp.concatenate(ys, axis=1) + dsk[...] * u_s
        g = _gelu_tanh(y).astype(BF16)
        glu = jnp.dot(g, wglu[...], preferred_element_type=F32)
        o_s = glu[:, :D_MODEL] * _sigmoid(glu[:, D_MODEL:])

        xpbuf[HIST_ROWS:HIST_ROWS + TM, :] = u_p
        tail = u_p[TM - HIST_ROWS:, :]
        row = lax.broadcasted_iota(jnp.int32, (TM, 1), 0)
        if is_prompt:
            pcarry = lanes[p][10]
            first_chain = (lax.broadcasted_iota(jnp.int32, (HIST_ROWS, POOL_WIDTH), 0) % N_CHAIN) == 0
            carried = jnp.where(tile_in_stream == 0, 0.0, pcarry[...])
            xpbuf[0:HIST_ROWS, :] = jnp.where(first_chain, carried, pltpu.roll(tail, 1, 0))
            new_carry = pltpu.roll(tail, HIST_ROWS - (N_CHAIN - 1), 0)
            pcarry[...] = new_carry
            hist_ref[p] = new_carry
            pos1 = tile_in_stream * TM + CHAIN_LEN * (row % N_CHAIN) + row // N_CHAIN + 1
        else:
            xpbuf[0:HIST_ROWS, :] = ins[3][p]
            hist_ref[p] = tail
            pos1 = PAST_LEN + row // N_CHAIN + 1

        o_ps = []
        for gi, w in enumerate(POOL_WINDOWS):
            c0 = gi * POOL_GROUP_IN
            acc = xpbuf[HIST_ROWS:HIST_ROWS + TM, c0:c0 + POOL_GROUP_IN]
            for k in range(1, w):
                acc = acc + xpbuf[HIST_ROWS - N_CHAIN * k:HIST_ROWS - N_CHAIN * k + TM, c0:c0 + POOL_GROUP_IN]
            cnt = jnp.minimum(w, pos1).astype(F32)
            pooled = acc / cnt
            z = (pooled - u_p[:, c0:c0 + POOL_GROUP_IN]).astype(BF16)
            o_ps.append(jnp.dot(z, poolw[gi], preferred_element_type=F32))
        o_p = jnp.concatenate(o_ps, axis=1) * pscale[...]

        merged = (_sigmoid(gate_s) * o_s + _sigmoid(gate_p) * o_p).astype(BF16)
        x2 = x + jnp.dot(merged, wout[...], preferred_element_type=F32)
        for cb in range(n_lane_blocks):
            res[cb] = x2[:, cb * LANES:(cb + 1) * LANES]
        for j in range(N_CHAIN):
            for cb in range(n_lane_blocks):
                x2_ref[p, CHAIN_LEN * j:CHAIN_LEN * (j + 1), cb * LANES:(cb + 1) * LANES] = res[
                    cb, pl.ds(j, CHAIN_LEN, stride=N_CHAIN), :]

    projected = [project(p) for p in range(N_PAR)]
    for p in range(N_PAR):
        recur(p)
    for p in range(N_PAR):
        finish(p, *projected[p])


def _const_spec(shape):
    nd = len(shape)
    return pl.BlockSpec(shape, lambda i, _nd=nd: (0,) * _nd)


def _mixer_weight_specs():
    return [
        _const_spec((1, D_MODEL)),
        _const_spec((D_MODEL, 3 * D_MODEL)),
        _const_spec((2, SSM_WIDTH // 2, N_FLAT)),
        _const_spec((2, HALF_FLAT, SSM_WIDTH // 2)),
        _const_spec((2, HALF_FLAT, SSM_WIDTH // 2)),
        _const_spec((1, SSM_WIDTH)),
        _const_spec((SSM_WIDTH, 2 * D_MODEL)),
        _const_spec((len(POOL_WINDOWS), POOL_GROUP_IN, POOL_GROUP_OUT)),
        _const_spec((1, D_MODEL)),
        _const_spec((D_MODEL, D_MODEL)),
        _const_spec((1, N_FLAT)),
        _const_spec((1, N_FLAT)),
        _const_spec((1, N_FLAT)),
        _const_spec((1, N_FLAT)),
    ]


def _mixer_common_scratch():
    return [
        pltpu.VMEM((D_MODEL // LANES, TM, LANES), F32),
        pltpu.VMEM((TM, 2 * N_FLAT), F32),
        pltpu.VMEM((D_MODEL // LANES, TM, LANES), F32),
        pltpu.VMEM((HIST_ROWS + TM, POOL_WIDTH), F32),
    ]


def _mixer_prompt(x, weights):
    n_streams, seq, _ = x.shape
    assert n_streams % N_PAR == 0 and seq % TM == 0
    tiles_per_stream = seq // TM
    blk = lambda shape: pl.BlockSpec(shape, lambda i: (i // tiles_per_stream, 0, 0))
    row_spec = pl.BlockSpec((N_PAR, TM, D_MODEL), lambda i: (i // tiles_per_stream, i % tiles_per_stream, 0))
    lane_scratch = _mixer_common_scratch() + [
        pltpu.VMEM((N_CHAIN, N_FLAT), F32), pltpu.VMEM((N_CHAIN, N_FLAT), F32),
        pltpu.VMEM((N_CHAIN, N_FLAT), F32), pltpu.VMEM((N_CHAIN, N_FLAT), F32),
        pltpu.VMEM((1, N_FLAT), F32), pltpu.VMEM((1, N_FLAT), F32),
        pltpu.VMEM((HIST_ROWS, POOL_WIDTH), F32),
    ]
    return pl.pallas_call(
        functools.partial(_mixer_kernel, True, tiles_per_stream),
        grid=(n_streams // N_PAR * tiles_per_stream,),
        in_specs=[row_spec] + _mixer_weight_specs(),
        out_specs=[row_spec, blk((N_PAR, 1, N_FLAT)), blk((N_PAR, 1, N_FLAT)),
                   blk((N_PAR, HIST_ROWS, POOL_WIDTH))],
        out_shape=[jax.ShapeDtypeStruct((n_streams, seq, D_MODEL), F32),
                   jax.ShapeDtypeStruct((n_streams, 1, N_FLAT), F32),
                   jax.ShapeDtypeStruct((n_streams, 1, N_FLAT), F32),
                   jax.ShapeDtypeStruct((n_streams, HIST_ROWS, POOL_WIDTH), F32)],
        scratch_shapes=lane_scratch * N_PAR,
        compiler_params=pltpu.CompilerParams(dimension_semantics=("arbitrary",), vmem_limit_bytes=VMEM_LIMIT),
        name="mixer_prompt",
    )(x, *weights)


def _mixer_sample(x, h0_re, h0_im, cache_t, weights):
    n_streams, seq, _ = x.shape
    assert seq == CHAIN_LEN and n_streams % (N_CHAIN * N_PAR) == 0
    n_tiles = n_streams // N_CHAIN
    blk = lambda shape: pl.BlockSpec(shape, lambda i: (i, 0, 0))
    row_spec = blk((N_PAR, TM, D_MODEL))
    st_spec = blk((N_PAR, N_CHAIN, N_FLAT))
    hist_spec = blk((N_PAR, HIST_ROWS, POOL_WIDTH))
    tiles = lambda v: v.reshape((n_tiles, -1) + v.shape[-1:])
    return pl.pallas_call(
        functools.partial(_mixer_kernel, False, 1),
        grid=(n_tiles // N_PAR,),
        in_specs=[row_spec, st_spec, st_spec, hist_spec] + _mixer_weight_specs(),
        out_specs=[row_spec, st_spec, st_spec, hist_spec],
        out_shape=[jax.ShapeDtypeStruct((n_tiles, TM, D_MODEL), F32),
                   jax.ShapeDtypeStruct((n_tiles, N_CHAIN, N_FLAT), F32),
                   jax.ShapeDtypeStruct((n_tiles, N_CHAIN, N_FLAT), F32),
                   jax.ShapeDtypeStruct((n_tiles, HIST_ROWS, POOL_WIDTH), F32)],
        scratch_shapes=_mixer_common_scratch() * N_PAR,
        compiler_params=pltpu.CompilerParams(dimension_semantics=("arbitrary",), vmem_limit_bytes=VMEM_LIMIT),
        name="mixer_sample",
    )(tiles(x), tiles(h0_re), tiles(h0_im), cache_t, *weights)


def _to_chunks(lo, hi):
    n = lo.shape[0] // CHUNK
    half = D_MODEL // 2
    both = jnp.concatenate([lo.reshape(n, CHUNK, half), hi.reshape(n, CHUNK, half)], axis=1)
    return both.astype(BF16)


def _from_chunks(blk):
    n = blk.shape[0]
    half = D_MODEL // 2
    f = blk.astype(F32)
    lo = f[:, :CHUNK, :].reshape(n * CHUNK, half).astype(BF16)
    hi = f[:, CHUNK:, :].reshape(n * CHUNK, half).astype(BF16)
    return lo, hi


def _route_kernel(n_prompt_tiles, x2p_ref, x2s_ref, g_ref, wr_ref, ltri_ref, utri_ref,
                  xs_ref, route_ref, nch_ref):
    i = pl.program_id(0)
    x = jnp.where(i < n_prompt_tiles, x2p_ref[...], x2s_ref[...])
    xn = _rms(x, g_ref[...]).astype(BF16)
    logits = jnp.dot(xn, wr_ref[...], preferred_element_type=F32)
    lane = lax.broadcasted_iota(jnp.int32, (TM, LANES), 1)
    lane_f = lane.astype(F32)
    big = jnp.float32(1 << 20)
    neg = jnp.float32(-jnp.inf)

    gmask = lane < N_EXPERT_GROUPS
    m = jnp.max(jnp.where(gmask, logits, neg), axis=1, keepdims=True)
    grp = jnp.min(jnp.where(gmask & (logits == m), lane_f, big), axis=1, keepdims=True)
    wg = 1.0 / jnp.sum(jnp.where(gmask, jnp.exp(logits - m), 0.0), axis=1, keepdims=True)

    eid = lane - N_EXPERT_GROUPS
    lane_grp = (eid >> 3).astype(F32)
    emask = (eid >= 0) & (eid < N_EXPERTS) & (lane_grp == grp)
    v1 = jnp.max(jnp.where(emask, logits, neg), axis=1, keepdims=True)
    i1 = jnp.min(jnp.where(emask & (logits == v1), lane_f, big), axis=1, keepdims=True)
    emask2 = emask & (lane_f != i1)
    v2 = jnp.max(jnp.where(emask2, logits, neg), axis=1, keepdims=True)
    i2 = jnp.min(jnp.where(emask2 & (logits == v2), lane_f, big), axis=1, keepdims=True)
    e21 = jnp.exp(v2 - v1)
    w1 = wg / (1.0 + e21)
    w2 = wg * e21 / (1.0 + e21)

    a1 = lane_f == (i1 - N_EXPERT_GROUPS)
    a2 = lane_f == (i2 - N_EXPERT_GROUPS)
    a = (a1 | a2).astype(F32)
    before = jnp.dot(ltri_ref[...], a.astype(BF16), preferred_element_type=F32)
    cnt = jnp.sum(a, axis=0, keepdims=True)
    nch = jnp.floor((cnt + (CHUNK - 1)) * (1.0 / CHUNK))
    nch16 = jnp.broadcast_to(nch, (16, LANES))
    start = jnp.dot(nch16.astype(BF16), utri_ref[...], preferred_element_type=F32)
    slot = before + CHUNK * start[0:1, :]
    d1 = jnp.sum(jnp.where(a1, slot, 0.0), axis=1, keepdims=True)
    d2 = jnp.sum(jnp.where(a2, slot, 0.0), axis=1, keepdims=True)

    route = jnp.where(lane == 0, d1, jnp.where(lane == 1, d2, jnp.where(lane == 2, w1,
                      jnp.where(lane == 3, w2, 0.0))))
    route_ref[...] = route
    nch_ref[0] = nch16[0:8, :].astype(jnp.int32)

    dt = jnp.transpose(jnp.where(lane < 2, route, 0.0)).astype(jnp.int32)
    d1row = dt[0:1, :]
    d2row = dt[1:2, :]
    dest = lax.broadcasted_iota(jnp.int32, (CAP_ROWS, TM), 0)
    perm = ((dest == d1row) | (dest == d2row)).astype(F32).astype(BF16)
    half = D_MODEL // 2
    lo = jnp.dot(perm, xn[:, :half], preferred_element_type=F32)
    hi = jnp.dot(perm, xn[:, half:], preferred_element_type=F32)
    xs_ref[...] = _to_chunks(lo, hi)


def _route(x2p, x2s, g_ffn, w_router):
    n_prompt_tiles = x2p.shape[0] // TM
    n_tiles = n_prompt_tiles + x2s.shape[0] // TM
    r = jnp.arange(TM)
    ltri = (r[None, :] < r[:, None]).astype(BF16)
    e = jnp.arange(LANES)
    utri = (e[:, None] < e[None, :]).astype(BF16)
    return pl.pallas_call(
        functools.partial(_route_kernel, n_prompt_tiles),
        grid=(n_tiles,),
        in_specs=[pl.BlockSpec((TM, D_MODEL), lambda i: (jnp.minimum(i, n_prompt_tiles - 1), 0)),
                  pl.BlockSpec((TM, D_MODEL), lambda i: (jnp.maximum(i - n_prompt_tiles, 0), 0)),
                  _const_spec((1, D_MODEL)), _const_spec((D_MODEL, LANES)),
                  _const_spec((TM, TM)), _const_spec((LANES, LANES))],
        out_specs=[pl.BlockSpec((CAP_CHUNKS, 2 * CHUNK, D_MODEL // 2), lambda i: (i, 0, 0)),
                   pl.BlockSpec((TM, LANES), lambda i: (i, 0)),
                   pl.BlockSpec((1, 8, LANES), lambda i: (i, 0, 0))],
        out_shape=[jax.ShapeDtypeStruct((n_tiles * CAP_CHUNKS, 2 * CHUNK, D_MODEL // 2), BF16),
                   jax.ShapeDtypeStruct((n_tiles * TM, LANES), F32),
                   jax.ShapeDtypeStruct((n_tiles, 8, LANES), jnp.int32)],
        compiler_params=pltpu.CompilerParams(dimension_semantics=("arbitrary",), vmem_limit_bytes=VMEM_LIMIT),
        name="route_sort",
    )(x2p, x2s, g_ffn, w_router, ltri, utri)


def _spare_chunk(k):
    return (1 + k // N_SPARE_SLOTS) * CAP_CHUNKS + MAX_USED_CHUNKS + k % N_SPARE_SLOTS


def _chunk_tables(nch, n_expert_tiles):
    n_tiles = nch.shape[0]
    i32 = jnp.int32
    start = jnp.cumsum(nch, axis=1) - nch
    off = jnp.cumsum(nch, axis=0) - nch
    per_expert = jnp.sum(nch, axis=0)
    tiles_e = (per_expert + TILE_CHUNKS - 1) // TILE_CHUNKS
    cum_tiles = jnp.cumsum(tiles_e)
    first_tile = cum_tiles - tiles_e
    n_active = cum_tiles[-1].astype(i32)
    t = jnp.arange(n_expert_tiles, dtype=i32)
    te = jnp.minimum(jnp.sum((t[:, None] >= cum_tiles[None, :]).astype(i32), axis=1), N_EXPERTS - 1)
    onehot = te[:, None] == jnp.arange(N_EXPERTS, dtype=i32)[None, :]
    pick = lambda tab: jnp.sum(jnp.where(onehot[:, :, None], tab.T[None, :, :], 0), axis=1)
    off_t, nch_t, start_t = pick(off), pick(nch), pick(start)
    k = t - jnp.sum(jnp.where(onehot, first_tile[None, :], 0), axis=1)
    q = (TILE_CHUNKS * k)[:, None] + jnp.arange(TILE_CHUNKS, dtype=i32)[None, :]
    in_run = (off_t[:, None, :] <= q[:, :, None]) & (q[:, :, None] < (off_t + nch_t)[:, None, :])
    run_src = (jnp.arange(n_tiles, dtype=i32) * CAP_CHUNKS)[None, :] + start_t - off_t
    src = jnp.sum(jnp.where(in_run, run_src[:, None, :], 0), axis=-1) + q
    valid = jnp.any(in_run, axis=-1) & (t < n_active)[:, None]
    src = jnp.where(valid, src, ZERO_CHUNK).astype(i32)
    spare = _spare_chunk((t % N_RING)[:, None] * TILE_CHUNKS + jnp.arange(TILE_CHUNKS, dtype=i32)[None, :])
    dst = jnp.where(valid, src, spare).astype(i32)
    dst = jnp.concatenate([spare[1:3], dst], axis=0)
    return first_tile.astype(i32), tiles_e.astype(i32), src.reshape(-1), dst.reshape(-1), n_active.reshape(1)


N_RING = 3


def _expert_kernel(first_ref, ntile_ref, src_ref, dst_ref, nact_ref, xs_hbm, wg_ref, wu_ref, wd_ref, ys_hbm,
                   xbuf0, xbuf1, xbuf2, obuf0, obuf1, obuf2, gsem, ssem, wg16, wu16, wd16):
    e = pl.program_id(0)
    n_active = nact_ref[0]
    xbufs = (xbuf0, xbuf1, xbuf2)
    obufs = (obuf0, obuf1, obuf2)

    def gather_copy(tile, slot, c):
        return pltpu.make_async_copy(xs_hbm.at[src_ref[tile * TILE_CHUNKS + c]], xbufs[slot].at[c],
                                     gsem.at[slot])

    def scatter_copy(tile, slot, c):
        return pltpu.make_async_copy(obufs[slot].at[c], ys_hbm.at[dst_ref[(tile + 2) * TILE_CHUNKS + c]],
                                     ssem.at[slot])

    def start_all(copy, tile, slot):
        for c in range(TILE_CHUNKS):
            copy(tile, slot, c).start()

    def wait_all(copy, tile, slot):
        for c in range(TILE_CHUNKS):
            copy(tile, slot, c).wait()

    @pl.when(e == 0)
    def _():
        start_all(gather_copy, 0, 0)
        start_all(gather_copy, 1, 1)
        obuf1[...] = jnp.zeros_like(obuf1)
        obuf2[...] = jnp.zeros_like(obuf2)
        start_all(scatter_copy, -2, 1)

    wg16[...] = wg_ref[0].astype(BF16)
    wu16[...] = wu_ref[0].astype(BF16)
    wd16[...] = wd_ref[0].astype(BF16)

    def do_tile(tile, slot):
        prev, prev2 = (slot + 2) % N_RING, (slot + 1) % N_RING
        wait_all(gather_copy, tile, slot)
        lo, hi = _from_chunks(xbufs[slot][...])
        x = jnp.concatenate([lo, hi], axis=1)
        gate = jnp.dot(x, wg16[...], preferred_element_type=F32)
        start_all(scatter_copy, tile - 1, prev)
        up = jnp.dot(x, wu16[...], preferred_element_type=F32)
        hmid = (gate * _sigmoid(gate) * up).astype(BF16)
        start_all(gather_copy, tile + 2, prev)
        y = jnp.dot(hmid, wd16[...], preferred_element_type=F32)
        half = D_MODEL // 2
        wait_all(scatter_copy, tile - 2, prev2)
        obufs[slot][...] = _to_chunks(y[:, :half], y[:, half:])

    def tile_body(k, carry):
        tile = first_ref[e] + k
        for slot in range(N_RING):
            @pl.when(tile % N_RING == slot)
            def _():
                do_tile(tile, slot)
        return carry

    lax.fori_loop(0, ntile_ref[e], tile_body, 0)

    @pl.when(e == pl.num_programs(0) - 1)
    def _():
        last = n_active - 1
        for slot in range(N_RING):
            @pl.when(last % N_RING == slot)
            def _():
                wait_all(scatter_copy, last - 1, (slot + 2) % N_RING)
                start_all(scatter_copy, last, slot)
                wait_all(scatter_copy, last, slot)
                wait_all(gather_copy, last + 1, (slot + 1) % N_RING)
                wait_all(gather_copy, last + 2, (slot + 2) % N_RING)


def _experts(first_tile, tiles_e, src, dst, n_active, xs, w_gate, w_up, w_down):
    chunk_shape = (2 * CHUNK, D_MODEL // 2)
    assert _spare_chunk(N_RING * TILE_CHUNKS - 1) < xs.shape[0]
    tile_buf = pltpu.VMEM((TILE_CHUNKS,) + chunk_shape, BF16)
    grid_spec = pltpu.PrefetchScalarGridSpec(
        num_scalar_prefetch=5,
        grid=(N_EXPERTS,),
        in_specs=[pl.BlockSpec(memory_space=pl.ANY),
                  pl.BlockSpec((1, D_MODEL, D_EXPERT), lambda e, *_: (e, 0, 0)),
                  pl.BlockSpec((1, D_MODEL, D_EXPERT), lambda e, *_: (e, 0, 0)),
                  pl.BlockSpec((1, D_EXPERT, D_MODEL), lambda e, *_: (e, 0, 0))],
        out_specs=pl.BlockSpec(memory_space=pl.ANY),
        scratch_shapes=[tile_buf] * (2 * N_RING) + [
                        pltpu.SemaphoreType.DMA((N_RING,)),
                        pltpu.SemaphoreType.DMA((N_RING,)),
                        pltpu.VMEM((D_MODEL, D_EXPERT), BF16),
                        pltpu.VMEM((D_MODEL, D_EXPERT), BF16),
                        pltpu.VMEM((D_EXPERT, D_MODEL), BF16)],
    )
    return pl.pallas_call(
        _expert_kernel,
        grid_spec=grid_spec,
        out_shape=jax.ShapeDtypeStruct(xs.shape, xs.dtype),
        input_output_aliases={5: 0},
        compiler_params=pltpu.CompilerParams(dimension_semantics=("arbitrary",), vmem_limit_bytes=VMEM_LIMIT),
        name="expert_ffn",
    )(first_tile, tiles_e, src, dst, n_active, xs, w_gate, w_up, w_down)


def _combine_kernel(n_prompt_tiles, x2p_ref, x2s_ref, route_ref, gfin_ref, ys_ref, outp_ref, outs_ref):
    i = pl.program_id(0)
    lo, hi = _from_chunks(ys_ref[...])
    route = route_ref[...]
    d1 = route[:, 0:1].astype(jnp.int32)
    d2 = route[:, 1:2].astype(jnp.int32)
    w1 = route[:, 2:3]
    w2 = route[:, 3:4]
    dest = lax.broadcasted_iota(jnp.int32, (TM, CAP_ROWS), 1)
    sel1 = (dest == d1).astype(F32).astype(BF16)
    sel2 = (dest == d2).astype(F32).astype(BF16)
    m1 = jnp.concatenate([jnp.dot(sel1, lo, preferred_element_type=F32),
                          jnp.dot(sel1, hi, preferred_element_type=F32)], axis=1)
    m2 = jnp.concatenate([jnp.dot(sel2, lo, preferred_element_type=F32),
                          jnp.dot(sel2, hi, preferred_element_type=F32)], axis=1)
    x = jnp.where(i < n_prompt_tiles, x2p_ref[...], x2s_ref[...])
    out = _rms(x + (w1 * m1 + w2 * m2), gfin_ref[...])

    @pl.when(i < n_prompt_tiles)
    def _():
        outp_ref[...] = out

    @pl.when(i >= n_prompt_tiles)
    def _():
        outs_ref[...] = out


def _combine(x2p, x2s, route, g_final, ys):
    n_prompt_tiles = x2p.shape[0] // TM
    n_tiles = n_prompt_tiles + x2s.shape[0] // TM
    p_spec = pl.BlockSpec((TM, D_MODEL), lambda i: (jnp.minimum(i, n_prompt_tiles - 1), 0))
    s_spec = pl.BlockSpec((TM, D_MODEL), lambda i: (jnp.maximum(i - n_prompt_tiles, 0), 0))
    return pl.pallas_call(
        functools.partial(_combine_kernel, n_prompt_tiles),
        grid=(n_tiles,),
        in_specs=[p_spec, s_spec,
                  pl.BlockSpec((TM, LANES), lambda i: (i, 0)),
                  _const_spec((1, D_MODEL)),
                  pl.BlockSpec((CAP_CHUNKS, 2 * CHUNK, D_MODEL // 2), lambda i: (i, 0, 0))],
        out_specs=[p_spec, s_spec],
        out_shape=[jax.ShapeDtypeStruct(x2p.shape, F32), jax.ShapeDtypeStruct(x2s.shape, F32)],
        compiler_params=pltpu.CompilerParams(dimension_semantics=("arbitrary",), vmem_limit_bytes=VMEM_LIMIT),
        name="combine_norm",
    )(x2p, x2s, route, g_final, ys)


def _block_diag_in(bb):
    t = bb.reshape(2, 16, SSM_STATE, SSM_GROUP).transpose(0, 1, 3, 2)
    eye = jnp.eye(16, dtype=bb.dtype)
    blk = t[:, :, :, None, :] * eye[None, :, None, :, None]
    return blk.reshape(2, 16 * SSM_GROUP, 16 * SSM_STATE)


def _block_diag_out(c):
    t = c.reshape(2, 16, SSM_GROUP, SSM_STATE).transpose(0, 1, 3, 2)
    eye = jnp.eye(16, dtype=c.dtype)
    blk = t[:, :, :, None, :] * eye[None, :, None, :, None]
    return blk.reshape(2, 16 * SSM_STATE, 16 * SSM_GROUP)


def kernel(x_prompt, x_sample, state_ssm_re, state_ssm_im, cache_pool, g_mix, w_in, ssm_a_re, ssm_a_im,
           ssm_log_dt, ssm_b_re, ssm_b_im, ssm_c_re, ssm_c_im, ssm_d, w_glu_a, w_glu_b, pool_w, pool_scale,
           w_out, g_ffn, w_router_group, w_router_expert, w_exp_gate, w_exp_up, w_exp_down, g_final):
    li = 0
    n_pb, seq_p, _ = x_prompt.shape
    n_sb, seq_s, _ = x_sample.shape

    a_re, a_im, a32_re, a32_im, bb_re, bb_im = _discretise(
        ssm_a_re[li], ssm_a_im[li], ssm_log_dt[li], ssm_b_re[li], ssm_b_im[li])
    row = lambda v: v.reshape(1, N_FLAT)
    wb = jnp.concatenate([_block_diag_in(bb_re), _block_diag_in(bb_im)], axis=2).astype(BF16)
    weights = [
        g_mix[li].reshape(1, D_MODEL),
        w_in[li].astype(BF16),
        wb,
        _block_diag_out(ssm_c_re[li]).astype(BF16),
        _block_diag_out(ssm_c_im[li]).astype(BF16),
        ssm_d[li].reshape(1, SSM_WIDTH),
        jnp.concatenate([w_glu_a[li], w_glu_b[li]], axis=1).astype(BF16),
        pool_w[li].astype(BF16),
        pool_scale[li].reshape(1, D_MODEL),
        w_out[li].astype(BF16),
        row(a_re), row(a_im), row(a32_re), row(a32_im),
    ]

    x2p, stp_re, stp_im, histp = _mixer_prompt(x_prompt, weights)
    x2p = x2p.reshape(n_pb * seq_p, D_MODEL)

    n_stiles = n_sb // N_CHAIN
    cache16 = jnp.pad(cache_pool[li], ((0, 0), (1, 0), (0, 0)))
    cache_t = cache16.reshape(n_stiles, N_CHAIN, 16, POOL_WIDTH).transpose(0, 2, 1, 3).reshape(
        n_stiles, HIST_ROWS, POOL_WIDTH)
    x2s, sts_re, sts_im, hists = _mixer_sample(
        x_sample, state_ssm_re[li].reshape(n_sb, N_FLAT), state_ssm_im[li].reshape(n_sb, N_FLAT),
        cache_t, weights)
    x2s = x2s.reshape(n_sb * seq_s, D_MODEL)

    w_router = jnp.concatenate(
        [w_router_group[li], w_router_expert[li].reshape(D_MODEL, N_EXPERTS),
         jnp.zeros((D_MODEL, LANES - N_EXPERT_GROUPS - N_EXPERTS), F32)], axis=1).astype(BF16)
    xs, route, nch = _route(x2p, x2s, g_ffn[li].reshape(1, D_MODEL), w_router)

    n_tiles = (x2p.shape[0] + x2s.shape[0]) // TM
    max_chunks = n_tiles * (2 * TM // CHUNK + N_EXPERTS)
    n_expert_tiles = max_chunks // TILE_CHUNKS + N_EXPERTS
    first_tile, tiles_e, src, dst, n_active = _chunk_tables(nch[:, 0, :N_EXPERTS], n_expert_tiles)
    ys = _experts(first_tile, tiles_e, src, dst, n_active, xs, w_exp_gate[li], w_exp_up[li], w_exp_down[li])
    yp, ysm = _combine(x2p, x2s, route, g_final.reshape(1, D_MODEL), ys)

    sd = state_ssm_re.dtype
    cd = cache_pool.dtype
    y_prompt = yp.reshape(n_pb, seq_p, D_MODEL)
    y_sample = ysm.reshape(n_sb, seq_s, D_MODEL)
    re_p = stp_re.reshape(1, n_pb, SSM_GROUPS, SSM_STATE).astype(sd)
    im_p = stp_im.reshape(1, n_pb, SSM_GROUPS, SSM_STATE).astype(sd)
    hist_p = histp[:, ::N_CHAIN, :][:, 1:, :].reshape(1, n_pb, POOL_HIST, POOL_WIDTH).astype(cd)
    re_s = sts_re.reshape(1, n_sb, SSM_GROUPS, SSM_STATE).astype(sd)
    im_s = sts_im.reshape(1, n_sb, SSM_GROUPS, SSM_STATE).astype(sd)
    hist_s = hists.reshape(n_stiles, 16, N_CHAIN, POOL_WIDTH).transpose(0, 2, 1, 3).reshape(
        n_sb, 16, POOL_WIDTH)[:, 1:, :].reshape(1, n_sb, POOL_HIST, POOL_WIDTH).astype(cd)
    return (y_prompt, y_sample, re_p, im_p, hist_p, re_s, im_s, hist_s)
```

```python
import functools
import math

import jax
import jax.numpy as jnp
from jax import lax
from jax.experimental import pallas as pl
from jax.experimental.pallas import tpu as pltpu

F32 = jnp.float32
BF16 = jnp.bfloat16

D_MODEL = 1024
SSM_WIDTH = 512
SSM_GROUPS = 32
SSM_GROUP = 16
SSM_STATE = 64
N_FLAT = SSM_GROUPS * SSM_STATE
HALF_FLAT = N_FLAT // 2
POOL_WIDTH = 512
POOL_WINDOWS = (2, 4, 8, 16)
POOL_GROUP_IN = 128
POOL_GROUP_OUT = 256
POOL_HIST = 15
N_EXPERTS = 32
EXPERTS_PER_GROUP = 8
N_EXPERT_GROUPS = 4
D_EXPERT = 512
EPS = 1e-6
PAST_LEN = 1024

TM = 256
N_CHAIN = 8
CHAIN_LEN = TM // N_CHAIN
HIST_ROWS = 16 * N_CHAIN

CHUNK = 8
TILE_CHUNKS = TM // CHUNK
CAP_CHUNKS = 96
CAP_ROWS = CAP_CHUNKS * CHUNK
MAX_USED_CHUNKS = 2 * TM // CHUNK + N_EXPERTS * (CHUNK - 1) // CHUNK
N_SPARE_SLOTS = CAP_CHUNKS - MAX_USED_CHUNKS - 1
ZERO_CHUNK = CAP_CHUNKS - 1
LANES = 128

VMEM_LIMIT = 52 * 1024 * 1024
STEP_TILES = 2


def _rms(x, g):
    r = lax.rsqrt(jnp.mean(x * x, axis=-1, keepdims=True) + EPS)
    return x * r * g


def _sigmoid(x):
    return 0.5 * jnp.tanh(0.5 * x) + 0.5


def _gelu_tanh(x):
    c = math.sqrt(2.0 / math.pi)
    return x * (0.5 * (1.0 + jnp.tanh(c * (x + 0.044715 * (x * x * x)))))


def _disc_kernel(lre_ref, lim_ref, ldt_ref, bre_ref, bim_ref,
                 are_ref, aim_ref, a32re_ref, a32im_ref, bbre_ref, bbim_ref):
    lam_re = jnp.minimum(lre_ref[...], -1e-4)
    lam_im = lim_ref[...]
    dt = jnp.exp(ldt_ref[...])
    mag = jnp.exp(lam_re * dt)
    ang = lam_im * dt
    a_re = mag * jnp.cos(ang)
    a_im = mag * jnp.sin(ang)
    num_re = a_re - 1.0
    num_im = a_im
    den = lam_re * lam_re + lam_im * lam_im
    k_re = (num_re * lam_re + num_im * lam_im) / den
    k_im = (num_im * lam_re - num_re * lam_im) / den
    br = bre_ref[...]
    bi = bim_ref[...]
    bbre_ref[...] = k_re * br - k_im * bi
    bbim_ref[...] = k_re * bi + k_im * br
    are_ref[...] = a_re
    aim_ref[...] = a_im
    pr, pi = a_re, a_im
    for _ in range(int(math.log2(CHAIN_LEN))):
        pr, pi = pr * pr - pi * pi, 2.0 * pr * pi
    a32re_ref[...] = pr
    a32im_ref[...] = pi


def _discretise(a_re, a_im, log_dt, b_re, b_im):
    col = lambda v: v.reshape(N_FLAT, 1)
    ldt = jnp.broadcast_to(log_dt[:, None], (SSM_GROUPS, SSM_STATE))
    outs = pl.pallas_call(
        _disc_kernel,
        out_shape=[jax.ShapeDtypeStruct((N_FLAT, 1), F32)] * 4
        + [jax.ShapeDtypeStruct((N_FLAT, SSM_GROUP), F32)] * 2,
        name="s5_discretise",
    )(col(a_re), col(a_im), col(ldt), b_re.reshape(N_FLAT, SSM_GROUP), b_im.reshape(N_FLAT, SSM_GROUP))
    return outs


def _scan_half(hbuf, h, ar, ai, init_re, init_im, store):
    cre = h * N_FLAT
    cim = cre + HALF_FLAT
    hr, hi = init_re, init_im
    for t in range(CHAIN_LEN):
        rows = pl.ds(N_CHAIN * t, N_CHAIN)
        br = hbuf[rows, cre:cre + HALF_FLAT]
        bi = hbuf[rows, cim:cim + HALF_FLAT]
        nr = ar * hr - ai * hi + br
        ni = ar * hi + ai * hr + bi
        if store:
            hbuf[rows, cre:cre + HALF_FLAT] = nr
            hbuf[rows, cim:cim + HALF_FLAT] = ni
        hr, hi = nr, ni
    return hr, hi


N_MIXER_WEIGHTS = 14
N_PAR = 2


def _mixer_kernel(is_prompt, tiles_per_stream, *refs):
    n_in = 1 if is_prompt else 4
    ins = refs[:n_in]
    (gmix, win, wb, wcre, wcim, dsk, wglu, poolw, pscale, wout, are, aim, a32re, a32im) = refs[
        n_in:n_in + N_MIXER_WEIGHTS]
    x2_ref, stre_ref, stim_ref, hist_ref = refs[n_in + N_MIXER_WEIGHTS:n_in + N_MIXER_WEIGHTS + 4]
    scratch = refs[n_in + N_MIXER_WEIGHTS + 4:]
    per = len(scratch) // N_PAR
    lanes = [scratch[p * per:(p + 1) * per] for p in range(N_PAR)]
    x_ref = ins[0]

    tile_in_stream = pl.program_id(0) % tiles_per_stream
    n_lane_blocks = D_MODEL // LANES
    half_w = SSM_WIDTH // 2
    ar_full = are[...]
    ai_full = aim[...]

    if is_prompt:
        @pl.when(pl.program_id(0) == 0)
        def _():
            for p in range(N_PAR):
                cre_s, cim_s, pcarry = lanes[p][8], lanes[p][9], lanes[p][10]
                cre_s[...] = jnp.zeros_like(cre_s)
                cim_s[...] = jnp.zeros_like(cim_s)
                pcarry[...] = jnp.zeros_like(pcarry)

    def a_half(h):
        f0 = h * HALF_FLAT
        return (jnp.broadcast_to(ar_full[:, f0:f0 + HALF_FLAT], (N_CHAIN, HALF_FLAT)),
                jnp.broadcast_to(ai_full[:, f0:f0 + HALF_FLAT], (N_CHAIN, HALF_FLAT)))

    def project(p):
        xperm, hbuf = lanes[p][0], lanes[p][1]
        for j in range(N_CHAIN):
            for cb in range(n_lane_blocks):
                xperm[cb, pl.ds(j, CHAIN_LEN, stride=N_CHAIN), :] = x_ref[
                    p, CHAIN_LEN * j:CHAIN_LEN * (j + 1), cb * LANES:(cb + 1) * LANES]
        x = jnp.concatenate([xperm[cb] for cb in range(n_lane_blocks)], axis=1)
        xn = _rms(x, gmix[...]).astype(BF16)
        proj = jnp.dot(xn, win[...], preferred_element_type=F32)
        ub = proj[:, :SSM_WIDTH].astype(BF16)
        for h in range(2):
            hbuf[:, h * N_FLAT:(h + 1) * N_FLAT] = jnp.dot(
                ub[:, h * half_w:(h + 1) * half_w], wb[h], preferred_element_type=F32)
        return x, proj

    def recur(p):
        hbuf = lanes[p][1]
        if is_prompt:
            fre, fim, hre, him, cre_s, cim_s = lanes[p][4:10]
            zeros = jnp.zeros((N_CHAIN, HALF_FLAT), F32)
            for h in range(2):
                f0 = h * HALF_FLAT
                ar, ai = a_half(h)
                lr, li = _scan_half(hbuf, h, ar, ai, zeros, zeros, store=False)
                fre[:, f0:f0 + HALF_FLAT] = lr
                fim[:, f0:f0 + HALF_FLAT] = li
            fresh = tile_in_stream == 0
            hre[0:1, :] = jnp.where(fresh, 0.0, cre_s[...])
            him[0:1, :] = jnp.where(fresh, 0.0, cim_s[...])
            p_re = a32re[...]
            p_im = a32im[...]
            for j in range(N_CHAIN - 1):
                sr = hre[j:j + 1, :]
                si = him[j:j + 1, :]
                hre[j + 1:j + 2, :] = fre[j:j + 1, :] + p_re * sr - p_im * si
                him[j + 1:j + 2, :] = fim[j:j + 1, :] + p_re * si + p_im * sr
            init_re = hre[...]
            init_im = him[...]
        else:
            init_re = ins[1][p]
            init_im = ins[2][p]
        fin_re = []
        fin_im = []
        for h in range(2):
            f0 = h * HALF_FLAT
            ar, ai = a_half(h)
            er, ei = _scan_half(hbuf, h, ar, ai, init_re[:, f0:f0 + HALF_FLAT],
                                init_im[:, f0:f0 + HALF_FLAT], store=True)
            fin_re.append(er)
            fin_im.append(ei)
        end_re = jnp.concatenate(fin_re, axis=1)
        end_im = jnp.concatenate(fin_im, axis=1)
        if is_prompt:
            cre_s[...] = end_re[N_CHAIN - 1:N_CHAIN, :]
            cim_s[...] = end_im[N_CHAIN - 1:N_CHAIN, :]
            stre_ref[p] = end_re[N_CHAIN - 1:N_CHAIN, :]
            stim_ref[p] = end_im[N_CHAIN - 1:N_CHAIN, :]
        else:
            stre_ref[p] = end_re
            stim_ref[p] = end_im

    def finish(p, x, proj):
        hbuf, res, xpbuf = lanes[p][1], lanes[p][2], lanes[p][3]
        u_s = proj[:, :SSM_WIDTH]
        u_p = proj[:, SSM_WIDTH:SSM_WIDTH + POOL_WIDTH]
        gate_s = proj[:, SSM_WIDTH + POOL_WIDTH:SSM_WIDTH + POOL_WIDTH + D_MODEL]
        gate_p = proj[:, SSM_WIDTH + POOL_WIDTH + D_MODEL:]
        ys = []
        for h in range(2):
            c0 = h * N_FLAT
            h_re = hbuf[:, c0:c0 + HALF_FLAT].astype(BF16)
            h_im = hbuf[:, c0 + HALF_FLAT:c0 + N_FLAT].astype(BF16)
            ys.append(jnp.dot(h_re, wcre[h], preferred_element_type=F32)
                      - jnp.dot(h_im, wcim[h], preferred_element_type=F32))
        y = jnp.concatenate(ys, axis=1) + dsk[...] * u_s
        g = _gelu_tanh(y).astype(BF16)
        glu = jnp.dot(g, wglu[...], preferred_element_type=F32)
        o_s = glu[:, :D_MODEL] * _sigmoid(glu[:, D_MODEL:])

        xpbuf[HIST_ROWS:HIST_ROWS + TM, :] = u_p
        tail = u_p[TM - HIST_ROWS:, :]
        row = lax.broadcasted_iota(jnp.int32, (TM, 1), 0)
        if is_prompt:
            pcarry = lanes[p][10]
            first_chain = (lax.broadcasted_iota(jnp.int32, (HIST_ROWS, POOL_WIDTH), 0) % N_CHAIN) == 0
            carried = jnp.where(tile_in_stream == 0, 0.0, pcarry[...])
            xpbuf[0:HIST_ROWS, :] = jnp.where(first_chain, carried, pltpu.roll(tail, 1, 0))
            new_carry = pltpu.roll(tail, HIST_ROWS - (N_CHAIN - 1), 0)
            pcarry[...] = new_carry
            hist_ref[p] = new_carry
            pos1 = tile_in_stream * TM + CHAIN_LEN * (row % N_CHAIN) + row // N_CHAIN + 1
        else:
            xpbuf[0:HIST_ROWS, :] = ins[3][p]
            hist_ref[p] = tail
            pos1 = PAST_LEN + row // N_CHAIN + 1

        o_ps = []
        for gi, w in enumerate(POOL_WINDOWS):
            c0 = gi * POOL_GROUP_IN
            acc = xpbuf[HIST_ROWS:HIST_ROWS + TM, c0:c0 + POOL_GROUP_IN]
            for k in range(1, w):
                acc = acc + xpbuf[HIST_ROWS - N_CHAIN * k:HIST_ROWS - N_CHAIN * k + TM, c0:c0 + POOL_GROUP_IN]
            cnt = jnp.minimum(w, pos1).astype(F32)
            pooled = acc / cnt
            z = (pooled - u_p[:, c0:c0 + POOL_GROUP_IN]).astype(BF16)
            o_ps.append(jnp.dot(z, poolw[gi], preferred_element_type=F32))
        o_p = jnp.concatenate(o_ps, axis=1) * pscale[...]

        merged = (_sigmoid(gate_s) * o_s + _sigmoid(gate_p) * o_p).astype(BF16)
        x2 = x + jnp.dot(merged, wout[...], preferred_element_type=F32)
        for cb in range(n_lane_blocks):
            res[cb] = x2[:, cb * LANES:(cb + 1) * LANES]
        for j in range(N_CHAIN):
            for cb in range(n_lane_blocks):
                x2_ref[p, CHAIN_LEN * j:CHAIN_LEN * (j + 1), cb * LANES:(cb + 1) * LANES] = res[
                    cb, pl.ds(j, CHAIN_LEN, stride=N_CHAIN), :]

    projected = [project(p) for p in range(N_PAR)]
    for p in range(N_PAR):
        recur(p)
    for p in range(N_PAR):
        finish(p, *projected[p])


def _const_spec(shape):
    nd = len(shape)
    return pl.BlockSpec(shape, lambda i, _nd=nd: (0,) * _nd)


def _mixer_weight_specs():
    return [
        _const_spec((1, D_MODEL)),
        _const_spec((D_MODEL, 3 * D_MODEL)),
        _const_spec((2, SSM_WIDTH // 2, N_FLAT)),
        _const_spec((2, HALF_FLAT, SSM_WIDTH // 2)),
        _const_spec((2, HALF_FLAT, SSM_WIDTH // 2)),
        _const_spec((1, SSM_WIDTH)),
        _const_spec((SSM_WIDTH, 2 * D_MODEL)),
        _const_spec((len(POOL_WINDOWS), POOL_GROUP_IN, POOL_GROUP_OUT)),
        _const_spec((1, D_MODEL)),
        _const_spec((D_MODEL, D_MODEL)),
        _const_spec((1, N_FLAT)),
        _const_spec((1, N_FLAT)),
        _const_spec((1, N_FLAT)),
        _const_spec((1, N_FLAT)),
    ]


def _mixer_common_scratch():
    return [
        pltpu.VMEM((D_MODEL // LANES, TM, LANES), F32),
        pltpu.VMEM((TM, 2 * N_FLAT), F32),
        pltpu.VMEM((D_MODEL // LANES, TM, LANES), F32),
        pltpu.VMEM((HIST_ROWS + TM, POOL_WIDTH), F32),
    ]


def _mixer_prompt(x, weights):
    n_streams, seq, _ = x.shape
    assert n_streams % N_PAR == 0 and seq % TM == 0
    tiles_per_stream = seq // TM
    blk = lambda shape: pl.BlockSpec(shape, lambda i: (i // tiles_per_stream, 0, 0))
    row_spec = pl.BlockSpec((N_PAR, TM, D_MODEL), lambda i: (i // tiles_per_stream, i % tiles_per_stream, 0))
    lane_scratch = _mixer_common_scratch() + [
        pltpu.VMEM((N_CHAIN, N_FLAT), F32), pltpu.VMEM((N_CHAIN, N_FLAT), F32),
        pltpu.VMEM((N_CHAIN, N_FLAT), F32), pltpu.VMEM((N_CHAIN, N_FLAT), F32),
        pltpu.VMEM((1, N_FLAT), F32), pltpu.VMEM((1, N_FLAT), F32),
        pltpu.VMEM((HIST_ROWS, POOL_WIDTH), F32),
    ]
    return pl.pallas_call(
        functools.partial(_mixer_kernel, True, tiles_per_stream),
        grid=(n_streams // N_PAR * tiles_per_stream,),
        in_specs=[row_spec] + _mixer_weight_specs(),
        out_specs=[row_spec, blk((N_PAR, 1, N_FLAT)), blk((N_PAR, 1, N_FLAT)),
                   blk((N_PAR, HIST_ROWS, POOL_WIDTH))],
        out_shape=[jax.ShapeDtypeStruct((n_streams, seq, D_MODEL), F32),
                   jax.ShapeDtypeStruct((n_streams, 1, N_FLAT), F32),
                   jax.ShapeDtypeStruct((n_streams, 1, N_FLAT), F32),
                   jax.ShapeDtypeStruct((n_streams, HIST_ROWS, POOL_WIDTH), F32)],
        scratch_shapes=lane_scratch * N_PAR,
        compiler_params=pltpu.CompilerParams(dimension_semantics=("arbitrary",), vmem_limit_bytes=VMEM_LIMIT),
        name="mixer_prompt",
    )(x, *weights)


def _mixer_sample(x, h0_re, h0_im, cache_t, weights):
    n_streams, seq, _ = x.shape
    assert seq == CHAIN_LEN and n_streams % (N_CHAIN * N_PAR) == 0
    n_tiles = n_streams // N_CHAIN
    blk = lambda shape: pl.BlockSpec(shape, lambda i: (i, 0, 0))
    row_spec = blk((N_PAR, TM, D_MODEL))
    st_spec = blk((N_PAR, N_CHAIN, N_FLAT))
    hist_spec = blk((N_PAR, HIST_ROWS, POOL_WIDTH))
    tiles = lambda v: v.reshape((n_tiles, -1) + v.shape[-1:])
    return pl.pallas_call(
        functools.partial(_mixer_kernel, False, 1),
        grid=(n_tiles // N_PAR,),
        in_specs=[row_spec, st_spec, st_spec, hist_spec] + _mixer_weight_specs(),
        out_specs=[row_spec, st_spec, st_spec, hist_spec],
        out_shape=[jax.ShapeDtypeStruct((n_tiles, TM, D_MODEL), F32),
                   jax.ShapeDtypeStruct((n_tiles, N_CHAIN, N_FLAT), F32),
                   jax.ShapeDtypeStruct((n_tiles, N_CHAIN, N_FLAT), F32),
                   jax.ShapeDtypeStruct((n_tiles, HIST_ROWS, POOL_WIDTH), F32)],
        scratch_shapes=_mixer_common_scratch() * N_PAR,
        compiler_params=pltpu.CompilerParams(dimension_semantics=("arbitrary",), vmem_limit_bytes=VMEM_LIMIT),
        name="mixer_sample",
    )(tiles(x), tiles(h0_re), tiles(h0_im), cache_t, *weights)


def _to_chunks(lo, hi):
    n = lo.shape[0] // CHUNK
    half = D_MODEL // 2
    both = jnp.concatenate([lo.reshape(n, CHUNK, half), hi.reshape(n, CHUNK, half)], axis=1)
    return both.astype(BF16)


def _from_chunks(blk):
    n = blk.shape[0]
    half = D_MODEL // 2
    f = blk.astype(F32)
    lo = f[:, :CHUNK, :].reshape(n * CHUNK, half).astype(BF16)
    hi = f[:, CHUNK:, :].reshape(n * CHUNK, half).astype(BF16)
    return lo, hi


def _route_kernel(n_prompt_steps, x2p_ref, x2s_ref, g_ref, wr_ref, ltri_ref, utri_ref,
                  xs_ref, route_ref, nch_ref):
    is_prompt = pl.program_id(0) < n_prompt_steps
    for h in range(STEP_TILES):
        _route_tile(h, is_prompt, x2p_ref, x2s_ref, g_ref, wr_ref, ltri_ref, utri_ref, xs_ref, route_ref, nch_ref)


def _route_tile(h, is_prompt, x2p_ref, x2s_ref, g_ref, wr_ref, ltri_ref, utri_ref, xs_ref, route_ref, nch_ref):
    rows = pl.ds(h * TM, TM)
    x = jnp.where(is_prompt, x2p_ref[rows, :], x2s_ref[rows, :])
    xn = _rms(x, g_ref[...]).astype(BF16)
    logits = jnp.dot(xn, wr_ref[...], preferred_element_type=F32)
    lane = lax.broadcasted_iota(jnp.int32, (TM, LANES), 1)
    lane_f = lane.astype(F32)
    big = jnp.float32(1 << 20)
    neg = jnp.float32(-jnp.inf)

    gmask = lane < N_EXPERT_GROUPS
    m = jnp.max(jnp.where(gmask, logits, neg), axis=1, keepdims=True)
    grp = jnp.min(jnp.where(gmask & (logits == m), lane_f, big), axis=1, keepdims=True)
    wg = 1.0 / jnp.sum(jnp.where(gmask, jnp.exp(logits - m), 0.0), axis=1, keepdims=True)

    eid = lane - N_EXPERT_GROUPS
    lane_grp = (eid >> 3).astype(F32)
    emask = (eid >= 0) & (eid < N_EXPERTS) & (lane_grp == grp)
    v1 = jnp.max(jnp.where(emask, logits, neg), axis=1, keepdims=True)
    i1 = jnp.min(jnp.where(emask & (logits == v1), lane_f, big), axis=1, keepdims=True)
    emask2 = emask & (lane_f != i1)
    v2 = jnp.max(jnp.where(emask2, logits, neg), axis=1, keepdims=True)
    i2 = jnp.min(jnp.where(emask2 & (logits == v2), lane_f, big), axis=1, keepdims=True)
    e21 = jnp.exp(v2 - v1)
    w1 = wg / (1.0 + e21)
    w2 = wg * e21 / (1.0 + e21)

    a1 = lane_f == (i1 - N_EXPERT_GROUPS)
    a2 = lane_f == (i2 - N_EXPERT_GROUPS)
    a = (a1 | a2).astype(F32)
    before = jnp.dot(ltri_ref[...], a.astype(BF16), preferred_element_type=F32)
    cnt = jnp.sum(a, axis=0, keepdims=True)
    nch = jnp.floor((cnt + (CHUNK - 1)) * (1.0 / CHUNK))
    nch16 = jnp.broadcast_to(nch, (16, LANES))
    start = jnp.dot(nch16.astype(BF16), utri_ref[...], preferred_element_type=F32)
    slot = before + CHUNK * start[0:1, :]
    d1 = jnp.sum(jnp.where(a1, slot, 0.0), axis=1, keepdims=True)
    d2 = jnp.sum(jnp.where(a2, slot, 0.0), axis=1, keepdims=True)

    route = jnp.where(lane == 0, d1, jnp.where(lane == 1, d2, jnp.where(lane == 2, w1,
                      jnp.where(lane == 3, w2, 0.0))))
    route_ref[rows, :] = route
    nch_ref[h] = nch16[0:8, :].astype(jnp.int32)

    dt = jnp.transpose(jnp.where(lane < 2, route, 0.0)).astype(jnp.int32)
    d1row = dt[0:1, :]
    d2row = dt[1:2, :]
    dest = lax.broadcasted_iota(jnp.int32, (CAP_ROWS, TM), 0)
    perm = ((dest == d1row) | (dest == d2row)).astype(F32).astype(BF16)
    half = D_MODEL // 2
    lo = jnp.dot(perm, xn[:, :half], preferred_element_type=F32)
    hi = jnp.dot(perm, xn[:, half:], preferred_element_type=F32)
    xs_ref[pl.ds(h * CAP_CHUNKS, CAP_CHUNKS)] = _to_chunks(lo, hi)


def _route(x2p, x2s, g_ffn, w_router):
    n_prompt_tiles = x2p.shape[0] // TM
    n_tiles = n_prompt_tiles + x2s.shape[0] // TM
    assert n_prompt_tiles % STEP_TILES == 0 and n_tiles % STEP_TILES == 0
    n_prompt_steps = n_prompt_tiles // STEP_TILES
    rows = STEP_TILES * TM
    r = jnp.arange(TM)
    ltri = (r[None, :] < r[:, None]).astype(BF16)
    e = jnp.arange(LANES)
    utri = (e[:, None] < e[None, :]).astype(BF16)
    return pl.pallas_call(
        functools.partial(_route_kernel, n_prompt_steps),
        grid=(n_tiles // STEP_TILES,),
        in_specs=[pl.BlockSpec((rows, D_MODEL), lambda i: (jnp.minimum(i, n_prompt_steps - 1), 0)),
                  pl.BlockSpec((rows, D_MODEL), lambda i: (jnp.maximum(i - n_prompt_steps, 0), 0)),
                  _const_spec((1, D_MODEL)), _const_spec((D_MODEL, LANES)),
                  _const_spec((TM, TM)), _const_spec((LANES, LANES))],
        out_specs=[pl.BlockSpec((STEP_TILES * CAP_CHUNKS, 2 * CHUNK, D_MODEL // 2), lambda i: (i, 0, 0)),
                   pl.BlockSpec((rows, LANES), lambda i: (i, 0)),
                   pl.BlockSpec((STEP_TILES, 8, LANES), lambda i: (i, 0, 0))],
        out_shape=[jax.ShapeDtypeStruct((n_tiles * CAP_CHUNKS, 2 * CHUNK, D_MODEL // 2), BF16),
                   jax.ShapeDtypeStruct((n_tiles * TM, LANES), F32),
                   jax.ShapeDtypeStruct((n_tiles, 8, LANES), jnp.int32)],
        compiler_params=pltpu.CompilerParams(dimension_semantics=("arbitrary",), vmem_limit_bytes=VMEM_LIMIT),
        name="route_sort",
    )(x2p, x2s, g_ffn, w_router, ltri, utri)


def _spare_chunk(k):
    return (1 + k // N_SPARE_SLOTS) * CAP_CHUNKS + MAX_USED_CHUNKS + k % N_SPARE_SLOTS


def _chunk_tables(nch, n_expert_tiles):
    n_tiles = nch.shape[0]
    i32 = jnp.int32
    start = jnp.cumsum(nch, axis=1) - nch
    off = jnp.cumsum(nch, axis=0) - nch
    per_expert = jnp.sum(nch, axis=0)
    tiles_e = (per_expert + TILE_CHUNKS - 1) // TILE_CHUNKS
    cum_tiles = jnp.cumsum(tiles_e)
    first_tile = cum_tiles - tiles_e
    n_active = cum_tiles[-1].astype(i32)
    t = jnp.arange(n_expert_tiles, dtype=i32)
    te = jnp.minimum(jnp.sum((t[:, None] >= cum_tiles[None, :]).astype(i32), axis=1), N_EXPERTS - 1)
    onehot = te[:, None] == jnp.arange(N_EXPERTS, dtype=i32)[None, :]
    pick = lambda tab: jnp.sum(jnp.where(onehot[:, :, None], tab.T[None, :, :], 0), axis=1)
    off_t, nch_t, start_t = pick(off), pick(nch), pick(start)
    k = t - jnp.sum(jnp.where(onehot, first_tile[None, :], 0), axis=1)
    q = (TILE_CHUNKS * k)[:, None] + jnp.arange(TILE_CHUNKS, dtype=i32)[None, :]
    in_run = (off_t[:, None, :] <= q[:, :, None]) & (q[:, :, None] < (off_t + nch_t)[:, None, :])
    run_src = (jnp.arange(n_tiles, dtype=i32) * CAP_CHUNKS)[None, :] + start_t - off_t
    src = jnp.sum(jnp.where(in_run, run_src[:, None, :], 0), axis=-1) + q
    valid = jnp.any(in_run, axis=-1) & (t < n_active)[:, None]
    src = jnp.where(valid, src, ZERO_CHUNK).astype(i32)
    spare = _spare_chunk((t % N_RING)[:, None] * TILE_CHUNKS + jnp.arange(TILE_CHUNKS, dtype=i32)[None, :])
    dst = jnp.where(valid, src, spare).astype(i32)
    dst = jnp.concatenate([spare[1:N_RING], dst], axis=0)
    return first_tile.astype(i32), tiles_e.astype(i32), src.reshape(-1), dst.reshape(-1), n_active.reshape(1)


N_RING = 4


def _expert_kernel(first_ref, ntile_ref, src_ref, dst_ref, nact_ref, xs_hbm, wg_ref, wu_ref, wd_ref, ys_hbm,
                   *scratch):
    xbufs = scratch[:N_RING]
    obufs = scratch[N_RING:2 * N_RING]
    gsem, ssem, wg16, wu16, wd16 = scratch[2 * N_RING:]
    e = pl.program_id(0)
    n_active = nact_ref[0]
    ahead = N_RING - 1

    def gather_copy(tile, slot, c):
        return pltpu.make_async_copy(xs_hbm.at[src_ref[tile * TILE_CHUNKS + c]], xbufs[slot].at[c],
                                     gsem.at[slot])

    def scatter_copy(tile, slot, c):
        return pltpu.make_async_copy(obufs[slot].at[c], ys_hbm.at[dst_ref[(tile + ahead) * TILE_CHUNKS + c]],
                                     ssem.at[slot])

    def start_all(copy, tile, slot):
        for c in range(TILE_CHUNKS):
            copy(tile, slot, c).start()

    def wait_all(copy, tile, slot):
        for c in range(TILE_CHUNKS):
            copy(tile, slot, c).wait()

    @pl.when(e == 0)
    def _():
        for v in range(ahead):
            start_all(gather_copy, v, v)
        for u in range(-ahead, 0):
            obufs[u % N_RING][...] = jnp.zeros_like(obufs[u % N_RING])
        for u in range(-ahead, -1):
            start_all(scatter_copy, u, u % N_RING)

    wg16[...] = wg_ref[0].astype(BF16)
    wu16[...] = wu_ref[0].astype(BF16)
    wd16[...] = wd_ref[0].astype(BF16)

    def do_tile(tile, slot):
        nxt = (slot + ahead) % N_RING
        old = (slot + 1) % N_RING
        wait_all(gather_copy, tile, slot)
        lo, hi = _from_chunks(xbufs[slot][...])
        x = jnp.concatenate([lo, hi], axis=1)
        gate = jnp.dot(x, wg16[...], preferred_element_type=F32)
        start_all(scatter_copy, tile - 1, nxt)
        up = jnp.dot(x, wu16[...], preferred_element_type=F32)
        hmid = (gate * _sigmoid(gate) * up).astype(BF16)
        start_all(gather_copy, tile + ahead, nxt)
        y = jnp.dot(hmid, wd16[...], preferred_element_type=F32)
        half = D_MODEL // 2
        wait_all(scatter_copy, tile - ahead, old)
        obufs[slot][...] = _to_chunks(y[:, :half], y[:, half:])

    def tile_body(k, carry):
        tile = first_ref[e] + k
        for slot in range(N_RING):
            @pl.when(tile % N_RING == slot)
            def _():
                do_tile(tile, slot)
        return carry

    lax.fori_loop(0, ntile_ref[e], tile_body, 0)

    @pl.when(e == pl.num_programs(0) - 1)
    def _():
        last = n_active - 1
        for slot in range(N_RING):
            @pl.when(last % N_RING == slot)
            def _():
                for d in range(N_RING - 2, 0, -1):
                    wait_all(scatter_copy, last - d, (slot - d) % N_RING)
                start_all(scatter_copy, last, slot)
                wait_all(scatter_copy, last, slot)
                for d in range(1, N_RING):
                    wait_all(gather_copy, last + d, (slot + d) % N_RING)


def _experts(first_tile, tiles_e, src, dst, n_active, xs, w_gate, w_up, w_down):
    chunk_shape = (2 * CHUNK, D_MODEL // 2)
    assert _spare_chunk(N_RING * TILE_CHUNKS - 1) < xs.shape[0]
    tile_buf = pltpu.VMEM((TILE_CHUNKS,) + chunk_shape, BF16)
    grid_spec = pltpu.PrefetchScalarGridSpec(
        num_scalar_prefetch=5,
        grid=(N_EXPERTS,),
        in_specs=[pl.BlockSpec(memory_space=pl.ANY),
                  pl.BlockSpec((1, D_MODEL, D_EXPERT), lambda e, *_: (e, 0, 0)),
                  pl.BlockSpec((1, D_MODEL, D_EXPERT), lambda e, *_: (e, 0, 0)),
                  pl.BlockSpec((1, D_EXPERT, D_MODEL), lambda e, *_: (e, 0, 0))],
        out_specs=pl.BlockSpec(memory_space=pl.ANY),
        scratch_shapes=[tile_buf] * (2 * N_RING) + [
                        pltpu.SemaphoreType.DMA((N_RING,)),
                        pltpu.SemaphoreType.DMA((N_RING,)),
                        pltpu.VMEM((D_MODEL, D_EXPERT), BF16),
                        pltpu.VMEM((D_MODEL, D_EXPERT), BF16),
                        pltpu.VMEM((D_EXPERT, D_MODEL), BF16)],
    )
    return pl.pallas_call(
        _expert_kernel,
        grid_spec=grid_spec,
        out_shape=jax.ShapeDtypeStruct(xs.shape, xs.dtype),
        input_output_aliases={5: 0},
        compiler_params=pltpu.CompilerParams(dimension_semantics=("arbitrary",), vmem_limit_bytes=VMEM_LIMIT),
        name="expert_ffn",
    )(first_tile, tiles_e, src, dst, n_active, xs, w_gate, w_up, w_down)


def _combine_kernel(n_prompt_steps, x2p_ref, x2s_ref, route_ref, gfin_ref, ys_ref, outp_ref, outs_ref):
    i = pl.program_id(0)
    is_prompt = i < n_prompt_steps
    outs = []
    for h in range(STEP_TILES):
        rows = pl.ds(h * TM, TM)
        lo, hi = _from_chunks(ys_ref[pl.ds(h * CAP_CHUNKS, CAP_CHUNKS)])
        route = route_ref[rows, :]
        d1 = route[:, 0:1].astype(jnp.int32)
        d2 = route[:, 1:2].astype(jnp.int32)
        w1 = route[:, 2:3]
        w2 = route[:, 3:4]
        dest = lax.broadcasted_iota(jnp.int32, (TM, CAP_ROWS), 1)
        sel1 = (dest == d1).astype(F32).astype(BF16)
        sel2 = (dest == d2).astype(F32).astype(BF16)
        m1 = jnp.concatenate([jnp.dot(sel1, lo, preferred_element_type=F32),
                              jnp.dot(sel1, hi, preferred_element_type=F32)], axis=1)
        m2 = jnp.concatenate([jnp.dot(sel2, lo, preferred_element_type=F32),
                              jnp.dot(sel2, hi, preferred_element_type=F32)], axis=1)
        x = jnp.where(is_prompt, x2p_ref[rows, :], x2s_ref[rows, :])
        outs.append(_rms(x + (w1 * m1 + w2 * m2), gfin_ref[...]))

    @pl.when(is_prompt)
    def _():
        for h in range(STEP_TILES):
            outp_ref[pl.ds(h * TM, TM), :] = outs[h]

    @pl.when(jnp.logical_not(is_prompt))
    def _():
        for h in range(STEP_TILES):
            outs_ref[pl.ds(h * TM, TM), :] = outs[h]


def _combine(x2p, x2s, route, g_final, ys):
    n_prompt_tiles = x2p.shape[0] // TM
    n_tiles = n_prompt_tiles + x2s.shape[0] // TM
    assert n_prompt_tiles % STEP_TILES == 0 and n_tiles % STEP_TILES == 0
    n_prompt_steps = n_prompt_tiles // STEP_TILES
    rows = STEP_TILES * TM
    p_spec = pl.BlockSpec((rows, D_MODEL), lambda i: (jnp.minimum(i, n_prompt_steps - 1), 0))
    s_spec = pl.BlockSpec((rows, D_MODEL), lambda i: (jnp.maximum(i - n_prompt_steps, 0), 0))
    return pl.pallas_call(
        functools.partial(_combine_kernel, n_prompt_steps),
        grid=(n_tiles // STEP_TILES,),
        in_specs=[p_spec, s_spec,
                  pl.BlockSpec((rows, LANES), lambda i: (i, 0)),
                  _const_spec((1, D_MODEL)),
                  pl.BlockSpec((STEP_TILES * CAP_CHUNKS, 2 * CHUNK, D_MODEL // 2), lambda i: (i, 0, 0))],
        out_specs=[p_spec, s_spec],
        out_shape=[jax.ShapeDtypeStruct(x2p.shape, F32), jax.ShapeDtypeStruct(x2s.shape, F32)],
        compiler_params=pltpu.CompilerParams(dimension_semantics=("arbitrary",), vmem_limit_bytes=VMEM_LIMIT),
        name="combine_norm",
    )(x2p, x2s, route, g_final, ys)


def _block_diag_in(bb):
    t = bb.reshape(2, 16, SSM_STATE, SSM_GROUP).transpose(0, 1, 3, 2)
    eye = jnp.eye(16, dtype=bb.dtype)
    blk = t[:, :, :, None, :] * eye[None, :, None, :, None]
    return blk.reshape(2, 16 * SSM_GROUP, 16 * SSM_STATE)


def _block_diag_out(c):
    t = c.reshape(2, 16, SSM_GROUP, SSM_STATE).transpose(0, 1, 3, 2)
    eye = jnp.eye(16, dtype=c.dtype)
    blk = t[:, :, :, None, :] * eye[None, :, None, :, None]
    return blk.reshape(2, 16 * SSM_STATE, 16 * SSM_GROUP)


def kernel(x_prompt, x_sample, state_ssm_re, state_ssm_im, cache_pool, g_mix, w_in, ssm_a_re, ssm_a_im,
           ssm_log_dt, ssm_b_re, ssm_b_im, ssm_c_re, ssm_c_im, ssm_d, w_glu_a, w_glu_b, pool_w, pool_scale,
           w_out, g_ffn, w_router_group, w_router_expert, w_exp_gate, w_exp_up, w_exp_down, g_final):
    li = 0
    n_pb, seq_p, _ = x_prompt.shape
    n_sb, seq_s, _ = x_sample.shape

    a_re, a_im, a32_re, a32_im, bb_re, bb_im = _discretise(
        ssm_a_re[li], ssm_a_im[li], ssm_log_dt[li], ssm_b_re[li], ssm_b_im[li])
    row = lambda v: v.reshape(1, N_FLAT)
    wb = jnp.concatenate([_block_diag_in(bb_re), _block_diag_in(bb_im)], axis=2).astype(BF16)
    weights = [
        g_mix[li].reshape(1, D_MODEL),
        w_in[li].astype(BF16),
        wb,
        _block_diag_out(ssm_c_re[li]).astype(BF16),
        _block_diag_out(ssm_c_im[li]).astype(BF16),
        ssm_d[li].reshape(1, SSM_WIDTH),
        jnp.concatenate([w_glu_a[li], w_glu_b[li]], axis=1).astype(BF16),
        pool_w[li].astype(BF16),
        pool_scale[li].reshape(1, D_MODEL),
        w_out[li].astype(BF16),
        row(a_re), row(a_im), row(a32_re), row(a32_im),
    ]

    x2p, stp_re, stp_im, histp = _mixer_prompt(x_prompt, weights)
    x2p = x2p.reshape(n_pb * seq_p, D_MODEL)

    n_stiles = n_sb // N_CHAIN
    cache16 = jnp.pad(cache_pool[li], ((0, 0), (1, 0), (0, 0)))
    cache_t = cache16.reshape(n_stiles, N_CHAIN, 16, POOL_WIDTH).transpose(0, 2, 1, 3).reshape(
        n_stiles, HIST_ROWS, POOL_WIDTH)
    x2s, sts_re, sts_im, hists = _mixer_sample(
        x_sample, state_ssm_re[li].reshape(n_sb, N_FLAT), state_ssm_im[li].reshape(n_sb, N_FLAT),
        cache_t, weights)
    x2s = x2s.reshape(n_sb * seq_s, D_MODEL)

    w_router = jnp.concatenate(
        [w_router_group[li], w_router_expert[li].reshape(D_MODEL, N_EXPERTS),
         jnp.zeros((D_MODEL, LANES - N_EXPERT_GROUPS - N_EXPERTS), F32)], axis=1).astype(BF16)
    xs, route, nch = _route(x2p, x2s, g_ffn[li].reshape(1, D_MODEL), w_router)

    n_tiles = (x2p.shape[0] + x2s.shape[0]) // TM
    max_chunks = n_tiles * (2 * TM // CHUNK + N_EXPERTS)
    n_expert_tiles = max_chunks // TILE_CHUNKS + N_EXPERTS
    first_tile, tiles_e, src, dst, n_active = _chunk_tables(nch[:, 0, :N_EXPERTS], n_expert_tiles)
    ys = _experts(first_tile, tiles_e, src, dst, n_active, xs, w_exp_gate[li], w_exp_up[li], w_exp_down[li])
    yp, ysm = _combine(x2p, x2s, route, g_final.reshape(1, D_MODEL), ys)

    sd = state_ssm_re.dtype
    cd = cache_pool.dtype
    y_prompt = yp.reshape(n_pb, seq_p, D_MODEL)
    y_sample = ysm.reshape(n_sb, seq_s, D_MODEL)
    re_p = stp_re.reshape(1, n_pb, SSM_GROUPS, SSM_STATE).astype(sd)
    im_p = stp_im.reshape(1, n_pb, SSM_GROUPS, SSM_STATE).astype(sd)
    hist_p = histp[:, ::N_CHAIN, :][:, 1:, :].reshape(1, n_pb, POOL_HIST, POOL_WIDTH).astype(cd)
    re_s = sts_re.reshape(1, n_sb, SSM_GROUPS, SSM_STATE).astype(sd)
    im_s = sts_im.reshape(1, n_sb, SSM_GROUPS, SSM_STATE).astype(sd)
    hist_s = hists.reshape(n_stiles, 16, N_CHAIN, POOL_WIDTH).transpose(0, 2, 1, 3).reshape(
        n_sb, 16, POOL_WIDTH)[:, 1:, :].reshape(1, n_sb, POOL_HIST, POOL_WIDTH).astype(cd)
    return (y_prompt, y_sample, re_p, im_p, hist_p, re_s, im_s, hist_s)
```

```python
import functools
import math

import jax
import jax.numpy as jnp
from jax import lax
from jax.experimental import pallas as pl
from jax.experimental.pallas import tpu as pltpu

F32 = jnp.float32
BF16 = jnp.bfloat16

D_MODEL = 1024
SSM_WIDTH = 512
SSM_GROUPS = 32
SSM_GROUP = 16
SSM_STATE = 64
N_FLAT = SSM_GROUPS * SSM_STATE
HALF_FLAT = N_FLAT // 2
POOL_WIDTH = 512
POOL_WINDOWS = (2, 4, 8, 16)
POOL_GROUP_IN = 128
POOL_GROUP_OUT = 256
POOL_HIST = 15
N_EXPERTS = 32
EXPERTS_PER_GROUP = 8
N_EXPERT_GROUPS = 4
D_EXPERT = 512
EPS = 1e-6
PAST_LEN = 1024

TM = 256
N_CHAIN = 8
CHAIN_LEN = TM // N_CHAIN
HIST_ROWS = 16 * N_CHAIN

CHUNK = 8
TILE_CHUNKS = TM // CHUNK
CAP_CHUNKS = 96
CAP_ROWS = CAP_CHUNKS * CHUNK
MAX_USED_CHUNKS = 2 * TM // CHUNK + N_EXPERTS * (CHUNK - 1) // CHUNK
N_SPARE_SLOTS = CAP_CHUNKS - MAX_USED_CHUNKS - 1
ZERO_CHUNK = CAP_CHUNKS - 1
LANES = 128

VMEM_LIMIT = 52 * 1024 * 1024
STEP_TILES = 4


def _rms(x, g):
    r = lax.rsqrt(jnp.mean(x * x, axis=-1, keepdims=True) + EPS)
    return x * r * g


def _sigmoid(x):
    return 0.5 * jnp.tanh(0.5 * x) + 0.5


def _gelu_tanh(x):
    c = math.sqrt(2.0 / math.pi)
    return x * (0.5 * (1.0 + jnp.tanh(c * (x + 0.044715 * (x * x * x)))))


def _disc_kernel(lre_ref, lim_ref, ldt_ref, bre_ref, bim_ref,
                 are_ref, aim_ref, a32re_ref, a32im_ref, bbre_ref, bbim_ref):
    lam_re = jnp.minimum(lre_ref[...], -1e-4)
    lam_im = lim_ref[...]
    dt = jnp.exp(ldt_ref[...])
    mag = jnp.exp(lam_re * dt)
    ang = lam_im * dt
    a_re = mag * jnp.cos(ang)
    a_im = mag * jnp.sin(ang)
    num_re = a_re - 1.0
    num_im = a_im
    den = lam_re * lam_re + lam_im * lam_im
    k_re = ((num_re * lam_re + num_im * lam_im) / den)[:, None, :]
    k_im = ((num_im * lam_re - num_re * lam_im) / den)[:, None, :]
    br = bre_ref[...]
    bi = bim_ref[...]
    bbre_ref[...] = k_re * br - k_im * bi
    bbim_ref[...] = k_re * bi + k_im * br
    are_ref[...] = a_re
    aim_ref[...] = a_im
    pr, pi = a_re, a_im
    for _ in range(int(math.log2(CHAIN_LEN))):
        pr, pi = pr * pr - pi * pi, 2.0 * pr * pi
    a32re_ref[...] = pr
    a32im_ref[...] = pi


def _discretise(a_re, a_im, log_dt, b_re, b_im):
    chan_major = lambda b: jnp.swapaxes(b, 1, 2)
    return pl.pallas_call(
        _disc_kernel,
        out_shape=[jax.ShapeDtypeStruct((SSM_GROUPS, SSM_STATE), F32)] * 4
        + [jax.ShapeDtypeStruct((SSM_GROUPS, SSM_GROUP, SSM_STATE), F32)] * 2,
        name="s5_discretise",
    )(a_re, a_im, log_dt.reshape(SSM_GROUPS, 1), chan_major(b_re), chan_major(b_im))


def _scan_half(hbuf, h, ar, ai, init_re, init_im, store):
    cre = h * N_FLAT
    cim = cre + HALF_FLAT
    hr, hi = init_re, init_im
    for t in range(CHAIN_LEN):
        rows = pl.ds(N_CHAIN * t, N_CHAIN)
        br = hbuf[rows, cre:cre + HALF_FLAT]
        bi = hbuf[rows, cim:cim + HALF_FLAT]
        nr = ar * hr - ai * hi + br
        ni = ar * hi + ai * hr + bi
        if store:
            hbuf[rows, cre:cre + HALF_FLAT] = nr
            hbuf[rows, cim:cim + HALF_FLAT] = ni
        hr, hi = nr, ni
    return hr, hi


N_MIXER_WEIGHTS = 14
N_PAR = 2


def _mixer_kernel(is_prompt, tiles_per_stream, *refs):
    n_in = 1 if is_prompt else 4
    ins = refs[:n_in]
    (gmix, win, wb, wcre, wcim, dsk, wglu, poolw, pscale, wout, are, aim, a32re, a32im) = refs[
        n_in:n_in + N_MIXER_WEIGHTS]
    x2_ref, stre_ref, stim_ref, hist_ref = refs[n_in + N_MIXER_WEIGHTS:n_in + N_MIXER_WEIGHTS + 4]
    scratch = refs[n_in + N_MIXER_WEIGHTS + 4:]
    per = len(scratch) // N_PAR
    lanes = [scratch[p * per:(p + 1) * per] for p in range(N_PAR)]
    x_ref = ins[0]

    tile_in_stream = pl.program_id(0) % tiles_per_stream
    n_lane_blocks = D_MODEL // LANES
    half_w = SSM_WIDTH // 2
    ar_full = are[...]
    ai_full = aim[...]

    if is_prompt:
        @pl.when(pl.program_id(0) == 0)
        def _():
            for p in range(N_PAR):
                cre_s, cim_s, pcarry = lanes[p][8], lanes[p][9], lanes[p][10]
                cre_s[...] = jnp.zeros_like(cre_s)
                cim_s[...] = jnp.zeros_like(cim_s)
                pcarry[...] = jnp.zeros_like(pcarry)

    def a_half(h):
        f0 = h * HALF_FLAT
        return (jnp.broadcast_to(ar_full[:, f0:f0 + HALF_FLAT], (N_CHAIN, HALF_FLAT)),
                jnp.broadcast_to(ai_full[:, f0:f0 + HALF_FLAT], (N_CHAIN, HALF_FLAT)))

    def project(p):
        xperm, hbuf = lanes[p][0], lanes[p][1]
        for j in range(N_CHAIN):
            for cb in range(n_lane_blocks):
                xperm[cb, pl.ds(j, CHAIN_LEN, stride=N_CHAIN), :] = x_ref[
                    p, CHAIN_LEN * j:CHAIN_LEN * (j + 1), cb * LANES:(cb + 1) * LANES]
        x = jnp.concatenate([xperm[cb] for cb in range(n_lane_blocks)], axis=1)
        xn = _rms(x, gmix[...]).astype(BF16)
        proj = jnp.dot(xn, win[...], preferred_element_type=F32)
        ub = proj[:, :SSM_WIDTH].astype(BF16)
        for h in range(2):
            hbuf[:, h * N_FLAT:(h + 1) * N_FLAT] = jnp.dot(
                ub[:, h * half_w:(h + 1) * half_w], wb[h], preferred_element_type=F32)
        return x, proj

    def recur(p):
        hbuf = lanes[p][1]
        if is_prompt:
            fre, fim, hre, him, cre_s, cim_s = lanes[p][4:10]
            zeros = jnp.zeros((N_CHAIN, HALF_FLAT), F32)
            for h in range(2):
                f0 = h * HALF_FLAT
                ar, ai = a_half(h)
                lr, li = _scan_half(hbuf, h, ar, ai, zeros, zeros, store=False)
                fre[:, f0:f0 + HALF_FLAT] = lr
                fim[:, f0:f0 + HALF_FLAT] = li
            fresh = tile_in_stream == 0
            hre[0:1, :] = jnp.where(fresh, 0.0, cre_s[...])
            him[0:1, :] = jnp.where(fresh, 0.0, cim_s[...])
            p_re = a32re[...]
            p_im = a32im[...]
            for j in range(N_CHAIN - 1):
                sr = hre[j:j + 1, :]
                si = him[j:j + 1, :]
                hre[j + 1:j + 2, :] = fre[j:j + 1, :] + p_re * sr - p_im * si
                him[j + 1:j + 2, :] = fim[j:j + 1, :] + p_re * si + p_im * sr
            init_re = hre[...]
            init_im = him[...]
        else:
            init_re = ins[1][p]
            init_im = ins[2][p]
        fin_re = []
        fin_im = []
        for h in range(2):
            f0 = h * HALF_FLAT
            ar, ai = a_half(h)
            er, ei = _scan_half(hbuf, h, ar, ai, init_re[:, f0:f0 + HALF_FLAT],
                                init_im[:, f0:f0 + HALF_FLAT], store=True)
            fin_re.append(er)
            fin_im.append(ei)
        end_re = jnp.concatenate(fin_re, axis=1)
        end_im = jnp.concatenate(fin_im, axis=1)
        if is_prompt:
            cre_s[...] = end_re[N_CHAIN - 1:N_CHAIN, :]
            cim_s[...] = end_im[N_CHAIN - 1:N_CHAIN, :]
            stre_ref[p] = end_re[N_CHAIN - 1:N_CHAIN, :]
            stim_ref[p] = end_im[N_CHAIN - 1:N_CHAIN, :]
        else:
            stre_ref[p] = end_re
            stim_ref[p] = end_im

    def finish(p, x, proj):
        hbuf, res, xpbuf = lanes[p][1], lanes[p][2], lanes[p][3]
        u_s = proj[:, :SSM_WIDTH]
        u_p = proj[:, SSM_WIDTH:SSM_WIDTH + POOL_WIDTH]
        gate_s = proj[:, SSM_WIDTH + POOL_WIDTH:SSM_WIDTH + POOL_WIDTH + D_MODEL]
        gate_p = proj[:, SSM_WIDTH + POOL_WIDTH + D_MODEL:]
        ys = []
        for h in range(2):
            c0 = h * N_FLAT
            h_re = hbuf[:, c0:c0 + HALF_FLAT].astype(BF16)
            h_im = hbuf[:, c0 + HALF_FLAT:c0 + N_FLAT].astype(BF16)
            ys.append(jnp.dot(h_re, wcre[h], preferred_element_type=F32)
                      - jnp.dot(h_im, wcim[h], preferred_element_type=F32))
        y = jnp.concatenate(ys, axis=1) + dsk[...] * u_s
        g = _gelu_tanh(y).astype(BF16)
        glu = jnp.dot(g, wglu[...], preferred_element_type=F32)
        o_s = glu[:, :D_MODEL] * _sigmoid(glu[:, D_MODEL:])

        xpbuf[HIST_ROWS:HIST_ROWS + TM, :] = u_p
        tail = u_p[TM - HIST_ROWS:, :]
        row = lax.broadcasted_iota(jnp.int32, (TM, 1), 0)
        if is_prompt:
            pcarry = lanes[p][10]
            first_chain = (lax.broadcasted_iota(jnp.int32, (HIST_ROWS, POOL_WIDTH), 0) % N_CHAIN) == 0
            carried = jnp.where(tile_in_stream == 0, 0.0, pcarry[...])
            xpbuf[0:HIST_ROWS, :] = jnp.where(first_chain, carried, pltpu.roll(tail, 1, 0))
            new_carry = pltpu.roll(tail, HIST_ROWS - (N_CHAIN - 1), 0)
            pcarry[...] = new_carry
            hist_ref[p] = new_carry
            pos1 = tile_in_stream * TM + CHAIN_LEN * (row % N_CHAIN) + row // N_CHAIN + 1
        else:
            xpbuf[0:HIST_ROWS, :] = ins[3][p]
            hist_ref[p] = tail
            pos1 = PAST_LEN + row // N_CHAIN + 1

        o_ps = []
        for gi, w in enumerate(POOL_WINDOWS):
            c0 = gi * POOL_GROUP_IN
            acc = xpbuf[HIST_ROWS:HIST_ROWS + TM, c0:c0 + POOL_GROUP_IN]
            for k in range(1, w):
                acc = acc + xpbuf[HIST_ROWS - N_CHAIN * k:HIST_ROWS - N_CHAIN * k + TM, c0:c0 + POOL_GROUP_IN]
            cnt = jnp.minimum(w, pos1).astype(F32)
            pooled = acc / cnt
            z = (pooled - u_p[:, c0:c0 + POOL_GROUP_IN]).astype(BF16)
            o_ps.append(jnp.dot(z, poolw[gi], preferred_element_type=F32))
        o_p = jnp.concatenate(o_ps, axis=1) * pscale[...]

        merged = (_sigmoid(gate_s) * o_s + _sigmoid(gate_p) * o_p).astype(BF16)
        x2 = x + jnp.dot(merged, wout[...], preferred_element_type=F32)
        for cb in range(n_lane_blocks):
            res[cb] = x2[:, cb * LANES:(cb + 1) * LANES]
        for j in range(N_CHAIN):
            for cb in range(n_lane_blocks):
                x2_ref[p, CHAIN_LEN * j:CHAIN_LEN * (j + 1), cb * LANES:(cb + 1) * LANES] = res[
                    cb, pl.ds(j, CHAIN_LEN, stride=N_CHAIN), :]

    projected = [None] * N_PAR
    for k in range(N_PAR + 2):
        if 0 <= k - 2 < N_PAR:
            finish(k - 2, *projected[k - 2])
        if 0 <= k - 1 < N_PAR:
            recur(k - 1)
        if k < N_PAR:
            projected[k] = project(k)


def _const_spec(shape):
    nd = len(shape)
    return pl.BlockSpec(shape, lambda i, _nd=nd: (0,) * _nd)


def _mixer_weight_specs():
    return [
        _const_spec((1, D_MODEL)),
        _const_spec((D_MODEL, 3 * D_MODEL)),
        _const_spec((2, SSM_WIDTH // 2, N_FLAT)),
        _const_spec((2, HALF_FLAT, SSM_WIDTH // 2)),
        _const_spec((2, HALF_FLAT, SSM_WIDTH // 2)),
        _const_spec((1, SSM_WIDTH)),
        _const_spec((SSM_WIDTH, 2 * D_MODEL)),
        _const_spec((len(POOL_WINDOWS), POOL_GROUP_IN, POOL_GROUP_OUT)),
        _const_spec((1, D_MODEL)),
        _const_spec((D_MODEL, D_MODEL)),
        _const_spec((1, N_FLAT)),
        _const_spec((1, N_FLAT)),
        _const_spec((1, N_FLAT)),
        _const_spec((1, N_FLAT)),
    ]


def _mixer_common_scratch():
    return [
        pltpu.VMEM((D_MODEL // LANES, TM, LANES), F32),
        pltpu.VMEM((TM, 2 * N_FLAT), F32),
        pltpu.VMEM((D_MODEL // LANES, TM, LANES), F32),
        pltpu.VMEM((HIST_ROWS + TM, POOL_WIDTH), F32),
    ]


def _mixer_prompt(x, weights):
    n_streams, seq, _ = x.shape
    assert n_streams % N_PAR == 0 and seq % TM == 0
    tiles_per_stream = seq // TM
    blk = lambda shape: pl.BlockSpec(shape, lambda i: (i // tiles_per_stream, 0, 0))
    row_spec = pl.BlockSpec((N_PAR, TM, D_MODEL), lambda i: (i // tiles_per_stream, i % tiles_per_stream, 0))
    lane_scratch = _mixer_common_scratch() + [
        pltpu.VMEM((N_CHAIN, N_FLAT), F32), pltpu.VMEM((N_CHAIN, N_FLAT), F32),
        pltpu.VMEM((N_CHAIN, N_FLAT), F32), pltpu.VMEM((N_CHAIN, N_FLAT), F32),
        pltpu.VMEM((1, N_FLAT), F32), pltpu.VMEM((1, N_FLAT), F32),
        pltpu.VMEM((HIST_ROWS, POOL_WIDTH), F32),
    ]
    return pl.pallas_call(
        functools.partial(_mixer_kernel, True, tiles_per_stream),
        grid=(n_streams // N_PAR * tiles_per_stream,),
        in_specs=[row_spec] + _mixer_weight_specs(),
        out_specs=[row_spec, blk((N_PAR, 1, N_FLAT)), blk((N_PAR, 1, N_FLAT)),
                   blk((N_PAR, HIST_ROWS, POOL_WIDTH))],
        out_shape=[jax.ShapeDtypeStruct((n_streams, seq, D_MODEL), F32),
                   jax.ShapeDtypeStruct((n_streams, 1, N_FLAT), F32),
                   jax.ShapeDtypeStruct((n_streams, 1, N_FLAT), F32),
                   jax.ShapeDtypeStruct((n_streams, HIST_ROWS, POOL_WIDTH), F32)],
        scratch_shapes=lane_scratch * N_PAR,
        compiler_params=pltpu.CompilerParams(dimension_semantics=("arbitrary",), vmem_limit_bytes=VMEM_LIMIT),
        name="mixer_prompt",
    )(x, *weights)


def _mixer_sample(x, h0_re, h0_im, cache_t, weights):
    n_streams, seq, _ = x.shape
    assert seq == CHAIN_LEN and n_streams % (N_CHAIN * N_PAR) == 0
    n_tiles = n_streams // N_CHAIN
    blk = lambda shape: pl.BlockSpec(shape, lambda i: (i, 0, 0))
    row_spec = blk((N_PAR, TM, D_MODEL))
    st_spec = blk((N_PAR, N_CHAIN, N_FLAT))
    hist_spec = blk((N_PAR, HIST_ROWS, POOL_WIDTH))
    tiles = lambda v: v.reshape((n_tiles, -1) + v.shape[-1:])
    return pl.pallas_call(
        functools.partial(_mixer_kernel, False, 1),
        grid=(n_tiles // N_PAR,),
        in_specs=[row_spec, st_spec, st_spec, hist_spec] + _mixer_weight_specs(),
        out_specs=[row_spec, st_spec, st_spec, hist_spec],
        out_shape=[jax.ShapeDtypeStruct((n_tiles, TM, D_MODEL), F32),
                   jax.ShapeDtypeStruct((n_tiles, N_CHAIN, N_FLAT), F32),
                   jax.ShapeDtypeStruct((n_tiles, N_CHAIN, N_FLAT), F32),
                   jax.ShapeDtypeStruct((n_tiles, HIST_ROWS, POOL_WIDTH), F32)],
        scratch_shapes=_mixer_common_scratch() * N_PAR,
        compiler_params=pltpu.CompilerParams(dimension_semantics=("arbitrary",), vmem_limit_bytes=VMEM_LIMIT),
        name="mixer_sample",
    )(tiles(x), tiles(h0_re), tiles(h0_im), cache_t, *weights)


def _to_chunks(lo, hi):
    n = lo.shape[0] // CHUNK
    half = D_MODEL // 2
    both = jnp.concatenate([lo.reshape(n, CHUNK, half), hi.reshape(n, CHUNK, half)], axis=1)
    return both.astype(BF16)


def _from_chunks(blk):
    n = blk.shape[0]
    half = D_MODEL // 2
    f = blk.astype(F32)
    lo = f[:, :CHUNK, :].reshape(n * CHUNK, half).astype(BF16)
    hi = f[:, CHUNK:, :].reshape(n * CHUNK, half).astype(BF16)
    return lo, hi


def _route_kernel(n_prompt_steps, x2p_ref, x2s_ref, g_ref, wr_ref, ltri_ref, utri_ref,
                  xs_ref, route_ref, nch_ref):
    is_prompt = pl.program_id(0) < n_prompt_steps
    for h in range(STEP_TILES):
        _route_tile(h, is_prompt, x2p_ref, x2s_ref, g_ref, wr_ref, ltri_ref, utri_ref, xs_ref, route_ref, nch_ref)


def _route_tile(h, is_prompt, x2p_ref, x2s_ref, g_ref, wr_ref, ltri_ref, utri_ref, xs_ref, route_ref, nch_ref):
    rows = pl.ds(h * TM, TM)
    x = jnp.where(is_prompt, x2p_ref[rows, :], x2s_ref[rows, :])
    xn = _rms(x, g_ref[...]).astype(BF16)
    logits = jnp.dot(xn, wr_ref[...], preferred_element_type=F32)
    lane = lax.broadcasted_iota(jnp.int32, (TM, LANES), 1)
    lane_f = lane.astype(F32)
    big = jnp.float32(1 << 20)
    neg = jnp.float32(-jnp.inf)

    gmask = lane < N_EXPERT_GROUPS
    m = jnp.max(jnp.where(gmask, logits, neg), axis=1, keepdims=True)
    grp = jnp.min(jnp.where(gmask & (logits == m), lane_f, big), axis=1, keepdims=True)
    wg = 1.0 / jnp.sum(jnp.where(gmask, jnp.exp(logits - m), 0.0), axis=1, keepdims=True)

    eid = lane - N_EXPERT_GROUPS
    lane_grp = (eid >> 3).astype(F32)
    emask = (eid >= 0) & (eid < N_EXPERTS) & (lane_grp == grp)
    v1 = jnp.max(jnp.where(emask, logits, neg), axis=1, keepdims=True)
    i1 = jnp.min(jnp.where(emask & (logits == v1), lane_f, big), axis=1, keepdims=True)
    emask2 = emask & (lane_f != i1)
    v2 = jnp.max(jnp.where(emask2, logits, neg), axis=1, keepdims=True)
    i2 = jnp.min(jnp.where(emask2 & (logits == v2), lane_f, big), axis=1, keepdims=True)
    e21 = jnp.exp(v2 - v1)
    w1 = wg / (1.0 + e21)
    w2 = wg * e21 / (1.0 + e21)

    a1 = lane_f == (i1 - N_EXPERT_GROUPS)
    a2 = lane_f == (i2 - N_EXPERT_GROUPS)
    a = (a1 | a2).astype(F32)
    before = jnp.dot(ltri_ref[...], a.astype(BF16), preferred_element_type=F32)
    cnt = jnp.sum(a, axis=0, keepdims=True)
    nch = jnp.floor((cnt + (CHUNK - 1)) * (1.0 / CHUNK))
    nch16 = jnp.broadcast_to(nch, (16, LANES))
    start = jnp.dot(nch16.astype(BF16), utri_ref[...], preferred_element_type=F32)
    slot = before + CHUNK * start[0:1, :]
    d1 = jnp.sum(jnp.where(a1, slot, 0.0), axis=1, keepdims=True)
    d2 = jnp.sum(jnp.where(a2, slot, 0.0), axis=1, keepdims=True)

    route = jnp.where(lane == 0, d1, jnp.where(lane == 1, d2, jnp.where(lane == 2, w1,
                      jnp.where(lane == 3, w2, 0.0))))
    route_ref[rows, :] = route
    nch_ref[h] = nch16[0:8, :].astype(jnp.int32)

    dt = jnp.transpose(jnp.where(lane < 2, route, 0.0)).astype(jnp.int32)
    d1row = dt[0:1, :]
    d2row = dt[1:2, :]
    dest = lax.broadcasted_iota(jnp.int32, (CAP_ROWS, TM), 0)
    perm = ((dest == d1row) | (dest == d2row)).astype(F32).astype(BF16)
    half = D_MODEL // 2
    lo = jnp.dot(perm, xn[:, :half], preferred_element_type=F32)
    hi = jnp.dot(perm, xn[:, half:], preferred_element_type=F32)
    xs_ref[pl.ds(h * CAP_CHUNKS, CAP_CHUNKS)] = _to_chunks(lo, hi)


def _route(x2p, x2s, g_ffn, w_router):
    n_prompt_tiles = x2p.shape[0] // TM
    n_tiles = n_prompt_tiles + x2s.shape[0] // TM
    assert n_prompt_tiles % STEP_TILES == 0 and n_tiles % STEP_TILES == 0
    n_prompt_steps = n_prompt_tiles // STEP_TILES
    rows = STEP_TILES * TM
    r = jnp.arange(TM)
    ltri = (r[None, :] < r[:, None]).astype(BF16)
    e = jnp.arange(LANES)
    utri = (e[:, None] < e[None, :]).astype(BF16)
    return pl.pallas_call(
        functools.partial(_route_kernel, n_prompt_steps),
        grid=(n_tiles // STEP_TILES,),
        in_specs=[pl.BlockSpec((rows, D_MODEL), lambda i: (jnp.minimum(i, n_prompt_steps - 1), 0)),
                  pl.BlockSpec((rows, D_MODEL), lambda i: (jnp.maximum(i - n_prompt_steps, 0), 0)),
                  _const_spec((1, D_MODEL)), _const_spec((D_MODEL, LANES)),
                  _const_spec((TM, TM)), _const_spec((LANES, LANES))],
        out_specs=[pl.BlockSpec((STEP_TILES * CAP_CHUNKS, 2 * CHUNK, D_MODEL // 2), lambda i: (i, 0, 0)),
                   pl.BlockSpec((rows, LANES), lambda i: (i, 0)),
                   pl.BlockSpec((STEP_TILES, 8, LANES), lambda i: (i, 0, 0))],
        out_shape=[jax.ShapeDtypeStruct((n_tiles * CAP_CHUNKS, 2 * CHUNK, D_MODEL // 2), BF16),
                   jax.ShapeDtypeStruct((n_tiles * TM, LANES), F32),
                   jax.ShapeDtypeStruct((n_tiles, 8, LANES), jnp.int32)],
        compiler_params=pltpu.CompilerParams(dimension_semantics=("arbitrary",), vmem_limit_bytes=VMEM_LIMIT),
        name="route_sort",
    )(x2p, x2s, g_ffn, w_router, ltri, utri)


def _spare_chunk(k):
    return (1 + k // N_SPARE_SLOTS) * CAP_CHUNKS + MAX_USED_CHUNKS + k % N_SPARE_SLOTS


def _chunk_tables(nch, n_expert_tiles):
    n_tiles = nch.shape[0]
    i32 = jnp.int32
    start = jnp.cumsum(nch, axis=1) - nch
    off = jnp.cumsum(nch, axis=0) - nch
    per_expert = jnp.sum(nch, axis=0)
    tiles_e = (per_expert + TILE_CHUNKS - 1) // TILE_CHUNKS
    cum_tiles = jnp.cumsum(tiles_e)
    first_tile = cum_tiles - tiles_e
    n_active = cum_tiles[-1].astype(i32)
    t = jnp.arange(n_expert_tiles, dtype=i32)
    te = jnp.minimum(jnp.sum((t[:, None] >= cum_tiles[None, :]).astype(i32), axis=1), N_EXPERTS - 1)
    onehot = te[:, None] == jnp.arange(N_EXPERTS, dtype=i32)[None, :]
    pick = lambda tab: jnp.sum(jnp.where(onehot[:, :, None], tab.T[None, :, :], 0), axis=1)
    off_t, nch_t, start_t = pick(off), pick(nch), pick(start)
    k = t - jnp.sum(jnp.where(onehot, first_tile[None, :], 0), axis=1)
    q = (TILE_CHUNKS * k)[:, None] + jnp.arange(TILE_CHUNKS, dtype=i32)[None, :]
    in_run = (off_t[:, None, :] <= q[:, :, None]) & (q[:, :, None] < (off_t + nch_t)[:, None, :])
    run_src = (jnp.arange(n_tiles, dtype=i32) * CAP_CHUNKS)[None, :] + start_t - off_t
    src = jnp.sum(jnp.where(in_run, run_src[:, None, :], 0), axis=-1) + q
    valid = jnp.any(in_run, axis=-1) & (t < n_active)[:, None]
    src = jnp.where(valid, src, ZERO_CHUNK).astype(i32)
    spare = _spare_chunk((t % N_RING)[:, None] * TILE_CHUNKS + jnp.arange(TILE_CHUNKS, dtype=i32)[None, :])
    dst = jnp.where(valid, src, spare).astype(i32)
    dst = jnp.concatenate([spare[1:N_RING], dst], axis=0)
    return first_tile.astype(i32), tiles_e.astype(i32), src.reshape(-1), dst.reshape(-1), n_active.reshape(1)


N_RING = 5


def _expert_kernel(first_ref, ntile_ref, src_ref, dst_ref, nact_ref, xs_hbm, wg_ref, wu_ref, wd_ref, ys_hbm,
                   *scratch):
    xbufs = scratch[:N_RING]
    obufs = scratch[N_RING:2 * N_RING]
    gsem, ssem, wg16, wu16, wd16 = scratch[2 * N_RING:]
    e = pl.program_id(0)
    n_active = nact_ref[0]
    ahead = N_RING - 1

    def gather_copy(tile, slot, c):
        return pltpu.make_async_copy(xs_hbm.at[src_ref[tile * TILE_CHUNKS + c]], xbufs[slot].at[c],
                                     gsem.at[slot])

    def scatter_copy(tile, slot, c):
        return pltpu.make_async_copy(obufs[slot].at[c], ys_hbm.at[dst_ref[(tile + ahead) * TILE_CHUNKS + c]],
                                     ssem.at[slot])

    def start_all(copy, tile, slot):
        for c in range(TILE_CHUNKS):
            copy(tile, slot, c).start()

    def wait_all(copy, tile, slot):
        for c in range(TILE_CHUNKS):
            copy(tile, slot, c).wait()

    @pl.when(e == 0)
    def _():
        for v in range(ahead):
            start_all(gather_copy, v, v)
        for u in range(-ahead, 0):
            obufs[u % N_RING][...] = jnp.zeros_like(obufs[u % N_RING])
        for u in range(-ahead, -1):
            start_all(scatter_copy, u, u % N_RING)

    wg16[...] = wg_ref[0].astype(BF16)
    wu16[...] = wu_ref[0].astype(BF16)
    wd16[...] = wd_ref[0].astype(BF16)

    def do_tile(tile, slot):
        nxt = (slot + ahead) % N_RING
        old = (slot + 1) % N_RING
        wait_all(gather_copy, tile, slot)
        lo, hi = _from_chunks(xbufs[slot][...])
        x = jnp.concatenate([lo, hi], axis=1)
        gate = jnp.dot(x, wg16[...], preferred_element_type=F32)
        start_all(scatter_copy, tile - 1, nxt)
        up = jnp.dot(x, wu16[...], preferred_element_type=F32)
        hmid = (gate * _sigmoid(gate) * up).astype(BF16)
        start_all(gather_copy, tile + ahead, nxt)
        y = jnp.dot(hmid, wd16[...], preferred_element_type=F32)
        half = D_MODEL // 2
        wait_all(scatter_copy, tile - ahead, old)
        obufs[slot][...] = _to_chunks(y[:, :half], y[:, half:])

    def tile_body(k, carry):
        tile = first_ref[e] + k
        for slot in range(N_RING):
            @pl.when(tile % N_RING == slot)
            def _():
                do_tile(tile, slot)
        return carry

    lax.fori_loop(0, ntile_ref[e], tile_body, 0)

    @pl.when(e == pl.num_programs(0) - 1)
    def _():
        last = n_active - 1
        for slot in range(N_RING):
            @pl.when(last % N_RING == slot)
            def _():
                for d in range(N_RING - 2, 0, -1):
                    wait_all(scatter_copy, last - d, (slot - d) % N_RING)
                start_all(scatter_copy, last, slot)
                wait_all(scatter_copy, last, slot)
                for d in range(1, N_RING):
                    wait_all(gather_copy, last + d, (slot + d) % N_RING)


def _experts(first_tile, tiles_e, src, dst, n_active, xs, w_gate, w_up, w_down):
    chunk_shape = (2 * CHUNK, D_MODEL // 2)
    assert _spare_chunk(N_RING * TILE_CHUNKS - 1) < xs.shape[0]
    tile_buf = pltpu.VMEM((TILE_CHUNKS,) + chunk_shape, BF16)
    grid_spec = pltpu.PrefetchScalarGridSpec(
        num_scalar_prefetch=5,
        grid=(N_EXPERTS,),
        in_specs=[pl.BlockSpec(memory_space=pl.ANY),
                  pl.BlockSpec((1, D_MODEL, D_EXPERT), lambda e, *_: (e, 0, 0)),
                  pl.BlockSpec((1, D_MODEL, D_EXPERT), lambda e, *_: (e, 0, 0)),
                  pl.BlockSpec((1, D_EXPERT, D_MODEL), lambda e, *_: (e, 0, 0))],
        out_specs=pl.BlockSpec(memory_space=pl.ANY),
        scratch_shapes=[tile_buf] * (2 * N_RING) + [
                        pltpu.SemaphoreType.DMA((N_RING,)),
                        pltpu.SemaphoreType.DMA((N_RING,)),
                        pltpu.VMEM((D_MODEL, D_EXPERT), BF16),
                        pltpu.VMEM((D_MODEL, D_EXPERT), BF16),
                        pltpu.VMEM((D_EXPERT, D_MODEL), BF16)],
    )
    return pl.pallas_call(
        _expert_kernel,
        grid_spec=grid_spec,
        out_shape=jax.ShapeDtypeStruct(xs.shape, xs.dtype),
        input_output_aliases={5: 0},
        compiler_params=pltpu.CompilerParams(dimension_semantics=("arbitrary",), vmem_limit_bytes=VMEM_LIMIT),
        name="expert_ffn",
    )(first_tile, tiles_e, src, dst, n_active, xs, w_gate, w_up, w_down)


def _combine_kernel(n_prompt_steps, x2p_ref, x2s_ref, route_ref, gfin_ref, ys_ref, outp_ref, outs_ref):
    i = pl.program_id(0)
    is_prompt = i < n_prompt_steps
    outs = []
    for h in range(STEP_TILES):
        rows = pl.ds(h * TM, TM)
        lo, hi = _from_chunks(ys_ref[pl.ds(h * CAP_CHUNKS, CAP_CHUNKS)])
        route = route_ref[rows, :]
        d1 = route[:, 0:1].astype(jnp.int32)
        d2 = route[:, 1:2].astype(jnp.int32)
        w1 = route[:, 2:3]
        w2 = route[:, 3:4]
        dest = lax.broadcasted_iota(jnp.int32, (TM, CAP_ROWS), 1)
        sel1 = (dest == d1).astype(F32).astype(BF16)
        sel2 = (dest == d2).astype(F32).astype(BF16)
        m1 = jnp.concatenate([jnp.dot(sel1, lo, preferred_element_type=F32),
                              jnp.dot(sel1, hi, preferred_element_type=F32)], axis=1)
        m2 = jnp.concatenate([jnp.dot(sel2, lo, preferred_element_type=F32),
                              jnp.dot(sel2, hi, preferred_element_type=F32)], axis=1)
        x = jnp.where(is_prompt, x2p_ref[rows, :], x2s_ref[rows, :])
        outs.append(_rms(x + (w1 * m1 + w2 * m2), gfin_ref[...]))

    @pl.when(is_prompt)
    def _():
        for h in range(STEP_TILES):
            outp_ref[pl.ds(h * TM, TM), :] = outs[h]

    @pl.when(jnp.logical_not(is_prompt))
    def _():
        for h in range(STEP_TILES):
            outs_ref[pl.ds(h * TM, TM), :] = outs[h]


def _combine(x2p, x2s, route, g_final, ys):
    n_prompt_tiles = x2p.shape[0] // TM
    n_tiles = n_prompt_tiles + x2s.shape[0] // TM
    assert n_prompt_tiles % STEP_TILES == 0 and n_tiles % STEP_TILES == 0
    n_prompt_steps = n_prompt_tiles // STEP_TILES
    rows = STEP_TILES * TM
    p_spec = pl.BlockSpec((rows, D_MODEL), lambda i: (jnp.minimum(i, n_prompt_steps - 1), 0))
    s_spec = pl.BlockSpec((rows, D_MODEL), lambda i: (jnp.maximum(i - n_prompt_steps, 0), 0))
    return pl.pallas_call(
        functools.partial(_combine_kernel, n_prompt_steps),
        grid=(n_tiles // STEP_TILES,),
        in_specs=[p_spec, s_spec,
                  pl.BlockSpec((rows, LANES), lambda i: (i, 0)),
                  _const_spec((1, D_MODEL)),
                  pl.BlockSpec((STEP_TILES * CAP_CHUNKS, 2 * CHUNK, D_MODEL // 2), lambda i: (i, 0, 0))],
        out_specs=[p_spec, s_spec],
        out_shape=[jax.ShapeDtypeStruct(x2p.shape, F32), jax.ShapeDtypeStruct(x2s.shape, F32)],
        compiler_params=pltpu.CompilerParams(dimension_semantics=("arbitrary",), vmem_limit_bytes=VMEM_LIMIT),
        name="combine_norm",
    )(x2p, x2s, route, g_final, ys)


def _block_diag_in(bb):
    t = bb.reshape(2, 16, SSM_GROUP, SSM_STATE)
    eye = jnp.eye(16, dtype=bb.dtype)
    blk = t[:, :, :, None, :] * eye[None, :, None, :, None]
    return blk.reshape(2, 16 * SSM_GROUP, 16 * SSM_STATE)


def _block_diag_out(c):
    t = c.reshape(2, 16, SSM_GROUP, SSM_STATE).transpose(0, 1, 3, 2)
    eye = jnp.eye(16, dtype=c.dtype)
    blk = t[:, :, :, None, :] * eye[None, :, None, :, None]
    return blk.reshape(2, 16 * SSM_STATE, 16 * SSM_GROUP)


def kernel(x_prompt, x_sample, state_ssm_re, state_ssm_im, cache_pool, g_mix, w_in, ssm_a_re, ssm_a_im,
           ssm_log_dt, ssm_b_re, ssm_b_im, ssm_c_re, ssm_c_im, ssm_d, w_glu_a, w_glu_b, pool_w, pool_scale,
           w_out, g_ffn, w_router_group, w_router_expert, w_exp_gate, w_exp_up, w_exp_down, g_final):
    li = 0
    n_pb, seq_p, _ = x_prompt.shape
    n_sb, seq_s, _ = x_sample.shape

    a_re, a_im, a32_re, a32_im, bb_re, bb_im = _discretise(
        ssm_a_re[li], ssm_a_im[li], ssm_log_dt[li], ssm_b_re[li], ssm_b_im[li])
    row = lambda v: v.reshape(1, N_FLAT)
    wb = jnp.concatenate([_block_diag_in(bb_re), _block_diag_in(bb_im)], axis=2).astype(BF16)
    weights = [
        g_mix[li].reshape(1, D_MODEL),
        w_in[li].astype(BF16),
        wb,
        _block_diag_out(ssm_c_re[li]).astype(BF16),
        _block_diag_out(ssm_c_im[li]).astype(BF16),
        ssm_d[li].reshape(1, SSM_WIDTH),
        jnp.concatenate([w_glu_a[li], w_glu_b[li]], axis=1).astype(BF16),
        pool_w[li].astype(BF16),
        pool_scale[li].reshape(1, D_MODEL),
        w_out[li].astype(BF16),
        row(a_re), row(a_im), row(a32_re), row(a32_im),
    ]

    x2p, stp_re, stp_im, histp = _mixer_prompt(x_prompt, weights)
    x2p = x2p.reshape(n_pb * seq_p, D_MODEL)

    n_stiles = n_sb // N_CHAIN
    cache16 = jnp.pad(cache_pool[li], ((0, 0), (1, 0), (0, 0)))
    cache_t = cache16.reshape(n_stiles, N_CHAIN, 16, POOL_WIDTH).transpose(0, 2, 1, 3).reshape(
        n_stiles, HIST_ROWS, POOL_WIDTH)
    x2s, sts_re, sts_im, hists = _mixer_sample(
        x_sample, state_ssm_re[li].reshape(n_sb, N_FLAT), state_ssm_im[li].reshape(n_sb, N_FLAT),
        cache_t, weights)
    x2s = x2s.reshape(n_sb * seq_s, D_MODEL)

    w_router = jnp.concatenate(
        [w_router_group[li], w_router_expert[li].reshape(D_MODEL, N_EXPERTS),
         jnp.zeros((D_MODEL, LANES - N_EXPERT_GROUPS - N_EXPERTS), F32)], axis=1).astype(BF16)
    xs, route, nch = _route(x2p, x2s, g_ffn[li].reshape(1, D_MODEL), w_router)

    n_tiles = (x2p.shape[0] + x2s.shape[0]) // TM
    max_chunks = n_tiles * (2 * TM // CHUNK + N_EXPERTS)
    n_expert_tiles = max_chunks // TILE_CHUNKS + N_EXPERTS
    first_tile, tiles_e, src, dst, n_active = _chunk_tables(nch[:, 0, :N_EXPERTS], n_expert_tiles)
    ys = _experts(first_tile, tiles_e, src, dst, n_active, xs, w_exp_gate[li], w_exp_up[li], w_exp_down[li])
    yp, ysm = _combine(x2p, x2s, route, g_final.reshape(1, D_MODEL), ys)

    sd = state_ssm_re.dtype
    cd = cache_pool.dtype
    y_prompt = yp.reshape(n_pb, seq_p, D_MODEL)
    y_sample = ysm.reshape(n_sb, seq_s, D_MODEL)
    re_p = stp_re.reshape(1, n_pb, SSM_GROUPS, SSM_STATE).astype(sd)
    im_p = stp_im.reshape(1, n_pb, SSM_GROUPS, SSM_STATE).astype(sd)
    hist_p = histp[:, ::N_CHAIN, :][:, 1:, :].reshape(1, n_pb, POOL_HIST, POOL_WIDTH).astype(cd)
    re_s = sts_re.reshape(1, n_sb, SSM_GROUPS, SSM_STATE).astype(sd)
    im_s = sts_im.reshape(1, n_sb, SSM_GROUPS, SSM_STATE).astype(sd)
    hist_s = hists.reshape(n_stiles, 16, N_CHAIN, POOL_WIDTH).transpose(0, 2, 1, 3).reshape(
        n_sb, 16, POOL_WIDTH)[:, 1:, :].reshape(1, n_sb, POOL_HIST, POOL_WIDTH).astype(cd)
    return (y_prompt, y_sample, re_p, im_p, hist_p, re_s, im_s, hist_s)
```

```python
import functools
import math

import jax
import jax.numpy as jnp
from jax import lax
from jax.experimental import pallas as pl
from jax.experimental.pallas import tpu as pltpu

F32 = jnp.float32
BF16 = jnp.bfloat16

D_MODEL = 1024
SSM_WIDTH = 512
SSM_GROUPS = 32
SSM_GROUP = 16
SSM_STATE = 64
N_FLAT = SSM_GROUPS * SSM_STATE
HALF_FLAT = N_FLAT // 2
POOL_WIDTH = 512
POOL_WINDOWS = (2, 4, 8, 16)
POOL_GROUP_IN = 128
POOL_GROUP_OUT = 256
POOL_HIST = 15
N_EXPERTS = 32
EXPERTS_PER_GROUP = 8
N_EXPERT_GROUPS = 4
D_EXPERT = 512
EPS = 1e-6
PAST_LEN = 1024

TM = 256
N_CHAIN = 8
CHAIN_LEN = TM // N_CHAIN
HIST_ROWS = 16 * N_CHAIN

CHUNK = 8
TILE_CHUNKS = TM // CHUNK
CAP_CHUNKS = 96
CAP_ROWS = CAP_CHUNKS * CHUNK
MAX_USED_CHUNKS = 2 * TM // CHUNK + N_EXPERTS * (CHUNK - 1) // CHUNK
N_SPARE_SLOTS = CAP_CHUNKS - MAX_USED_CHUNKS - 1
ZERO_CHUNK = CAP_CHUNKS - 1
LANES = 128

VMEM_LIMIT = 52 * 1024 * 1024
STEP_TILES = 4


def _rms(x, g):
    r = lax.rsqrt(jnp.mean(x * x, axis=-1, keepdims=True) + EPS)
    return x * r * g


def _sigmoid(x):
    return 0.5 * jnp.tanh(0.5 * x) + 0.5


def _gelu_tanh(x):
    c = math.sqrt(2.0 / math.pi)
    return x * (0.5 * (1.0 + jnp.tanh(c * (x + 0.044715 * (x * x * x)))))


def _disc_kernel(lre_ref, lim_ref, ldt_ref, bre_ref, bim_ref,
                 are_ref, aim_ref, a32re_ref, a32im_ref, bbre_ref, bbim_ref):
    lam_re = jnp.minimum(lre_ref[...], -1e-4)
    lam_im = lim_ref[...]
    dt = jnp.exp(ldt_ref[...])
    mag = jnp.exp(lam_re * dt)
    ang = lam_im * dt
    a_re = mag * jnp.cos(ang)
    a_im = mag * jnp.sin(ang)
    num_re = a_re - 1.0
    num_im = a_im
    den = lam_re * lam_re + lam_im * lam_im
    k_re = ((num_re * lam_re + num_im * lam_im) / den)[:, None, :]
    k_im = ((num_im * lam_re - num_re * lam_im) / den)[:, None, :]
    br = bre_ref[...]
    bi = bim_ref[...]
    bbre_ref[...] = k_re * br - k_im * bi
    bbim_ref[...] = k_re * bi + k_im * br
    are_ref[...] = a_re
    aim_ref[...] = a_im
    pr, pi = a_re, a_im
    for _ in range(int(math.log2(CHAIN_LEN))):
        pr, pi = pr * pr - pi * pi, 2.0 * pr * pi
    a32re_ref[...] = pr
    a32im_ref[...] = pi


def _discretise(a_re, a_im, log_dt, b_re, b_im):
    chan_major = lambda b: jnp.swapaxes(b, 1, 2)
    return pl.pallas_call(
        _disc_kernel,
        out_shape=[jax.ShapeDtypeStruct((SSM_GROUPS, SSM_STATE), F32)] * 4
        + [jax.ShapeDtypeStruct((SSM_GROUPS, SSM_GROUP, SSM_STATE), F32)] * 2,
        name="s5_discretise",
    )(a_re, a_im, log_dt.reshape(SSM_GROUPS, 1), chan_major(b_re), chan_major(b_im))


def _scan_half(hbuf, h, ar, ai, init_re, init_im, store):
    cre = h * N_FLAT
    cim = cre + HALF_FLAT
    hr, hi = init_re, init_im
    for t in range(CHAIN_LEN):
        rows = pl.ds(N_CHAIN * t, N_CHAIN)
        br = hbuf[rows, cre:cre + HALF_FLAT]
        bi = hbuf[rows, cim:cim + HALF_FLAT]
        nr = ar * hr - ai * hi + br
        ni = ar * hi + ai * hr + bi
        if store:
            hbuf[rows, cre:cre + HALF_FLAT] = nr
            hbuf[rows, cim:cim + HALF_FLAT] = ni
        hr, hi = nr, ni
    return hr, hi


N_MIXER_WEIGHTS = 18
N_PAR = 2


def _mixer_kernel(is_prompt, tiles_per_stream, *refs):
    n_in = 1 if is_prompt else 4
    n_aliased = 0 if is_prompt else 3
    ins = refs[:n_in]
    (gmix, win, wb, wcre, wcim, dsk, wglu, poolw, pscale, wout, are, aim, a32re, a32im,
     gffn, wrt, ltri, utri) = refs[n_in:n_in + N_MIXER_WEIGHTS]
    n_out0 = n_in + N_MIXER_WEIGHTS + n_aliased
    x2_ref, stre_ref, stim_ref, hist_ref, xs_ref, route_ref, nch_ref = refs[n_out0:n_out0 + 7]
    scratch = refs[n_out0 + 7:]
    per = len(scratch) // N_PAR
    lanes = [scratch[p * per:(p + 1) * per] for p in range(N_PAR)]
    x_ref = ins[0]

    tile_in_stream = pl.program_id(0) % tiles_per_stream
    n_lane_blocks = D_MODEL // LANES
    half_w = SSM_WIDTH // 2
    ar_full = are[...]
    ai_full = aim[...]

    if is_prompt:
        @pl.when(pl.program_id(0) == 0)
        def _():
            for p in range(N_PAR):
                cre_s, cim_s, pcarry = lanes[p][9], lanes[p][10], lanes[p][11]
                cre_s[...] = jnp.zeros_like(cre_s)
                cim_s[...] = jnp.zeros_like(cim_s)
                pcarry[...] = jnp.zeros_like(pcarry)

    def a_half(h):
        f0 = h * HALF_FLAT
        return (jnp.broadcast_to(ar_full[:, f0:f0 + HALF_FLAT], (N_CHAIN, HALF_FLAT)),
                jnp.broadcast_to(ai_full[:, f0:f0 + HALF_FLAT], (N_CHAIN, HALF_FLAT)))

    def project(p):
        xperm, hbuf = lanes[p][0], lanes[p][1]
        for j in range(N_CHAIN):
            for cb in range(n_lane_blocks):
                xperm[cb, pl.ds(j, CHAIN_LEN, stride=N_CHAIN), :] = x_ref[
                    p, CHAIN_LEN * j:CHAIN_LEN * (j + 1), cb * LANES:(cb + 1) * LANES]
        x = jnp.concatenate([xperm[cb] for cb in range(n_lane_blocks)], axis=1)
        xn = _rms(x, gmix[...]).astype(BF16)
        proj = jnp.dot(xn, win[...], preferred_element_type=F32)
        ub = proj[:, :SSM_WIDTH].astype(BF16)
        for h in range(2):
            hbuf[:, h * N_FLAT:(h + 1) * N_FLAT] = jnp.dot(
                ub[:, h * half_w:(h + 1) * half_w], wb[h], preferred_element_type=F32)
        return x, proj

    def recur(p):
        hbuf = lanes[p][1]
        if is_prompt:
            fre, fim, hre, him, cre_s, cim_s = lanes[p][5:11]
            zeros = jnp.zeros((N_CHAIN, HALF_FLAT), F32)
            for h in range(2):
                f0 = h * HALF_FLAT
                ar, ai = a_half(h)
                lr, li = _scan_half(hbuf, h, ar, ai, zeros, zeros, store=False)
                fre[:, f0:f0 + HALF_FLAT] = lr
                fim[:, f0:f0 + HALF_FLAT] = li
            fresh = tile_in_stream == 0
            hre[0:1, :] = jnp.where(fresh, 0.0, cre_s[...])
            him[0:1, :] = jnp.where(fresh, 0.0, cim_s[...])
            p_re = a32re[...]
            p_im = a32im[...]
            for j in range(N_CHAIN - 1):
                sr = hre[j:j + 1, :]
                si = him[j:j + 1, :]
                hre[j + 1:j + 2, :] = fre[j:j + 1, :] + p_re * sr - p_im * si
                him[j + 1:j + 2, :] = fim[j:j + 1, :] + p_re * si + p_im * sr
            init_re = hre[...]
            init_im = him[...]
        else:
            init_re = ins[1][p]
            init_im = ins[2][p]
        fin_re = []
        fin_im = []
        for h in range(2):
            f0 = h * HALF_FLAT
            ar, ai = a_half(h)
            er, ei = _scan_half(hbuf, h, ar, ai, init_re[:, f0:f0 + HALF_FLAT],
                                init_im[:, f0:f0 + HALF_FLAT], store=True)
            fin_re.append(er)
            fin_im.append(ei)
        end_re = jnp.concatenate(fin_re, axis=1)
        end_im = jnp.concatenate(fin_im, axis=1)
        if is_prompt:
            cre_s[...] = end_re[N_CHAIN - 1:N_CHAIN, :]
            cim_s[...] = end_im[N_CHAIN - 1:N_CHAIN, :]
            stre_ref[p] = end_re[N_CHAIN - 1:N_CHAIN, :]
            stim_ref[p] = end_im[N_CHAIN - 1:N_CHAIN, :]
        else:
            stre_ref[p] = end_re
            stim_ref[p] = end_im

    def finish(p, x, proj):
        hbuf, res, xpbuf, rbuf = lanes[p][1], lanes[p][2], lanes[p][3], lanes[p][4]
        u_s = proj[:, :SSM_WIDTH]
        u_p = proj[:, SSM_WIDTH:SSM_WIDTH + POOL_WIDTH]
        gate_s = proj[:, SSM_WIDTH + POOL_WIDTH:SSM_WIDTH + POOL_WIDTH + D_MODEL]
        gate_p = proj[:, SSM_WIDTH + POOL_WIDTH + D_MODEL:]
        ys = []
        for h in range(2):
            c0 = h * N_FLAT
            h_re = hbuf[:, c0:c0 + HALF_FLAT].astype(BF16)
            h_im = hbuf[:, c0 + HALF_FLAT:c0 + N_FLAT].astype(BF16)
            ys.append(jnp.dot(h_re, wcre[h], preferred_element_type=F32)
                      - jnp.dot(h_im, wcim[h], preferred_element_type=F32))
        y = jnp.concatenate(ys, axis=1) + dsk[...] * u_s
        g = _gelu_tanh(y).astype(BF16)
        glu = jnp.dot(g, wglu[...], preferred_element_type=F32)
        o_s = glu[:, :D_MODEL] * _sigmoid(glu[:, D_MODEL:])

        xpbuf[HIST_ROWS:HIST_ROWS + TM, :] = u_p
        tail = u_p[TM - HIST_ROWS:, :]
        row = lax.broadcasted_iota(jnp.int32, (TM, 1), 0)
        if is_prompt:
            pcarry = lanes[p][11]
            first_chain = (lax.broadcasted_iota(jnp.int32, (HIST_ROWS, POOL_WIDTH), 0) % N_CHAIN) == 0
            carried = jnp.where(tile_in_stream == 0, 0.0, pcarry[...])
            xpbuf[0:HIST_ROWS, :] = jnp.where(first_chain, carried, pltpu.roll(tail, 1, 0))
            new_carry = pltpu.roll(tail, HIST_ROWS - (N_CHAIN - 1), 0)
            pcarry[...] = new_carry
            hist_ref[p] = new_carry
            pos1 = tile_in_stream * TM + CHAIN_LEN * (row % N_CHAIN) + row // N_CHAIN + 1
        else:
            xpbuf[0:HIST_ROWS, :] = ins[3][p]
            hist_ref[p] = tail
            pos1 = PAST_LEN + row // N_CHAIN + 1

        o_ps = []
        for gi, w in enumerate(POOL_WINDOWS):
            c0 = gi * POOL_GROUP_IN
            acc = xpbuf[HIST_ROWS:HIST_ROWS + TM, c0:c0 + POOL_GROUP_IN]
            for k in range(1, w):
                acc = acc + xpbuf[HIST_ROWS - N_CHAIN * k:HIST_ROWS - N_CHAIN * k + TM, c0:c0 + POOL_GROUP_IN]
            cnt = jnp.minimum(w, pos1).astype(F32)
            pooled = acc / cnt
            z = (pooled - u_p[:, c0:c0 + POOL_GROUP_IN]).astype(BF16)
            o_ps.append(jnp.dot(z, poolw[gi], preferred_element_type=F32))
        o_p = jnp.concatenate(o_ps, axis=1) * pscale[...]

        merged = (_sigmoid(gate_s) * o_s + _sigmoid(gate_p) * o_p).astype(BF16)
        x2 = x + jnp.dot(merged, wout[...], preferred_element_type=F32)
        for cb in range(n_lane_blocks):
            res[cb] = x2[:, cb * LANES:(cb + 1) * LANES]
        for j in range(N_CHAIN):
            for cb in range(n_lane_blocks):
                x2_ref[p, CHAIN_LEN * j:CHAIN_LEN * (j + 1), cb * LANES:(cb + 1) * LANES] = res[
                    cb, pl.ds(j, CHAIN_LEN, stride=N_CHAIN), :]

        xn2 = _rms(x2, gffn[...]).astype(BF16)
        route, nch16, perm = _route_rows(xn2, wrt, ltri, utri)
        half = D_MODEL // 2
        lo = jnp.dot(perm, xn2[:, :half], preferred_element_type=F32)
        hi = jnp.dot(perm, xn2[:, half:], preferred_element_type=F32)
        xs_ref[0, pl.ds(p * CAP_CHUNKS, CAP_CHUNKS)] = _to_chunks(lo, hi)
        nch_ref[0, pl.ds(p * 8, 8), :] = nch16[0:8, :].astype(jnp.int32)
        rbuf[...] = route
        for j in range(N_CHAIN):
            route_ref[0, pl.ds(p * TM + CHAIN_LEN * j, CHAIN_LEN), :] = rbuf[pl.ds(j, CHAIN_LEN, stride=N_CHAIN), :]

    projected = [None] * N_PAR
    for k in range(N_PAR + 2):
        if 0 <= k - 2 < N_PAR:
            finish(k - 2, *projected[k - 2])
        if 0 <= k - 1 < N_PAR:
            recur(k - 1)
        if k < N_PAR:
            projected[k] = project(k)


def _const_spec(shape):
    nd = len(shape)
    return pl.BlockSpec(shape, lambda i, _nd=nd: (0,) * _nd)


def _mixer_weight_specs():
    return [
        _const_spec((1, D_MODEL)),
        _const_spec((D_MODEL, 3 * D_MODEL)),
        _const_spec((2, SSM_WIDTH // 2, N_FLAT)),
        _const_spec((2, HALF_FLAT, SSM_WIDTH // 2)),
        _const_spec((2, HALF_FLAT, SSM_WIDTH // 2)),
        _const_spec((1, SSM_WIDTH)),
        _const_spec((SSM_WIDTH, 2 * D_MODEL)),
        _const_spec((len(POOL_WINDOWS), POOL_GROUP_IN, POOL_GROUP_OUT)),
        _const_spec((1, D_MODEL)),
        _const_spec((D_MODEL, D_MODEL)),
        _const_spec((1, N_FLAT)),
        _const_spec((1, N_FLAT)),
        _const_spec((1, N_FLAT)),
        _const_spec((1, N_FLAT)),
        _const_spec((1, D_MODEL)),
        _const_spec((D_MODEL, LANES)),
        _const_spec((TM, TM)),
        _const_spec((LANES, LANES)),
    ]


def _mixer_common_scratch():
    return [
        pltpu.VMEM((D_MODEL // LANES, TM, LANES), F32),
        pltpu.VMEM((TM, 2 * N_FLAT), F32),
        pltpu.VMEM((D_MODEL // LANES, TM, LANES), F32),
        pltpu.VMEM((HIST_ROWS + TM, POOL_WIDTH), F32),
        pltpu.VMEM((TM, LANES), F32),
    ]


def _routing_out(n_pairs, first_pair):
    shapes = [jax.ShapeDtypeStruct((n_pairs, N_PAR * CAP_CHUNKS, 2 * CHUNK, D_MODEL // 2), BF16),
              jax.ShapeDtypeStruct((n_pairs, N_PAR * TM, LANES), F32),
              jax.ShapeDtypeStruct((n_pairs, N_PAR * 8, LANES), jnp.int32)]
    specs = [pl.BlockSpec((1, N_PAR * CAP_CHUNKS, 2 * CHUNK, D_MODEL // 2), lambda i: (first_pair(i), 0, 0, 0)),
             pl.BlockSpec((1, N_PAR * TM, LANES), lambda i: (first_pair(i), 0, 0)),
             pl.BlockSpec((1, N_PAR * 8, LANES), lambda i: (first_pair(i), 0, 0))]
    return shapes, specs


def _mixer_prompt(x, weights, n_pairs_total):
    n_streams, seq, _ = x.shape
    assert n_streams % N_PAR == 0 and seq % TM == 0
    tiles_per_stream = seq // TM
    groups = n_streams // N_PAR
    r_shapes, r_specs = _routing_out(n_pairs_total, lambda i: (i % tiles_per_stream) * groups + i // tiles_per_stream)
    blk = lambda shape: pl.BlockSpec(shape, lambda i: (i // tiles_per_stream, 0, 0))
    row_spec = pl.BlockSpec((N_PAR, TM, D_MODEL), lambda i: (i // tiles_per_stream, i % tiles_per_stream, 0))
    lane_scratch = _mixer_common_scratch() + [
        pltpu.VMEM((N_CHAIN, N_FLAT), F32), pltpu.VMEM((N_CHAIN, N_FLAT), F32),
        pltpu.VMEM((N_CHAIN, N_FLAT), F32), pltpu.VMEM((N_CHAIN, N_FLAT), F32),
        pltpu.VMEM((1, N_FLAT), F32), pltpu.VMEM((1, N_FLAT), F32),
        pltpu.VMEM((HIST_ROWS, POOL_WIDTH), F32),
    ]
    return pl.pallas_call(
        functools.partial(_mixer_kernel, True, tiles_per_stream),
        grid=(n_streams // N_PAR * tiles_per_stream,),
        in_specs=[row_spec] + _mixer_weight_specs(),
        out_specs=[row_spec, blk((N_PAR, 1, N_FLAT)), blk((N_PAR, 1, N_FLAT)),
                   blk((N_PAR, HIST_ROWS, POOL_WIDTH))] + r_specs,
        out_shape=[jax.ShapeDtypeStruct((n_streams, seq, D_MODEL), F32),
                   jax.ShapeDtypeStruct((n_streams, 1, N_FLAT), F32),
                   jax.ShapeDtypeStruct((n_streams, 1, N_FLAT), F32),
                   jax.ShapeDtypeStruct((n_streams, HIST_ROWS, POOL_WIDTH), F32)] + r_shapes,
        scratch_shapes=lane_scratch * N_PAR,
        compiler_params=pltpu.CompilerParams(dimension_semantics=("arbitrary",), vmem_limit_bytes=VMEM_LIMIT),
        name="mixer_prompt",
    )(x, *weights)


def _mixer_sample(x, h0_re, h0_im, cache_t, weights, routing, first_pair):
    n_streams, seq, _ = x.shape
    assert seq == CHAIN_LEN and n_streams % (N_CHAIN * N_PAR) == 0
    n_tiles = n_streams // N_CHAIN
    r_shapes, r_specs = _routing_out(routing[0].shape[0], lambda i: first_pair + i)
    n_plain_inputs = 4 + N_MIXER_WEIGHTS
    blk = lambda shape: pl.BlockSpec(shape, lambda i: (i, 0, 0))
    row_spec = blk((N_PAR, TM, D_MODEL))
    st_spec = blk((N_PAR, N_CHAIN, N_FLAT))
    hist_spec = blk((N_PAR, HIST_ROWS, POOL_WIDTH))
    tiles = lambda v: v.reshape((n_tiles, -1) + v.shape[-1:])
    return pl.pallas_call(
        functools.partial(_mixer_kernel, False, 1),
        grid=(n_tiles // N_PAR,),
        in_specs=[row_spec, st_spec, st_spec, hist_spec] + _mixer_weight_specs()
        + [pl.BlockSpec(memory_space=pl.ANY)] * 3,
        out_specs=[row_spec, st_spec, st_spec, hist_spec] + r_specs,
        out_shape=[jax.ShapeDtypeStruct((n_tiles, TM, D_MODEL), F32),
                   jax.ShapeDtypeStruct((n_tiles, N_CHAIN, N_FLAT), F32),
                   jax.ShapeDtypeStruct((n_tiles, N_CHAIN, N_FLAT), F32),
                   jax.ShapeDtypeStruct((n_tiles, HIST_ROWS, POOL_WIDTH), F32)] + r_shapes,
        input_output_aliases={n_plain_inputs + k: 4 + k for k in range(3)},
        scratch_shapes=_mixer_common_scratch() * N_PAR,
        compiler_params=pltpu.CompilerParams(dimension_semantics=("arbitrary",), vmem_limit_bytes=VMEM_LIMIT),
        name="mixer_sample",
    )(tiles(x), tiles(h0_re), tiles(h0_im), cache_t, *weights, *routing)


def _to_chunks(lo, hi):
    n = lo.shape[0] // CHUNK
    half = D_MODEL // 2
    both = jnp.concatenate([lo.reshape(n, CHUNK, half), hi.reshape(n, CHUNK, half)], axis=1)
    return both.astype(BF16)


def _from_chunks(blk):
    n = blk.shape[0]
    half = D_MODEL // 2
    f = blk.astype(F32)
    lo = f[:, :CHUNK, :].reshape(n * CHUNK, half).astype(BF16)
    hi = f[:, CHUNK:, :].reshape(n * CHUNK, half).astype(BF16)
    return lo, hi


def _route_rows(xn, wr_ref, ltri_ref, utri_ref):
    logits = jnp.dot(xn, wr_ref[...], preferred_element_type=F32)
    lane = lax.broadcasted_iota(jnp.int32, (TM, LANES), 1)
    lane_f = lane.astype(F32)
    big = jnp.float32(1 << 20)
    neg = jnp.float32(-jnp.inf)

    gmask = lane < N_EXPERT_GROUPS
    m = jnp.max(jnp.where(gmask, logits, neg), axis=1, keepdims=True)
    grp = jnp.min(jnp.where(gmask & (logits == m), lane_f, big), axis=1, keepdims=True)
    wg = 1.0 / jnp.sum(jnp.where(gmask, jnp.exp(logits - m), 0.0), axis=1, keepdims=True)

    eid = lane - N_EXPERT_GROUPS
    lane_grp = (eid >> 3).astype(F32)
    emask = (eid >= 0) & (eid < N_EXPERTS) & (lane_grp == grp)
    v1 = jnp.max(jnp.where(emask, logits, neg), axis=1, keepdims=True)
    i1 = jnp.min(jnp.where(emask & (logits == v1), lane_f, big), axis=1, keepdims=True)
    emask2 = emask & (lane_f != i1)
    v2 = jnp.max(jnp.where(emask2, logits, neg), axis=1, keepdims=True)
    i2 = jnp.min(jnp.where(emask2 & (logits == v2), lane_f, big), axis=1, keepdims=True)
    e21 = jnp.exp(v2 - v1)
    w1 = wg / (1.0 + e21)
    w2 = wg * e21 / (1.0 + e21)

    a1 = lane_f == (i1 - N_EXPERT_GROUPS)
    a2 = lane_f == (i2 - N_EXPERT_GROUPS)
    a = (a1 | a2).astype(F32)
    before = jnp.dot(ltri_ref[...], a.astype(BF16), preferred_element_type=F32)
    cnt = jnp.sum(a, axis=0, keepdims=True)
    nch = jnp.floor((cnt + (CHUNK - 1)) * (1.0 / CHUNK))
    nch16 = jnp.broadcast_to(nch, (16, LANES))
    start = jnp.dot(nch16.astype(BF16), utri_ref[...], preferred_element_type=F32)
    slot = before + CHUNK * start[0:1, :]
    d1 = jnp.sum(jnp.where(a1, slot, 0.0), axis=1, keepdims=True)
    d2 = jnp.sum(jnp.where(a2, slot, 0.0), axis=1, keepdims=True)

    route = jnp.where(lane == 0, d1, jnp.where(lane == 1, d2, jnp.where(lane == 2, w1,
                      jnp.where(lane == 3, w2, 0.0))))
    dt = jnp.transpose(jnp.where(lane < 2, route, 0.0)).astype(jnp.int32)
    d1row = dt[0:1, :]
    d2row = dt[1:2, :]
    dest = lax.broadcasted_iota(jnp.int32, (CAP_ROWS, TM), 0)
    perm = ((dest == d1row) | (dest == d2row)).astype(F32).astype(BF16)
    return route, nch16, perm


def _spare_chunk(k):
    return (1 + k // N_SPARE_SLOTS) * CAP_CHUNKS + MAX_USED_CHUNKS + k % N_SPARE_SLOTS


def _chunk_tables(nch, n_expert_tiles):
    n_tiles = nch.shape[0]
    i32 = jnp.int32
    start = jnp.cumsum(nch, axis=1) - nch
    off = jnp.cumsum(nch, axis=0) - nch
    per_expert = jnp.sum(nch, axis=0)
    tiles_e = (per_expert + TILE_CHUNKS - 1) // TILE_CHUNKS
    cum_tiles = jnp.cumsum(tiles_e)
    first_tile = cum_tiles - tiles_e
    n_active = cum_tiles[-1].astype(i32)
    t = jnp.arange(n_expert_tiles, dtype=i32)
    te = jnp.minimum(jnp.sum((t[:, None] >= cum_tiles[None, :]).astype(i32), axis=1), N_EXPERTS - 1)
    onehot = te[:, None] == jnp.arange(N_EXPERTS, dtype=i32)[None, :]
    pick = lambda tab: jnp.sum(jnp.where(onehot[:, :, None], tab.T[None, :, :], 0), axis=1)
    off_t, nch_t, start_t = pick(off), pick(nch), pick(start)
    k = t - jnp.sum(jnp.where(onehot, first_tile[None, :], 0), axis=1)
    q = (TILE_CHUNKS * k)[:, None] + jnp.arange(TILE_CHUNKS, dtype=i32)[None, :]
    in_run = (off_t[:, None, :] <= q[:, :, None]) & (q[:, :, None] < (off_t + nch_t)[:, None, :])
    run_src = (jnp.arange(n_tiles, dtype=i32) * CAP_CHUNKS)[None, :] + start_t - off_t
    src = jnp.sum(jnp.where(in_run, run_src[:, None, :], 0), axis=-1) + q
    valid = jnp.any(in_run, axis=-1) & (t < n_active)[:, None]
    src = jnp.where(valid, src, ZERO_CHUNK).astype(i32)
    spare = _spare_chunk((t % N_RING)[:, None] * TILE_CHUNKS + jnp.arange(TILE_CHUNKS, dtype=i32)[None, :])
    dst = jnp.where(valid, src, spare).astype(i32)
    dst = jnp.concatenate([spare[1:N_RING], dst], axis=0)
    return first_tile.astype(i32), tiles_e.astype(i32), src.reshape(-1), dst.reshape(-1), n_active.reshape(1)


N_RING = 5


def _expert_kernel(first_ref, ntile_ref, src_ref, dst_ref, nact_ref, xs_hbm, wg_ref, wu_ref, wd_ref, ys_hbm,
                   *scratch):
    xbufs = scratch[:N_RING]
    obufs = scratch[N_RING:2 * N_RING]
    gsem, ssem, wg16, wu16, wd16 = scratch[2 * N_RING:]
    e = pl.program_id(0)
    n_active = nact_ref[0]
    ahead = N_RING - 1

    def gather_copy(tile, slot, c):
        return pltpu.make_async_copy(xs_hbm.at[src_ref[tile * TILE_CHUNKS + c]], xbufs[slot].at[c],
                                     gsem.at[slot])

    def scatter_copy(tile, slot, c):
        return pltpu.make_async_copy(obufs[slot].at[c], ys_hbm.at[dst_ref[(tile + ahead) * TILE_CHUNKS + c]],
                                     ssem.at[slot])

    def start_all(copy, tile, slot):
        for c in range(TILE_CHUNKS):
            copy(tile, slot, c).start()

    def wait_all(copy, tile, slot):
        for c in range(TILE_CHUNKS):
            copy(tile, slot, c).wait()

    @pl.when(e == 0)
    def _():
        for v in range(ahead):
            start_all(gather_copy, v, v)
        for u in range(-ahead, 0):
            obufs[u % N_RING][...] = jnp.zeros_like(obufs[u % N_RING])
        for u in range(-ahead, -1):
            start_all(scatter_copy, u, u % N_RING)

    wg16[...] = wg_ref[0].astype(BF16)
    wu16[...] = wu_ref[0].astype(BF16)
    wd16[...] = wd_ref[0].astype(BF16)

    def do_tile(tile, slot):
        nxt = (slot + ahead) % N_RING
        old = (slot + 1) % N_RING
        wait_all(gather_copy, tile, slot)
        lo, hi = _from_chunks(xbufs[slot][...])
        x = jnp.concatenate([lo, hi], axis=1)
        gate = jnp.dot(x, wg16[...], preferred_element_type=F32)
        start_all(scatter_copy, tile - 1, nxt)
        up = jnp.dot(x, wu16[...], preferred_element_type=F32)
        hmid = (gate * _sigmoid(gate) * up).astype(BF16)
        start_all(gather_copy, tile + ahead, nxt)
        y = jnp.dot(hmid, wd16[...], preferred_element_type=F32)
        half = D_MODEL // 2
        wait_all(scatter_copy, tile - ahead, old)
        obufs[slot][...] = _to_chunks(y[:, :half], y[:, half:])

    def tile_body(k, carry):
        tile = first_ref[e] + k
        for slot in range(N_RING):
            @pl.when(tile % N_RING == slot)
            def _():
                do_tile(tile, slot)
        return carry

    lax.fori_loop(0, ntile_ref[e], tile_body, 0)

    @pl.when(e == pl.num_programs(0) - 1)
    def _():
        last = n_active - 1
        for slot in range(N_RING):
            @pl.when(last % N_RING == slot)
            def _():
                for d in range(N_RING - 2, 0, -1):
                    wait_all(scatter_copy, last - d, (slot - d) % N_RING)
                start_all(scatter_copy, last, slot)
                wait_all(scatter_copy, last, slot)
                for d in range(1, N_RING):
                    wait_all(gather_copy, last + d, (slot + d) % N_RING)


def _experts(first_tile, tiles_e, src, dst, n_active, xs, w_gate, w_up, w_down):
    chunk_shape = (2 * CHUNK, D_MODEL // 2)
    assert _spare_chunk(N_RING * TILE_CHUNKS - 1) < xs.shape[0]
    tile_buf = pltpu.VMEM((TILE_CHUNKS,) + chunk_shape, BF16)
    grid_spec = pltpu.PrefetchScalarGridSpec(
        num_scalar_prefetch=5,
        grid=(N_EXPERTS,),
        in_specs=[pl.BlockSpec(memory_space=pl.ANY),
                  pl.BlockSpec((1, D_MODEL, D_EXPERT), lambda e, *_: (e, 0, 0)),
                  pl.BlockSpec((1, D_MODEL, D_EXPERT), lambda e, *_: (e, 0, 0)),
                  pl.BlockSpec((1, D_EXPERT, D_MODEL), lambda e, *_: (e, 0, 0))],
        out_specs=pl.BlockSpec(memory_space=pl.ANY),
        scratch_shapes=[tile_buf] * (2 * N_RING) + [
                        pltpu.SemaphoreType.DMA((N_RING,)),
                        pltpu.SemaphoreType.DMA((N_RING,)),
                        pltpu.VMEM((D_MODEL, D_EXPERT), BF16),
                        pltpu.VMEM((D_MODEL, D_EXPERT), BF16),
                        pltpu.VMEM((D_EXPERT, D_MODEL), BF16)],
    )
    return pl.pallas_call(
        _expert_kernel,
        grid_spec=grid_spec,
        out_shape=jax.ShapeDtypeStruct(xs.shape, xs.dtype),
        input_output_aliases={5: 0},
        compiler_params=pltpu.CompilerParams(dimension_semantics=("arbitrary",), vmem_limit_bytes=VMEM_LIMIT),
        name="expert_ffn",
    )(first_tile, tiles_e, src, dst, n_active, xs, w_gate, w_up, w_down)


def _combine_kernel(n_prompt_steps, x2p_ref, x2s_ref, route_ref, gfin_ref, ys_ref, outp_ref, outs_ref):
    i = pl.program_id(0)
    is_prompt = i < n_prompt_steps
    outs = []
    for h in range(STEP_TILES):
        rows = pl.ds(h * TM, TM)
        lo, hi = _from_chunks(ys_ref[pl.ds(h * CAP_CHUNKS, CAP_CHUNKS)])
        route = route_ref[rows, :]
        d1 = route[:, 0:1].astype(jnp.int32)
        d2 = route[:, 1:2].astype(jnp.int32)
        w1 = route[:, 2:3]
        w2 = route[:, 3:4]
        dest = lax.broadcasted_iota(jnp.int32, (TM, CAP_ROWS), 1)
        sel1 = (dest == d1).astype(F32).astype(BF16)
        sel2 = (dest == d2).astype(F32).astype(BF16)
        m1 = jnp.concatenate([jnp.dot(sel1, lo, preferred_element_type=F32),
                              jnp.dot(sel1, hi, preferred_element_type=F32)], axis=1)
        m2 = jnp.concatenate([jnp.dot(sel2, lo, preferred_element_type=F32),
                              jnp.dot(sel2, hi, preferred_element_type=F32)], axis=1)
        x = jnp.where(is_prompt, x2p_ref[h], x2s_ref[rows, :])
        outs.append(_rms(x + (w1 * m1 + w2 * m2), gfin_ref[...]))

    @pl.when(is_prompt)
    def _():
        for h in range(STEP_TILES):
            outp_ref[h] = outs[h]

    @pl.when(jnp.logical_not(is_prompt))
    def _():
        for h in range(STEP_TILES):
            outs_ref[pl.ds(h * TM, TM), :] = outs[h]


def _combine(x2p, x2s, route, g_final, ys):
    n_streams, seq, _ = x2p.shape
    assert n_streams == STEP_TILES and x2s.shape[0] % (STEP_TILES * TM) == 0
    n_prompt_steps = seq // TM
    n_tiles = n_prompt_steps * n_streams + x2s.shape[0] // TM
    rows = STEP_TILES * TM
    p_spec = pl.BlockSpec((STEP_TILES, TM, D_MODEL), lambda i: (0, jnp.minimum(i, n_prompt_steps - 1), 0))
    s_spec = pl.BlockSpec((rows, D_MODEL), lambda i: (jnp.maximum(i - n_prompt_steps, 0), 0))
    return pl.pallas_call(
        functools.partial(_combine_kernel, n_prompt_steps),
        grid=(n_tiles // STEP_TILES,),
        in_specs=[p_spec, s_spec,
                  pl.BlockSpec((rows, LANES), lambda i: (i, 0)),
                  _const_spec((1, D_MODEL)),
                  pl.BlockSpec((STEP_TILES * CAP_CHUNKS, 2 * CHUNK, D_MODEL // 2), lambda i: (i, 0, 0))],
        out_specs=[p_spec, s_spec],
        out_shape=[jax.ShapeDtypeStruct(x2p.shape, F32), jax.ShapeDtypeStruct(x2s.shape, F32)],
        compiler_params=pltpu.CompilerParams(dimension_semantics=("arbitrary",), vmem_limit_bytes=VMEM_LIMIT),
        name="combine_norm",
    )(x2p, x2s, route, g_final, ys)


def _block_diag_in(bb):
    t = bb.reshape(2, 16, SSM_GROUP, SSM_STATE)
    eye = jnp.eye(16, dtype=bb.dtype)
    blk = t[:, :, :, None, :] * eye[None, :, None, :, None]
    return blk.reshape(2, 16 * SSM_GROUP, 16 * SSM_STATE)


def _block_diag_out(c):
    t = c.reshape(2, 16, SSM_GROUP, SSM_STATE).transpose(0, 1, 3, 2)
    eye = jnp.eye(16, dtype=c.dtype)
    blk = t[:, :, :, None, :] * eye[None, :, None, :, None]
    return blk.reshape(2, 16 * SSM_STATE, 16 * SSM_GROUP)


def kernel(x_prompt, x_sample, state_ssm_re, state_ssm_im, cache_pool, g_mix, w_in, ssm_a_re, ssm_a_im,
           ssm_log_dt, ssm_b_re, ssm_b_im, ssm_c_re, ssm_c_im, ssm_d, w_glu_a, w_glu_b, pool_w, pool_scale,
           w_out, g_ffn, w_router_group, w_router_expert, w_exp_gate, w_exp_up, w_exp_down, g_final):
    li = 0
    n_pb, seq_p, _ = x_prompt.shape
    n_sb, seq_s, _ = x_sample.shape

    a_re, a_im, a32_re, a32_im, bb_re, bb_im = _discretise(
        ssm_a_re[li], ssm_a_im[li], ssm_log_dt[li], ssm_b_re[li], ssm_b_im[li])
    row = lambda v: v.reshape(1, N_FLAT)
    r = jnp.arange(TM)
    e = jnp.arange(LANES)
    wb = jnp.concatenate([_block_diag_in(bb_re), _block_diag_in(bb_im)], axis=2).astype(BF16)
    weights = [
        g_mix[li].reshape(1, D_MODEL),
        w_in[li].astype(BF16),
        wb,
        _block_diag_out(ssm_c_re[li]).astype(BF16),
        _block_diag_out(ssm_c_im[li]).astype(BF16),
        ssm_d[li].reshape(1, SSM_WIDTH),
        jnp.concatenate([w_glu_a[li], w_glu_b[li]], axis=1).astype(BF16),
        pool_w[li].astype(BF16),
        pool_scale[li].reshape(1, D_MODEL),
        w_out[li].astype(BF16),
        row(a_re), row(a_im), row(a32_re), row(a32_im),
        g_ffn[li].reshape(1, D_MODEL),
        jnp.concatenate(
            [w_router_group[li], w_router_expert[li].reshape(D_MODEL, N_EXPERTS),
             jnp.zeros((D_MODEL, LANES - N_EXPERT_GROUPS - N_EXPERTS), F32)], axis=1).astype(BF16),
        (r[None, :] < r[:, None]).astype(BF16),
        (e[:, None] < e[None, :]).astype(BF16),
    ]

    n_stiles = n_sb // N_CHAIN
    n_ptiles = n_pb * (seq_p // TM)
    n_tiles = n_ptiles + n_stiles
    x2p, stp_re, stp_im, histp, *routing = _mixer_prompt(x_prompt, weights, n_tiles // N_PAR)

    cache16 = jnp.pad(cache_pool[li], ((0, 0), (1, 0), (0, 0)))
    cache_t = cache16.reshape(n_stiles, N_CHAIN, 16, POOL_WIDTH).transpose(0, 2, 1, 3).reshape(
        n_stiles, HIST_ROWS, POOL_WIDTH)
    x2s, sts_re, sts_im, hists, xs, route, nch = _mixer_sample(
        x_sample, state_ssm_re[li].reshape(n_sb, N_FLAT), state_ssm_im[li].reshape(n_sb, N_FLAT),
        cache_t, weights, routing, first_pair=n_ptiles // N_PAR)
    x2s = x2s.reshape(n_sb * seq_s, D_MODEL)
    xs = xs.reshape(n_tiles * CAP_CHUNKS, 2 * CHUNK, D_MODEL // 2)
    route = route.reshape(n_tiles * TM, LANES)
    nch = nch.reshape(n_tiles, 8, LANES)

    max_chunks = n_tiles * (2 * TM // CHUNK + N_EXPERTS)
    n_expert_tiles = max_chunks // TILE_CHUNKS + N_EXPERTS
    first_tile, tiles_e, src, dst, n_active = _chunk_tables(nch[:, 0, :N_EXPERTS], n_expert_tiles)
    ys = _experts(first_tile, tiles_e, src, dst, n_active, xs, w_exp_gate[li], w_exp_up[li], w_exp_down[li])
    yp, ysm = _combine(x2p, x2s, route, g_final.reshape(1, D_MODEL), ys)

    sd = state_ssm_re.dtype
    cd = cache_pool.dtype
    y_prompt = yp.reshape(n_pb, seq_p, D_MODEL)
    y_sample = ysm.reshape(n_sb, seq_s, D_MODEL)
    re_p = stp_re.reshape(1, n_pb, SSM_GROUPS, SSM_STATE).astype(sd)
    im_p = stp_im.reshape(1, n_pb, SSM_GROUPS, SSM_STATE).astype(sd)
    hist_p = histp[:, ::N_CHAIN, :][:, 1:, :].reshape(1, n_pb, POOL_HIST, POOL_WIDTH).astype(cd)
    re_s = sts_re.reshape(1, n_sb, SSM_GROUPS, SSM_STATE).astype(sd)
    im_s = sts_im.reshape(1, n_sb, SSM_GROUPS, SSM_STATE).astype(sd)
    hist_s = hists.reshape(n_stiles, 16, N_CHAIN, POOL_WIDTH).transpose(0, 2, 1, 3).reshape(
        n_sb, 16, POOL_WIDTH)[:, 1:, :].reshape(1, n_sb, POOL_HIST, POOL_WIDTH).astype(cd)
    return (y_prompt, y_sample, re_p, im_p, hist_p, re_s, im_s, hist_s)
```

```python
import functools
import math

import jax
import jax.numpy as jnp
from jax import lax
from jax.experimental import pallas as pl
from jax.experimental.pallas import tpu as pltpu

F32 = jnp.float32
BF16 = jnp.bfloat16

D_MODEL = 1024
SSM_WIDTH = 512
SSM_GROUPS = 32
SSM_GROUP = 16
SSM_STATE = 64
N_FLAT = SSM_GROUPS * SSM_STATE
HALF_FLAT = N_FLAT // 2
POOL_WIDTH = 512
POOL_WINDOWS = (2, 4, 8, 16)
POOL_GROUP_IN = 128
POOL_GROUP_OUT = 256
POOL_HIST = 15
N_EXPERTS = 32
EXPERTS_PER_GROUP = 8
N_EXPERT_GROUPS = 4
D_EXPERT = 512
EPS = 1e-6
PAST_LEN = 1024

TM = 256
N_CHAIN = 8
CHAIN_LEN = TM // N_CHAIN
HIST_ROWS = 16 * N_CHAIN

CHUNK = 8
TILE_CHUNKS = TM // CHUNK
CAP_CHUNKS = 96
CAP_ROWS = CAP_CHUNKS * CHUNK
MAX_USED_CHUNKS = 2 * TM // CHUNK + N_EXPERTS * (CHUNK - 1) // CHUNK
N_SPARE_SLOTS = CAP_CHUNKS - MAX_USED_CHUNKS - 1
ZERO_CHUNK = CAP_CHUNKS - 1
LANES = 128

V7X_VMEM_BYTES = 64 * 1024 * 1024
VMEM_LIMIT = V7X_VMEM_BYTES * 13 // 16
STEP_TILES = 4


def _rms(x, g):
    r = lax.rsqrt(jnp.mean(x * x, axis=-1, keepdims=True) + EPS)
    return x * r * g


def _sigmoid(x):
    return 0.5 * jnp.tanh(0.5 * x) + 0.5


def _gelu_tanh(x):
    c = math.sqrt(2.0 / math.pi)
    return x * (0.5 * (1.0 + jnp.tanh(c * (x + 0.044715 * (x * x * x)))))


def _disc_kernel(lre_ref, lim_ref, ldt_ref, bre_ref, bim_ref,
                 are_ref, aim_ref, a32re_ref, a32im_ref, bbre_ref, bbim_ref):
    lam_re = jnp.minimum(lre_ref[...], -1e-4)
    lam_im = lim_ref[...]
    dt = jnp.exp(ldt_ref[...])
    mag = jnp.exp(lam_re * dt)
    ang = lam_im * dt
    a_re = mag * jnp.cos(ang)
    a_im = mag * jnp.sin(ang)
    num_re = a_re - 1.0
    num_im = a_im
    den = lam_re * lam_re + lam_im * lam_im
    k_re = ((num_re * lam_re + num_im * lam_im) / den)[:, None, :]
    k_im = ((num_im * lam_re - num_re * lam_im) / den)[:, None, :]
    br = bre_ref[...]
    bi = bim_ref[...]
    bbre_ref[...] = k_re * br - k_im * bi
    bbim_ref[...] = k_re * bi + k_im * br
    are_ref[...] = a_re
    aim_ref[...] = a_im
    pr, pi = a_re, a_im
    for _ in range(int(math.log2(CHAIN_LEN))):
        pr, pi = pr * pr - pi * pi, 2.0 * pr * pi
    a32re_ref[...] = pr
    a32im_ref[...] = pi


def _discretise(a_re, a_im, log_dt, b_re, b_im):
    chan_major = lambda b: jnp.swapaxes(b, 1, 2)
    return pl.pallas_call(
        _disc_kernel,
        out_shape=[jax.ShapeDtypeStruct((SSM_GROUPS, SSM_STATE), F32)] * 4
        + [jax.ShapeDtypeStruct((SSM_GROUPS, SSM_GROUP, SSM_STATE), F32)] * 2,
        name="s5_discretise",
    )(a_re, a_im, log_dt.reshape(SSM_GROUPS, 1), chan_major(b_re), chan_major(b_im))


def _scan_half(hbuf, h, ar, ai, init_re, init_im, store):
    cre = h * N_FLAT
    cim = cre + HALF_FLAT
    hr, hi = init_re, init_im
    for t in range(CHAIN_LEN):
        rows = pl.ds(N_CHAIN * t, N_CHAIN)
        br = hbuf[rows, cre:cre + HALF_FLAT]
        bi = hbuf[rows, cim:cim + HALF_FLAT]
        nr = ar * hr - ai * hi + br
        ni = ar * hi + ai * hr + bi
        if store:
            hbuf[rows, cre:cre + HALF_FLAT] = nr
            hbuf[rows, cim:cim + HALF_FLAT] = ni
        hr, hi = nr, ni
    return hr, hi


N_MIXER_WEIGHTS = 14
N_PAR = 2


def _mixer_kernel(is_prompt, tiles_per_stream, *refs):
    n_in = 1 if is_prompt else 4
    ins = refs[:n_in]
    (gmix, win, wb, wcre, wcim, dsk, wglu, poolw, pscale, wout, are, aim, a32re, a32im) = refs[
        n_in:n_in + N_MIXER_WEIGHTS]
    x2_ref, stre_ref, stim_ref, hist_ref = refs[n_in + N_MIXER_WEIGHTS:n_in + N_MIXER_WEIGHTS + 4]
    scratch = refs[n_in + N_MIXER_WEIGHTS + 4:]
    per = len(scratch) // N_PAR
    lanes = [scratch[p * per:(p + 1) * per] for p in range(N_PAR)]
    x_ref = ins[0]

    tile_in_stream = pl.program_id(0) % tiles_per_stream
    n_lane_blocks = D_MODEL // LANES
    half_w = SSM_WIDTH // 2
    ar_full = are[...]
    ai_full = aim[...]

    if is_prompt:
        @pl.when(pl.program_id(0) == 0)
        def _():
            for p in range(N_PAR):
                cre_s, cim_s, pcarry = lanes[p][8], lanes[p][9], lanes[p][10]
                cre_s[...] = jnp.zeros_like(cre_s)
                cim_s[...] = jnp.zeros_like(cim_s)
                pcarry[...] = jnp.zeros_like(pcarry)

    def a_half(h):
        f0 = h * HALF_FLAT
        return (jnp.broadcast_to(ar_full[:, f0:f0 + HALF_FLAT], (N_CHAIN, HALF_FLAT)),
                jnp.broadcast_to(ai_full[:, f0:f0 + HALF_FLAT], (N_CHAIN, HALF_FLAT)))

    def project(p):
        xperm, hbuf = lanes[p][0], lanes[p][1]
        for j in range(N_CHAIN):
            for cb in range(n_lane_blocks):
                xperm[cb, pl.ds(j, CHAIN_LEN, stride=N_CHAIN), :] = x_ref[
                    p, CHAIN_LEN * j:CHAIN_LEN * (j + 1), cb * LANES:(cb + 1) * LANES]
        x = jnp.concatenate([xperm[cb] for cb in range(n_lane_blocks)], axis=1)
        xn = _rms(x, gmix[...]).astype(BF16)
        proj = jnp.dot(xn, win[...], preferred_element_type=F32)
        ub = proj[:, :SSM_WIDTH].astype(BF16)
        for h in range(2):
            hbuf[:, h * N_FLAT:(h + 1) * N_FLAT] = jnp.dot(
                ub[:, h * half_w:(h + 1) * half_w], wb[h], preferred_element_type=F32)
        return x, proj

    def recur(p):
        hbuf = lanes[p][1]
        if is_prompt:
            fre, fim, hre, him, cre_s, cim_s = lanes[p][4:10]
            zeros = jnp.zeros((N_CHAIN, HALF_FLAT), F32)
            for h in range(2):
                f0 = h * HALF_FLAT
                ar, ai = a_half(h)
                lr, li = _scan_half(hbuf, h, ar, ai, zeros, zeros, store=False)
                fre[:, f0:f0 + HALF_FLAT] = lr
                fim[:, f0:f0 + HALF_FLAT] = li
            fresh = tile_in_stream == 0
            hre[0:1, :] = jnp.where(fresh, 0.0, cre_s[...])
            him[0:1, :] = jnp.where(fresh, 0.0, cim_s[...])
            p_re = a32re[...]
            p_im = a32im[...]
            for j in range(N_CHAIN - 1):
                sr = hre[j:j + 1, :]
                si = him[j:j + 1, :]
                hre[j + 1:j + 2, :] = fre[j:j + 1, :] + p_re * sr - p_im * si
                him[j + 1:j + 2, :] = fim[j:j + 1, :] + p_re * si + p_im * sr
            init_re = hre[...]
            init_im = him[...]
        else:
            init_re = ins[1][p]
            init_im = ins[2][p]
        fin_re = []
        fin_im = []
        for h in range(2):
            f0 = h * HALF_FLAT
            ar, ai = a_half(h)
            er, ei = _scan_half(hbuf, h, ar, ai, init_re[:, f0:f0 + HALF_FLAT],
                                init_im[:, f0:f0 + HALF_FLAT], store=True)
            fin_re.append(er)
            fin_im.append(ei)
        end_re = jnp.concatenate(fin_re, axis=1)
        end_im = jnp.concatenate(fin_im, axis=1)
        if is_prompt:
            cre_s[...] = end_re[N_CHAIN - 1:N_CHAIN, :]
            cim_s[...] = end_im[N_CHAIN - 1:N_CHAIN, :]
            stre_ref[p] = end_re[N_CHAIN - 1:N_CHAIN, :]
            stim_ref[p] = end_im[N_CHAIN - 1:N_CHAIN, :]
        else:
            stre_ref[p] = end_re
            stim_ref[p] = end_im

    def finish(p, x, proj):
        hbuf, res, xpbuf = lanes[p][1], lanes[p][2], lanes[p][3]
        u_s = proj[:, :SSM_WIDTH]
        u_p = proj[:, SSM_WIDTH:SSM_WIDTH + POOL_WIDTH]
        gate_s = proj[:, SSM_WIDTH + POOL_WIDTH:SSM_WIDTH + POOL_WIDTH + D_MODEL]
        gate_p = proj[:, SSM_WIDTH + POOL_WIDTH + D_MODEL:]
        ys = []
        for h in range(2):
            c0 = h * N_FLAT
            h_re = hbuf[:, c0:c0 + HALF_FLAT].astype(BF16)
            h_im = hbuf[:, c0 + HALF_FLAT:c0 + N_FLAT].astype(BF16)
            ys.append(jnp.dot(h_re, wcre[h], preferred_element_type=F32)
                      - jnp.dot(h_im, wcim[h], preferred_element_type=F32))
        y = jnp.concatenate(ys, axis=1) + dsk[...] * u_s
        g = _gelu_tanh(y).astype(BF16)
        glu = jnp.dot(g, wglu[...], preferred_element_type=F32)
        o_s = glu[:, :D_MODEL] * _sigmoid(glu[:, D_MODEL:])

        xpbuf[HIST_ROWS:HIST_ROWS + TM, :] = u_p
        tail = u_p[TM - HIST_ROWS:, :]
        row = lax.broadcasted_iota(jnp.int32, (TM, 1), 0)
        if is_prompt:
            pcarry = lanes[p][10]
            first_chain = (lax.broadcasted_iota(jnp.int32, (HIST_ROWS, POOL_WIDTH), 0) % N_CHAIN) == 0
            carried = jnp.where(tile_in_stream == 0, 0.0, pcarry[...])
            xpbuf[0:HIST_ROWS, :] = jnp.where(first_chain, carried, pltpu.roll(tail, 1, 0))
            new_carry = pltpu.roll(tail, HIST_ROWS - (N_CHAIN - 1), 0)
            pcarry[...] = new_carry
            hist_ref[p] = new_carry
            pos1 = tile_in_stream * TM + CHAIN_LEN * (row % N_CHAIN) + row // N_CHAIN + 1
        else:
            xpbuf[0:HIST_ROWS, :] = ins[3][p]
            hist_ref[p] = tail
            pos1 = PAST_LEN + row // N_CHAIN + 1

        o_ps = []
        for gi, w in enumerate(POOL_WINDOWS):
            c0 = gi * POOL_GROUP_IN
            acc = xpbuf[HIST_ROWS:HIST_ROWS + TM, c0:c0 + POOL_GROUP_IN]
            for k in range(1, w):
                acc = acc + xpbuf[HIST_ROWS - N_CHAIN * k:HIST_ROWS - N_CHAIN * k + TM, c0:c0 + POOL_GROUP_IN]
            cnt = jnp.minimum(w, pos1).astype(F32)
            pooled = acc / cnt
            z = (pooled - u_p[:, c0:c0 + POOL_GROUP_IN]).astype(BF16)
            o_ps.append(jnp.dot(z, poolw[gi], preferred_element_type=F32))
        o_p = jnp.concatenate(o_ps, axis=1) * pscale[...]

        merged = (_sigmoid(gate_s) * o_s + _sigmoid(gate_p) * o_p).astype(BF16)
        x2 = x + jnp.dot(merged, wout[...], preferred_element_type=F32)
        for cb in range(n_lane_blocks):
            res[cb] = x2[:, cb * LANES:(cb + 1) * LANES]
        for j in range(N_CHAIN):
            for cb in range(n_lane_blocks):
                x2_ref[p, CHAIN_LEN * j:CHAIN_LEN * (j + 1), cb * LANES:(cb + 1) * LANES] = res[
                    cb, pl.ds(j, CHAIN_LEN, stride=N_CHAIN), :]

    projected = [None] * N_PAR
    for k in range(N_PAR + 2):
        if 0 <= k - 2 < N_PAR:
            finish(k - 2, *projected[k - 2])
        if 0 <= k - 1 < N_PAR:
            recur(k - 1)
        if k < N_PAR:
            projected[k] = project(k)


def _const_spec(shape):
    nd = len(shape)
    return pl.BlockSpec(shape, lambda i, _nd=nd: (0,) * _nd)


def _mixer_weight_specs():
    return [
        _const_spec((1, D_MODEL)),
        _const_spec((D_MODEL, 3 * D_MODEL)),
        _const_spec((2, SSM_WIDTH // 2, N_FLAT)),
        _const_spec((2, HALF_FLAT, SSM_WIDTH // 2)),
        _const_spec((2, HALF_FLAT, SSM_WIDTH // 2)),
        _const_spec((1, SSM_WIDTH)),
        _const_spec((SSM_WIDTH, 2 * D_MODEL)),
        _const_spec((len(POOL_WINDOWS), POOL_GROUP_IN, POOL_GROUP_OUT)),
        _const_spec((1, D_MODEL)),
        _const_spec((D_MODEL, D_MODEL)),
        _const_spec((1, N_FLAT)),
        _const_spec((1, N_FLAT)),
        _const_spec((1, N_FLAT)),
        _const_spec((1, N_FLAT)),
    ]


def _mixer_common_scratch():
    return [
        pltpu.VMEM((D_MODEL // LANES, TM, LANES), F32),
        pltpu.VMEM((TM, 2 * N_FLAT), F32),
        pltpu.VMEM((D_MODEL // LANES, TM, LANES), F32),
        pltpu.VMEM((HIST_ROWS + TM, POOL_WIDTH), F32),
    ]


def _mixer_prompt(x, weights):
    n_streams, seq, _ = x.shape
    assert n_streams % N_PAR == 0 and seq % TM == 0
    tiles_per_stream = seq // TM
    blk = lambda shape: pl.BlockSpec(shape, lambda i: (i // tiles_per_stream, 0, 0))
    row_spec = pl.BlockSpec((N_PAR, TM, D_MODEL), lambda i: (i // tiles_per_stream, i % tiles_per_stream, 0))
    lane_scratch = _mixer_common_scratch() + [
        pltpu.VMEM((N_CHAIN, N_FLAT), F32), pltpu.VMEM((N_CHAIN, N_FLAT), F32),
        pltpu.VMEM((N_CHAIN, N_FLAT), F32), pltpu.VMEM((N_CHAIN, N_FLAT), F32),
        pltpu.VMEM((1, N_FLAT), F32), pltpu.VMEM((1, N_FLAT), F32),
        pltpu.VMEM((HIST_ROWS, POOL_WIDTH), F32),
    ]
    return pl.pallas_call(
        functools.partial(_mixer_kernel, True, tiles_per_stream),
        grid=(n_streams // N_PAR * tiles_per_stream,),
        in_specs=[row_spec] + _mixer_weight_specs(),
        out_specs=[row_spec, blk((N_PAR, 1, N_FLAT)), blk((N_PAR, 1, N_FLAT)),
                   blk((N_PAR, HIST_ROWS, POOL_WIDTH))],
        out_shape=[jax.ShapeDtypeStruct((n_streams, seq, D_MODEL), F32),
                   jax.ShapeDtypeStruct((n_streams, 1, N_FLAT), F32),
                   jax.ShapeDtypeStruct((n_streams, 1, N_FLAT), F32),
                   jax.ShapeDtypeStruct((n_streams, HIST_ROWS, POOL_WIDTH), F32)],
        scratch_shapes=lane_scratch * N_PAR,
        compiler_params=pltpu.CompilerParams(dimension_semantics=("arbitrary",), vmem_limit_bytes=VMEM_LIMIT),
        name="mixer_prompt",
    )(x, *weights)


def _mixer_sample(x, h0_re, h0_im, cache_t, weights):
    n_streams, seq, _ = x.shape
    assert seq == CHAIN_LEN and n_streams % (N_CHAIN * N_PAR) == 0
    n_tiles = n_streams // N_CHAIN
    blk = lambda shape: pl.BlockSpec(shape, lambda i: (i, 0, 0))
    row_spec = blk((N_PAR, TM, D_MODEL))
    st_spec = blk((N_PAR, N_CHAIN, N_FLAT))
    hist_spec = blk((N_PAR, HIST_ROWS, POOL_WIDTH))
    tiles = lambda v: v.reshape((n_tiles, -1) + v.shape[-1:])
    return pl.pallas_call(
        functools.partial(_mixer_kernel, False, 1),
        grid=(n_tiles // N_PAR,),
        in_specs=[row_spec, st_spec, st_spec, hist_spec] + _mixer_weight_specs(),
        out_specs=[row_spec, st_spec, st_spec, hist_spec],
        out_shape=[jax.ShapeDtypeStruct((n_tiles, TM, D_MODEL), F32),
                   jax.ShapeDtypeStruct((n_tiles, N_CHAIN, N_FLAT), F32),
                   jax.ShapeDtypeStruct((n_tiles, N_CHAIN, N_FLAT), F32),
                   jax.ShapeDtypeStruct((n_tiles, HIST_ROWS, POOL_WIDTH), F32)],
        scratch_shapes=_mixer_common_scratch() * N_PAR,
        compiler_params=pltpu.CompilerParams(dimension_semantics=("arbitrary",), vmem_limit_bytes=VMEM_LIMIT),
        name="mixer_sample",
    )(tiles(x), tiles(h0_re), tiles(h0_im), cache_t, *weights)


def _to_chunks(lo, hi):
    n = lo.shape[0] // CHUNK
    half = D_MODEL // 2
    both = jnp.concatenate([lo.reshape(n, CHUNK, half), hi.reshape(n, CHUNK, half)], axis=1)
    return both.astype(BF16)


def _from_chunks(blk):
    n = blk.shape[0]
    half = D_MODEL // 2
    f = blk.astype(F32)
    lo = f[:, :CHUNK, :].reshape(n * CHUNK, half).astype(BF16)
    hi = f[:, CHUNK:, :].reshape(n * CHUNK, half).astype(BF16)
    return lo, hi


def _route_kernel(n_prompt_steps, x2p_ref, x2s_ref, g_ref, wr_ref, ltri_ref, utri_ref,
                  xs_ref, route_ref, nch_ref):
    is_prompt = pl.program_id(0) < n_prompt_steps
    tiles = range(STEP_TILES)
    each = lambda fn, *cols: [fn(*(c[h] for c in cols)) for h in tiles]
    row_max = lambda v: jnp.max(v, axis=1, keepdims=True)
    row_min = lambda v: jnp.min(v, axis=1, keepdims=True)
    row_sum = lambda v: jnp.sum(v, axis=1, keepdims=True)

    lane = lax.broadcasted_iota(jnp.int32, (TM, LANES), 1)
    lane_f = lane.astype(F32)
    big = jnp.float32(1 << 20)
    neg = jnp.float32(-jnp.inf)
    gmask = lane < N_EXPERT_GROUPS
    eid = lane - N_EXPERT_GROUPS
    lane_grp = (eid >> 3).astype(F32)
    is_expert = (eid >= 0) & (eid < N_EXPERTS)

    xn = [_rms(jnp.where(is_prompt, x2p_ref[pl.ds(h * TM, TM), :], x2s_ref[pl.ds(h * TM, TM), :]),
               g_ref[...]).astype(BF16) for h in tiles]
    logits = each(lambda v: jnp.dot(v, wr_ref[...], preferred_element_type=F32), xn)
    m = each(lambda lg: row_max(jnp.where(gmask, lg, neg)), logits)
    grp = each(lambda lg, mm: row_min(jnp.where(gmask & (lg == mm), lane_f, big)), logits, m)
    wg = each(lambda lg, mm: 1.0 / row_sum(jnp.where(gmask, jnp.exp(lg - mm), 0.0)), logits, m)
    emask = each(lambda g: is_expert & (lane_grp == g), grp)
    v1 = each(lambda lg, em: row_max(jnp.where(em, lg, neg)), logits, emask)
    i1 = each(lambda lg, em, v: row_min(jnp.where(em & (lg == v), lane_f, big)), logits, emask, v1)
    emask2 = each(lambda em, i: em & (lane_f != i), emask, i1)
    v2 = each(lambda lg, em: row_max(jnp.where(em, lg, neg)), logits, emask2)
    i2 = each(lambda lg, em, v: row_min(jnp.where(em & (lg == v), lane_f, big)), logits, emask2, v2)
    e21 = each(lambda a, b: jnp.exp(b - a), v1, v2)
    w1 = each(lambda g, e: g / (1.0 + e), wg, e21)
    w2 = each(lambda g, e: g * e / (1.0 + e), wg, e21)

    a1 = each(lambda i: lane_f == (i - N_EXPERT_GROUPS), i1)
    a2 = each(lambda i: lane_f == (i - N_EXPERT_GROUPS), i2)
    a = each(lambda p, q: (p | q).astype(F32), a1, a2)
    before = each(lambda v: jnp.dot(ltri_ref[...], v.astype(BF16), preferred_element_type=F32), a)
    cnt = each(lambda v: jnp.sum(v, axis=0, keepdims=True), a)
    nch16 = each(lambda c: jnp.broadcast_to(jnp.floor((c + (CHUNK - 1)) * (1.0 / CHUNK)), (16, LANES)), cnt)
    start = each(lambda n: jnp.dot(n.astype(BF16), utri_ref[...], preferred_element_type=F32), nch16)
    slot = each(lambda bf, st: bf + CHUNK * st[0:1, :], before, start)
    d1 = each(lambda p, sl: row_sum(jnp.where(p, sl, 0.0)), a1, slot)
    d2 = each(lambda p, sl: row_sum(jnp.where(p, sl, 0.0)), a2, slot)
    route = each(lambda p, q, u, v: jnp.where(lane == 0, p, jnp.where(lane == 1, q, jnp.where(
        lane == 2, u, jnp.where(lane == 3, v, 0.0)))), d1, d2, w1, w2)

    dest = lax.broadcasted_iota(jnp.int32, (CAP_ROWS, TM), 0)
    half = D_MODEL // 2
    for h in tiles:
        route_ref[pl.ds(h * TM, TM), :] = route[h]
        nch_ref[h] = nch16[h][0:8, :].astype(jnp.int32)
        dt = jnp.transpose(jnp.where(lane < 2, route[h], 0.0)).astype(jnp.int32)
        perm = ((dest == dt[0:1, :]) | (dest == dt[1:2, :])).astype(F32).astype(BF16)
        lo = jnp.dot(perm, xn[h][:, :half], preferred_element_type=F32)
        hi = jnp.dot(perm, xn[h][:, half:], preferred_element_type=F32)
        xs_ref[pl.ds(h * CAP_CHUNKS, CAP_CHUNKS)] = _to_chunks(lo, hi)


def _route(x2p, x2s, g_ffn, w_router):
    n_prompt_tiles = x2p.shape[0] // TM
    n_tiles = n_prompt_tiles + x2s.shape[0] // TM
    assert n_prompt_tiles % STEP_TILES == 0 and n_tiles % STEP_TILES == 0
    n_prompt_steps = n_prompt_tiles // STEP_TILES
    rows = STEP_TILES * TM
    r = jnp.arange(TM)
    ltri = (r[None, :] < r[:, None]).astype(BF16)
    e = jnp.arange(LANES)
    utri = (e[:, None] < e[None, :]).astype(BF16)
    return pl.pallas_call(
        functools.partial(_route_kernel, n_prompt_steps),
        grid=(n_tiles // STEP_TILES,),
        in_specs=[pl.BlockSpec((rows, D_MODEL), lambda i: (jnp.minimum(i, n_prompt_steps - 1), 0)),
                  pl.BlockSpec((rows, D_MODEL), lambda i: (jnp.maximum(i - n_prompt_steps, 0), 0)),
                  _const_spec((1, D_MODEL)), _const_spec((D_MODEL, LANES)),
                  _const_spec((TM, TM)), _const_spec((LANES, LANES))],
        out_specs=[pl.BlockSpec((STEP_TILES * CAP_CHUNKS, 2 * CHUNK, D_MODEL // 2), lambda i: (i, 0, 0)),
                   pl.BlockSpec((rows, LANES), lambda i: (i, 0)),
                   pl.BlockSpec((STEP_TILES, 8, LANES), lambda i: (i, 0, 0))],
        out_shape=[jax.ShapeDtypeStruct((n_tiles * CAP_CHUNKS, 2 * CHUNK, D_MODEL // 2), BF16),
                   jax.ShapeDtypeStruct((n_tiles * TM, LANES), F32),
                   jax.ShapeDtypeStruct((n_tiles, 8, LANES), jnp.int32)],
        compiler_params=pltpu.CompilerParams(dimension_semantics=("arbitrary",), vmem_limit_bytes=VMEM_LIMIT),
        name="route_sort",
    )(x2p, x2s, g_ffn, w_router, ltri, utri)


def _spare_chunk(k):
    return (1 + k // N_SPARE_SLOTS) * CAP_CHUNKS + MAX_USED_CHUNKS + k % N_SPARE_SLOTS


def _chunk_tables(nch, n_expert_tiles):
    n_tiles = nch.shape[0]
    i32 = jnp.int32
    start = jnp.cumsum(nch, axis=1) - nch
    off = jnp.cumsum(nch, axis=0) - nch
    per_expert = jnp.sum(nch, axis=0)
    tiles_e = (per_expert + TILE_CHUNKS - 1) // TILE_CHUNKS
    cum_tiles = jnp.cumsum(tiles_e)
    first_tile = cum_tiles - tiles_e
    n_active = cum_tiles[-1].astype(i32)
    t = jnp.arange(n_expert_tiles, dtype=i32)
    te = jnp.minimum(jnp.sum((t[:, None] >= cum_tiles[None, :]).astype(i32), axis=1), N_EXPERTS - 1)
    onehot = te[:, None] == jnp.arange(N_EXPERTS, dtype=i32)[None, :]
    pick = lambda tab: jnp.sum(jnp.where(onehot[:, :, None], tab.T[None, :, :], 0), axis=1)
    off_t, nch_t, start_t = pick(off), pick(nch), pick(start)
    k = t - jnp.sum(jnp.where(onehot, first_tile[None, :], 0), axis=1)
    q = (TILE_CHUNKS * k)[:, None] + jnp.arange(TILE_CHUNKS, dtype=i32)[None, :]
    in_run = (off_t[:, None, :] <= q[:, :, None]) & (q[:, :, None] < (off_t + nch_t)[:, None, :])
    run_src = (jnp.arange(n_tiles, dtype=i32) * CAP_CHUNKS)[None, :] + start_t - off_t
    src = jnp.sum(jnp.where(in_run, run_src[:, None, :], 0), axis=-1) + q
    valid = jnp.any(in_run, axis=-1) & (t < n_active)[:, None]
    src = jnp.where(valid, src, ZERO_CHUNK).astype(i32)
    spare = _spare_chunk((t % N_RING)[:, None] * TILE_CHUNKS + jnp.arange(TILE_CHUNKS, dtype=i32)[None, :])
    dst = jnp.where(valid, src, spare).astype(i32)
    dst = jnp.concatenate([spare[1:N_RING], dst], axis=0)
    return first_tile.astype(i32), tiles_e.astype(i32), src.reshape(-1), dst.reshape(-1), n_active.reshape(1)


N_RING = 5


def _expert_kernel(first_ref, ntile_ref, src_ref, dst_ref, nact_ref, xs_hbm, wg_ref, wu_ref, wd_ref, ys_hbm,
                   *scratch):
    xbufs = scratch[:N_RING]
    obufs = scratch[N_RING:2 * N_RING]
    gsem, ssem, wg16, wu16, wd16 = scratch[2 * N_RING:]
    e = pl.program_id(0)
    n_active = nact_ref[0]
    ahead = N_RING - 1

    def gather_copy(tile, slot, c):
        return pltpu.make_async_copy(xs_hbm.at[src_ref[tile * TILE_CHUNKS + c]], xbufs[slot].at[c],
                                     gsem.at[slot])

    def scatter_copy(tile, slot, c):
        return pltpu.make_async_copy(obufs[slot].at[c], ys_hbm.at[dst_ref[(tile + ahead) * TILE_CHUNKS + c]],
                                     ssem.at[slot])

    def start_all(copy, tile, slot):
        for c in range(TILE_CHUNKS):
            copy(tile, slot, c).start()

    def wait_all(copy, tile, slot):
        for c in range(TILE_CHUNKS):
            copy(tile, slot, c).wait()

    @pl.when(e == 0)
    def _():
        for v in range(ahead):
            start_all(gather_copy, v, v)
        for u in range(-ahead, 0):
            obufs[u % N_RING][...] = jnp.zeros_like(obufs[u % N_RING])
        for u in range(-ahead, -1):
            start_all(scatter_copy, u, u % N_RING)

    wg16[...] = wg_ref[0].astype(BF16)
    wu16[...] = wu_ref[0].astype(BF16)
    wd16[...] = wd_ref[0].astype(BF16)

    def do_tile(tile, slot):
        nxt = (slot + ahead) % N_RING
        old = (slot + 1) % N_RING
        wait_all(gather_copy, tile, slot)
        lo, hi = _from_chunks(xbufs[slot][...])
        x = jnp.concatenate([lo, hi], axis=1)
        gate = jnp.dot(x, wg16[...], preferred_element_type=F32)
        start_all(scatter_copy, tile - 1, nxt)
        up = jnp.dot(x, wu16[...], preferred_element_type=F32)
        hmid = (gate * _sigmoid(gate) * up).astype(BF16)
        start_all(gather_copy, tile + ahead, nxt)
        y = jnp.dot(hmid, wd16[...], preferred_element_type=F32)
        half = D_MODEL // 2
        wait_all(scatter_copy, tile - ahead, old)
        obufs[slot][...] = _to_chunks(y[:, :half], y[:, half:])

    def tile_body(k, carry):
        tile = first_ref[e] + k
        for slot in range(N_RING):
            @pl.when(tile % N_RING == slot)
            def _():
                do_tile(tile, slot)
        return carry

    lax.fori_loop(0, ntile_ref[e], tile_body, 0)

    @pl.when(e == pl.num_programs(0) - 1)
    def _():
        last = n_active - 1
        for slot in range(N_RING):
            @pl.when(last % N_RING == slot)
            def _():
                for d in range(N_RING - 2, 0, -1):
                    wait_all(scatter_copy, last - d, (slot - d) % N_RING)
                start_all(scatter_copy, last, slot)
                wait_all(scatter_copy, last, slot)
                for d in range(1, N_RING):
                    wait_all(gather_copy, last + d, (slot + d) % N_RING)


def _experts(first_tile, tiles_e, src, dst, n_active, xs, w_gate, w_up, w_down):
    chunk_shape = (2 * CHUNK, D_MODEL // 2)
    assert _spare_chunk(N_RING * TILE_CHUNKS - 1) < xs.shape[0]
    tile_buf = pltpu.VMEM((TILE_CHUNKS,) + chunk_shape, BF16)
    grid_spec = pltpu.PrefetchScalarGridSpec(
        num_scalar_prefetch=5,
        grid=(N_EXPERTS,),
        in_specs=[pl.BlockSpec(memory_space=pl.ANY),
                  pl.BlockSpec((1, D_MODEL, D_EXPERT), lambda e, *_: (e, 0, 0)),
                  pl.BlockSpec((1, D_MODEL, D_EXPERT), lambda e, *_: (e, 0, 0)),
                  pl.BlockSpec((1, D_EXPERT, D_MODEL), lambda e, *_: (e, 0, 0))],
        out_specs=pl.BlockSpec(memory_space=pl.ANY),
        scratch_shapes=[tile_buf] * (2 * N_RING) + [
                        pltpu.SemaphoreType.DMA((N_RING,)),
                        pltpu.SemaphoreType.DMA((N_RING,)),
                        pltpu.VMEM((D_MODEL, D_EXPERT), BF16),
                        pltpu.VMEM((D_MODEL, D_EXPERT), BF16),
                        pltpu.VMEM((D_EXPERT, D_MODEL), BF16)],
    )
    return pl.pallas_call(
        _expert_kernel,
        grid_spec=grid_spec,
        out_shape=jax.ShapeDtypeStruct(xs.shape, xs.dtype),
        input_output_aliases={5: 0},
        compiler_params=pltpu.CompilerParams(dimension_semantics=("arbitrary",), vmem_limit_bytes=VMEM_LIMIT),
        name="expert_ffn",
    )(first_tile, tiles_e, src, dst, n_active, xs, w_gate, w_up, w_down)


def _combine_kernel(n_prompt_steps, x2p_ref, x2s_ref, route_ref, gfin_ref, ys_ref, outp_ref, outs_ref):
    i = pl.program_id(0)
    is_prompt = i < n_prompt_steps
    outs = []
    for h in range(STEP_TILES):
        rows = pl.ds(h * TM, TM)
        lo, hi = _from_chunks(ys_ref[pl.ds(h * CAP_CHUNKS, CAP_CHUNKS)])
        route = route_ref[rows, :]
        d1 = route[:, 0:1].astype(jnp.int32)
        d2 = route[:, 1:2].astype(jnp.int32)
        w1 = route[:, 2:3]
        w2 = route[:, 3:4]
        dest = lax.broadcasted_iota(jnp.int32, (TM, CAP_ROWS), 1)
        sel1 = (dest == d1).astype(F32).astype(BF16)
        sel2 = (dest == d2).astype(F32).astype(BF16)
        m1 = jnp.concatenate([jnp.dot(sel1, lo, preferred_element_type=F32),
                              jnp.dot(sel1, hi, preferred_element_type=F32)], axis=1)
        m2 = jnp.concatenate([jnp.dot(sel2, lo, preferred_element_type=F32),
                              jnp.dot(sel2, hi, preferred_element_type=F32)], axis=1)
        x = jnp.where(is_prompt, x2p_ref[rows, :], x2s_ref[rows, :])
        outs.append(_rms(x + (w1 * m1 + w2 * m2), gfin_ref[...]))

    @pl.when(is_prompt)
    def _():
        for h in range(STEP_TILES):
            outp_ref[pl.ds(h * TM, TM), :] = outs[h]

    @pl.when(jnp.logical_not(is_prompt))
    def _():
        for h in range(STEP_TILES):
            outs_ref[pl.ds(h * TM, TM), :] = outs[h]


def _combine(x2p, x2s, route, g_final, ys):
    n_prompt_tiles = x2p.shape[0] // TM
    n_tiles = n_prompt_tiles + x2s.shape[0] // TM
    assert n_prompt_tiles % STEP_TILES == 0 and n_tiles % STEP_TILES == 0
    n_prompt_steps = n_prompt_tiles // STEP_TILES
    rows = STEP_TILES * TM
    p_spec = pl.BlockSpec((rows, D_MODEL), lambda i: (jnp.minimum(i, n_prompt_steps - 1), 0))
    s_spec = pl.BlockSpec((rows, D_MODEL), lambda i: (jnp.maximum(i - n_prompt_steps, 0), 0))
    return pl.pallas_call(
        functools.partial(_combine_kernel, n_prompt_steps),
        grid=(n_tiles // STEP_TILES,),
        in_specs=[p_spec, s_spec,
                  pl.BlockSpec((rows, LANES), lambda i: (i, 0)),
                  _const_spec((1, D_MODEL)),
                  pl.BlockSpec((STEP_TILES * CAP_CHUNKS, 2 * CHUNK, D_MODEL // 2), lambda i: (i, 0, 0))],
        out_specs=[p_spec, s_spec],
        out_shape=[jax.ShapeDtypeStruct(x2p.shape, F32), jax.ShapeDtypeStruct(x2s.shape, F32)],
        compiler_params=pltpu.CompilerParams(dimension_semantics=("arbitrary",), vmem_limit_bytes=VMEM_LIMIT),
        name="combine_norm",
    )(x2p, x2s, route, g_final, ys)


def _block_diag_in(bb):
    t = bb.reshape(2, 16, SSM_GROUP, SSM_STATE)
    eye = jnp.eye(16, dtype=bb.dtype)
    blk = t[:, :, :, None, :] * eye[None, :, None, :, None]
    return blk.reshape(2, 16 * SSM_GROUP, 16 * SSM_STATE)


def _block_diag_out(c):
    t = c.reshape(2, 16, SSM_GROUP, SSM_STATE).transpose(0, 1, 3, 2)
    eye = jnp.eye(16, dtype=c.dtype)
    blk = t[:, :, :, None, :] * eye[None, :, None, :, None]
    return blk.reshape(2, 16 * SSM_STATE, 16 * SSM_GROUP)


def kernel(x_prompt, x_sample, state_ssm_re, state_ssm_im, cache_pool, g_mix, w_in, ssm_a_re, ssm_a_im,
           ssm_log_dt, ssm_b_re, ssm_b_im, ssm_c_re, ssm_c_im, ssm_d, w_glu_a, w_glu_b, pool_w, pool_scale,
           w_out, g_ffn, w_router_group, w_router_expert, w_exp_gate, w_exp_up, w_exp_down, g_final):
    li = 0
    n_pb, seq_p, _ = x_prompt.shape
    n_sb, seq_s, _ = x_sample.shape

    a_re, a_im, a32_re, a32_im, bb_re, bb_im = _discretise(
        ssm_a_re[li], ssm_a_im[li], ssm_log_dt[li], ssm_b_re[li], ssm_b_im[li])
    row = lambda v: v.reshape(1, N_FLAT)
    wb = jnp.concatenate([_block_diag_in(bb_re), _block_diag_in(bb_im)], axis=2).astype(BF16)
    weights = [
        g_mix[li].reshape(1, D_MODEL),
        w_in[li].astype(BF16),
        wb,
        _block_diag_out(ssm_c_re[li]).astype(BF16),
        _block_diag_out(ssm_c_im[li]).astype(BF16),
        ssm_d[li].reshape(1, SSM_WIDTH),
        jnp.concatenate([w_glu_a[li], w_glu_b[li]], axis=1).astype(BF16),
        pool_w[li].astype(BF16),
        pool_scale[li].reshape(1, D_MODEL),
        w_out[li].astype(BF16),
        row(a_re), row(a_im), row(a32_re), row(a32_im),
    ]

    x2p, stp_re, stp_im, histp = _mixer_prompt(x_prompt, weights)
    x2p = x2p.reshape(n_pb * seq_p, D_MODEL)

    n_stiles = n_sb // N_CHAIN
    cache16 = jnp.pad(cache_pool[li], ((0, 0), (1, 0), (0, 0)))
    cache_t = cache16.reshape(n_stiles, N_CHAIN, 16, POOL_WIDTH).transpose(0, 2, 1, 3).reshape(
        n_stiles, HIST_ROWS, POOL_WIDTH)
    x2s, sts_re, sts_im, hists = _mixer_sample(
        x_sample, state_ssm_re[li].reshape(n_sb, N_FLAT), state_ssm_im[li].reshape(n_sb, N_FLAT),
        cache_t, weights)
    x2s = x2s.reshape(n_sb * seq_s, D_MODEL)

    w_router = jnp.concatenate(
        [w_router_group[li], w_router_expert[li].reshape(D_MODEL, N_EXPERTS),
         jnp.zeros((D_MODEL, LANES - N_EXPERT_GROUPS - N_EXPERTS), F32)], axis=1).astype(BF16)
    xs, route, nch = _route(x2p, x2s, g_ffn[li].reshape(1, D_MODEL), w_router)

    n_tiles = (x2p.shape[0] + x2s.shape[0]) // TM
    max_chunks = n_tiles * (2 * TM // CHUNK + N_EXPERTS)
    n_expert_tiles = max_chunks // TILE_CHUNKS + N_EXPERTS
    first_tile, tiles_e, src, dst, n_active = _chunk_tables(nch[:, 0, :N_EXPERTS], n_expert_tiles)
    ys = _experts(first_tile, tiles_e, src, dst, n_active, xs, w_exp_gate[li], w_exp_up[li], w_exp_down[li])
    yp, ysm = _combine(x2p, x2s, route, g_final.reshape(1, D_MODEL), ys)

    sd = state_ssm_re.dtype
    cd = cache_pool.dtype
    y_prompt = yp.reshape(n_pb, seq_p, D_MODEL)
    y_sample = ysm.reshape(n_sb, seq_s, D_MODEL)
    re_p = stp_re.reshape(1, n_pb, SSM_GROUPS, SSM_STATE).astype(sd)
    im_p = stp_im.reshape(1, n_pb, SSM_GROUPS, SSM_STATE).astype(sd)
    hist_p = histp[:, ::N_CHAIN, :][:, 1:, :].reshape(1, n_pb, POOL_HIST, POOL_WIDTH).astype(cd)
    re_s = sts_re.reshape(1, n_sb, SSM_GROUPS, SSM_STATE).astype(sd)
    im_s = sts_im.reshape(1, n_sb, SSM_GROUPS, SSM_STATE).astype(sd)
    hist_s = hists.reshape(n_stiles, 16, N_CHAIN, POOL_WIDTH).transpose(0, 2, 1, 3).reshape(
        n_sb, 16, POOL_WIDTH)[:, 1:, :].reshape(1, n_sb, POOL_HIST, POOL_WIDTH).astype(cd)
    return (y_prompt, y_sample, re_p, im_p, hist_p, re_s, im_s, hist_s)
```

```python
import functools
import math

import jax
import jax.numpy as jnp
from jax import lax
from jax.experimental import pallas as pl
from jax.experimental.pallas import tpu as pltpu

F32 = jnp.float32
BF16 = jnp.bfloat16

D_MODEL = 1024
SSM_WIDTH = 512
SSM_GROUPS = 32
SSM_GROUP = 16
SSM_STATE = 64
N_FLAT = SSM_GROUPS * SSM_STATE
HALF_FLAT = N_FLAT // 2
POOL_WIDTH = 512
POOL_WINDOWS = (2, 4, 8, 16)
POOL_GROUP_IN = 128
POOL_GROUP_OUT = 256
POOL_HIST = 15
N_EXPERTS = 32
EXPERTS_PER_GROUP = 8
N_EXPERT_GROUPS = 4
D_EXPERT = 512
EPS = 1e-6
PAST_LEN = 1024

TM = 256
N_CHAIN = 8
CHAIN_LEN = TM // N_CHAIN
HIST_ROWS = 16 * N_CHAIN

CHUNK = 8
TILE_CHUNKS = TM // CHUNK
CAP_CHUNKS = 96
CAP_ROWS = CAP_CHUNKS * CHUNK
MAX_USED_CHUNKS = 2 * TM // CHUNK + N_EXPERTS * (CHUNK - 1) // CHUNK
N_SPARE_SLOTS = CAP_CHUNKS - MAX_USED_CHUNKS - 1
ZERO_CHUNK = CAP_CHUNKS - 1
LANES = 128

V7X_VMEM_BYTES = 64 * 1024 * 1024
VMEM_LIMIT = V7X_VMEM_BYTES * 13 // 16
STEP_TILES = 4


def _rms(x, g):
    r = lax.rsqrt(jnp.mean(x * x, axis=-1, keepdims=True) + EPS)
    return x * r * g


def _sigmoid(x):
    return 0.5 * jnp.tanh(0.5 * x) + 0.5


def _gelu_tanh(x):
    c = math.sqrt(2.0 / math.pi)
    return x * (0.5 * (1.0 + jnp.tanh(c * (x + 0.044715 * (x * x * x)))))


def _disc_kernel(lre_ref, lim_ref, ldt_ref, bre_ref, bim_ref,
                 are_ref, aim_ref, a32re_ref, a32im_ref, bbre_ref, bbim_ref):
    lam_re = jnp.minimum(lre_ref[...], -1e-4)
    lam_im = lim_ref[...]
    dt = jnp.exp(ldt_ref[...])
    mag = jnp.exp(lam_re * dt)
    ang = lam_im * dt
    a_re = mag * jnp.cos(ang)
    a_im = mag * jnp.sin(ang)
    num_re = a_re - 1.0
    num_im = a_im
    den = lam_re * lam_re + lam_im * lam_im
    k_re = ((num_re * lam_re + num_im * lam_im) / den)[:, None, :]
    k_im = ((num_im * lam_re - num_re * lam_im) / den)[:, None, :]
    br = bre_ref[...]
    bi = bim_ref[...]
    bbre_ref[...] = k_re * br - k_im * bi
    bbim_ref[...] = k_re * bi + k_im * br
    are_ref[...] = a_re
    aim_ref[...] = a_im
    pr, pi = a_re, a_im
    for _ in range(int(math.log2(CHAIN_LEN))):
        pr, pi = pr * pr - pi * pi, 2.0 * pr * pi
    a32re_ref[...] = pr
    a32im_ref[...] = pi


def _discretise(a_re, a_im, log_dt, b_re, b_im):
    chan_major = lambda b: jnp.swapaxes(b, 1, 2)
    return pl.pallas_call(
        _disc_kernel,
        out_shape=[jax.ShapeDtypeStruct((SSM_GROUPS, SSM_STATE), F32)] * 4
        + [jax.ShapeDtypeStruct((SSM_GROUPS, SSM_GROUP, SSM_STATE), F32)] * 2,
        name="s5_discretise",
    )(a_re, a_im, log_dt.reshape(SSM_GROUPS, 1), chan_major(b_re), chan_major(b_im))


def _scan_half(hbuf, h, ar, ai, init_re, init_im, store):
    cre = h * N_FLAT
    cim = cre + HALF_FLAT
    hr, hi = init_re, init_im
    for t in range(CHAIN_LEN):
        rows = pl.ds(N_CHAIN * t, N_CHAIN)
        br = hbuf[rows, cre:cre + HALF_FLAT]
        bi = hbuf[rows, cim:cim + HALF_FLAT]
        nr = ar * hr - ai * hi + br
        ni = ar * hi + ai * hr + bi
        if store:
            hbuf[rows, cre:cre + HALF_FLAT] = nr
            hbuf[rows, cim:cim + HALF_FLAT] = ni
        hr, hi = nr, ni
    return hr, hi


N_MIXER_WEIGHTS = 14
N_PAR = 2


def _mixer_kernel(is_prompt, tiles_per_stream, *refs):
    n_in = 1 if is_prompt else 4
    ins = refs[:n_in]
    (gmix, win, wb, wcre, wcim, dsk, wglu, poolw, pscale, wout, are, aim, a32re, a32im) = refs[
        n_in:n_in + N_MIXER_WEIGHTS]
    x2_ref, stre_ref, stim_ref, hist_ref = refs[n_in + N_MIXER_WEIGHTS:n_in + N_MIXER_WEIGHTS + 4]
    scratch = refs[n_in + N_MIXER_WEIGHTS + 4:]
    per = len(scratch) // N_PAR
    lanes = [scratch[p * per:(p + 1) * per] for p in range(N_PAR)]
    x_ref = ins[0]

    tile_in_stream = pl.program_id(0) % tiles_per_stream
    n_lane_blocks = D_MODEL // LANES
    half_w = SSM_WIDTH // 2
    ar_full = are[...]
    ai_full = aim[...]

    if is_prompt:
        @pl.when(pl.program_id(0) == 0)
        def _():
            for p in range(N_PAR):
                cre_s, cim_s, pcarry = lanes[p][8], lanes[p][9], lanes[p][10]
                cre_s[...] = jnp.zeros_like(cre_s)
                cim_s[...] = jnp.zeros_like(cim_s)
                pcarry[...] = jnp.zeros_like(pcarry)

    def a_half(h):
        f0 = h * HALF_FLAT
        return (jnp.broadcast_to(ar_full[:, f0:f0 + HALF_FLAT], (N_CHAIN, HALF_FLAT)),
                jnp.broadcast_to(ai_full[:, f0:f0 + HALF_FLAT], (N_CHAIN, HALF_FLAT)))

    def project(p):
        xperm, hbuf = lanes[p][0], lanes[p][1]
        for j in range(N_CHAIN):
            for cb in range(n_lane_blocks):
                xperm[cb, pl.ds(j, CHAIN_LEN, stride=N_CHAIN), :] = x_ref[
                    p, CHAIN_LEN * j:CHAIN_LEN * (j + 1), cb * LANES:(cb + 1) * LANES]
        x = jnp.concatenate([xperm[cb] for cb in range(n_lane_blocks)], axis=1)
        xn = _rms(x, gmix[...]).astype(BF16)
        proj = jnp.dot(xn, win[...], preferred_element_type=F32)
        ub = proj[:, :SSM_WIDTH].astype(BF16)
        for h in range(2):
            hbuf[:, h * N_FLAT:(h + 1) * N_FLAT] = jnp.dot(
                ub[:, h * half_w:(h + 1) * half_w], wb[h], preferred_element_type=F32)
        return x, proj

    def recur(p):
        hbuf = lanes[p][1]
        if is_prompt:
            fre, fim, hre, him, cre_s, cim_s = lanes[p][4:10]
            zeros = jnp.zeros((N_CHAIN, HALF_FLAT), F32)
            for h in range(2):
                f0 = h * HALF_FLAT
                ar, ai = a_half(h)
                lr, li = _scan_half(hbuf, h, ar, ai, zeros, zeros, store=False)
                fre[:, f0:f0 + HALF_FLAT] = lr
                fim[:, f0:f0 + HALF_FLAT] = li
            fresh = tile_in_stream == 0
            hre[0:1, :] = jnp.where(fresh, 0.0, cre_s[...])
            him[0:1, :] = jnp.where(fresh, 0.0, cim_s[...])
            p_re = a32re[...]
            p_im = a32im[...]
            for j in range(N_CHAIN - 1):
                sr = hre[j:j + 1, :]
                si = him[j:j + 1, :]
                hre[j + 1:j + 2, :] = fre[j:j + 1, :] + p_re * sr - p_im * si
                him[j + 1:j + 2, :] = fim[j:j + 1, :] + p_re * si + p_im * sr
            init_re = hre[...]
            init_im = him[...]
        else:
            init_re = ins[1][p]
            init_im = ins[2][p]
        fin_re = []
        fin_im = []
        for h in range(2):
            f0 = h * HALF_FLAT
            ar, ai = a_half(h)
            er, ei = _scan_half(hbuf, h, ar, ai, init_re[:, f0:f0 + HALF_FLAT],
                                init_im[:, f0:f0 + HALF_FLAT], store=True)
            fin_re.append(er)
            fin_im.append(ei)
        end_re = jnp.concatenate(fin_re, axis=1)
        end_im = jnp.concatenate(fin_im, axis=1)
        if is_prompt:
            cre_s[...] = end_re[N_CHAIN - 1:N_CHAIN, :]
            cim_s[...] = end_im[N_CHAIN - 1:N_CHAIN, :]
            stre_ref[p] = end_re[N_CHAIN - 1:N_CHAIN, :]
            stim_ref[p] = end_im[N_CHAIN - 1:N_CHAIN, :]
        else:
            stre_ref[p] = end_re
            stim_ref[p] = end_im

    def finish(p, x, proj):
        hbuf, res, xpbuf = lanes[p][1], lanes[p][2], lanes[p][3]
        u_s = proj[:, :SSM_WIDTH]
        u_p = proj[:, SSM_WIDTH:SSM_WIDTH + POOL_WIDTH]
        gate_s = proj[:, SSM_WIDTH + POOL_WIDTH:SSM_WIDTH + POOL_WIDTH + D_MODEL]
        gate_p = proj[:, SSM_WIDTH + POOL_WIDTH + D_MODEL:]
        ys = []
        for h in range(2):
            c0 = h * N_FLAT
            h_re = hbuf[:, c0:c0 + HALF_FLAT].astype(BF16)
            h_im = hbuf[:, c0 + HALF_FLAT:c0 + N_FLAT].astype(BF16)
            ys.append(jnp.dot(h_re, wcre[h], preferred_element_type=F32)
                      - jnp.dot(h_im, wcim[h], preferred_element_type=F32))
        y = jnp.concatenate(ys, axis=1) + dsk[...] * u_s
        g = _gelu_tanh(y).astype(BF16)
        glu = jnp.dot(g, wglu[...], preferred_element_type=F32)
        o_s = glu[:, :D_MODEL] * _sigmoid(glu[:, D_MODEL:])

        xpbuf[HIST_ROWS:HIST_ROWS + TM, :] = u_p
        tail = u_p[TM - HIST_ROWS:, :]
        row = lax.broadcasted_iota(jnp.int32, (TM, 1), 0)
        if is_prompt:
            pcarry = lanes[p][10]
            first_chain = (lax.broadcasted_iota(jnp.int32, (HIST_ROWS, POOL_WIDTH), 0) % N_CHAIN) == 0
            carried = jnp.where(tile_in_stream == 0, 0.0, pcarry[...])
            xpbuf[0:HIST_ROWS, :] = jnp.where(first_chain, carried, pltpu.roll(tail, 1, 0))
            new_carry = pltpu.roll(tail, HIST_ROWS - (N_CHAIN - 1), 0)
            pcarry[...] = new_carry
            hist_ref[p] = new_carry
            pos1 = tile_in_stream * TM + CHAIN_LEN * (row % N_CHAIN) + row // N_CHAIN + 1
        else:
            xpbuf[0:HIST_ROWS, :] = ins[3][p]
            hist_ref[p] = tail
            pos1 = PAST_LEN + row // N_CHAIN + 1

        o_ps = []
        for gi, w in enumerate(POOL_WINDOWS):
            c0 = gi * POOL_GROUP_IN
            acc = xpbuf[HIST_ROWS:HIST_ROWS + TM, c0:c0 + POOL_GROUP_IN]
            for k in range(1, w):
                acc = acc + xpbuf[HIST_ROWS - N_CHAIN * k:HIST_ROWS - N_CHAIN * k + TM, c0:c0 + POOL_GROUP_IN]
            cnt = jnp.minimum(w, pos1).astype(F32)
            pooled = acc / cnt
            z = (pooled - u_p[:, c0:c0 + POOL_GROUP_IN]).astype(BF16)
            o_ps.append(jnp.dot(z, poolw[gi], preferred_element_type=F32))
        o_p = jnp.concatenate(o_ps, axis=1) * pscale[...]

        merged = (_sigmoid(gate_s) * o_s + _sigmoid(gate_p) * o_p).astype(BF16)
        x2 = x + jnp.dot(merged, wout[...], preferred_element_type=F32)
        for cb in range(n_lane_blocks):
            res[cb] = x2[:, cb * LANES:(cb + 1) * LANES]
        for j in range(N_CHAIN):
            for cb in range(n_lane_blocks):
                x2_ref[p, CHAIN_LEN * j:CHAIN_LEN * (j + 1), cb * LANES:(cb + 1) * LANES] = res[
                    cb, pl.ds(j, CHAIN_LEN, stride=N_CHAIN), :]

    projected = [None] * N_PAR
    for k in range(N_PAR + 2):
        if 0 <= k - 2 < N_PAR:
            finish(k - 2, *projected[k - 2])
        if 0 <= k - 1 < N_PAR:
            recur(k - 1)
        if k < N_PAR:
            projected[k] = project(k)


def _const_spec(shape):
    nd = len(shape)
    return pl.BlockSpec(shape, lambda i, _nd=nd: (0,) * _nd)


def _mixer_weight_specs():
    return [
        _const_spec((1, D_MODEL)),
        _const_spec((D_MODEL, 3 * D_MODEL)),
        _const_spec((2, SSM_WIDTH // 2, N_FLAT)),
        _const_spec((2, HALF_FLAT, SSM_WIDTH // 2)),
        _const_spec((2, HALF_FLAT, SSM_WIDTH // 2)),
        _const_spec((1, SSM_WIDTH)),
        _const_spec((SSM_WIDTH, 2 * D_MODEL)),
        _const_spec((len(POOL_WINDOWS), POOL_GROUP_IN, POOL_GROUP_OUT)),
        _const_spec((1, D_MODEL)),
        _const_spec((D_MODEL, D_MODEL)),
        _const_spec((1, N_FLAT)),
        _const_spec((1, N_FLAT)),
        _const_spec((1, N_FLAT)),
        _const_spec((1, N_FLAT)),
    ]


def _mixer_common_scratch():
    return [
        pltpu.VMEM((D_MODEL // LANES, TM, LANES), F32),
        pltpu.VMEM((TM, 2 * N_FLAT), F32),
        pltpu.VMEM((D_MODEL // LANES, TM, LANES), F32),
        pltpu.VMEM((HIST_ROWS + TM, POOL_WIDTH), F32),
    ]


def _mixer_prompt(x, weights):
    n_streams, seq, _ = x.shape
    assert n_streams % N_PAR == 0 and seq % TM == 0
    tiles_per_stream = seq // TM
    blk = lambda shape: pl.BlockSpec(shape, lambda i: (i // tiles_per_stream, 0, 0))
    row_spec = pl.BlockSpec((N_PAR, TM, D_MODEL), lambda i: (i // tiles_per_stream, i % tiles_per_stream, 0))
    lane_scratch = _mixer_common_scratch() + [
        pltpu.VMEM((N_CHAIN, N_FLAT), F32), pltpu.VMEM((N_CHAIN, N_FLAT), F32),
        pltpu.VMEM((N_CHAIN, N_FLAT), F32), pltpu.VMEM((N_CHAIN, N_FLAT), F32),
        pltpu.VMEM((1, N_FLAT), F32), pltpu.VMEM((1, N_FLAT), F32),
        pltpu.VMEM((HIST_ROWS, POOL_WIDTH), F32),
    ]
    return pl.pallas_call(
        functools.partial(_mixer_kernel, True, tiles_per_stream),
        grid=(n_streams // N_PAR * tiles_per_stream,),
        in_specs=[row_spec] + _mixer_weight_specs(),
        out_specs=[row_spec, blk((N_PAR, 1, N_FLAT)), blk((N_PAR, 1, N_FLAT)),
                   blk((N_PAR, HIST_ROWS, POOL_WIDTH))],
        out_shape=[jax.ShapeDtypeStruct((n_streams, seq, D_MODEL), F32),
                   jax.ShapeDtypeStruct((n_streams, 1, N_FLAT), F32),
                   jax.ShapeDtypeStruct((n_streams, 1, N_FLAT), F32),
                   jax.ShapeDtypeStruct((n_streams, HIST_ROWS, POOL_WIDTH), F32)],
        scratch_shapes=lane_scratch * N_PAR,
        compiler_params=pltpu.CompilerParams(dimension_semantics=("arbitrary",), vmem_limit_bytes=VMEM_LIMIT),
        name="mixer_prompt",
    )(x, *weights)


def _mixer_sample(x, h0_re, h0_im, cache_t, weights):
    n_streams, seq, _ = x.shape
    assert seq == CHAIN_LEN and n_streams % (N_CHAIN * N_PAR) == 0
    n_tiles = n_streams // N_CHAIN
    blk = lambda shape: pl.BlockSpec(shape, lambda i: (i, 0, 0))
    row_spec = blk((N_PAR, TM, D_MODEL))
    st_spec = blk((N_PAR, N_CHAIN, N_FLAT))
    hist_spec = blk((N_PAR, HIST_ROWS, POOL_WIDTH))
    tiles = lambda v: v.reshape((n_tiles, -1) + v.shape[-1:])
    return pl.pallas_call(
        functools.partial(_mixer_kernel, False, 1),
        grid=(n_tiles // N_PAR,),
        in_specs=[row_spec, st_spec, st_spec, hist_spec] + _mixer_weight_specs(),
        out_specs=[row_spec, st_spec, st_spec, hist_spec],
        out_shape=[jax.ShapeDtypeStruct((n_tiles, TM, D_MODEL), F32),
                   jax.ShapeDtypeStruct((n_tiles, N_CHAIN, N_FLAT), F32),
                   jax.ShapeDtypeStruct((n_tiles, N_CHAIN, N_FLAT), F32),
                   jax.ShapeDtypeStruct((n_tiles, HIST_ROWS, POOL_WIDTH), F32)],
        scratch_shapes=_mixer_common_scratch() * N_PAR,
        compiler_params=pltpu.CompilerParams(dimension_semantics=("arbitrary",), vmem_limit_bytes=VMEM_LIMIT),
        name="mixer_sample",
    )(tiles(x), tiles(h0_re), tiles(h0_im), cache_t, *weights)


def _to_chunks(lo, hi):
    n = lo.shape[0] // CHUNK
    half = D_MODEL // 2
    both = jnp.concatenate([lo.reshape(n, CHUNK, half), hi.reshape(n, CHUNK, half)], axis=1)
    return both.astype(BF16)


def _from_chunks(blk):
    n = blk.shape[0]
    half = D_MODEL // 2
    f = blk.astype(F32)
    lo = f[:, :CHUNK, :].reshape(n * CHUNK, half).astype(BF16)
    hi = f[:, CHUNK:, :].reshape(n * CHUNK, half).astype(BF16)
    return lo, hi


def _route_kernel(n_prompt_steps, x2p_ref, x2s_ref, g_ref, wr_ref, ltri_ref, utri_ref,
                  xs_ref, route_ref, nch_ref):
    is_prompt = pl.program_id(0) < n_prompt_steps
    tiles = range(STEP_TILES)
    each = lambda fn, *cols: [fn(*(c[h] for c in cols)) for h in tiles]
    row_max = lambda v: jnp.max(v, axis=1, keepdims=True)
    row_min = lambda v: jnp.min(v, axis=1, keepdims=True)
    row_sum = lambda v: jnp.sum(v, axis=1, keepdims=True)

    lane = lax.broadcasted_iota(jnp.int32, (TM, LANES), 1)
    lane_f = lane.astype(F32)
    big = jnp.float32(1 << 20)
    neg = jnp.float32(-jnp.inf)
    gmask = lane < N_EXPERT_GROUPS
    eid = lane - N_EXPERT_GROUPS
    lane_grp = (eid >> 3).astype(F32)
    is_expert = (eid >= 0) & (eid < N_EXPERTS)

    xn = [_rms(jnp.where(is_prompt, x2p_ref[pl.ds(h * TM, TM), :], x2s_ref[pl.ds(h * TM, TM), :]),
               g_ref[...]).astype(BF16) for h in tiles]
    logits = each(lambda v: jnp.dot(v, wr_ref[...], preferred_element_type=F32), xn)
    m = each(lambda lg: row_max(jnp.where(gmask, lg, neg)), logits)
    grp = each(lambda lg, mm: row_min(jnp.where(gmask & (lg == mm), lane_f, big)), logits, m)
    wg = each(lambda lg, mm: 1.0 / row_sum(jnp.where(gmask, jnp.exp(lg - mm), 0.0)), logits, m)
    emask = each(lambda g: is_expert & (lane_grp == g), grp)
    v1 = each(lambda lg, em: row_max(jnp.where(em, lg, neg)), logits, emask)
    i1 = each(lambda lg, em, v: row_min(jnp.where(em & (lg == v), lane_f, big)), logits, emask, v1)
    emask2 = each(lambda em, i: em & (lane_f != i), emask, i1)
    v2 = each(lambda lg, em: row_max(jnp.where(em, lg, neg)), logits, emask2)
    i2 = each(lambda lg, em, v: row_min(jnp.where(em & (lg == v), lane_f, big)), logits, emask2, v2)
    e21 = each(lambda a, b: jnp.exp(b - a), v1, v2)
    w1 = each(lambda g, e: g / (1.0 + e), wg, e21)
    w2 = each(lambda g, e: g * e / (1.0 + e), wg, e21)

    a1 = each(lambda i: lane_f == (i - N_EXPERT_GROUPS), i1)
    a2 = each(lambda i: lane_f == (i - N_EXPERT_GROUPS), i2)
    a = each(lambda p, q: (p | q).astype(F32), a1, a2)
    before = each(lambda v: jnp.dot(ltri_ref[...], v.astype(BF16), preferred_element_type=F32), a)
    cnt = each(lambda v: jnp.sum(v, axis=0, keepdims=True), a)
    nch16 = each(lambda c: jnp.broadcast_to(jnp.floor((c + (CHUNK - 1)) * (1.0 / CHUNK)), (16, LANES)), cnt)
    start = each(lambda n: jnp.dot(n.astype(BF16), utri_ref[...], preferred_element_type=F32), nch16)
    slot = each(lambda bf, st: bf + CHUNK * st[0:1, :], before, start)
    d1 = each(lambda p, sl: row_sum(jnp.where(p, sl, 0.0)), a1, slot)
    d2 = each(lambda p, sl: row_sum(jnp.where(p, sl, 0.0)), a2, slot)
    route = each(lambda p, q, u, v: jnp.where(lane == 0, p, jnp.where(lane == 1, q, jnp.where(
        lane == 2, u, jnp.where(lane == 3, v, 0.0)))), d1, d2, w1, w2)

    dest = lax.broadcasted_iota(jnp.int32, (CAP_ROWS, TM), 0)
    half = D_MODEL // 2
    for h in tiles:
        route_ref[pl.ds(h * TM, TM), :] = route[h]
        nch_ref[h] = nch16[h][0:8, :].astype(jnp.int32)
        dt = jnp.transpose(jnp.where(lane < 2, route[h], 0.0)).astype(jnp.int32)
        perm = ((dest == dt[0:1, :]) | (dest == dt[1:2, :])).astype(F32).astype(BF16)
        lo = jnp.dot(perm, xn[h][:, :half], preferred_element_type=F32)
        hi = jnp.dot(perm, xn[h][:, half:], preferred_element_type=F32)
        xs_ref[pl.ds(h * CAP_CHUNKS, CAP_CHUNKS)] = _to_chunks(lo, hi)


def _route(x2p, x2s, g_ffn, w_router):
    n_prompt_tiles = x2p.shape[0] // TM
    n_tiles = n_prompt_tiles + x2s.shape[0] // TM
    assert n_prompt_tiles % STEP_TILES == 0 and n_tiles % STEP_TILES == 0
    n_prompt_steps = n_prompt_tiles // STEP_TILES
    rows = STEP_TILES * TM
    r = jnp.arange(TM)
    ltri = (r[None, :] < r[:, None]).astype(BF16)
    e = jnp.arange(LANES)
    utri = (e[:, None] < e[None, :]).astype(BF16)
    return pl.pallas_call(
        functools.partial(_route_kernel, n_prompt_steps),
        grid=(n_tiles // STEP_TILES,),
        in_specs=[pl.BlockSpec((rows, D_MODEL), lambda i: (jnp.minimum(i, n_prompt_steps - 1), 0)),
                  pl.BlockSpec((rows, D_MODEL), lambda i: (jnp.maximum(i - n_prompt_steps, 0), 0)),
                  _const_spec((1, D_MODEL)), _const_spec((D_MODEL, LANES)),
                  _const_spec((TM, TM)), _const_spec((LANES, LANES))],
        out_specs=[pl.BlockSpec((STEP_TILES * CAP_CHUNKS, 2 * CHUNK, D_MODEL // 2), lambda i: (i, 0, 0)),
                   pl.BlockSpec((rows, LANES), lambda i: (i, 0)),
                   pl.BlockSpec((STEP_TILES, 8, LANES), lambda i: (i, 0, 0))],
        out_shape=[jax.ShapeDtypeStruct((n_tiles * CAP_CHUNKS, 2 * CHUNK, D_MODEL // 2), BF16),
                   jax.ShapeDtypeStruct((n_tiles * TM, LANES), F32),
                   jax.ShapeDtypeStruct((n_tiles, 8, LANES), jnp.int32)],
        compiler_params=pltpu.CompilerParams(dimension_semantics=("arbitrary",), vmem_limit_bytes=VMEM_LIMIT),
        name="route_sort",
    )(x2p, x2s, g_ffn, w_router, ltri, utri)


def _spare_chunk(k):
    return (1 + k // N_SPARE_SLOTS) * CAP_CHUNKS + MAX_USED_CHUNKS + k % N_SPARE_SLOTS


def _chunk_tables(nch, n_expert_tiles):
    n_tiles = nch.shape[0]
    i32 = jnp.int32
    start = jnp.cumsum(nch, axis=1) - nch
    off = jnp.cumsum(nch, axis=0) - nch
    per_expert = jnp.sum(nch, axis=0)
    tiles_e = (per_expert + TILE_CHUNKS - 1) // TILE_CHUNKS
    cum_tiles = jnp.cumsum(tiles_e)
    first_tile = cum_tiles - tiles_e
    n_active = cum_tiles[-1].astype(i32)
    t = jnp.arange(n_expert_tiles, dtype=i32)
    te = jnp.minimum(jnp.sum((t[:, None] >= cum_tiles[None, :]).astype(i32), axis=1), N_EXPERTS - 1)
    onehot = te[:, None] == jnp.arange(N_EXPERTS, dtype=i32)[None, :]
    pick = lambda tab: jnp.sum(jnp.where(onehot[:, :, None], tab.T[None, :, :], 0), axis=1)
    off_t, nch_t, start_t = pick(off), pick(nch), pick(start)
    k = t - jnp.sum(jnp.where(onehot, first_tile[None, :], 0), axis=1)
    q = (TILE_CHUNKS * k)[:, None] + jnp.arange(TILE_CHUNKS, dtype=i32)[None, :]
    in_run = (off_t[:, None, :] <= q[:, :, None]) & (q[:, :, None] < (off_t + nch_t)[:, None, :])
    run_src = (jnp.arange(n_tiles, dtype=i32) * CAP_CHUNKS)[None, :] + start_t - off_t
    src = jnp.sum(jnp.where(in_run, run_src[:, None, :], 0), axis=-1) + q
    valid = jnp.any(in_run, axis=-1) & (t < n_active)[:, None]
    src = jnp.where(valid, src, ZERO_CHUNK).astype(i32)
    spare = _spare_chunk((t % N_RING)[:, None] * TILE_CHUNKS + jnp.arange(TILE_CHUNKS, dtype=i32)[None, :])
    dst = jnp.where(valid, src, spare).astype(i32)
    dst = jnp.concatenate([spare[1:N_RING], dst], axis=0)
    return first_tile.astype(i32), tiles_e.astype(i32), src.reshape(-1), dst.reshape(-1), n_active.reshape(1)


N_RING = 5


def _expert_kernel(first_ref, ntile_ref, src_ref, dst_ref, nact_ref, xs_hbm, wg_ref, wu_ref, wd_ref, ys_hbm,
                   *scratch):
    xbufs = scratch[:N_RING]
    obufs = scratch[N_RING:2 * N_RING]
    gsem, ssem, wg16, wu16, wd16 = scratch[2 * N_RING:]
    e = pl.program_id(0)
    n_active = nact_ref[0]
    ahead = N_RING - 1

    def gather_copy(tile, slot, c):
        return pltpu.make_async_copy(xs_hbm.at[src_ref[tile * TILE_CHUNKS + c]], xbufs[slot].at[c],
                                     gsem.at[slot])

    def scatter_copy(tile, slot, c):
        return pltpu.make_async_copy(obufs[slot].at[c], ys_hbm.at[dst_ref[(tile + ahead) * TILE_CHUNKS + c]],
                                     ssem.at[slot])

    def start_all(copy, tile, slot):
        for c in range(TILE_CHUNKS):
            copy(tile, slot, c).start()

    def wait_all(copy, tile, slot):
        for c in range(TILE_CHUNKS):
            copy(tile, slot, c).wait()

    @pl.when(e == 0)
    def _():
        for v in range(ahead):
            start_all(gather_copy, v, v)
        for u in range(-ahead, 0):
            obufs[u % N_RING][...] = jnp.zeros_like(obufs[u % N_RING])
        for u in range(-ahead, -1):
            start_all(scatter_copy, u, u % N_RING)

    wg16[...] = wg_ref[0].astype(BF16)
    wu16[...] = wu_ref[0].astype(BF16)
    wd16[...] = wd_ref[0].astype(BF16)

    def do_tile(tile, slot):
        nxt = (slot + ahead) % N_RING
        old = (slot + 1) % N_RING
        wait_all(gather_copy, tile, slot)
        lo, hi = _from_chunks(xbufs[slot][...])
        x = jnp.concatenate([lo, hi], axis=1)
        gate = jnp.dot(x, wg16[...], preferred_element_type=F32)
        start_all(scatter_copy, tile - 1, nxt)
        up = jnp.dot(x, wu16[...], preferred_element_type=F32)
        hmid = (gate * _sigmoid(gate) * up).astype(BF16)
        start_all(gather_copy, tile + ahead, nxt)
        y = jnp.dot(hmid, wd16[...], preferred_element_type=F32)
        half = D_MODEL // 2
        wait_all(scatter_copy, tile - ahead, old)
        obufs[slot][...] = _to_chunks(y[:, :half], y[:, half:])

    def tile_body(k, carry):
        tile = first_ref[e] + k
        for slot in range(N_RING):
            @pl.when(tile % N_RING == slot)
            def _():
                do_tile(tile, slot)
        return carry

    lax.fori_loop(0, ntile_ref[e], tile_body, 0)

    @pl.when(e == pl.num_programs(0) - 1)
    def _():
        last = n_active - 1
        for slot in range(N_RING):
            @pl.when(last % N_RING == slot)
            def _():
                for d in range(N_RING - 2, 0, -1):
                    wait_all(scatter_copy, last - d, (slot - d) % N_RING)
                start_all(scatter_copy, last, slot)
                wait_all(scatter_copy, last, slot)
                for d in range(1, N_RING):
                    wait_all(gather_copy, last + d, (slot + d) % N_RING)


def _experts(first_tile, tiles_e, src, dst, n_active, xs, w_gate, w_up, w_down):
    chunk_shape = (2 * CHUNK, D_MODEL // 2)
    assert _spare_chunk(N_RING * TILE_CHUNKS - 1) < xs.shape[0]
    tile_buf = pltpu.VMEM((TILE_CHUNKS,) + chunk_shape, BF16)
    grid_spec = pltpu.PrefetchScalarGridSpec(
        num_scalar_prefetch=5,
        grid=(N_EXPERTS,),
        in_specs=[pl.BlockSpec(memory_space=pl.ANY),
                  pl.BlockSpec((1, D_MODEL, D_EXPERT), lambda e, *_: (e, 0, 0)),
                  pl.BlockSpec((1, D_MODEL, D_EXPERT), lambda e, *_: (e, 0, 0)),
                  pl.BlockSpec((1, D_EXPERT, D_MODEL), lambda e, *_: (e, 0, 0))],
        out_specs=pl.BlockSpec(memory_space=pl.ANY),
        scratch_shapes=[tile_buf] * (2 * N_RING) + [
                        pltpu.SemaphoreType.DMA((N_RING,)),
                        pltpu.SemaphoreType.DMA((N_RING,)),
                        pltpu.VMEM((D_MODEL, D_EXPERT), BF16),
                        pltpu.VMEM((D_MODEL, D_EXPERT), BF16),
                        pltpu.VMEM((D_EXPERT, D_MODEL), BF16)],
    )
    return pl.pallas_call(
        _expert_kernel,
        grid_spec=grid_spec,
        out_shape=jax.ShapeDtypeStruct(xs.shape, xs.dtype),
        input_output_aliases={5: 0},
        compiler_params=pltpu.CompilerParams(dimension_semantics=("arbitrary",), vmem_limit_bytes=VMEM_LIMIT),
        name="expert_ffn",
    )(first_tile, tiles_e, src, dst, n_active, xs, w_gate, w_up, w_down)


def _combine_kernel(n_prompt_steps, x2p_ref, x2s_ref, route_ref, gfin_ref, ys_ref, outp_ref, outs_ref):
    i = pl.program_id(0)
    is_prompt = i < n_prompt_steps
    outs = []
    for h in range(STEP_TILES):
        rows = pl.ds(h * TM, TM)
        lo, hi = _from_chunks(ys_ref[pl.ds(h * CAP_CHUNKS, CAP_CHUNKS)])
        route = route_ref[rows, :]
        d1 = route[:, 0:1].astype(jnp.int32)
        d2 = route[:, 1:2].astype(jnp.int32)
        w1 = route[:, 2:3]
        w2 = route[:, 3:4]
        dest = lax.broadcasted_iota(jnp.int32, (TM, CAP_ROWS), 1)
        sel = jnp.where(dest == d1, w1, jnp.where(dest == d2, w2, 0.0)).astype(BF16)
        moe = jnp.concatenate([jnp.dot(sel, lo, preferred_element_type=F32),
                               jnp.dot(sel, hi, preferred_element_type=F32)], axis=1)
        x = jnp.where(is_prompt, x2p_ref[rows, :], x2s_ref[rows, :])
        outs.append(_rms(x + moe, gfin_ref[...]))

    @pl.when(is_prompt)
    def _():
        for h in range(STEP_TILES):
            outp_ref[pl.ds(h * TM, TM), :] = outs[h]

    @pl.when(jnp.logical_not(is_prompt))
    def _():
        for h in range(STEP_TILES):
            outs_ref[pl.ds(h * TM, TM), :] = outs[h]


def _combine(x2p, x2s, route, g_final, ys):
    n_prompt_tiles = x2p.shape[0] // TM
    n_tiles = n_prompt_tiles + x2s.shape[0] // TM
    assert n_prompt_tiles % STEP_TILES == 0 and n_tiles % STEP_TILES == 0
    n_prompt_steps = n_prompt_tiles // STEP_TILES
    rows = STEP_TILES * TM
    p_spec = pl.BlockSpec((rows, D_MODEL), lambda i: (jnp.minimum(i, n_prompt_steps - 1), 0))
    s_spec = pl.BlockSpec((rows, D_MODEL), lambda i: (jnp.maximum(i - n_prompt_steps, 0), 0))
    return pl.pallas_call(
        functools.partial(_combine_kernel, n_prompt_steps),
        grid=(n_tiles // STEP_TILES,),
        in_specs=[p_spec, s_spec,
                  pl.BlockSpec((rows, LANES), lambda i: (i, 0)),
                  _const_spec((1, D_MODEL)),
                  pl.BlockSpec((STEP_TILES * CAP_CHUNKS, 2 * CHUNK, D_MODEL // 2), lambda i: (i, 0, 0))],
        out_specs=[p_spec, s_spec],
        out_shape=[jax.ShapeDtypeStruct(x2p.shape, F32), jax.ShapeDtypeStruct(x2s.shape, F32)],
        compiler_params=pltpu.CompilerParams(dimension_semantics=("arbitrary",), vmem_limit_bytes=VMEM_LIMIT),
        name="combine_norm",
    )(x2p, x2s, route, g_final, ys)


def _block_diag_in(bb):
    t = bb.reshape(2, 16, SSM_GROUP, SSM_STATE)
    eye = jnp.eye(16, dtype=bb.dtype)
    blk = t[:, :, :, None, :] * eye[None, :, None, :, None]
    return blk.reshape(2, 16 * SSM_GROUP, 16 * SSM_STATE)


def _block_diag_out(c):
    t = c.reshape(2, 16, SSM_GROUP, SSM_STATE).transpose(0, 1, 3, 2)
    eye = jnp.eye(16, dtype=c.dtype)
    blk = t[:, :, :, None, :] * eye[None, :, None, :, None]
    return blk.reshape(2, 16 * SSM_STATE, 16 * SSM_GROUP)


def kernel(x_prompt, x_sample, state_ssm_re, state_ssm_im, cache_pool, g_mix, w_in, ssm_a_re, ssm_a_im,
           ssm_log_dt, ssm_b_re, ssm_b_im, ssm_c_re, ssm_c_im, ssm_d, w_glu_a, w_glu_b, pool_w, pool_scale,
           w_out, g_ffn, w_router_group, w_router_expert, w_exp_gate, w_exp_up, w_exp_down, g_final):
    li = 0
    n_pb, seq_p, _ = x_prompt.shape
    n_sb, seq_s, _ = x_sample.shape

    a_re, a_im, a32_re, a32_im, bb_re, bb_im = _discretise(
        ssm_a_re[li], ssm_a_im[li], ssm_log_dt[li], ssm_b_re[li], ssm_b_im[li])
    row = lambda v: v.reshape(1, N_FLAT)
    wb = jnp.concatenate([_block_diag_in(bb_re), _block_diag_in(bb_im)], axis=2).astype(BF16)
    weights = [
        g_mix[li].reshape(1, D_MODEL),
        w_in[li].astype(BF16),
        wb,
        _block_diag_out(ssm_c_re[li]).astype(BF16),
        _block_diag_out(ssm_c_im[li]).astype(BF16),
        ssm_d[li].reshape(1, SSM_WIDTH),
        jnp.concatenate([w_glu_a[li], w_glu_b[li]], axis=1).astype(BF16),
        pool_w[li].astype(BF16),
        pool_scale[li].reshape(1, D_MODEL),
        w_out[li].astype(BF16),
        row(a_re), row(a_im), row(a32_re), row(a32_im),
    ]

    x2p, stp_re, stp_im, histp = _mixer_prompt(x_prompt, weights)
    x2p = x2p.reshape(n_pb * seq_p, D_MODEL)

    n_stiles = n_sb // N_CHAIN
    cache16 = jnp.pad(cache_pool[li], ((0, 0), (1, 0), (0, 0)))
    cache_t = cache16.reshape(n_stiles, N_CHAIN, 16, POOL_WIDTH).transpose(0, 2, 1, 3).reshape(
        n_stiles, HIST_ROWS, POOL_WIDTH)
    x2s, sts_re, sts_im, hists = _mixer_sample(
        x_sample, state_ssm_re[li].reshape(n_sb, N_FLAT), state_ssm_im[li].reshape(n_sb, N_FLAT),
        cache_t, weights)
    x2s = x2s.reshape(n_sb * seq_s, D_MODEL)

    w_router = jnp.concatenate(
        [w_router_group[li], w_router_expert[li].reshape(D_MODEL, N_EXPERTS),
         jnp.zeros((D_MODEL, LANES - N_EXPERT_GROUPS - N_EXPERTS), F32)], axis=1).astype(BF16)
    xs, route, nch = _route(x2p, x2s, g_ffn[li].reshape(1, D_MODEL), w_router)

    n_tiles = (x2p.shape[0] + x2s.shape[0]) // TM
    max_chunks = n_tiles * (2 * TM // CHUNK + N_EXPERTS)
    n_expert_tiles = max_chunks // TILE_CHUNKS + N_EXPERTS
    first_tile, tiles_e, src, dst, n_active = _chunk_tables(nch[:, 0, :N_EXPERTS], n_expert_tiles)
    ys = _experts(first_tile, tiles_e, src, dst, n_active, xs, w_exp_gate[li], w_exp_up[li], w_exp_down[li])
    yp, ysm = _combine(x2p, x2s, route, g_final.reshape(1, D_MODEL), ys)

    sd = state_ssm_re.dtype
    cd = cache_pool.dtype
    y_prompt = yp.reshape(n_pb, seq_p, D_MODEL)
    y_sample = ysm.reshape(n_sb, seq_s, D_MODEL)
    re_p = stp_re.reshape(1, n_pb, SSM_GROUPS, SSM_STATE).astype(sd)
    im_p = stp_im.reshape(1, n_pb, SSM_GROUPS, SSM_STATE).astype(sd)
    hist_p = histp[:, ::N_CHAIN, :][:, 1:, :].reshape(1, n_pb, POOL_HIST, POOL_WIDTH).astype(cd)
    re_s = sts_re.reshape(1, n_sb, SSM_GROUPS, SSM_STATE).astype(sd)
    im_s = sts_im.reshape(1, n_sb, SSM_GROUPS, SSM_STATE).astype(sd)
    hist_s = hists.reshape(n_stiles, 16, N_CHAIN, POOL_WIDTH).transpose(0, 2, 1, 3).reshape(
        n_sb, 16, POOL_WIDTH)[:, 1:, :].reshape(1, n_sb, POOL_HIST, POOL_WIDTH).astype(cd)
    return (y_prompt, y_sample, re_p, im_p, hist_p, re_s, im_s, hist_s)
```

```python
import functools
import math

import jax
import jax.numpy as jnp
from jax import lax
from jax.experimental import pallas as pl
from jax.experimental.pallas import tpu as pltpu

F32 = jnp.float32
BF16 = jnp.bfloat16

D_MODEL = 1024
SSM_WIDTH = 512
SSM_GROUPS = 32
SSM_GROUP = 16
SSM_STATE = 64
N_FLAT = SSM_GROUPS * SSM_STATE
HALF_FLAT = N_FLAT // 2
POOL_WIDTH = 512
POOL_WINDOWS = (2, 4, 8, 16)
POOL_GROUP_IN = 128
POOL_GROUP_OUT = 256
POOL_HIST = 15
N_EXPERTS = 32
EXPERTS_PER_GROUP = 8
N_EXPERT_GROUPS = 4
D_EXPERT = 512
EPS = 1e-6
PAST_LEN = 1024

TM = 256
N_CHAIN = 8
CHAIN_LEN = TM // N_CHAIN
HIST_ROWS = 16 * N_CHAIN

CHUNK = 8
EXPERT_SUBTILES = 2
TILE_CHUNKS = EXPERT_SUBTILES * TM // CHUNK
CAP_CHUNKS = 96
CAP_ROWS = CAP_CHUNKS * CHUNK
MAX_USED_CHUNKS = 2 * TM // CHUNK + N_EXPERTS * (CHUNK - 1) // CHUNK
N_SPARE_SLOTS = CAP_CHUNKS - MAX_USED_CHUNKS - 1
ZERO_CHUNK = CAP_CHUNKS - 1
LANES = 128

V7X_VMEM_BYTES = 64 * 1024 * 1024
VMEM_LIMIT = V7X_VMEM_BYTES * 13 // 16
STEP_TILES = 4


def _rms(x, g):
    r = lax.rsqrt(jnp.mean(x * x, axis=-1, keepdims=True) + EPS)
    return x * r * g


def _sigmoid(x):
    return 0.5 * jnp.tanh(0.5 * x) + 0.5


def _gelu_tanh(x):
    c = math.sqrt(2.0 / math.pi)
    return x * (0.5 * (1.0 + jnp.tanh(c * (x + 0.044715 * (x * x * x)))))


def _disc_kernel(lre_ref, lim_ref, ldt_ref, bre_ref, bim_ref,
                 are_ref, aim_ref, a32re_ref, a32im_ref, bbre_ref, bbim_ref):
    lam_re = jnp.minimum(lre_ref[...], -1e-4)
    lam_im = lim_ref[...]
    dt = jnp.exp(ldt_ref[...])
    mag = jnp.exp(lam_re * dt)
    ang = lam_im * dt
    a_re = mag * jnp.cos(ang)
    a_im = mag * jnp.sin(ang)
    num_re = a_re - 1.0
    num_im = a_im
    den = lam_re * lam_re + lam_im * lam_im
    k_re = ((num_re * lam_re + num_im * lam_im) / den)[:, None, :]
    k_im = ((num_im * lam_re - num_re * lam_im) / den)[:, None, :]
    br = bre_ref[...]
    bi = bim_ref[...]
    bbre_ref[...] = k_re * br - k_im * bi
    bbim_ref[...] = k_re * bi + k_im * br
    are_ref[...] = a_re
    aim_ref[...] = a_im
    pr, pi = a_re, a_im
    for _ in range(int(math.log2(CHAIN_LEN))):
        pr, pi = pr * pr - pi * pi, 2.0 * pr * pi
    a32re_ref[...] = pr
    a32im_ref[...] = pi


def _discretise(a_re, a_im, log_dt, b_re, b_im):
    chan_major = lambda b: jnp.swapaxes(b, 1, 2)
    return pl.pallas_call(
        _disc_kernel,
        out_shape=[jax.ShapeDtypeStruct((SSM_GROUPS, SSM_STATE), F32)] * 4
        + [jax.ShapeDtypeStruct((SSM_GROUPS, SSM_GROUP, SSM_STATE), F32)] * 2,
        name="s5_discretise",
    )(a_re, a_im, log_dt.reshape(SSM_GROUPS, 1), chan_major(b_re), chan_major(b_im))


def _scan_half(hbuf, h, ar, ai, init_re, init_im, store):
    cre = h * N_FLAT
    cim = cre + HALF_FLAT
    hr, hi = init_re, init_im
    for t in range(CHAIN_LEN):
        rows = pl.ds(N_CHAIN * t, N_CHAIN)
        br = hbuf[rows, cre:cre + HALF_FLAT]
        bi = hbuf[rows, cim:cim + HALF_FLAT]
        nr = ar * hr - ai * hi + br
        ni = ar * hi + ai * hr + bi
        if store:
            hbuf[rows, cre:cre + HALF_FLAT] = nr
            hbuf[rows, cim:cim + HALF_FLAT] = ni
        hr, hi = nr, ni
    return hr, hi


N_MIXER_WEIGHTS = 14
N_PAR = 2


def _mixer_kernel(is_prompt, tiles_per_stream, *refs):
    n_in = 1 if is_prompt else 4
    ins = refs[:n_in]
    (gmix, win, wb, wcre, wcim, dsk, wglu, poolw, pscale, wout, are, aim, a32re, a32im) = refs[
        n_in:n_in + N_MIXER_WEIGHTS]
    x2_ref, stre_ref, stim_ref, hist_ref = refs[n_in + N_MIXER_WEIGHTS:n_in + N_MIXER_WEIGHTS + 4]
    scratch = refs[n_in + N_MIXER_WEIGHTS + 4:]
    per = len(scratch) // N_PAR
    lanes = [scratch[p * per:(p + 1) * per] for p in range(N_PAR)]
    x_ref = ins[0]

    tile_in_stream = pl.program_id(0) % tiles_per_stream
    n_lane_blocks = D_MODEL // LANES
    half_w = SSM_WIDTH // 2
    ar_full = are[...]
    ai_full = aim[...]

    if is_prompt:
        @pl.when(pl.program_id(0) == 0)
        def _():
            for p in range(N_PAR):
                cre_s, cim_s, pcarry = lanes[p][8], lanes[p][9], lanes[p][10]
                cre_s[...] = jnp.zeros_like(cre_s)
                cim_s[...] = jnp.zeros_like(cim_s)
                pcarry[...] = jnp.zeros_like(pcarry)

    def a_half(h):
        f0 = h * HALF_FLAT
        return (jnp.broadcast_to(ar_full[:, f0:f0 + HALF_FLAT], (N_CHAIN, HALF_FLAT)),
                jnp.broadcast_to(ai_full[:, f0:f0 + HALF_FLAT], (N_CHAIN, HALF_FLAT)))

    def project(p):
        xperm, hbuf = lanes[p][0], lanes[p][1]
        for j in range(N_CHAIN):
            for cb in range(n_lane_blocks):
                xperm[cb, pl.ds(j, CHAIN_LEN, stride=N_CHAIN), :] = x_ref[
                    p, CHAIN_LEN * j:CHAIN_LEN * (j + 1), cb * LANES:(cb + 1) * LANES]
        x = jnp.concatenate([xperm[cb] for cb in range(n_lane_blocks)], axis=1)
        xn = _rms(x, gmix[...]).astype(BF16)
        proj = jnp.dot(xn, win[...], preferred_element_type=F32)
        ub = proj[:, :SSM_WIDTH].astype(BF16)
        for h in range(2):
            hbuf[:, h * N_FLAT:(h + 1) * N_FLAT] = jnp.dot(
                ub[:, h * half_w:(h + 1) * half_w], wb[h], preferred_element_type=F32)
        return x, proj

    def recur(p):
        hbuf = lanes[p][1]
        if is_prompt:
            fre, fim, hre, him, cre_s, cim_s = lanes[p][4:10]
            zeros = jnp.zeros((N_CHAIN, HALF_FLAT), F32)
            for h in range(2):
                f0 = h * HALF_FLAT
                ar, ai = a_half(h)
                lr, li = _scan_half(hbuf, h, ar, ai, zeros, zeros, store=False)
                fre[:, f0:f0 + HALF_FLAT] = lr
                fim[:, f0:f0 + HALF_FLAT] = li
            fresh = tile_in_stream == 0
            hre[0:1, :] = jnp.where(fresh, 0.0, cre_s[...])
            him[0:1, :] = jnp.where(fresh, 0.0, cim_s[...])
            p_re = a32re[...]
            p_im = a32im[...]
            for j in range(N_CHAIN - 1):
                sr = hre[j:j + 1, :]
                si = him[j:j + 1, :]
                hre[j + 1:j + 2, :] = fre[j:j + 1, :] + p_re * sr - p_im * si
                him[j + 1:j + 2, :] = fim[j:j + 1, :] + p_re * si + p_im * sr
            init_re = hre[...]
            init_im = him[...]
        else:
            init_re = ins[1][p]
            init_im = ins[2][p]
        fin_re = []
        fin_im = []
        for h in range(2):
            f0 = h * HALF_FLAT
            ar, ai = a_half(h)
            er, ei = _scan_half(hbuf, h, ar, ai, init_re[:, f0:f0 + HALF_FLAT],
                                init_im[:, f0:f0 + HALF_FLAT], store=True)
            fin_re.append(er)
            fin_im.append(ei)
        end_re = jnp.concatenate(fin_re, axis=1)
        end_im = jnp.concatenate(fin_im, axis=1)
        if is_prompt:
            cre_s[...] = end_re[N_CHAIN - 1:N_CHAIN, :]
            cim_s[...] = end_im[N_CHAIN - 1:N_CHAIN, :]
            stre_ref[p] = end_re[N_CHAIN - 1:N_CHAIN, :]
            stim_ref[p] = end_im[N_CHAIN - 1:N_CHAIN, :]
        else:
            stre_ref[p] = end_re
            stim_ref[p] = end_im

    def finish(p, x, proj):
        hbuf, res, xpbuf = lanes[p][1], lanes[p][2], lanes[p][3]
        u_s = proj[:, :SSM_WIDTH]
        u_p = proj[:, SSM_WIDTH:SSM_WIDTH + POOL_WIDTH]
        gate_s = proj[:, SSM_WIDTH + POOL_WIDTH:SSM_WIDTH + POOL_WIDTH + D_MODEL]
        gate_p = proj[:, SSM_WIDTH + POOL_WIDTH + D_MODEL:]
        ys = []
        for h in range(2):
            c0 = h * N_FLAT
            h_re = hbuf[:, c0:c0 + HALF_FLAT].astype(BF16)
            h_im = hbuf[:, c0 + HALF_FLAT:c0 + N_FLAT].astype(BF16)
            ys.append(jnp.dot(h_re, wcre[h], preferred_element_type=F32)
                      - jnp.dot(h_im, wcim[h], preferred_element_type=F32))
        y = jnp.concatenate(ys, axis=1) + dsk[...] * u_s
        g = _gelu_tanh(y).astype(BF16)
        glu = jnp.dot(g, wglu[...], preferred_element_type=F32)
        o_s = glu[:, :D_MODEL] * _sigmoid(glu[:, D_MODEL:])

        xpbuf[HIST_ROWS:HIST_ROWS + TM, :] = u_p
        tail = u_p[TM - HIST_ROWS:, :]
        row = lax.broadcasted_iota(jnp.int32, (TM, 1), 0)
        if is_prompt:
            pcarry = lanes[p][10]
            first_chain = (lax.broadcasted_iota(jnp.int32, (HIST_ROWS, POOL_WIDTH), 0) % N_CHAIN) == 0
            carried = jnp.where(tile_in_stream == 0, 0.0, pcarry[...])
            xpbuf[0:HIST_ROWS, :] = jnp.where(first_chain, carried, pltpu.roll(tail, 1, 0))
            new_carry = pltpu.roll(tail, HIST_ROWS - (N_CHAIN - 1), 0)
            pcarry[...] = new_carry
            hist_ref[p] = new_carry
            pos1 = tile_in_stream * TM + CHAIN_LEN * (row % N_CHAIN) + row // N_CHAIN + 1
        else:
            xpbuf[0:HIST_ROWS, :] = ins[3][p]
            hist_ref[p] = tail
            pos1 = PAST_LEN + row // N_CHAIN + 1

        o_ps = []
        for gi, w in enumerate(POOL_WINDOWS):
            c0 = gi * POOL_GROUP_IN
            acc = xpbuf[HIST_ROWS:HIST_ROWS + TM, c0:c0 + POOL_GROUP_IN]
            for k in range(1, w):
                acc = acc + xpbuf[HIST_ROWS - N_CHAIN * k:HIST_ROWS - N_CHAIN * k + TM, c0:c0 + POOL_GROUP_IN]
            cnt = jnp.minimum(w, pos1).astype(F32)
            pooled = acc / cnt
            z = (pooled - u_p[:, c0:c0 + POOL_GROUP_IN]).astype(BF16)
            o_ps.append(jnp.dot(z, poolw[gi], preferred_element_type=F32))
        o_p = jnp.concatenate(o_ps, axis=1) * pscale[...]

        merged = (_sigmoid(gate_s) * o_s + _sigmoid(gate_p) * o_p).astype(BF16)
        x2 = x + jnp.dot(merged, wout[...], preferred_element_type=F32)
        for cb in range(n_lane_blocks):
            res[cb] = x2[:, cb * LANES:(cb + 1) * LANES]
        for j in range(N_CHAIN):
            for cb in range(n_lane_blocks):
                x2_ref[p, CHAIN_LEN * j:CHAIN_LEN * (j + 1), cb * LANES:(cb + 1) * LANES] = res[
                    cb, pl.ds(j, CHAIN_LEN, stride=N_CHAIN), :]

    projected = [None] * N_PAR
    for k in range(N_PAR + 2):
        if 0 <= k - 2 < N_PAR:
            finish(k - 2, *projected[k - 2])
        if 0 <= k - 1 < N_PAR:
            recur(k - 1)
        if k < N_PAR:
            projected[k] = project(k)


def _const_spec(shape):
    nd = len(shape)
    return pl.BlockSpec(shape, lambda i, _nd=nd: (0,) * _nd)


def _mixer_weight_specs():
    return [
        _const_spec((1, D_MODEL)),
        _const_spec((D_MODEL, 3 * D_MODEL)),
        _const_spec((2, SSM_WIDTH // 2, N_FLAT)),
        _const_spec((2, HALF_FLAT, SSM_WIDTH // 2)),
        _const_spec((2, HALF_FLAT, SSM_WIDTH // 2)),
        _const_spec((1, SSM_WIDTH)),
        _const_spec((SSM_WIDTH, 2 * D_MODEL)),
        _const_spec((len(POOL_WINDOWS), POOL_GROUP_IN, POOL_GROUP_OUT)),
        _const_spec((1, D_MODEL)),
        _const_spec((D_MODEL, D_MODEL)),
        _const_spec((1, N_FLAT)),
        _const_spec((1, N_FLAT)),
        _const_spec((1, N_FLAT)),
        _const_spec((1, N_FLAT)),
    ]


def _mixer_common_scratch():
    return [
        pltpu.VMEM((D_MODEL // LANES, TM, LANES), F32),
        pltpu.VMEM((TM, 2 * N_FLAT), F32),
        pltpu.VMEM((D_MODEL // LANES, TM, LANES), F32),
        pltpu.VMEM((HIST_ROWS + TM, POOL_WIDTH), F32),
    ]


def _mixer_prompt(x, weights):
    n_streams, seq, _ = x.shape
    assert n_streams % N_PAR == 0 and seq % TM == 0
    tiles_per_stream = seq // TM
    blk = lambda shape: pl.BlockSpec(shape, lambda i: (i // tiles_per_stream, 0, 0))
    row_spec = pl.BlockSpec((N_PAR, TM, D_MODEL), lambda i: (i // tiles_per_stream, i % tiles_per_stream, 0))
    lane_scratch = _mixer_common_scratch() + [
        pltpu.VMEM((N_CHAIN, N_FLAT), F32), pltpu.VMEM((N_CHAIN, N_FLAT), F32),
        pltpu.VMEM((N_CHAIN, N_FLAT), F32), pltpu.VMEM((N_CHAIN, N_FLAT), F32),
        pltpu.VMEM((1, N_FLAT), F32), pltpu.VMEM((1, N_FLAT), F32),
        pltpu.VMEM((HIST_ROWS, POOL_WIDTH), F32),
    ]
    return pl.pallas_call(
        functools.partial(_mixer_kernel, True, tiles_per_stream),
        grid=(n_streams // N_PAR * tiles_per_stream,),
        in_specs=[row_spec] + _mixer_weight_specs(),
        out_specs=[row_spec, blk((N_PAR, 1, N_FLAT)), blk((N_PAR, 1, N_FLAT)),
                   blk((N_PAR, HIST_ROWS, POOL_WIDTH))],
        out_shape=[jax.ShapeDtypeStruct((n_streams, seq, D_MODEL), F32),
                   jax.ShapeDtypeStruct((n_streams, 1, N_FLAT), F32),
                   jax.ShapeDtypeStruct((n_streams, 1, N_FLAT), F32),
                   jax.ShapeDtypeStruct((n_streams, HIST_ROWS, POOL_WIDTH), F32)],
        scratch_shapes=lane_scratch * N_PAR,
        compiler_params=pltpu.CompilerParams(dimension_semantics=("arbitrary",), vmem_limit_bytes=VMEM_LIMIT),
        name="mixer_prompt",
    )(x, *weights)


def _mixer_sample(x, h0_re, h0_im, cache_t, weights):
    n_streams, seq, _ = x.shape
    assert seq == CHAIN_LEN and n_streams % (N_CHAIN * N_PAR) == 0
    n_tiles = n_streams // N_CHAIN
    blk = lambda shape: pl.BlockSpec(shape, lambda i: (i, 0, 0))
    row_spec = blk((N_PAR, TM, D_MODEL))
    st_spec = blk((N_PAR, N_CHAIN, N_FLAT))
    hist_spec = blk((N_PAR, HIST_ROWS, POOL_WIDTH))
    tiles = lambda v: v.reshape((n_tiles, -1) + v.shape[-1:])
    return pl.pallas_call(
        functools.partial(_mixer_kernel, False, 1),
        grid=(n_tiles // N_PAR,),
        in_specs=[row_spec, st_spec, st_spec, hist_spec] + _mixer_weight_specs(),
        out_specs=[row_spec, st_spec, st_spec, hist_spec],
        out_shape=[jax.ShapeDtypeStruct((n_tiles, TM, D_MODEL), F32),
                   jax.ShapeDtypeStruct((n_tiles, N_CHAIN, N_FLAT), F32),
                   jax.ShapeDtypeStruct((n_tiles, N_CHAIN, N_FLAT), F32),
                   jax.ShapeDtypeStruct((n_tiles, HIST_ROWS, POOL_WIDTH), F32)],
        scratch_shapes=_mixer_common_scratch() * N_PAR,
        compiler_params=pltpu.CompilerParams(dimension_semantics=("arbitrary",), vmem_limit_bytes=VMEM_LIMIT),
        name="mixer_sample",
    )(tiles(x), tiles(h0_re), tiles(h0_im), cache_t, *weights)


def _to_chunks(lo, hi):
    n = lo.shape[0] // CHUNK
    half = D_MODEL // 2
    both = jnp.concatenate([lo.reshape(n, CHUNK, half), hi.reshape(n, CHUNK, half)], axis=1)
    return both.astype(BF16)


def _from_chunks(blk):
    n = blk.shape[0]
    half = D_MODEL // 2
    f = blk.astype(F32)
    lo = f[:, :CHUNK, :].reshape(n * CHUNK, half).astype(BF16)
    hi = f[:, CHUNK:, :].reshape(n * CHUNK, half).astype(BF16)
    return lo, hi


def _route_kernel(n_prompt_steps, x2p_ref, x2s_ref, g_ref, wr_ref, ltri_ref, utri_ref,
                  xs_ref, route_ref, nch_ref):
    is_prompt = pl.program_id(0) < n_prompt_steps
    tiles = range(STEP_TILES)
    each = lambda fn, *cols: [fn(*(c[h] for c in cols)) for h in tiles]
    row_max = lambda v: jnp.max(v, axis=1, keepdims=True)
    row_min = lambda v: jnp.min(v, axis=1, keepdims=True)
    row_sum = lambda v: jnp.sum(v, axis=1, keepdims=True)

    lane = lax.broadcasted_iota(jnp.int32, (TM, LANES), 1)
    lane_f = lane.astype(F32)
    big = jnp.float32(1 << 20)
    neg = jnp.float32(-jnp.inf)
    gmask = lane < N_EXPERT_GROUPS
    eid = lane - N_EXPERT_GROUPS
    lane_grp = (eid >> 3).astype(F32)
    is_expert = (eid >= 0) & (eid < N_EXPERTS)

    xn = [_rms(jnp.where(is_prompt, x2p_ref[pl.ds(h * TM, TM), :], x2s_ref[pl.ds(h * TM, TM), :]),
               g_ref[...]).astype(BF16) for h in tiles]
    logits = each(lambda v: jnp.dot(v, wr_ref[...], preferred_element_type=F32), xn)
    m = each(lambda lg: row_max(jnp.where(gmask, lg, neg)), logits)
    grp = each(lambda lg, mm: row_min(jnp.where(gmask & (lg == mm), lane_f, big)), logits, m)
    wg = each(lambda lg, mm: 1.0 / row_sum(jnp.where(gmask, jnp.exp(lg - mm), 0.0)), logits, m)
    emask = each(lambda g: is_expert & (lane_grp == g), grp)
    v1 = each(lambda lg, em: row_max(jnp.where(em, lg, neg)), logits, emask)
    i1 = each(lambda lg, em, v: row_min(jnp.where(em & (lg == v), lane_f, big)), logits, emask, v1)
    emask2 = each(lambda em, i: em & (lane_f != i), emask, i1)
    v2 = each(lambda lg, em: row_max(jnp.where(em, lg, neg)), logits, emask2)
    i2 = each(lambda lg, em, v: row_min(jnp.where(em & (lg == v), lane_f, big)), logits, emask2, v2)
    e21 = each(lambda a, b: jnp.exp(b - a), v1, v2)
    w1 = each(lambda g, e: g / (1.0 + e), wg, e21)
    w2 = each(lambda g, e: g * e / (1.0 + e), wg, e21)

    a1 = each(lambda i: lane_f == (i - N_EXPERT_GROUPS), i1)
    a2 = each(lambda i: lane_f == (i - N_EXPERT_GROUPS), i2)
    a = each(lambda p, q: (p | q).astype(F32), a1, a2)
    before = each(lambda v: jnp.dot(ltri_ref[...], v.astype(BF16), preferred_element_type=F32), a)
    cnt = each(lambda v: jnp.sum(v, axis=0, keepdims=True), a)
    nch16 = each(lambda c: jnp.broadcast_to(jnp.floor((c + (CHUNK - 1)) * (1.0 / CHUNK)), (16, LANES)), cnt)
    start = each(lambda n: jnp.dot(n.astype(BF16), utri_ref[...], preferred_element_type=F32), nch16)
    slot = each(lambda bf, st: bf + CHUNK * st[0:1, :], before, start)
    d1 = each(lambda p, sl: row_sum(jnp.where(p, sl, 0.0)), a1, slot)
    d2 = each(lambda p, sl: row_sum(jnp.where(p, sl, 0.0)), a2, slot)
    route = each(lambda p, q, u, v: jnp.where(lane == 0, p, jnp.where(lane == 1, q, jnp.where(
        lane == 2, u, jnp.where(lane == 3, v, 0.0)))), d1, d2, w1, w2)

    dest = lax.broadcasted_iota(jnp.int32, (CAP_ROWS, TM), 0)
    half = D_MODEL // 2
    for h in tiles:
        route_ref[pl.ds(h * TM, TM), :] = route[h]
        nch_ref[h] = nch16[h][0:8, :].astype(jnp.int32)
        dt = jnp.transpose(jnp.where(lane < 2, route[h], 0.0)).astype(jnp.int32)
        perm = ((dest == dt[0:1, :]) | (dest == dt[1:2, :])).astype(F32).astype(BF16)
        lo = jnp.dot(perm, xn[h][:, :half], preferred_element_type=F32)
        hi = jnp.dot(perm, xn[h][:, half:], preferred_element_type=F32)
        xs_ref[pl.ds(h * CAP_CHUNKS, CAP_CHUNKS)] = _to_chunks(lo, hi)


def _route(x2p, x2s, g_ffn, w_router):
    n_prompt_tiles = x2p.shape[0] // TM
    n_tiles = n_prompt_tiles + x2s.shape[0] // TM
    assert n_prompt_tiles % STEP_TILES == 0 and n_tiles % STEP_TILES == 0
    n_prompt_steps = n_prompt_tiles // STEP_TILES
    rows = STEP_TILES * TM
    r = jnp.arange(TM)
    ltri = (r[None, :] < r[:, None]).astype(BF16)
    e = jnp.arange(LANES)
    utri = (e[:, None] < e[None, :]).astype(BF16)
    return pl.pallas_call(
        functools.partial(_route_kernel, n_prompt_steps),
        grid=(n_tiles // STEP_TILES,),
        in_specs=[pl.BlockSpec((rows, D_MODEL), lambda i: (jnp.minimum(i, n_prompt_steps - 1), 0)),
                  pl.BlockSpec((rows, D_MODEL), lambda i: (jnp.maximum(i - n_prompt_steps, 0), 0)),
                  _const_spec((1, D_MODEL)), _const_spec((D_MODEL, LANES)),
                  _const_spec((TM, TM)), _const_spec((LANES, LANES))],
        out_specs=[pl.BlockSpec((STEP_TILES * CAP_CHUNKS, 2 * CHUNK, D_MODEL // 2), lambda i: (i, 0, 0)),
                   pl.BlockSpec((rows, LANES), lambda i: (i, 0)),
                   pl.BlockSpec((STEP_TILES, 8, LANES), lambda i: (i, 0, 0))],
        out_shape=[jax.ShapeDtypeStruct((n_tiles * CAP_CHUNKS, 2 * CHUNK, D_MODEL // 2), BF16),
                   jax.ShapeDtypeStruct((n_tiles * TM, LANES), F32),
                   jax.ShapeDtypeStruct((n_tiles, 8, LANES), jnp.int32)],
        compiler_params=pltpu.CompilerParams(dimension_semantics=("arbitrary",), vmem_limit_bytes=VMEM_LIMIT),
        name="route_sort",
    )(x2p, x2s, g_ffn, w_router, ltri, utri)


def _spare_chunk(k):
    return (1 + k // N_SPARE_SLOTS) * CAP_CHUNKS + MAX_USED_CHUNKS + k % N_SPARE_SLOTS


def _chunk_tables(nch, n_expert_tiles):
    n_tiles = nch.shape[0]
    i32 = jnp.int32
    start = jnp.cumsum(nch, axis=1) - nch
    off = jnp.cumsum(nch, axis=0) - nch
    per_expert = jnp.sum(nch, axis=0)
    tiles_e = (per_expert + TILE_CHUNKS - 1) // TILE_CHUNKS
    cum_tiles = jnp.cumsum(tiles_e)
    first_tile = cum_tiles - tiles_e
    n_active = cum_tiles[-1].astype(i32)
    t = jnp.arange(n_expert_tiles, dtype=i32)
    te = jnp.minimum(jnp.sum((t[:, None] >= cum_tiles[None, :]).astype(i32), axis=1), N_EXPERTS - 1)
    onehot = te[:, None] == jnp.arange(N_EXPERTS, dtype=i32)[None, :]
    pick = lambda tab: jnp.sum(jnp.where(onehot[:, :, None], tab.T[None, :, :], 0), axis=1)
    off_t, nch_t, start_t = pick(off), pick(nch), pick(start)
    k = t - jnp.sum(jnp.where(onehot, first_tile[None, :], 0), axis=1)
    q = (TILE_CHUNKS * k)[:, None] + jnp.arange(TILE_CHUNKS, dtype=i32)[None, :]
    in_run = (off_t[:, None, :] <= q[:, :, None]) & (q[:, :, None] < (off_t + nch_t)[:, None, :])
    run_src = (jnp.arange(n_tiles, dtype=i32) * CAP_CHUNKS)[None, :] + start_t - off_t
    src = jnp.sum(jnp.where(in_run, run_src[:, None, :], 0), axis=-1) + q
    valid = jnp.any(in_run, axis=-1) & (t < n_active)[:, None]
    src = jnp.where(valid, src, ZERO_CHUNK).astype(i32)
    spare = _spare_chunk((t % N_RING)[:, None] * TILE_CHUNKS + jnp.arange(TILE_CHUNKS, dtype=i32)[None, :])
    dst = jnp.where(valid, src, spare).astype(i32)
    dst = jnp.concatenate([spare[1:N_RING], dst], axis=0)
    return first_tile.astype(i32), tiles_e.astype(i32), src.reshape(-1), dst.reshape(-1), n_active.reshape(1)


N_RING = 3


def _expert_kernel(first_ref, ntile_ref, src_ref, dst_ref, nact_ref, xs_hbm, wg_ref, wu_ref, wd_ref, ys_hbm,
                   *scratch):
    xbufs = scratch[:N_RING]
    obufs = scratch[N_RING:2 * N_RING]
    gsem, ssem, wg16, wu16, wd16 = scratch[2 * N_RING:]
    e = pl.program_id(0)
    n_active = nact_ref[0]
    ahead = N_RING - 1

    def gather_copy(tile, slot, c):
        return pltpu.make_async_copy(xs_hbm.at[src_ref[tile * TILE_CHUNKS + c]], xbufs[slot].at[c],
                                     gsem.at[slot])

    def scatter_copy(tile, slot, c):
        return pltpu.make_async_copy(obufs[slot].at[c], ys_hbm.at[dst_ref[(tile + ahead) * TILE_CHUNKS + c]],
                                     ssem.at[slot])

    def start_all(copy, tile, slot):
        for c in range(TILE_CHUNKS):
            copy(tile, slot, c).start()

    def wait_all(copy, tile, slot):
        for c in range(TILE_CHUNKS):
            copy(tile, slot, c).wait()

    @pl.when(e == 0)
    def _():
        for v in range(ahead):
            start_all(gather_copy, v, v)
        for u in range(-ahead, 0):
            obufs[u % N_RING][...] = jnp.zeros_like(obufs[u % N_RING])
        for u in range(-ahead, -1):
            start_all(scatter_copy, u, u % N_RING)

    wg16[...] = wg_ref[0].astype(BF16)
    wu16[...] = wu_ref[0].astype(BF16)
    wd16[...] = wd_ref[0].astype(BF16)

    def do_tile(tile, slot):
        nxt = (slot + ahead) % N_RING
        old = (slot + 1) % N_RING
        subs = range(EXPERT_SUBTILES)
        sub_chunks = TM // CHUNK
        half = D_MODEL // 2
        wait_all(gather_copy, tile, slot)
        x = [jnp.concatenate(_from_chunks(xbufs[slot][pl.ds(u * sub_chunks, sub_chunks)]), axis=1)
             for u in subs]
        gate = [jnp.dot(x[u], wg16[...], preferred_element_type=F32) for u in subs]
        start_all(scatter_copy, tile - 1, nxt)
        up = [jnp.dot(x[u], wu16[...], preferred_element_type=F32) for u in subs]
        hmid = [(gate[u] * _sigmoid(gate[u]) * up[u]).astype(BF16) for u in subs]
        start_all(gather_copy, tile + ahead, nxt)
        y = [jnp.dot(hmid[u], wd16[...], preferred_element_type=F32) for u in subs]
        wait_all(scatter_copy, tile - ahead, old)
        for u in subs:
            obufs[slot][pl.ds(u * sub_chunks, sub_chunks)] = _to_chunks(y[u][:, :half], y[u][:, half:])

    def tile_body(k, carry):
        tile = first_ref[e] + k
        for slot in range(N_RING):
            @pl.when(tile % N_RING == slot)
            def _():
                do_tile(tile, slot)
        return carry

    lax.fori_loop(0, ntile_ref[e], tile_body, 0)

    @pl.when(e == pl.num_programs(0) - 1)
    def _():
        last = n_active - 1
        for slot in range(N_RING):
            @pl.when(last % N_RING == slot)
            def _():
                for d in range(N_RING - 2, 0, -1):
                    wait_all(scatter_copy, last - d, (slot - d) % N_RING)
                start_all(scatter_copy, last, slot)
                wait_all(scatter_copy, last, slot)
                for d in range(1, N_RING):
                    wait_all(gather_copy, last + d, (slot + d) % N_RING)


def _experts(first_tile, tiles_e, src, dst, n_active, xs, w_gate, w_up, w_down):
    chunk_shape = (2 * CHUNK, D_MODEL // 2)
    assert _spare_chunk(N_RING * TILE_CHUNKS - 1) < xs.shape[0]
    tile_buf = pltpu.VMEM((TILE_CHUNKS,) + chunk_shape, BF16)
    grid_spec = pltpu.PrefetchScalarGridSpec(
        num_scalar_prefetch=5,
        grid=(N_EXPERTS,),
        in_specs=[pl.BlockSpec(memory_space=pl.ANY),
                  pl.BlockSpec((1, D_MODEL, D_EXPERT), lambda e, *_: (e, 0, 0)),
                  pl.BlockSpec((1, D_MODEL, D_EXPERT), lambda e, *_: (e, 0, 0)),
                  pl.BlockSpec((1, D_EXPERT, D_MODEL), lambda e, *_: (e, 0, 0))],
        out_specs=pl.BlockSpec(memory_space=pl.ANY),
        scratch_shapes=[tile_buf] * (2 * N_RING) + [
                        pltpu.SemaphoreType.DMA((N_RING,)),
                        pltpu.SemaphoreType.DMA((N_RING,)),
                        pltpu.VMEM((D_MODEL, D_EXPERT), BF16),
                        pltpu.VMEM((D_MODEL, D_EXPERT), BF16),
                        pltpu.VMEM((D_EXPERT, D_MODEL), BF16)],
    )
    return pl.pallas_call(
        _expert_kernel,
        grid_spec=grid_spec,
        out_shape=jax.ShapeDtypeStruct(xs.shape, xs.dtype),
        input_output_aliases={5: 0},
        compiler_params=pltpu.CompilerParams(dimension_semantics=("arbitrary",), vmem_limit_bytes=VMEM_LIMIT),
        name="expert_ffn",
    )(first_tile, tiles_e, src, dst, n_active, xs, w_gate, w_up, w_down)


def _combine_kernel(n_prompt_steps, x2p_ref, x2s_ref, route_ref, gfin_ref, ys_ref, outp_ref, outs_ref):
    i = pl.program_id(0)
    is_prompt = i < n_prompt_steps
    outs = []
    for h in range(STEP_TILES):
        rows = pl.ds(h * TM, TM)
        lo, hi = _from_chunks(ys_ref[pl.ds(h * CAP_CHUNKS, CAP_CHUNKS)])
        route = route_ref[rows, :]
        d1 = route[:, 0:1].astype(jnp.int32)
        d2 = route[:, 1:2].astype(jnp.int32)
        w1 = route[:, 2:3]
        w2 = route[:, 3:4]
        dest = lax.broadcasted_iota(jnp.int32, (TM, CAP_ROWS), 1)
        sel = jnp.where(dest == d1, w1, jnp.where(dest == d2, w2, 0.0)).astype(BF16)
        moe = jnp.concatenate([jnp.dot(sel, lo, preferred_element_type=F32),
                               jnp.dot(sel, hi, preferred_element_type=F32)], axis=1)
        x = jnp.where(is_prompt, x2p_ref[rows, :], x2s_ref[rows, :])
        outs.append(_rms(x + moe, gfin_ref[...]))

    @pl.when(is_prompt)
    def _():
        for h in range(STEP_TILES):
            outp_ref[pl.ds(h * TM, TM), :] = outs[h]

    @pl.when(jnp.logical_not(is_prompt))
    def _():
        for h in range(STEP_TILES):
            outs_ref[pl.ds(h * TM, TM), :] = outs[h]


def _combine(x2p, x2s, route, g_final, ys):
    n_prompt_tiles = x2p.shape[0] // TM
    n_tiles = n_prompt_tiles + x2s.shape[0] // TM
    assert n_prompt_tiles % STEP_TILES == 0 and n_tiles % STEP_TILES == 0
    n_prompt_steps = n_prompt_tiles // STEP_TILES
    rows = STEP_TILES * TM
    p_spec = pl.BlockSpec((rows, D_MODEL), lambda i: (jnp.minimum(i, n_prompt_steps - 1), 0))
    s_spec = pl.BlockSpec((rows, D_MODEL), lambda i: (jnp.maximum(i - n_prompt_steps, 0), 0))
    return pl.pallas_call(
        functools.partial(_combine_kernel, n_prompt_steps),
        grid=(n_tiles // STEP_TILES,),
        in_specs=[p_spec, s_spec,
                  pl.BlockSpec((rows, LANES), lambda i: (i, 0)),
                  _const_spec((1, D_MODEL)),
                  pl.BlockSpec((STEP_TILES * CAP_CHUNKS, 2 * CHUNK, D_MODEL // 2), lambda i: (i, 0, 0))],
        out_specs=[p_spec, s_spec],
        out_shape=[jax.ShapeDtypeStruct(x2p.shape, F32), jax.ShapeDtypeStruct(x2s.shape, F32)],
        compiler_params=pltpu.CompilerParams(dimension_semantics=("arbitrary",), vmem_limit_bytes=VMEM_LIMIT),
        name="combine_norm",
    )(x2p, x2s, route, g_final, ys)


def _block_diag_in(bb):
    t = bb.reshape(2, 16, SSM_GROUP, SSM_STATE)
    eye = jnp.eye(16, dtype=bb.dtype)
    blk = t[:, :, :, None, :] * eye[None, :, None, :, None]
    return blk.reshape(2, 16 * SSM_GROUP, 16 * SSM_STATE)


def _block_diag_out(c):
    t = c.reshape(2, 16, SSM_GROUP, SSM_STATE).transpose(0, 1, 3, 2)
    eye = jnp.eye(16, dtype=c.dtype)
    blk = t[:, :, :, None, :] * eye[None, :, None, :, None]
    return blk.reshape(2, 16 * SSM_STATE, 16 * SSM_GROUP)


def kernel(x_prompt, x_sample, state_ssm_re, state_ssm_im, cache_pool, g_mix, w_in, ssm_a_re, ssm_a_im,
           ssm_log_dt, ssm_b_re, ssm_b_im, ssm_c_re, ssm_c_im, ssm_d, w_glu_a, w_glu_b, pool_w, pool_scale,
           w_out, g_ffn, w_router_group, w_router_expert, w_exp_gate, w_exp_up, w_exp_down, g_final):
    li = 0
    n_pb, seq_p, _ = x_prompt.shape
    n_sb, seq_s, _ = x_sample.shape

    a_re, a_im, a32_re, a32_im, bb_re, bb_im = _discretise(
        ssm_a_re[li], ssm_a_im[li], ssm_log_dt[li], ssm_b_re[li], ssm_b_im[li])
    row = lambda v: v.reshape(1, N_FLAT)
    wb = jnp.concatenate([_block_diag_in(bb_re), _block_diag_in(bb_im)], axis=2).astype(BF16)
    weights = [
        g_mix[li].reshape(1, D_MODEL),
        w_in[li].astype(BF16),
        wb,
        _block_diag_out(ssm_c_re[li]).astype(BF16),
        _block_diag_out(ssm_c_im[li]).astype(BF16),
        ssm_d[li].reshape(1, SSM_WIDTH),
        jnp.concatenate([w_glu_a[li], w_glu_b[li]], axis=1).astype(BF16),
        pool_w[li].astype(BF16),
        pool_scale[li].reshape(1, D_MODEL),
        w_out[li].astype(BF16),
        row(a_re), row(a_im), row(a32_re), row(a32_im),
    ]

    x2p, stp_re, stp_im, histp = _mixer_prompt(x_prompt, weights)
    x2p = x2p.reshape(n_pb * seq_p, D_MODEL)

    n_stiles = n_sb // N_CHAIN
    cache16 = jnp.pad(cache_pool[li], ((0, 0), (1, 0), (0, 0)))
    cache_t = cache16.reshape(n_stiles, N_CHAIN, 16, POOL_WIDTH).transpose(0, 2, 1, 3).reshape(
        n_stiles, HIST_ROWS, POOL_WIDTH)
    x2s, sts_re, sts_im, hists = _mixer_sample(
        x_sample, state_ssm_re[li].reshape(n_sb, N_FLAT), state_ssm_im[li].reshape(n_sb, N_FLAT),
        cache_t, weights)
    x2s = x2s.reshape(n_sb * seq_s, D_MODEL)

    w_router = jnp.concatenate(
        [w_router_group[li], w_router_expert[li].reshape(D_MODEL, N_EXPERTS),
         jnp.zeros((D_MODEL, LANES - N_EXPERT_GROUPS - N_EXPERTS), F32)], axis=1).astype(BF16)
    xs, route, nch = _route(x2p, x2s, g_ffn[li].reshape(1, D_MODEL), w_router)

    n_tiles = (x2p.shape[0] + x2s.shape[0]) // TM
    max_chunks = n_tiles * (2 * TM // CHUNK + N_EXPERTS)
    n_expert_tiles = max_chunks // TILE_CHUNKS + N_EXPERTS
    first_tile, tiles_e, src, dst, n_active = _chunk_tables(nch[:, 0, :N_EXPERTS], n_expert_tiles)
    ys = _experts(first_tile, tiles_e, src, dst, n_active, xs, w_exp_gate[li], w_exp_up[li], w_exp_down[li])
    yp, ysm = _combine(x2p, x2s, route, g_final.reshape(1, D_MODEL), ys)

    sd = state_ssm_re.dtype
    cd = cache_pool.dtype
    y_prompt = yp.reshape(n_pb, seq_p, D_MODEL)
    y_sample = ysm.reshape(n_sb, seq_s, D_MODEL)
    re_p = stp_re.reshape(1, n_pb, SSM_GROUPS, SSM_STATE).astype(sd)
    im_p = stp_im.reshape(1, n_pb, SSM_GROUPS, SSM_STATE).astype(sd)
    hist_p = histp[:, ::N_CHAIN, :][:, 1:, :].reshape(1, n_pb, POOL_HIST, POOL_WIDTH).astype(cd)
    re_s = sts_re.reshape(1, n_sb, SSM_GROUPS, SSM_STATE).astype(sd)
    im_s = sts_im.reshape(1, n_sb, SSM_GROUPS, SSM_STATE).astype(sd)
    hist_s = hists.reshape(n_stiles, 16, N_CHAIN, POOL_WIDTH).transpose(0, 2, 1, 3).reshape(
        n_sb, 16, POOL_WIDTH)[:, 1:, :].reshape(1, n_sb, POOL_HIST, POOL_WIDTH).astype(cd)
    return (y_prompt, y_sample, re_p, im_p, hist_p, re_s, im_s, hist_s)
```

```python
import functools
import math

import jax
import jax.numpy as jnp
from jax import lax
from jax.experimental import pallas as pl
from jax.experimental.pallas import tpu as pltpu

F32 = jnp.float32
BF16 = jnp.bfloat16

D_MODEL = 1024
SSM_WIDTH = 512
SSM_GROUPS = 32
SSM_GROUP = 16
SSM_STATE = 64
N_FLAT = SSM_GROUPS * SSM_STATE
HALF_FLAT = N_FLAT // 2
POOL_WIDTH = 512
POOL_WINDOWS = (2, 4, 8, 16)
POOL_GROUP_IN = 128
POOL_GROUP_OUT = 256
POOL_HIST = 15
N_EXPERTS = 32
EXPERTS_PER_GROUP = 8
N_EXPERT_GROUPS = 4
D_EXPERT = 512
EPS = 1e-6
PAST_LEN = 1024

TM = 256
N_CHAIN = 8
CHAIN_LEN = TM // N_CHAIN
HIST_ROWS = 16 * N_CHAIN

CHUNK = 8
TILE_CHUNKS = TM // CHUNK
CAP_CHUNKS = 96
CAP_ROWS = CAP_CHUNKS * CHUNK
MAX_USED_CHUNKS = 2 * TM // CHUNK + N_EXPERTS * (CHUNK - 1) // CHUNK
N_SPARE_SLOTS = CAP_CHUNKS - MAX_USED_CHUNKS - 1
ZERO_CHUNK = CAP_CHUNKS - 1
LANES = 128

V7X_VMEM_BYTES = 64 * 1024 * 1024
VMEM_LIMIT = V7X_VMEM_BYTES * 13 // 16
STEP_TILES = 4


def _rms(x, g):
    r = lax.rsqrt(jnp.mean(x * x, axis=-1, keepdims=True) + EPS)
    return x * r * g


def _sigmoid(x):
    return 0.5 * jnp.tanh(0.5 * x) + 0.5


def _gelu_tanh(x):
    c = math.sqrt(2.0 / math.pi)
    return x * (0.5 * (1.0 + jnp.tanh(c * (x + 0.044715 * (x * x * x)))))


def _disc_kernel(lre_ref, lim_ref, ldt_ref, bre_ref, bim_ref, cre_ref, cim_ref,
                 tab_ref, wb_ref, wcre_ref, wcim_ref):
    lam_re = jnp.minimum(lre_ref[...], -1e-4)
    lam_im = lim_ref[...]
    dt = jnp.exp(ldt_ref[...])
    mag = jnp.exp(lam_re * dt)
    ang = lam_im * dt
    a_re = mag * jnp.cos(ang)
    a_im = mag * jnp.sin(ang)
    num_re = a_re - 1.0
    num_im = a_im
    den = lam_re * lam_re + lam_im * lam_im
    k_re = ((num_re * lam_re + num_im * lam_im) / den)[:, None, :]
    k_im = ((num_im * lam_re - num_re * lam_im) / den)[:, None, :]
    br = bre_ref[...]
    bi = bim_ref[...]
    bb_re = (k_re * br - k_im * bi).astype(BF16)
    bb_im = (k_re * bi + k_im * br).astype(BF16)
    pr, pi = a_re, a_im
    for _ in range(int(math.log2(CHAIN_LEN))):
        pr, pi = pr * pr - pi * pi, 2.0 * pr * pi
    wb_ref[...] = jnp.zeros_like(wb_ref)
    wcre_ref[...] = jnp.zeros_like(wcre_ref)
    wcim_ref[...] = jnp.zeros_like(wcim_ref)
    groups_per_half = SSM_GROUPS // 2
    for g in range(SSM_GROUPS):
        h, gl = divmod(g, groups_per_half)
        q0, n0 = gl * SSM_GROUP, gl * SSM_STATE
        wb_ref[h, q0:q0 + SSM_GROUP, n0:n0 + SSM_STATE] = bb_re[g]
        wb_ref[h, q0:q0 + SSM_GROUP, HALF_FLAT + n0:HALF_FLAT + n0 + SSM_STATE] = bb_im[g]
        wcre_ref[h, n0:n0 + SSM_STATE, q0:q0 + SSM_GROUP] = cre_ref[g].astype(BF16)
        wcim_ref[h, n0:n0 + SSM_STATE, q0:q0 + SSM_GROUP] = cim_ref[g].astype(BF16)
        for k, v in enumerate((a_re, a_im, pr, pi)):
            tab_ref[k:k + 1, g * SSM_STATE:(g + 1) * SSM_STATE] = v[g:g + 1, :]


def _discretise(a_re, a_im, log_dt, b_re, b_im, c_re, c_im):
    chan_major = lambda b: jnp.swapaxes(b, 1, 2)
    half_w = SSM_WIDTH // 2
    return pl.pallas_call(
        _disc_kernel,
        out_shape=[jax.ShapeDtypeStruct((4, N_FLAT), F32),
                   jax.ShapeDtypeStruct((2, half_w, 2 * HALF_FLAT), BF16),
                   jax.ShapeDtypeStruct((2, HALF_FLAT, half_w), BF16),
                   jax.ShapeDtypeStruct((2, HALF_FLAT, half_w), BF16)],
        name="s5_discretise",
    )(a_re, a_im, log_dt.reshape(SSM_GROUPS, 1), chan_major(b_re), chan_major(b_im),
      jnp.swapaxes(c_re, 1, 2), jnp.swapaxes(c_im, 1, 2))


def _scan_half(hbuf, h, ar, ai, init_re, init_im, store):
    cre = h * N_FLAT
    cim = cre + HALF_FLAT
    hr, hi = init_re, init_im
    for t in range(CHAIN_LEN):
        rows = pl.ds(N_CHAIN * t, N_CHAIN)
        br = hbuf[rows, cre:cre + HALF_FLAT]
        bi = hbuf[rows, cim:cim + HALF_FLAT]
        nr = ar * hr - ai * hi + br
        ni = ar * hi + ai * hr + bi
        if store:
            hbuf[rows, cre:cre + HALF_FLAT] = nr
            hbuf[rows, cim:cim + HALF_FLAT] = ni
        hr, hi = nr, ni
    return hr, hi


N_MIXER_WEIGHTS = 11
N_PAR = 2


def _mixer_kernel(is_prompt, tiles_per_stream, *refs):
    n_in = 1 if is_prompt else 4
    ins = refs[:n_in]
    (gmix, win, wb, wcre, wcim, dsk, wglu, poolw, pscale, wout, atab) = refs[n_in:n_in + N_MIXER_WEIGHTS]
    x2_ref, stre_ref, stim_ref, hist_ref = refs[n_in + N_MIXER_WEIGHTS:n_in + N_MIXER_WEIGHTS + 4]
    scratch = refs[n_in + N_MIXER_WEIGHTS + 4:]
    per = len(scratch) // N_PAR
    lanes = [scratch[p * per:(p + 1) * per] for p in range(N_PAR)]
    x_ref = ins[0]

    tile_in_stream = pl.program_id(0) % tiles_per_stream
    n_lane_blocks = D_MODEL // LANES
    half_w = SSM_WIDTH // 2
    ar_full = atab[0:1, :]
    ai_full = atab[1:2, :]

    if is_prompt:
        @pl.when(pl.program_id(0) == 0)
        def _():
            for p in range(N_PAR):
                cre_s, cim_s, pcarry = lanes[p][8], lanes[p][9], lanes[p][10]
                cre_s[...] = jnp.zeros_like(cre_s)
                cim_s[...] = jnp.zeros_like(cim_s)
                pcarry[...] = jnp.zeros_like(pcarry)

    def a_half(h):
        f0 = h * HALF_FLAT
        return (jnp.broadcast_to(ar_full[:, f0:f0 + HALF_FLAT], (N_CHAIN, HALF_FLAT)),
                jnp.broadcast_to(ai_full[:, f0:f0 + HALF_FLAT], (N_CHAIN, HALF_FLAT)))

    def project(p):
        xperm, hbuf = lanes[p][0], lanes[p][1]
        for j in range(N_CHAIN):
            for cb in range(n_lane_blocks):
                xperm[cb, pl.ds(j, CHAIN_LEN, stride=N_CHAIN), :] = x_ref[
                    p, CHAIN_LEN * j:CHAIN_LEN * (j + 1), cb * LANES:(cb + 1) * LANES]
        x = jnp.concatenate([xperm[cb] for cb in range(n_lane_blocks)], axis=1)
        xn = _rms(x, gmix[...]).astype(BF16)
        proj = jnp.dot(xn, win[...], preferred_element_type=F32)
        ub = proj[:, :SSM_WIDTH].astype(BF16)
        for h in range(2):
            hbuf[:, h * N_FLAT:(h + 1) * N_FLAT] = jnp.dot(
                ub[:, h * half_w:(h + 1) * half_w], wb[h], preferred_element_type=F32)
        return x, proj

    def recur(p):
        hbuf = lanes[p][1]
        if is_prompt:
            fre, fim, hre, him, cre_s, cim_s = lanes[p][4:10]
            zeros = jnp.zeros((N_CHAIN, HALF_FLAT), F32)
            for h in range(2):
                f0 = h * HALF_FLAT
                ar, ai = a_half(h)
                lr, li = _scan_half(hbuf, h, ar, ai, zeros, zeros, store=False)
                fre[:, f0:f0 + HALF_FLAT] = lr
                fim[:, f0:f0 + HALF_FLAT] = li
            fresh = tile_in_stream == 0
            hre[0:1, :] = jnp.where(fresh, 0.0, cre_s[...])
            him[0:1, :] = jnp.where(fresh, 0.0, cim_s[...])
            p_re = atab[2:3, :]
            p_im = atab[3:4, :]
            for j in range(N_CHAIN - 1):
                sr = hre[j:j + 1, :]
                si = him[j:j + 1, :]
                hre[j + 1:j + 2, :] = fre[j:j + 1, :] + p_re * sr - p_im * si
                him[j + 1:j + 2, :] = fim[j:j + 1, :] + p_re * si + p_im * sr
            init_re = hre[...]
            init_im = him[...]
        else:
            init_re = ins[1][p]
            init_im = ins[2][p]
        fin_re = []
        fin_im = []
        for h in range(2):
            f0 = h * HALF_FLAT
            ar, ai = a_half(h)
            er, ei = _scan_half(hbuf, h, ar, ai, init_re[:, f0:f0 + HALF_FLAT],
                                init_im[:, f0:f0 + HALF_FLAT], store=True)
            fin_re.append(er)
            fin_im.append(ei)
        end_re = jnp.concatenate(fin_re, axis=1)
        end_im = jnp.concatenate(fin_im, axis=1)
        if is_prompt:
            cre_s[...] = end_re[N_CHAIN - 1:N_CHAIN, :]
            cim_s[...] = end_im[N_CHAIN - 1:N_CHAIN, :]
            stre_ref[p] = end_re[N_CHAIN - 1:N_CHAIN, :]
            stim_ref[p] = end_im[N_CHAIN - 1:N_CHAIN, :]
        else:
            stre_ref[p] = end_re
            stim_ref[p] = end_im

    def finish(p, x, proj):
        hbuf, res, xpbuf = lanes[p][1], lanes[p][2], lanes[p][3]
        u_s = proj[:, :SSM_WIDTH]
        u_p = proj[:, SSM_WIDTH:SSM_WIDTH + POOL_WIDTH]
        gate_s = proj[:, SSM_WIDTH + POOL_WIDTH:SSM_WIDTH + POOL_WIDTH + D_MODEL]
        gate_p = proj[:, SSM_WIDTH + POOL_WIDTH + D_MODEL:]
        ys = []
        for h in range(2):
            c0 = h * N_FLAT
            h_re = hbuf[:, c0:c0 + HALF_FLAT].astype(BF16)
            h_im = hbuf[:, c0 + HALF_FLAT:c0 + N_FLAT].astype(BF16)
            ys.append(jnp.dot(h_re, wcre[h], preferred_element_type=F32)
                      - jnp.dot(h_im, wcim[h], preferred_element_type=F32))
        y = jnp.concatenate(ys, axis=1) + dsk[...] * u_s
        g = _gelu_tanh(y).astype(BF16)
        glu = jnp.dot(g, wglu[...], preferred_element_type=F32)
        o_s = glu[:, :D_MODEL] * _sigmoid(glu[:, D_MODEL:])

        xpbuf[HIST_ROWS:HIST_ROWS + TM, :] = u_p
        tail = u_p[TM - HIST_ROWS:, :]
        row = lax.broadcasted_iota(jnp.int32, (TM, 1), 0)
        if is_prompt:
            pcarry = lanes[p][10]
            first_chain = (lax.broadcasted_iota(jnp.int32, (HIST_ROWS, POOL_WIDTH), 0) % N_CHAIN) == 0
            carried = jnp.where(tile_in_stream == 0, 0.0, pcarry[...])
            xpbuf[0:HIST_ROWS, :] = jnp.where(first_chain, carried, pltpu.roll(tail, 1, 0))
            new_carry = pltpu.roll(tail, HIST_ROWS - (N_CHAIN - 1), 0)
            pcarry[...] = new_carry
            hist_ref[p] = new_carry
            pos1 = tile_in_stream * TM + CHAIN_LEN * (row % N_CHAIN) + row // N_CHAIN + 1
        else:
            xpbuf[0:HIST_ROWS, :] = ins[3][p]
            hist_ref[p] = tail
            pos1 = PAST_LEN + row // N_CHAIN + 1

        o_ps = []
        for gi, w in enumerate(POOL_WINDOWS):
            c0 = gi * POOL_GROUP_IN
            acc = xpbuf[HIST_ROWS:HIST_ROWS + TM, c0:c0 + POOL_GROUP_IN]
            for k in range(1, w):
                acc = acc + xpbuf[HIST_ROWS - N_CHAIN * k:HIST_ROWS - N_CHAIN * k + TM, c0:c0 + POOL_GROUP_IN]
            cnt = jnp.minimum(w, pos1).astype(F32)
            pooled = acc / cnt
            z = (pooled - u_p[:, c0:c0 + POOL_GROUP_IN]).astype(BF16)
            o_ps.append(jnp.dot(z, poolw[gi], preferred_element_type=F32))
        o_p = jnp.concatenate(o_ps, axis=1) * pscale[...]

        merged = (_sigmoid(gate_s) * o_s + _sigmoid(gate_p) * o_p).astype(BF16)
        x2 = x + jnp.dot(merged, wout[...], preferred_element_type=F32)
        for cb in range(n_lane_blocks):
            res[cb] = x2[:, cb * LANES:(cb + 1) * LANES]
        for j in range(N_CHAIN):
            for cb in range(n_lane_blocks):
                x2_ref[p, CHAIN_LEN * j:CHAIN_LEN * (j + 1), cb * LANES:(cb + 1) * LANES] = res[
                    cb, pl.ds(j, CHAIN_LEN, stride=N_CHAIN), :]

    projected = [None] * N_PAR
    for k in range(N_PAR + 2):
        if 0 <= k - 2 < N_PAR:
            finish(k - 2, *projected[k - 2])
        if 0 <= k - 1 < N_PAR:
            recur(k - 1)
        if k < N_PAR:
            projected[k] = project(k)


def _const_spec(shape):
    nd = len(shape)
    return pl.BlockSpec(shape, lambda i, _nd=nd: (0,) * _nd)


def _mixer_weight_specs():
    return [
        _const_spec((1, D_MODEL)),
        _const_spec((D_MODEL, 3 * D_MODEL)),
        _const_spec((2, SSM_WIDTH // 2, N_FLAT)),
        _const_spec((2, HALF_FLAT, SSM_WIDTH // 2)),
        _const_spec((2, HALF_FLAT, SSM_WIDTH // 2)),
        _const_spec((1, SSM_WIDTH)),
        _const_spec((SSM_WIDTH, 2 * D_MODEL)),
        _const_spec((len(POOL_WINDOWS), POOL_GROUP_IN, POOL_GROUP_OUT)),
        _const_spec((1, D_MODEL)),
        _const_spec((D_MODEL, D_MODEL)),
        _const_spec((4, N_FLAT)),
    ]


def _mixer_common_scratch():
    return [
        pltpu.VMEM((D_MODEL // LANES, TM, LANES), F32),
        pltpu.VMEM((TM, 2 * N_FLAT), F32),
        pltpu.VMEM((D_MODEL // LANES, TM, LANES), F32),
        pltpu.VMEM((HIST_ROWS + TM, POOL_WIDTH), F32),
    ]


def _mixer_prompt(x, weights):
    n_streams, seq, _ = x.shape
    assert n_streams % N_PAR == 0 and seq % TM == 0
    tiles_per_stream = seq // TM
    blk = lambda shape: pl.BlockSpec(shape, lambda i: (i // tiles_per_stream, 0, 0))
    row_spec = pl.BlockSpec((N_PAR, TM, D_MODEL), lambda i: (i // tiles_per_stream, i % tiles_per_stream, 0))
    lane_scratch = _mixer_common_scratch() + [
        pltpu.VMEM((N_CHAIN, N_FLAT), F32), pltpu.VMEM((N_CHAIN, N_FLAT), F32),
        pltpu.VMEM((N_CHAIN, N_FLAT), F32), pltpu.VMEM((N_CHAIN, N_FLAT), F32),
        pltpu.VMEM((1, N_FLAT), F32), pltpu.VMEM((1, N_FLAT), F32),
        pltpu.VMEM((HIST_ROWS, POOL_WIDTH), F32),
    ]
    return pl.pallas_call(
        functools.partial(_mixer_kernel, True, tiles_per_stream),
        grid=(n_streams // N_PAR * tiles_per_stream,),
        in_specs=[row_spec] + _mixer_weight_specs(),
        out_specs=[row_spec, blk((N_PAR, 1, N_FLAT)), blk((N_PAR, 1, N_FLAT)),
                   blk((N_PAR, HIST_ROWS, POOL_WIDTH))],
        out_shape=[jax.ShapeDtypeStruct((n_streams, seq, D_MODEL), F32),
                   jax.ShapeDtypeStruct((n_streams, 1, N_FLAT), F32),
                   jax.ShapeDtypeStruct((n_streams, 1, N_FLAT), F32),
                   jax.ShapeDtypeStruct((n_streams, HIST_ROWS, POOL_WIDTH), F32)],
        scratch_shapes=lane_scratch * N_PAR,
        compiler_params=pltpu.CompilerParams(dimension_semantics=("arbitrary",), vmem_limit_bytes=VMEM_LIMIT),
        name="mixer_prompt",
    )(x, *weights)


def _mixer_sample(x, h0_re, h0_im, cache_t, weights):
    n_streams, seq, _ = x.shape
    assert seq == CHAIN_LEN and n_streams % (N_CHAIN * N_PAR) == 0
    n_tiles = n_streams // N_CHAIN
    blk = lambda shape: pl.BlockSpec(shape, lambda i: (i, 0, 0))
    row_spec = blk((N_PAR, TM, D_MODEL))
    st_spec = blk((N_PAR, N_CHAIN, N_FLAT))
    hist_spec = blk((N_PAR, HIST_ROWS, POOL_WIDTH))
    tiles = lambda v: v.reshape((n_tiles, -1) + v.shape[-1:])
    return pl.pallas_call(
        functools.partial(_mixer_kernel, False, 1),
        grid=(n_tiles // N_PAR,),
        in_specs=[row_spec, st_spec, st_spec, hist_spec] + _mixer_weight_specs(),
        out_specs=[row_spec, st_spec, st_spec, hist_spec],
        out_shape=[jax.ShapeDtypeStruct((n_tiles, TM, D_MODEL), F32),
                   jax.ShapeDtypeStruct((n_tiles, N_CHAIN, N_FLAT), F32),
                   jax.ShapeDtypeStruct((n_tiles, N_CHAIN, N_FLAT), F32),
                   jax.ShapeDtypeStruct((n_tiles, HIST_ROWS, POOL_WIDTH), F32)],
        scratch_shapes=_mixer_common_scratch() * N_PAR,
        compiler_params=pltpu.CompilerParams(dimension_semantics=("arbitrary",), vmem_limit_bytes=VMEM_LIMIT),
        name="mixer_sample",
    )(tiles(x), tiles(h0_re), tiles(h0_im), cache_t, *weights)


def _to_chunks(lo, hi):
    n = lo.shape[0] // CHUNK
    half = D_MODEL // 2
    both = jnp.concatenate([lo.reshape(n, CHUNK, half), hi.reshape(n, CHUNK, half)], axis=1)
    return both.astype(BF16)


def _from_chunks(blk):
    n = blk.shape[0]
    half = D_MODEL // 2
    f = blk.astype(F32)
    lo = f[:, :CHUNK, :].reshape(n * CHUNK, half).astype(BF16)
    hi = f[:, CHUNK:, :].reshape(n * CHUNK, half).astype(BF16)
    return lo, hi


def _route_kernel(n_prompt_steps, x2p_ref, x2s_ref, g_ref, wr_ref, ltri_ref, utri_ref,
                  xs_ref, route_ref, nch_ref):
    is_prompt = pl.program_id(0) < n_prompt_steps
    tiles = range(STEP_TILES)
    each = lambda fn, *cols: [fn(*(c[h] for c in cols)) for h in tiles]
    row_max = lambda v: jnp.max(v, axis=1, keepdims=True)
    row_min = lambda v: jnp.min(v, axis=1, keepdims=True)
    row_sum = lambda v: jnp.sum(v, axis=1, keepdims=True)

    lane = lax.broadcasted_iota(jnp.int32, (TM, LANES), 1)
    lane_f = lane.astype(F32)
    big = jnp.float32(1 << 20)
    neg = jnp.float32(-jnp.inf)
    gmask = lane < N_EXPERT_GROUPS
    eid = lane - N_EXPERT_GROUPS
    lane_grp = (eid >> 3).astype(F32)
    is_expert = (eid >= 0) & (eid < N_EXPERTS)

    xn = [_rms(jnp.where(is_prompt, x2p_ref[pl.ds(h * TM, TM), :], x2s_ref[pl.ds(h * TM, TM), :]),
               g_ref[...]).astype(BF16) for h in tiles]
    logits = each(lambda v: jnp.dot(v, wr_ref[...], preferred_element_type=F32), xn)
    m = each(lambda lg: row_max(jnp.where(gmask, lg, neg)), logits)
    grp = each(lambda lg, mm: row_min(jnp.where(gmask & (lg == mm), lane_f, big)), logits, m)
    wg = each(lambda lg, mm: 1.0 / row_sum(jnp.where(gmask, jnp.exp(lg - mm), 0.0)), logits, m)
    emask = each(lambda g: is_expert & (lane_grp == g), grp)
    v1 = each(lambda lg, em: row_max(jnp.where(em, lg, neg)), logits, emask)
    i1 = each(lambda lg, em, v: row_min(jnp.where(em & (lg == v), lane_f, big)), logits, emask, v1)
    emask2 = each(lambda em, i: em & (lane_f != i), emask, i1)
    v2 = each(lambda lg, em: row_max(jnp.where(em, lg, neg)), logits, emask2)
    i2 = each(lambda lg, em, v: row_min(jnp.where(em & (lg == v), lane_f, big)), logits, emask2, v2)
    e21 = each(lambda a, b: jnp.exp(b - a), v1, v2)
    w1 = each(lambda g, e: g / (1.0 + e), wg, e21)
    w2 = each(lambda g, e: g * e / (1.0 + e), wg, e21)

    a1 = each(lambda i: lane_f == (i - N_EXPERT_GROUPS), i1)
    a2 = each(lambda i: lane_f == (i - N_EXPERT_GROUPS), i2)
    a = each(lambda p, q: (p | q).astype(F32), a1, a2)
    before = each(lambda v: jnp.dot(ltri_ref[...], v.astype(BF16), preferred_element_type=F32), a)
    cnt = each(lambda v: jnp.sum(v, axis=0, keepdims=True), a)
    nch16 = each(lambda c: jnp.broadcast_to(jnp.floor((c + (CHUNK - 1)) * (1.0 / CHUNK)), (16, LANES)), cnt)
    start = each(lambda n: jnp.dot(n.astype(BF16), utri_ref[...], preferred_element_type=F32), nch16)
    slot = each(lambda bf, st: bf + CHUNK * st[0:1, :], before, start)
    d1 = each(lambda p, sl: row_sum(jnp.where(p, sl, 0.0)), a1, slot)
    d2 = each(lambda p, sl: row_sum(jnp.where(p, sl, 0.0)), a2, slot)
    route = each(lambda p, q, u, v: jnp.where(lane == 0, p, jnp.where(lane == 1, q, jnp.where(
        lane == 2, u, jnp.where(lane == 3, v, 0.0)))), d1, d2, w1, w2)

    dest = lax.broadcasted_iota(jnp.int32, (CAP_ROWS, TM), 0)
    half = D_MODEL // 2
    for h in tiles:
        route_ref[pl.ds(h * TM, TM), :] = route[h]
        nch_ref[h] = nch16[h][0:8, :].astype(jnp.int32)
        dt = jnp.transpose(jnp.where(lane < 2, route[h], 0.0)).astype(jnp.int32)
        perm = ((dest == dt[0:1, :]) | (dest == dt[1:2, :])).astype(F32).astype(BF16)
        lo = jnp.dot(perm, xn[h][:, :half], preferred_element_type=F32)
        hi = jnp.dot(perm, xn[h][:, half:], preferred_element_type=F32)
        xs_ref[pl.ds(h * CAP_CHUNKS, CAP_CHUNKS)] = _to_chunks(lo, hi)


def _route(x2p, x2s, g_ffn, w_router):
    n_prompt_tiles = x2p.shape[0] // TM
    n_tiles = n_prompt_tiles + x2s.shape[0] // TM
    assert n_prompt_tiles % STEP_TILES == 0 and n_tiles % STEP_TILES == 0
    n_prompt_steps = n_prompt_tiles // STEP_TILES
    rows = STEP_TILES * TM
    r = jnp.arange(TM)
    ltri = (r[None, :] < r[:, None]).astype(BF16)
    e = jnp.arange(LANES)
    utri = (e[:, None] < e[None, :]).astype(BF16)
    return pl.pallas_call(
        functools.partial(_route_kernel, n_prompt_steps),
        grid=(n_tiles // STEP_TILES,),
        in_specs=[pl.BlockSpec((rows, D_MODEL), lambda i: (jnp.minimum(i, n_prompt_steps - 1), 0)),
                  pl.BlockSpec((rows, D_MODEL), lambda i: (jnp.maximum(i - n_prompt_steps, 0), 0)),
                  _const_spec((1, D_MODEL)), _const_spec((D_MODEL, LANES)),
                  _const_spec((TM, TM)), _const_spec((LANES, LANES))],
        out_specs=[pl.BlockSpec((STEP_TILES * CAP_CHUNKS, 2 * CHUNK, D_MODEL // 2), lambda i: (i, 0, 0)),
                   pl.BlockSpec((rows, LANES), lambda i: (i, 0)),
                   pl.BlockSpec((STEP_TILES, 8, LANES), lambda i: (i, 0, 0))],
        out_shape=[jax.ShapeDtypeStruct((n_tiles * CAP_CHUNKS, 2 * CHUNK, D_MODEL // 2), BF16),
                   jax.ShapeDtypeStruct((n_tiles * TM, LANES), F32),
                   jax.ShapeDtypeStruct((n_tiles, 8, LANES), jnp.int32)],
        compiler_params=pltpu.CompilerParams(dimension_semantics=("arbitrary",), vmem_limit_bytes=VMEM_LIMIT),
        name="route_sort",
    )(x2p, x2s, g_ffn, w_router, ltri, utri)


def _spare_chunk(k):
    return (1 + k // N_SPARE_SLOTS) * CAP_CHUNKS + MAX_USED_CHUNKS + k % N_SPARE_SLOTS


def _chunk_tables(nch, n_expert_tiles):
    n_tiles = nch.shape[0]
    i32 = jnp.int32
    start = jnp.cumsum(nch, axis=1) - nch
    off = jnp.cumsum(nch, axis=0) - nch
    per_expert = jnp.sum(nch, axis=0)
    tiles_e = (per_expert + TILE_CHUNKS - 1) // TILE_CHUNKS
    cum_tiles = jnp.cumsum(tiles_e)
    first_tile = cum_tiles - tiles_e
    n_active = cum_tiles[-1].astype(i32)
    t = jnp.arange(n_expert_tiles, dtype=i32)
    te = jnp.minimum(jnp.sum((t[:, None] >= cum_tiles[None, :]).astype(i32), axis=1), N_EXPERTS - 1)
    onehot = te[:, None] == jnp.arange(N_EXPERTS, dtype=i32)[None, :]
    pick = lambda tab: jnp.sum(jnp.where(onehot[:, :, None], tab.T[None, :, :], 0), axis=1)
    off_t, nch_t, start_t = pick(off), pick(nch), pick(start)
    k = t - jnp.sum(jnp.where(onehot, first_tile[None, :], 0), axis=1)
    q = (TILE_CHUNKS * k)[:, None] + jnp.arange(TILE_CHUNKS, dtype=i32)[None, :]
    in_run = (off_t[:, None, :] <= q[:, :, None]) & (q[:, :, None] < (off_t + nch_t)[:, None, :])
    run_src = (jnp.arange(n_tiles, dtype=i32) * CAP_CHUNKS)[None, :] + start_t - off_t
    src = jnp.sum(jnp.where(in_run, run_src[:, None, :], 0), axis=-1) + q
    valid = jnp.any(in_run, axis=-1) & (t < n_active)[:, None]
    src = jnp.where(valid, src, ZERO_CHUNK).astype(i32)
    spare = _spare_chunk((t % N_RING)[:, None] * TILE_CHUNKS + jnp.arange(TILE_CHUNKS, dtype=i32)[None, :])
    dst = jnp.where(valid, src, spare).astype(i32)
    dst = jnp.concatenate([spare[1:N_RING], dst], axis=0)
    return first_tile.astype(i32), tiles_e.astype(i32), src.reshape(-1), dst.reshape(-1), n_active.reshape(1)


N_RING = 5


def _expert_kernel(first_ref, ntile_ref, src_ref, dst_ref, nact_ref, xs_hbm, wg_ref, wu_ref, wd_ref, ys_hbm,
                   *scratch):
    xbufs = scratch[:N_RING]
    obufs = scratch[N_RING:2 * N_RING]
    gsem, ssem, wg16, wu16, wd16 = scratch[2 * N_RING:]
    e = pl.program_id(0)
    n_active = nact_ref[0]
    ahead = N_RING - 1

    def gather_copy(tile, slot, c):
        return pltpu.make_async_copy(xs_hbm.at[src_ref[tile * TILE_CHUNKS + c]], xbufs[slot].at[c],
                                     gsem.at[slot])

    def scatter_copy(tile, slot, c):
        return pltpu.make_async_copy(obufs[slot].at[c], ys_hbm.at[dst_ref[(tile + ahead) * TILE_CHUNKS + c]],
                                     ssem.at[slot])

    def start_all(copy, tile, slot):
        for c in range(TILE_CHUNKS):
            copy(tile, slot, c).start()

    def wait_all(copy, tile, slot):
        for c in range(TILE_CHUNKS):
            copy(tile, slot, c).wait()

    @pl.when(e == 0)
    def _():
        for v in range(ahead):
            start_all(gather_copy, v, v)
        for u in range(-ahead, 0):
            obufs[u % N_RING][...] = jnp.zeros_like(obufs[u % N_RING])
        for u in range(-ahead, -1):
            start_all(scatter_copy, u, u % N_RING)

    wg16[...] = wg_ref[0].astype(BF16)
    wu16[...] = wu_ref[0].astype(BF16)
    wd16[...] = wd_ref[0].astype(BF16)

    def do_tile(tile, slot):
        nxt = (slot + ahead) % N_RING
        old = (slot + 1) % N_RING
        wait_all(gather_copy, tile, slot)
        lo, hi = _from_chunks(xbufs[slot][...])
        x = jnp.concatenate([lo, hi], axis=1)
        gate = jnp.dot(x, wg16[...], preferred_element_type=F32)
        start_all(scatter_copy, tile - 1, nxt)
        up = jnp.dot(x, wu16[...], preferred_element_type=F32)
        hmid = (gate * _sigmoid(gate) * up).astype(BF16)
        start_all(gather_copy, tile + ahead, nxt)
        y = jnp.dot(hmid, wd16[...], preferred_element_type=F32)
        half = D_MODEL // 2
        wait_all(scatter_copy, tile - ahead, old)
        obufs[slot][...] = _to_chunks(y[:, :half], y[:, half:])

    def tile_body(k, carry):
        tile = first_ref[e] + k
        for slot in range(N_RING):
            @pl.when(tile % N_RING == slot)
            def _():
                do_tile(tile, slot)
        return carry

    lax.fori_loop(0, ntile_ref[e], tile_body, 0)

    @pl.when(e == pl.num_programs(0) - 1)
    def _():
        last = n_active - 1
        for slot in range(N_RING):
            @pl.when(last % N_RING == slot)
            def _():
                for d in range(N_RING - 2, 0, -1):
                    wait_all(scatter_copy, last - d, (slot - d) % N_RING)
                start_all(scatter_copy, last, slot)
                wait_all(scatter_copy, last, slot)
                for d in range(1, N_RING):
                    wait_all(gather_copy, last + d, (slot + d) % N_RING)


def _experts(first_tile, tiles_e, src, dst, n_active, xs, w_gate, w_up, w_down):
    chunk_shape = (2 * CHUNK, D_MODEL // 2)
    assert _spare_chunk(N_RING * TILE_CHUNKS - 1) < xs.shape[0]
    tile_buf = pltpu.VMEM((TILE_CHUNKS,) + chunk_shape, BF16)
    grid_spec = pltpu.PrefetchScalarGridSpec(
        num_scalar_prefetch=5,
        grid=(N_EXPERTS,),
        in_specs=[pl.BlockSpec(memory_space=pl.ANY),
                  pl.BlockSpec((1, D_MODEL, D_EXPERT), lambda e, *_: (e, 0, 0)),
                  pl.BlockSpec((1, D_MODEL, D_EXPERT), lambda e, *_: (e, 0, 0)),
                  pl.BlockSpec((1, D_EXPERT, D_MODEL), lambda e, *_: (e, 0, 0))],
        out_specs=pl.BlockSpec(memory_space=pl.ANY),
        scratch_shapes=[tile_buf] * (2 * N_RING) + [
                        pltpu.SemaphoreType.DMA((N_RING,)),
                        pltpu.SemaphoreType.DMA((N_RING,)),
                        pltpu.VMEM((D_MODEL, D_EXPERT), BF16),
                        pltpu.VMEM((D_MODEL, D_EXPERT), BF16),
                        pltpu.VMEM((D_EXPERT, D_MODEL), BF16)],
    )
    return pl.pallas_call(
        _expert_kernel,
        grid_spec=grid_spec,
        out_shape=jax.ShapeDtypeStruct(xs.shape, xs.dtype),
        input_output_aliases={5: 0},
        compiler_params=pltpu.CompilerParams(dimension_semantics=("arbitrary",), vmem_limit_bytes=VMEM_LIMIT),
        name="expert_ffn",
    )(first_tile, tiles_e, src, dst, n_active, xs, w_gate, w_up, w_down)


def _combine_kernel(n_prompt_steps, x2p_ref, x2s_ref, route_ref, gfin_ref, ys_ref, outp_ref, outs_ref):
    i = pl.program_id(0)
    is_prompt = i < n_prompt_steps
    outs = []
    for h in range(STEP_TILES):
        rows = pl.ds(h * TM, TM)
        lo, hi = _from_chunks(ys_ref[pl.ds(h * CAP_CHUNKS, CAP_CHUNKS)])
        route = route_ref[rows, :]
        d1 = route[:, 0:1].astype(jnp.int32)
        d2 = route[:, 1:2].astype(jnp.int32)
        w1 = route[:, 2:3]
        w2 = route[:, 3:4]
        dest = lax.broadcasted_iota(jnp.int32, (TM, CAP_ROWS), 1)
        sel = jnp.where(dest == d1, w1, jnp.where(dest == d2, w2, 0.0)).astype(BF16)
        moe = jnp.concatenate([jnp.dot(sel, lo, preferred_element_type=F32),
                               jnp.dot(sel, hi, preferred_element_type=F32)], axis=1)
        x = jnp.where(is_prompt, x2p_ref[rows, :], x2s_ref[rows, :])
        outs.append(_rms(x + moe, gfin_ref[...]))

    @pl.when(is_prompt)
    def _():
        for h in range(STEP_TILES):
            outp_ref[pl.ds(h * TM, TM), :] = outs[h]

    @pl.when(jnp.logical_not(is_prompt))
    def _():
        for h in range(STEP_TILES):
            outs_ref[pl.ds(h * TM, TM), :] = outs[h]


def _combine(x2p, x2s, route, g_final, ys):
    n_prompt_tiles = x2p.shape[0] // TM
    n_tiles = n_prompt_tiles + x2s.shape[0] // TM
    assert n_prompt_tiles % STEP_TILES == 0 and n_tiles % STEP_TILES == 0
    n_prompt_steps = n_prompt_tiles // STEP_TILES
    rows = STEP_TILES * TM
    p_spec = pl.BlockSpec((rows, D_MODEL), lambda i: (jnp.minimum(i, n_prompt_steps - 1), 0))
    s_spec = pl.BlockSpec((rows, D_MODEL), lambda i: (jnp.maximum(i - n_prompt_steps, 0), 0))
    return pl.pallas_call(
        functools.partial(_combine_kernel, n_prompt_steps),
        grid=(n_tiles // STEP_TILES,),
        in_specs=[p_spec, s_spec,
                  pl.BlockSpec((rows, LANES), lambda i: (i, 0)),
                  _const_spec((1, D_MODEL)),
                  pl.BlockSpec((STEP_TILES * CAP_CHUNKS, 2 * CHUNK, D_MODEL // 2), lambda i: (i, 0, 0))],
        out_specs=[p_spec, s_spec],
        out_shape=[jax.ShapeDtypeStruct(x2p.shape, F32), jax.ShapeDtypeStruct(x2s.shape, F32)],
        compiler_params=pltpu.CompilerParams(dimension_semantics=("arbitrary",), vmem_limit_bytes=VMEM_LIMIT),
        name="combine_norm",
    )(x2p, x2s, route, g_final, ys)


def kernel(x_prompt, x_sample, state_ssm_re, state_ssm_im, cache_pool, g_mix, w_in, ssm_a_re, ssm_a_im,
           ssm_log_dt, ssm_b_re, ssm_b_im, ssm_c_re, ssm_c_im, ssm_d, w_glu_a, w_glu_b, pool_w, pool_scale,
           w_out, g_ffn, w_router_group, w_router_expert, w_exp_gate, w_exp_up, w_exp_down, g_final):
    li = 0
    n_pb, seq_p, _ = x_prompt.shape
    n_sb, seq_s, _ = x_sample.shape

    a_tab, wb, wc_re, wc_im = _discretise(
        ssm_a_re[li], ssm_a_im[li], ssm_log_dt[li], ssm_b_re[li], ssm_b_im[li], ssm_c_re[li], ssm_c_im[li])
    weights = [
        g_mix[li].reshape(1, D_MODEL),
        w_in[li].astype(BF16),
        wb,
        wc_re,
        wc_im,
        ssm_d[li].reshape(1, SSM_WIDTH),
        jnp.concatenate([w_glu_a[li], w_glu_b[li]], axis=1).astype(BF16),
        pool_w[li].astype(BF16),
        pool_scale[li].reshape(1, D_MODEL),
        w_out[li].astype(BF16),
        a_tab,
    ]

    x2p, stp_re, stp_im, histp = _mixer_prompt(x_prompt, weights)
    x2p = x2p.reshape(n_pb * seq_p, D_MODEL)

    n_stiles = n_sb // N_CHAIN
    cache16 = jnp.pad(cache_pool[li], ((0, 0), (1, 0), (0, 0)))
    cache_t = cache16.reshape(n_stiles, N_CHAIN, 16, POOL_WIDTH).transpose(0, 2, 1, 3).reshape(
        n_stiles, HIST_ROWS, POOL_WIDTH)
    x2s, sts_re, sts_im, hists = _mixer_sample(
        x_sample, state_ssm_re[li].reshape(n_sb, N_FLAT), state_ssm_im[li].reshape(n_sb, N_FLAT),
        cache_t, weights)
    x2s = x2s.reshape(n_sb * seq_s, D_MODEL)

    w_router = jnp.concatenate(
        [w_router_group[li], w_router_expert[li].reshape(D_MODEL, N_EXPERTS),
         jnp.zeros((D_MODEL, LANES - N_EXPERT_GROUPS - N_EXPERTS), F32)], axis=1).astype(BF16)
    xs, route, nch = _route(x2p, x2s, g_ffn[li].reshape(1, D_MODEL), w_router)

    n_tiles = (x2p.shape[0] + x2s.shape[0]) // TM
    max_chunks = n_tiles * (2 * TM // CHUNK + N_EXPERTS)
    n_expert_tiles = max_chunks // TILE_CHUNKS + N_EXPERTS
    first_tile, tiles_e, src, dst, n_active = _chunk_tables(nch[:, 0, :N_EXPERTS], n_expert_tiles)
    ys = _experts(first_tile, tiles_e, src, dst, n_active, xs, w_exp_gate[li], w_exp_up[li], w_exp_down[li])
    yp, ysm = _combine(x2p, x2s, route, g_final.reshape(1, D_MODEL), ys)

    sd = state_ssm_re.dtype
    cd = cache_pool.dtype
    y_prompt = yp.reshape(n_pb, seq_p, D_MODEL)
    y_sample = ysm.reshape(n_sb, seq_s, D_MODEL)
    re_p = stp_re.reshape(1, n_pb, SSM_GROUPS, SSM_STATE).astype(sd)
    im_p = stp_im.reshape(1, n_pb, SSM_GROUPS, SSM_STATE).astype(sd)
    hist_p = histp[:, ::N_CHAIN, :][:, 1:, :].reshape(1, n_pb, POOL_HIST, POOL_WIDTH).astype(cd)
    re_s = sts_re.reshape(1, n_sb, SSM_GROUPS, SSM_STATE).astype(sd)
    im_s = sts_im.reshape(1, n_sb, SSM_GROUPS, SSM_STATE).astype(sd)
    hist_s = hists.reshape(n_stiles, 16, N_CHAIN, POOL_WIDTH).transpose(0, 2, 1, 3).reshape(
        n_sb, 16, POOL_WIDTH)[:, 1:, :].reshape(1, n_sb, POOL_HIST, POOL_WIDTH).astype(cd)
    return (y_prompt, y_sample, re_p, im_p, hist_p, re_s, im_s, hist_s)
```

```python
import functools
import math

import jax
import jax.numpy as jnp
from jax import lax
from jax.experimental import pallas as pl
from jax.experimental.pallas import tpu as pltpu

F32 = jnp.float32
BF16 = jnp.bfloat16

D_MODEL = 1024
SSM_WIDTH = 512
SSM_GROUPS = 32
SSM_GROUP = 16
SSM_STATE = 64
N_FLAT = SSM_GROUPS * SSM_STATE
HALF_FLAT = N_FLAT // 2
POOL_WIDTH = 512
POOL_WINDOWS = (2, 4, 8, 16)
POOL_GROUP_IN = 128
POOL_GROUP_OUT = 256
POOL_HIST = 15
N_EXPERTS = 32
EXPERTS_PER_GROUP = 8
N_EXPERT_GROUPS = 4
D_EXPERT = 512
EPS = 1e-6
PAST_LEN = 1024

TM = 256
N_CHAIN = 8
CHAIN_LEN = TM // N_CHAIN
HIST_ROWS = 16 * N_CHAIN

CHUNK = 8
TILE_CHUNKS = TM // CHUNK
CAP_CHUNKS = 96
CAP_ROWS = CAP_CHUNKS * CHUNK
MAX_USED_CHUNKS = 2 * TM // CHUNK + N_EXPERTS * (CHUNK - 1) // CHUNK
N_SPARE_SLOTS = CAP_CHUNKS - MAX_USED_CHUNKS - 1
ZERO_CHUNK = CAP_CHUNKS - 1
LANES = 128

V7X_VMEM_BYTES = 64 * 1024 * 1024
VMEM_LIMIT = V7X_VMEM_BYTES * 13 // 16
STEP_TILES = 4


def _rms(x, g):
    r = lax.rsqrt(jnp.mean(x * x, axis=-1, keepdims=True) + EPS)
    return x * r * g


def _sigmoid(x):
    return 0.5 * jnp.tanh(0.5 * x) + 0.5


def _gelu_tanh(x):
    c = math.sqrt(2.0 / math.pi)
    return x * (0.5 * (1.0 + jnp.tanh(c * (x + 0.044715 * (x * x * x)))))


def _disc_kernel(lre_ref, lim_ref, ldt_ref, bre_ref, bim_ref, cre_ref, cim_ref,
                 tab_ref, wb_ref, wcre_ref, wcim_ref):
    lam_re = jnp.minimum(lre_ref[...], -1e-4)
    lam_im = lim_ref[...]
    dt = jnp.exp(ldt_ref[...])
    mag = jnp.exp(lam_re * dt)
    ang = lam_im * dt
    a_re = mag * jnp.cos(ang)
    a_im = mag * jnp.sin(ang)
    num_re = a_re - 1.0
    num_im = a_im
    den = lam_re * lam_re + lam_im * lam_im
    k_re = (num_re * lam_re + num_im * lam_im) / den
    k_im = (num_im * lam_re - num_re * lam_im) / den
    pr, pi = a_re, a_im
    for _ in range(int(math.log2(CHAIN_LEN))):
        pr, pi = pr * pr - pi * pi, 2.0 * pr * pi
    wb_ref[...] = jnp.zeros_like(wb_ref)
    wcre_ref[...] = jnp.zeros_like(wcre_ref)
    wcim_ref[...] = jnp.zeros_like(wcim_ref)
    groups_per_half = SSM_GROUPS // 2
    for g in range(SSM_GROUPS):
        h, gl = divmod(g, groups_per_half)
        q0, n0 = gl * SSM_GROUP, gl * SSM_STATE
        br = bre_ref[g].T
        bi = bim_ref[g].T
        kr, ki = k_re[g:g + 1, :], k_im[g:g + 1, :]
        wb_ref[h, q0:q0 + SSM_GROUP, n0:n0 + SSM_STATE] = (kr * br - ki * bi).astype(BF16)
        wb_ref[h, q0:q0 + SSM_GROUP, HALF_FLAT + n0:HALF_FLAT + n0 + SSM_STATE] = (kr * bi + ki * br).astype(BF16)
        wcre_ref[h, n0:n0 + SSM_STATE, q0:q0 + SSM_GROUP] = cre_ref[g].T.astype(BF16)
        wcim_ref[h, n0:n0 + SSM_STATE, q0:q0 + SSM_GROUP] = cim_ref[g].T.astype(BF16)
        for k, v in enumerate((a_re, a_im, pr, pi)):
            tab_ref[k:k + 1, g * SSM_STATE:(g + 1) * SSM_STATE] = v[g:g + 1, :]


def _discretise(a_re, a_im, log_dt, b_re, b_im, c_re, c_im):
    half_w = SSM_WIDTH // 2
    return pl.pallas_call(
        _disc_kernel,
        out_shape=[jax.ShapeDtypeStruct((4, N_FLAT), F32),
                   jax.ShapeDtypeStruct((2, half_w, 2 * HALF_FLAT), BF16),
                   jax.ShapeDtypeStruct((2, HALF_FLAT, half_w), BF16),
                   jax.ShapeDtypeStruct((2, HALF_FLAT, half_w), BF16)],
        name="s5_discretise",
    )(a_re, a_im, log_dt.reshape(SSM_GROUPS, 1), b_re, b_im, c_re, c_im)


def _scan_half(hbuf, h, ar, ai, init_re, init_im, store):
    cre = h * N_FLAT
    cim = cre + HALF_FLAT
    hr, hi = init_re, init_im
    for t in range(CHAIN_LEN):
        rows = pl.ds(N_CHAIN * t, N_CHAIN)
        br = hbuf[rows, cre:cre + HALF_FLAT]
        bi = hbuf[rows, cim:cim + HALF_FLAT]
        nr = ar * hr - ai * hi + br
        ni = ar * hi + ai * hr + bi
        if store:
            hbuf[rows, cre:cre + HALF_FLAT] = nr
            hbuf[rows, cim:cim + HALF_FLAT] = ni
        hr, hi = nr, ni
    return hr, hi


N_MIXER_WEIGHTS = 11
N_PAR = 2


def _mixer_kernel(is_prompt, tiles_per_stream, *refs):
    n_in = 1 if is_prompt else 4
    ins = refs[:n_in]
    (gmix, win, wb, wcre, wcim, dsk, wglu, poolw, pscale, wout, atab) = refs[n_in:n_in + N_MIXER_WEIGHTS]
    x2_ref, stre_ref, stim_ref, hist_ref = refs[n_in + N_MIXER_WEIGHTS:n_in + N_MIXER_WEIGHTS + 4]
    scratch = refs[n_in + N_MIXER_WEIGHTS + 4:]
    per = len(scratch) // N_PAR
    lanes = [scratch[p * per:(p + 1) * per] for p in range(N_PAR)]
    x_ref = ins[0]

    tile_in_stream = pl.program_id(0) % tiles_per_stream
    n_lane_blocks = D_MODEL // LANES
    half_w = SSM_WIDTH // 2
    ar_full = atab[0:1, :]
    ai_full = atab[1:2, :]

    if is_prompt:
        @pl.when(pl.program_id(0) == 0)
        def _():
            for p in range(N_PAR):
                cre_s, cim_s, pcarry = lanes[p][8], lanes[p][9], lanes[p][10]
                cre_s[...] = jnp.zeros_like(cre_s)
                cim_s[...] = jnp.zeros_like(cim_s)
                pcarry[...] = jnp.zeros_like(pcarry)

    def a_half(h):
        f0 = h * HALF_FLAT
        return (jnp.broadcast_to(ar_full[:, f0:f0 + HALF_FLAT], (N_CHAIN, HALF_FLAT)),
                jnp.broadcast_to(ai_full[:, f0:f0 + HALF_FLAT], (N_CHAIN, HALF_FLAT)))

    def project(p):
        xperm, hbuf = lanes[p][0], lanes[p][1]
        for j in range(N_CHAIN):
            for cb in range(n_lane_blocks):
                xperm[cb, pl.ds(j, CHAIN_LEN, stride=N_CHAIN), :] = x_ref[
                    p, CHAIN_LEN * j:CHAIN_LEN * (j + 1), cb * LANES:(cb + 1) * LANES]
        x = jnp.concatenate([xperm[cb] for cb in range(n_lane_blocks)], axis=1)
        xn = _rms(x, gmix[...]).astype(BF16)
        proj = jnp.dot(xn, win[...], preferred_element_type=F32)
        ub = proj[:, :SSM_WIDTH].astype(BF16)
        for h in range(2):
            hbuf[:, h * N_FLAT:(h + 1) * N_FLAT] = jnp.dot(
                ub[:, h * half_w:(h + 1) * half_w], wb[h], preferred_element_type=F32)
        return x, proj

    def recur(p):
        hbuf = lanes[p][1]
        if is_prompt:
            fre, fim, hre, him, cre_s, cim_s = lanes[p][4:10]
            zeros = jnp.zeros((N_CHAIN, HALF_FLAT), F32)
            for h in range(2):
                f0 = h * HALF_FLAT
                ar, ai = a_half(h)
                lr, li = _scan_half(hbuf, h, ar, ai, zeros, zeros, store=False)
                fre[:, f0:f0 + HALF_FLAT] = lr
                fim[:, f0:f0 + HALF_FLAT] = li
            fresh = tile_in_stream == 0
            hre[0:1, :] = jnp.where(fresh, 0.0, cre_s[...])
            him[0:1, :] = jnp.where(fresh, 0.0, cim_s[...])
            p_re = atab[2:3, :]
            p_im = atab[3:4, :]
            for j in range(N_CHAIN - 1):
                sr = hre[j:j + 1, :]
                si = him[j:j + 1, :]
                hre[j + 1:j + 2, :] = fre[j:j + 1, :] + p_re * sr - p_im * si
                him[j + 1:j + 2, :] = fim[j:j + 1, :] + p_re * si + p_im * sr
            init_re = hre[...]
            init_im = him[...]
        else:
            init_re = ins[1][p]
            init_im = ins[2][p]
        fin_re = []
        fin_im = []
        for h in range(2):
            f0 = h * HALF_FLAT
            ar, ai = a_half(h)
            er, ei = _scan_half(hbuf, h, ar, ai, init_re[:, f0:f0 + HALF_FLAT],
                                init_im[:, f0:f0 + HALF_FLAT], store=True)
            fin_re.append(er)
            fin_im.append(ei)
        end_re = jnp.concatenate(fin_re, axis=1)
        end_im = jnp.concatenate(fin_im, axis=1)
        if is_prompt:
            cre_s[...] = end_re[N_CHAIN - 1:N_CHAIN, :]
            cim_s[...] = end_im[N_CHAIN - 1:N_CHAIN, :]
            stre_ref[p] = end_re[N_CHAIN - 1:N_CHAIN, :]
            stim_ref[p] = end_im[N_CHAIN - 1:N_CHAIN, :]
        else:
            stre_ref[p] = end_re
            stim_ref[p] = end_im

    def finish(p, x, proj):
        hbuf, res, xpbuf = lanes[p][1], lanes[p][2], lanes[p][3]
        u_s = proj[:, :SSM_WIDTH]
        u_p = proj[:, SSM_WIDTH:SSM_WIDTH + POOL_WIDTH]
        gate_s = proj[:, SSM_WIDTH + POOL_WIDTH:SSM_WIDTH + POOL_WIDTH + D_MODEL]
        gate_p = proj[:, SSM_WIDTH + POOL_WIDTH + D_MODEL:]
        ys = []
        for h in range(2):
            c0 = h * N_FLAT
            h_re = hbuf[:, c0:c0 + HALF_FLAT].astype(BF16)
            h_im = hbuf[:, c0 + HALF_FLAT:c0 + N_FLAT].astype(BF16)
            ys.append(jnp.dot(h_re, wcre[h], preferred_element_type=F32)
                      - jnp.dot(h_im, wcim[h], preferred_element_type=F32))
        y = jnp.concatenate(ys, axis=1) + dsk[...] * u_s
        g = _gelu_tanh(y).astype(BF16)
        glu = jnp.dot(g, wglu[...], preferred_element_type=F32)
        o_s = glu[:, :D_MODEL] * _sigmoid(glu[:, D_MODEL:])

        xpbuf[HIST_ROWS:HIST_ROWS + TM, :] = u_p
        tail = u_p[TM - HIST_ROWS:, :]
        row = lax.broadcasted_iota(jnp.int32, (TM, 1), 0)
        if is_prompt:
            pcarry = lanes[p][10]
            first_chain = (lax.broadcasted_iota(jnp.int32, (HIST_ROWS, POOL_WIDTH), 0) % N_CHAIN) == 0
            carried = jnp.where(tile_in_stream == 0, 0.0, pcarry[...])
            xpbuf[0:HIST_ROWS, :] = jnp.where(first_chain, carried, pltpu.roll(tail, 1, 0))
            new_carry = pltpu.roll(tail, HIST_ROWS - (N_CHAIN - 1), 0)
            pcarry[...] = new_carry
            hist_ref[p] = new_carry
            pos1 = tile_in_stream * TM + CHAIN_LEN * (row % N_CHAIN) + row // N_CHAIN + 1
        else:
            xpbuf[0:HIST_ROWS, :] = ins[3][p]
            hist_ref[p] = tail
            pos1 = PAST_LEN + row // N_CHAIN + 1

        o_ps = []
        for gi, w in enumerate(POOL_WINDOWS):
            c0 = gi * POOL_GROUP_IN
            acc = xpbuf[HIST_ROWS:HIST_ROWS + TM, c0:c0 + POOL_GROUP_IN]
            for k in range(1, w):
                acc = acc + xpbuf[HIST_ROWS - N_CHAIN * k:HIST_ROWS - N_CHAIN * k + TM, c0:c0 + POOL_GROUP_IN]
            cnt = jnp.minimum(w, pos1).astype(F32)
            pooled = acc / cnt
            z = (pooled - u_p[:, c0:c0 + POOL_GROUP_IN]).astype(BF16)
            o_ps.append(jnp.dot(z, poolw[gi], preferred_element_type=F32))
        o_p = jnp.concatenate(o_ps, axis=1) * pscale[...]

        merged = (_sigmoid(gate_s) * o_s + _sigmoid(gate_p) * o_p).astype(BF16)
        x2 = x + jnp.dot(merged, wout[...], preferred_element_type=F32)
        for cb in range(n_lane_blocks):
            res[cb] = x2[:, cb * LANES:(cb + 1) * LANES]
        for j in range(N_CHAIN):
            for cb in range(n_lane_blocks):
                x2_ref[p, CHAIN_LEN * j:CHAIN_LEN * (j + 1), cb * LANES:(cb + 1) * LANES] = res[
                    cb, pl.ds(j, CHAIN_LEN, stride=N_CHAIN), :]

    projected = [None] * N_PAR
    for k in range(N_PAR + 2):
        if 0 <= k - 2 < N_PAR:
            finish(k - 2, *projected[k - 2])
        if 0 <= k - 1 < N_PAR:
            recur(k - 1)
        if k < N_PAR:
            projected[k] = project(k)


def _const_spec(shape):
    nd = len(shape)
    return pl.BlockSpec(shape, lambda i, _nd=nd: (0,) * _nd)


def _mixer_weight_specs():
    return [
        _const_spec((1, D_MODEL)),
        _const_spec((D_MODEL, 3 * D_MODEL)),
        _const_spec((2, SSM_WIDTH // 2, N_FLAT)),
        _const_spec((2, HALF_FLAT, SSM_WIDTH // 2)),
        _const_spec((2, HALF_FLAT, SSM_WIDTH // 2)),
        _const_spec((1, SSM_WIDTH)),
        _const_spec((SSM_WIDTH, 2 * D_MODEL)),
        _const_spec((len(POOL_WINDOWS), POOL_GROUP_IN, POOL_GROUP_OUT)),
        _const_spec((1, D_MODEL)),
        _const_spec((D_MODEL, D_MODEL)),
        _const_spec((4, N_FLAT)),
    ]


def _mixer_common_scratch():
    return [
        pltpu.VMEM((D_MODEL // LANES, TM, LANES), F32),
        pltpu.VMEM((TM, 2 * N_FLAT), F32),
        pltpu.VMEM((D_MODEL // LANES, TM, LANES), F32),
        pltpu.VMEM((HIST_ROWS + TM, POOL_WIDTH), F32),
    ]


def _mixer_prompt(x, weights):
    n_streams, seq, _ = x.shape
    assert n_streams % N_PAR == 0 and seq % TM == 0
    tiles_per_stream = seq // TM
    blk = lambda shape: pl.BlockSpec(shape, lambda i: (i // tiles_per_stream, 0, 0))
    row_spec = pl.BlockSpec((N_PAR, TM, D_MODEL), lambda i: (i // tiles_per_stream, i % tiles_per_stream, 0))
    lane_scratch = _mixer_common_scratch() + [
        pltpu.VMEM((N_CHAIN, N_FLAT), F32), pltpu.VMEM((N_CHAIN, N_FLAT), F32),
        pltpu.VMEM((N_CHAIN, N_FLAT), F32), pltpu.VMEM((N_CHAIN, N_FLAT), F32),
        pltpu.VMEM((1, N_FLAT), F32), pltpu.VMEM((1, N_FLAT), F32),
        pltpu.VMEM((HIST_ROWS, POOL_WIDTH), F32),
    ]
    return pl.pallas_call(
        functools.partial(_mixer_kernel, True, tiles_per_stream),
        grid=(n_streams // N_PAR * tiles_per_stream,),
        in_specs=[row_spec] + _mixer_weight_specs(),
        out_specs=[row_spec, blk((N_PAR, 1, N_FLAT)), blk((N_PAR, 1, N_FLAT)),
                   blk((N_PAR, HIST_ROWS, POOL_WIDTH))],
        out_shape=[jax.ShapeDtypeStruct((n_streams, seq, D_MODEL), F32),
                   jax.ShapeDtypeStruct((n_streams, 1, N_FLAT), F32),
                   jax.ShapeDtypeStruct((n_streams, 1, N_FLAT), F32),
                   jax.ShapeDtypeStruct((n_streams, HIST_ROWS, POOL_WIDTH), F32)],
        scratch_shapes=lane_scratch * N_PAR,
        compiler_params=pltpu.CompilerParams(dimension_semantics=("arbitrary",), vmem_limit_bytes=VMEM_LIMIT),
        name="mixer_prompt",
    )(x, *weights)


def _mixer_sample(x, h0_re, h0_im, cache_t, weights):
    n_streams, seq, _ = x.shape
    assert seq == CHAIN_LEN and n_streams % (N_CHAIN * N_PAR) == 0
    n_tiles = n_streams // N_CHAIN
    blk = lambda shape: pl.BlockSpec(shape, lambda i: (i, 0, 0))
    row_spec = blk((N_PAR, TM, D_MODEL))
    st_spec = blk((N_PAR, N_CHAIN, N_FLAT))
    hist_spec = blk((N_PAR, HIST_ROWS, POOL_WIDTH))
    tiles = lambda v: v.reshape((n_tiles, -1) + v.shape[-1:])
    return pl.pallas_call(
        functools.partial(_mixer_kernel, False, 1),
        grid=(n_tiles // N_PAR,),
        in_specs=[row_spec, st_spec, st_spec, hist_spec] + _mixer_weight_specs(),
        out_specs=[row_spec, st_spec, st_spec, hist_spec],
        out_shape=[jax.ShapeDtypeStruct((n_tiles, TM, D_MODEL), F32),
                   jax.ShapeDtypeStruct((n_tiles, N_CHAIN, N_FLAT), F32),
                   jax.ShapeDtypeStruct((n_tiles, N_CHAIN, N_FLAT), F32),
                   jax.ShapeDtypeStruct((n_tiles, HIST_ROWS, POOL_WIDTH), F32)],
        scratch_shapes=_mixer_common_scratch() * N_PAR,
        compiler_params=pltpu.CompilerParams(dimension_semantics=("arbitrary",), vmem_limit_bytes=VMEM_LIMIT),
        name="mixer_sample",
    )(tiles(x), tiles(h0_re), tiles(h0_im), cache_t, *weights)


def _to_chunks(lo, hi):
    n = lo.shape[0] // CHUNK
    half = D_MODEL // 2
    both = jnp.concatenate([lo.reshape(n, CHUNK, half), hi.reshape(n, CHUNK, half)], axis=1)
    return both.astype(BF16)


def _from_chunks(blk):
    n = blk.shape[0]
    half = D_MODEL // 2
    f = blk.astype(F32)
    lo = f[:, :CHUNK, :].reshape(n * CHUNK, half).astype(BF16)
    hi = f[:, CHUNK:, :].reshape(n * CHUNK, half).astype(BF16)
    return lo, hi


def _route_kernel(n_prompt_steps, x2p_ref, x2s_ref, g_ref, wr_ref, ltri_ref, utri_ref,
                  xs_ref, route_ref, nch_ref):
    is_prompt = pl.program_id(0) < n_prompt_steps
    tiles = range(STEP_TILES)
    each = lambda fn, *cols: [fn(*(c[h] for c in cols)) for h in tiles]
    row_max = lambda v: jnp.max(v, axis=1, keepdims=True)
    row_min = lambda v: jnp.min(v, axis=1, keepdims=True)
    row_sum = lambda v: jnp.sum(v, axis=1, keepdims=True)

    lane = lax.broadcasted_iota(jnp.int32, (TM, LANES), 1)
    lane_f = lane.astype(F32)
    big = jnp.float32(1 << 20)
    neg = jnp.float32(-jnp.inf)
    gmask = lane < N_EXPERT_GROUPS
    eid = lane - N_EXPERT_GROUPS
    lane_grp = (eid >> 3).astype(F32)
    is_expert = (eid >= 0) & (eid < N_EXPERTS)

    xn = [_rms(jnp.where(is_prompt, x2p_ref[pl.ds(h * TM, TM), :], x2s_ref[pl.ds(h * TM, TM), :]),
               g_ref[...]).astype(BF16) for h in tiles]
    logits = each(lambda v: jnp.dot(v, wr_ref[...], preferred_element_type=F32), xn)
    m = each(lambda lg: row_max(jnp.where(gmask, lg, neg)), logits)
    grp = each(lambda lg, mm: row_min(jnp.where(gmask & (lg == mm), lane_f, big)), logits, m)
    wg = each(lambda lg, mm: 1.0 / row_sum(jnp.where(gmask, jnp.exp(lg - mm), 0.0)), logits, m)
    emask = each(lambda g: is_expert & (lane_grp == g), grp)
    v1 = each(lambda lg, em: row_max(jnp.where(em, lg, neg)), logits, emask)
    i1 = each(lambda lg, em, v: row_min(jnp.where(em & (lg == v), lane_f, big)), logits, emask, v1)
    emask2 = each(lambda em, i: em & (lane_f != i), emask, i1)
    v2 = each(lambda lg, em: row_max(jnp.where(em, lg, neg)), logits, emask2)
    i2 = each(lambda lg, em, v: row_min(jnp.where(em & (lg == v), lane_f, big)), logits, emask2, v2)
    e21 = each(lambda a, b: jnp.exp(b - a), v1, v2)
    w1 = each(lambda g, e: g / (1.0 + e), wg, e21)
    w2 = each(lambda g, e: g * e / (1.0 + e), wg, e21)

    a1 = each(lambda i: lane_f == (i - N_EXPERT_GROUPS), i1)
    a2 = each(lambda i: lane_f == (i - N_EXPERT_GROUPS), i2)
    a = each(lambda p, q: (p | q).astype(F32), a1, a2)
    before = each(lambda v: jnp.dot(ltri_ref[...], v.astype(BF16), preferred_element_type=F32), a)
    cnt = each(lambda v: jnp.sum(v, axis=0, keepdims=True), a)
    nch16 = each(lambda c: jnp.broadcast_to(jnp.floor((c + (CHUNK - 1)) * (1.0 / CHUNK)), (16, LANES)), cnt)
    start = each(lambda n: jnp.dot(n.astype(BF16), utri_ref[...], preferred_element_type=F32), nch16)
    slot = each(lambda bf, st: bf + CHUNK * st[0:1, :], before, start)
    d1 = each(lambda p, sl: row_sum(jnp.where(p, sl, 0.0)), a1, slot)
    d2 = each(lambda p, sl: row_sum(jnp.where(p, sl, 0.0)), a2, slot)
    route = each(lambda p, q, u, v: jnp.where(lane == 0, p, jnp.where(lane == 1, q, jnp.where(
        lane == 2, u, jnp.where(lane == 3, v, 0.0)))), d1, d2, w1, w2)

    dest = lax.broadcasted_iota(jnp.int32, (CAP_ROWS, TM), 0)
    half = D_MODEL // 2
    for h in tiles:
        route_ref[pl.ds(h * TM, TM), :] = route[h]
        nch_ref[h] = nch16[h][0:8, :].astype(jnp.int32)
        dt = jnp.transpose(jnp.where(lane < 2, route[h], 0.0)).astype(jnp.int32)
        perm = ((dest == dt[0:1, :]) | (dest == dt[1:2, :])).astype(F32).astype(BF16)
        lo = jnp.dot(perm, xn[h][:, :half], preferred_element_type=F32)
        hi = jnp.dot(perm, xn[h][:, half:], preferred_element_type=F32)
        xs_ref[pl.ds(h * CAP_CHUNKS, CAP_CHUNKS)] = _to_chunks(lo, hi)


def _route(x2p, x2s, g_ffn, w_router):
    n_prompt_tiles = x2p.shape[0] // TM
    n_tiles = n_prompt_tiles + x2s.shape[0] // TM
    assert n_prompt_tiles % STEP_TILES == 0 and n_tiles % STEP_TILES == 0
    n_prompt_steps = n_prompt_tiles // STEP_TILES
    rows = STEP_TILES * TM
    r = jnp.arange(TM)
    ltri = (r[None, :] < r[:, None]).astype(BF16)
    e = jnp.arange(LANES)
    utri = (e[:, None] < e[None, :]).astype(BF16)
    return pl.pallas_call(
        functools.partial(_route_kernel, n_prompt_steps),
        grid=(n_tiles // STEP_TILES,),
        in_specs=[pl.BlockSpec((rows, D_MODEL), lambda i: (jnp.minimum(i, n_prompt_steps - 1), 0)),
                  pl.BlockSpec((rows, D_MODEL), lambda i: (jnp.maximum(i - n_prompt_steps, 0), 0)),
                  _const_spec((1, D_MODEL)), _const_spec((D_MODEL, LANES)),
                  _const_spec((TM, TM)), _const_spec((LANES, LANES))],
        out_specs=[pl.BlockSpec((STEP_TILES * CAP_CHUNKS, 2 * CHUNK, D_MODEL // 2), lambda i: (i, 0, 0)),
                   pl.BlockSpec((rows, LANES), lambda i: (i, 0)),
                   pl.BlockSpec((STEP_TILES, 8, LANES), lambda i: (i, 0, 0))],
        out_shape=[jax.ShapeDtypeStruct((n_tiles * CAP_CHUNKS, 2 * CHUNK, D_MODEL // 2), BF16),
                   jax.ShapeDtypeStruct((n_tiles * TM, LANES), F32),
                   jax.ShapeDtypeStruct((n_tiles, 8, LANES), jnp.int32)],
        compiler_params=pltpu.CompilerParams(dimension_semantics=("arbitrary",), vmem_limit_bytes=VMEM_LIMIT),
        name="route_sort",
    )(x2p, x2s, g_ffn, w_router, ltri, utri)


def _spare_chunk(k):
    return (1 + k // N_SPARE_SLOTS) * CAP_CHUNKS + MAX_USED_CHUNKS + k % N_SPARE_SLOTS


def _chunk_tables(nch, n_expert_tiles):
    n_tiles = nch.shape[0]
    i32 = jnp.int32
    start = jnp.cumsum(nch, axis=1) - nch
    off = jnp.cumsum(nch, axis=0) - nch
    per_expert = jnp.sum(nch, axis=0)
    tiles_e = (per_expert + TILE_CHUNKS - 1) // TILE_CHUNKS
    cum_tiles = jnp.cumsum(tiles_e)
    first_tile = cum_tiles - tiles_e
    n_active = cum_tiles[-1].astype(i32)
    t = jnp.arange(n_expert_tiles, dtype=i32)
    te = jnp.minimum(jnp.sum((t[:, None] >= cum_tiles[None, :]).astype(i32), axis=1), N_EXPERTS - 1)
    onehot = te[:, None] == jnp.arange(N_EXPERTS, dtype=i32)[None, :]
    pick = lambda tab: jnp.sum(jnp.where(onehot[:, :, None], tab.T[None, :, :], 0), axis=1)
    off_t, nch_t, start_t = pick(off), pick(nch), pick(start)
    k = t - jnp.sum(jnp.where(onehot, first_tile[None, :], 0), axis=1)
    q = (TILE_CHUNKS * k)[:, None] + jnp.arange(TILE_CHUNKS, dtype=i32)[None, :]
    in_run = (off_t[:, None, :] <= q[:, :, None]) & (q[:, :, None] < (off_t + nch_t)[:, None, :])
    run_src = (jnp.arange(n_tiles, dtype=i32) * CAP_CHUNKS)[None, :] + start_t - off_t
    src = jnp.sum(jnp.where(in_run, run_src[:, None, :], 0), axis=-1) + q
    valid = jnp.any(in_run, axis=-1) & (t < n_active)[:, None]
    src = jnp.where(valid, src, ZERO_CHUNK).astype(i32)
    spare = _spare_chunk((t % N_RING)[:, None] * TILE_CHUNKS + jnp.arange(TILE_CHUNKS, dtype=i32)[None, :])
    dst = jnp.where(valid, src, spare).astype(i32)
    dst = jnp.concatenate([spare[1:N_RING], dst], axis=0)
    return first_tile.astype(i32), tiles_e.astype(i32), src.reshape(-1), dst.reshape(-1), n_active.reshape(1)


N_RING = 5


def _expert_kernel(first_ref, ntile_ref, src_ref, dst_ref, nact_ref, xs_hbm, wg_ref, wu_ref, wd_ref, ys_hbm,
                   *scratch):
    xbufs = scratch[:N_RING]
    obufs = scratch[N_RING:2 * N_RING]
    gsem, ssem, wg16, wu16, wd16 = scratch[2 * N_RING:]
    e = pl.program_id(0)
    n_active = nact_ref[0]
    ahead = N_RING - 1

    def gather_copy(tile, slot, c):
        return pltpu.make_async_copy(xs_hbm.at[src_ref[tile * TILE_CHUNKS + c]], xbufs[slot].at[c],
                                     gsem.at[slot])

    def scatter_copy(tile, slot, c):
        return pltpu.make_async_copy(obufs[slot].at[c], ys_hbm.at[dst_ref[(tile + ahead) * TILE_CHUNKS + c]],
                                     ssem.at[slot])

    def start_all(copy, tile, slot):
        for c in range(TILE_CHUNKS):
            copy(tile, slot, c).start()

    def wait_all(copy, tile, slot):
        for c in range(TILE_CHUNKS):
            copy(tile, slot, c).wait()

    @pl.when(e == 0)
    def _():
        for v in range(ahead):
            start_all(gather_copy, v, v)
        for u in range(-ahead, 0):
            obufs[u % N_RING][...] = jnp.zeros_like(obufs[u % N_RING])
        for u in range(-ahead, -1):
            start_all(scatter_copy, u, u % N_RING)

    wg16[...] = wg_ref[0].astype(BF16)
    wu16[...] = wu_ref[0].astype(BF16)
    wd16[...] = wd_ref[0].astype(BF16)

    def do_tile(tile, slot):
        nxt = (slot + ahead) % N_RING
        old = (slot + 1) % N_RING
        wait_all(gather_copy, tile, slot)
        lo, hi = _from_chunks(xbufs[slot][...])
        x = jnp.concatenate([lo, hi], axis=1)
        gate = jnp.dot(x, wg16[...], preferred_element_type=F32)
        start_all(scatter_copy, tile - 1, nxt)
        up = jnp.dot(x, wu16[...], preferred_element_type=F32)
        hmid = (gate * _sigmoid(gate) * up).astype(BF16)
        start_all(gather_copy, tile + ahead, nxt)
        y = jnp.dot(hmid, wd16[...], preferred_element_type=F32)
        half = D_MODEL // 2
        wait_all(scatter_copy, tile - ahead, old)
        obufs[slot][...] = _to_chunks(y[:, :half], y[:, half:])

    def tile_body(k, carry):
        tile = first_ref[e] + k
        for slot in range(N_RING):
            @pl.when(tile % N_RING == slot)
            def _():
                do_tile(tile, slot)
        return carry

    lax.fori_loop(0, ntile_ref[e], tile_body, 0)

    @pl.when(e == pl.num_programs(0) - 1)
    def _():
        last = n_active - 1
        for slot in range(N_RING):
            @pl.when(last % N_RING == slot)
            def _():
                for d in range(N_RING - 2, 0, -1):
                    wait_all(scatter_copy, last - d, (slot - d) % N_RING)
                start_all(scatter_copy, last, slot)
                wait_all(scatter_copy, last, slot)
                for d in range(1, N_RING):
                    wait_all(gather_copy, last + d, (slot + d) % N_RING)


def _experts(first_tile, tiles_e, src, dst, n_active, xs, w_gate, w_up, w_down):
    chunk_shape = (2 * CHUNK, D_MODEL // 2)
    assert _spare_chunk(N_RING * TILE_CHUNKS - 1) < xs.shape[0]
    tile_buf = pltpu.VMEM((TILE_CHUNKS,) + chunk_shape, BF16)
    grid_spec = pltpu.PrefetchScalarGridSpec(
        num_scalar_prefetch=5,
        grid=(N_EXPERTS,),
        in_specs=[pl.BlockSpec(memory_space=pl.ANY),
                  pl.BlockSpec((1, D_MODEL, D_EXPERT), lambda e, *_: (e, 0, 0)),
                  pl.BlockSpec((1, D_MODEL, D_EXPERT), lambda e, *_: (e, 0, 0)),
                  pl.BlockSpec((1, D_EXPERT, D_MODEL), lambda e, *_: (e, 0, 0))],
        out_specs=pl.BlockSpec(memory_space=pl.ANY),
        scratch_shapes=[tile_buf] * (2 * N_RING) + [
                        pltpu.SemaphoreType.DMA((N_RING,)),
                        pltpu.SemaphoreType.DMA((N_RING,)),
                        pltpu.VMEM((D_MODEL, D_EXPERT), BF16),
                        pltpu.VMEM((D_MODEL, D_EXPERT), BF16),
                        pltpu.VMEM((D_EXPERT, D_MODEL), BF16)],
    )
    return pl.pallas_call(
        _expert_kernel,
        grid_spec=grid_spec,
        out_shape=jax.ShapeDtypeStruct(xs.shape, xs.dtype),
        input_output_aliases={5: 0},
        compiler_params=pltpu.CompilerParams(dimension_semantics=("arbitrary",), vmem_limit_bytes=VMEM_LIMIT),
        name="expert_ffn",
    )(first_tile, tiles_e, src, dst, n_active, xs, w_gate, w_up, w_down)


def _combine_kernel(n_prompt_steps, x2p_ref, x2s_ref, route_ref, gfin_ref, ys_ref, outp_ref, outs_ref):
    i = pl.program_id(0)
    is_prompt = i < n_prompt_steps
    outs = []
    for h in range(STEP_TILES):
        rows = pl.ds(h * TM, TM)
        lo, hi = _from_chunks(ys_ref[pl.ds(h * CAP_CHUNKS, CAP_CHUNKS)])
        route = route_ref[rows, :]
        d1 = route[:, 0:1].astype(jnp.int32)
        d2 = route[:, 1:2].astype(jnp.int32)
        w1 = route[:, 2:3]
        w2 = route[:, 3:4]
        dest = lax.broadcasted_iota(jnp.int32, (TM, CAP_ROWS), 1)
        sel = jnp.where(dest == d1, w1, jnp.where(dest == d2, w2, 0.0)).astype(BF16)
        moe = jnp.concatenate([jnp.dot(sel, lo, preferred_element_type=F32),
                               jnp.dot(sel, hi, preferred_element_type=F32)], axis=1)
        x = jnp.where(is_prompt, x2p_ref[rows, :], x2s_ref[rows, :])
        outs.append(_rms(x + moe, gfin_ref[...]))

    @pl.when(is_prompt)
    def _():
        for h in range(STEP_TILES):
            outp_ref[pl.ds(h * TM, TM), :] = outs[h]

    @pl.when(jnp.logical_not(is_prompt))
    def _():
        for h in range(STEP_TILES):
            outs_ref[pl.ds(h * TM, TM), :] = outs[h]


def _combine(x2p, x2s, route, g_final, ys):
    n_prompt_tiles = x2p.shape[0] // TM
    n_tiles = n_prompt_tiles + x2s.shape[0] // TM
    assert n_prompt_tiles % STEP_TILES == 0 and n_tiles % STEP_TILES == 0
    n_prompt_steps = n_prompt_tiles // STEP_TILES
    rows = STEP_TILES * TM
    p_spec = pl.BlockSpec((rows, D_MODEL), lambda i: (jnp.minimum(i, n_prompt_steps - 1), 0))
    s_spec = pl.BlockSpec((rows, D_MODEL), lambda i: (jnp.maximum(i - n_prompt_steps, 0), 0))
    return pl.pallas_call(
        functools.partial(_combine_kernel, n_prompt_steps),
        grid=(n_tiles // STEP_TILES,),
        in_specs=[p_spec, s_spec,
                  pl.BlockSpec((rows, LANES), lambda i: (i, 0)),
                  _const_spec((1, D_MODEL)),
                  pl.BlockSpec((STEP_TILES * CAP_CHUNKS, 2 * CHUNK, D_MODEL // 2), lambda i: (i, 0, 0))],
        out_specs=[p_spec, s_spec],
        out_shape=[jax.ShapeDtypeStruct(x2p.shape, F32), jax.ShapeDtypeStruct(x2s.shape, F32)],
        compiler_params=pltpu.CompilerParams(dimension_semantics=("arbitrary",), vmem_limit_bytes=VMEM_LIMIT),
        name="combine_norm",
    )(x2p, x2s, route, g_final, ys)


def kernel(x_prompt, x_sample, state_ssm_re, state_ssm_im, cache_pool, g_mix, w_in, ssm_a_re, ssm_a_im,
           ssm_log_dt, ssm_b_re, ssm_b_im, ssm_c_re, ssm_c_im, ssm_d, w_glu_a, w_glu_b, pool_w, pool_scale,
           w_out, g_ffn, w_router_group, w_router_expert, w_exp_gate, w_exp_up, w_exp_down, g_final):
    li = 0
    n_pb, seq_p, _ = x_prompt.shape
    n_sb, seq_s, _ = x_sample.shape

    a_tab, wb, wc_re, wc_im = _discretise(
        ssm_a_re[li], ssm_a_im[li], ssm_log_dt[li], ssm_b_re[li], ssm_b_im[li], ssm_c_re[li], ssm_c_im[li])
    weights = [
        g_mix[li].reshape(1, D_MODEL),
        w_in[li].astype(BF16),
        wb,
        wc_re,
        wc_im,
        ssm_d[li].reshape(1, SSM_WIDTH),
        jnp.concatenate([w_glu_a[li], w_glu_b[li]], axis=1).astype(BF16),
        pool_w[li].astype(BF16),
        pool_scale[li].reshape(1, D_MODEL),
        w_out[li].astype(BF16),
        a_tab,
    ]

    x2p, stp_re, stp_im, histp = _mixer_prompt(x_prompt, weights)
    x2p = x2p.reshape(n_pb * seq_p, D_MODEL)

    n_stiles = n_sb // N_CHAIN
    cache16 = jnp.pad(cache_pool[li], ((0, 0), (1, 0), (0, 0)))
    cache_t = cache16.reshape(n_stiles, N_CHAIN, 16, POOL_WIDTH).transpose(0, 2, 1, 3).reshape(
        n_stiles, HIST_ROWS, POOL_WIDTH)
    x2s, sts_re, sts_im, hists = _mixer_sample(
        x_sample, state_ssm_re[li].reshape(n_sb, N_FLAT), state_ssm_im[li].reshape(n_sb, N_FLAT),
        cache_t, weights)
    x2s = x2s.reshape(n_sb * seq_s, D_MODEL)

    w_router = jnp.concatenate(
        [w_router_group[li], w_router_expert[li].reshape(D_MODEL, N_EXPERTS),
         jnp.zeros((D_MODEL, LANES - N_EXPERT_GROUPS - N_EXPERTS), F32)], axis=1).astype(BF16)
    xs, route, nch = _route(x2p, x2s, g_ffn[li].reshape(1, D_MODEL), w_router)

    n_tiles = (x2p.shape[0] + x2s.shape[0]) // TM
    max_chunks = n_tiles * (2 * TM // CHUNK + N_EXPERTS)
    n_expert_tiles = max_chunks // TILE_CHUNKS + N_EXPERTS
    first_tile, tiles_e, src, dst, n_active = _chunk_tables(nch[:, 0, :N_EXPERTS], n_expert_tiles)
    ys = _experts(first_tile, tiles_e, src, dst, n_active, xs, w_exp_gate[li], w_exp_up[li], w_exp_down[li])
    yp, ysm = _combine(x2p, x2s, route, g_final.reshape(1, D_MODEL), ys)

    sd = state_ssm_re.dtype
    cd = cache_pool.dtype
    y_prompt = yp.reshape(n_pb, seq_p, D_MODEL)
    y_sample = ysm.reshape(n_sb, seq_s, D_MODEL)
    re_p = stp_re.reshape(1, n_pb, SSM_GROUPS, SSM_STATE).astype(sd)
    im_p = stp_im.reshape(1, n_pb, SSM_GROUPS, SSM_STATE).astype(sd)
    hist_p = histp[:, ::N_CHAIN, :][:, 1:, :].reshape(1, n_pb, POOL_HIST, POOL_WIDTH).astype(cd)
    re_s = sts_re.reshape(1, n_sb, SSM_GROUPS, SSM_STATE).astype(sd)
    im_s = sts_im.reshape(1, n_sb, SSM_GROUPS, SSM_STATE).astype(sd)
    hist_s = hists.reshape(n_stiles, 16, N_CHAIN, POOL_WIDTH).transpose(0, 2, 1, 3).reshape(
        n_sb, 16, POOL_WIDTH)[:, 1:, :].reshape(1, n_sb, POOL_HIST, POOL_WIDTH).astype(cd)
    return (y_prompt, y_sample, re_p, im_p, hist_p, re_s, im_s, hist_s)
```

```python
import functools
import math

import jax
import jax.numpy as jnp
from jax import lax
from jax.experimental import pallas as pl
from jax.experimental.pallas import tpu as pltpu

F32 = jnp.float32
BF16 = jnp.bfloat16

D_MODEL = 1024
SSM_WIDTH = 512
SSM_GROUPS = 32
SSM_GROUP = 16
SSM_STATE = 64
N_FLAT = SSM_GROUPS * SSM_STATE
HALF_FLAT = N_FLAT // 2
POOL_WIDTH = 512
POOL_WINDOWS = (2, 4, 8, 16)
POOL_GROUP_IN = 128
POOL_GROUP_OUT = 256
POOL_HIST = 15
N_EXPERTS = 32
EXPERTS_PER_GROUP = 8
N_EXPERT_GROUPS = 4
D_EXPERT = 512
EPS = 1e-6
PAST_LEN = 1024

TM = 256
N_CHAIN = 8
CHAIN_LEN = TM // N_CHAIN
HIST_ROWS = 16 * N_CHAIN

CHUNK = 8
TILE_CHUNKS = TM // CHUNK
CAP_CHUNKS = 96
CAP_ROWS = CAP_CHUNKS * CHUNK
MAX_USED_CHUNKS = 2 * TM // CHUNK + N_EXPERTS * (CHUNK - 1) // CHUNK
N_SPARE_SLOTS = CAP_CHUNKS - MAX_USED_CHUNKS - 1
ZERO_CHUNK = CAP_CHUNKS - 1
LANES = 128

V7X_VMEM_BYTES = 64 * 1024 * 1024
VMEM_LIMIT = V7X_VMEM_BYTES * 13 // 16
STEP_TILES = 4


def _rms(x, g):
    r = lax.rsqrt(jnp.mean(x * x, axis=-1, keepdims=True) + EPS)
    return x * r * g


def _sigmoid(x):
    return 0.5 * jnp.tanh(0.5 * x) + 0.5


def _gelu_tanh(x):
    c = math.sqrt(2.0 / math.pi)
    return x * (0.5 * (1.0 + jnp.tanh(c * (x + 0.044715 * (x * x * x)))))


def _disc_kernel(lre_ref, lim_ref, ldt_ref, bre_ref, bim_ref, cre_ref, cim_ref,
                 tab_ref, wb_ref, wcre_ref, wcim_ref):
    lam_re = jnp.minimum(lre_ref[...], -1e-4)
    lam_im = lim_ref[...]
    dt = jnp.exp(ldt_ref[...])
    mag = jnp.exp(lam_re * dt)
    ang = lam_im * dt
    a_re = mag * jnp.cos(ang)
    a_im = mag * jnp.sin(ang)
    num_re = a_re - 1.0
    num_im = a_im
    den = lam_re * lam_re + lam_im * lam_im
    k_re = ((num_re * lam_re + num_im * lam_im) / den)[:, None, :]
    k_im = ((num_im * lam_re - num_re * lam_im) / den)[:, None, :]
    br = bre_ref[...]
    bi = bim_ref[...]
    bb_re = (k_re * br - k_im * bi).astype(BF16)
    bb_im = (k_re * bi + k_im * br).astype(BF16)
    pr, pi = a_re, a_im
    for _ in range(int(math.log2(CHAIN_LEN))):
        pr, pi = pr * pr - pi * pi, 2.0 * pr * pi
    wb_ref[...] = jnp.zeros_like(wb_ref)
    wcre_ref[...] = jnp.zeros_like(wcre_ref)
    wcim_ref[...] = jnp.zeros_like(wcim_ref)
    groups_per_half = SSM_GROUPS // 2
    for g in range(SSM_GROUPS):
        h, gl = divmod(g, groups_per_half)
        q0, n0 = gl * SSM_GROUP, gl * SSM_STATE
        wb_ref[h, q0:q0 + SSM_GROUP, n0:n0 + SSM_STATE] = bb_re[g]
        wb_ref[h, q0:q0 + SSM_GROUP, HALF_FLAT + n0:HALF_FLAT + n0 + SSM_STATE] = bb_im[g]
        wcre_ref[h, n0:n0 + SSM_STATE, q0:q0 + SSM_GROUP] = cre_ref[g].astype(BF16)
        wcim_ref[h, n0:n0 + SSM_STATE, q0:q0 + SSM_GROUP] = cim_ref[g].astype(BF16)
        for k, v in enumerate((a_re, a_im, pr, pi)):
            tab_ref[k:k + 1, g * SSM_STATE:(g + 1) * SSM_STATE] = v[g:g + 1, :]


def _discretise(a_re, a_im, log_dt, b_re, b_im, c_re, c_im):
    chan_major = lambda b: jnp.swapaxes(b, 1, 2)
    half_w = SSM_WIDTH // 2
    return pl.pallas_call(
        _disc_kernel,
        out_shape=[jax.ShapeDtypeStruct((4, N_FLAT), F32),
                   jax.ShapeDtypeStruct((2, half_w, 2 * HALF_FLAT), BF16),
                   jax.ShapeDtypeStruct((2, HALF_FLAT, half_w), BF16),
                   jax.ShapeDtypeStruct((2, HALF_FLAT, half_w), BF16)],
        name="s5_discretise",
    )(a_re, a_im, log_dt.reshape(SSM_GROUPS, 1), chan_major(b_re), chan_major(b_im),
      jnp.swapaxes(c_re, 1, 2), jnp.swapaxes(c_im, 1, 2))


def _scan_half(hbuf, h, ar, ai, init_re, init_im, store):
    cre = h * N_FLAT
    cim = cre + HALF_FLAT
    hr, hi = init_re, init_im
    for t in range(CHAIN_LEN):
        rows = pl.ds(N_CHAIN * t, N_CHAIN)
        br = hbuf[rows, cre:cre + HALF_FLAT]
        bi = hbuf[rows, cim:cim + HALF_FLAT]
        nr = ar * hr - ai * hi + br
        ni = ar * hi + ai * hr + bi
        if store:
            hbuf[rows, cre:cre + HALF_FLAT] = nr
            hbuf[rows, cim:cim + HALF_FLAT] = ni
        hr, hi = nr, ni
    return hr, hi


N_MIXER_WEIGHTS = 11
N_PAR = 2


def _mixer_kernel(is_prompt, tiles_per_stream, *refs):
    n_in = 1 if is_prompt else 4
    ins = refs[:n_in]
    (gmix, win, wb, wcre, wcim, dsk, wglu, poolw, pscale, wout, atab) = refs[n_in:n_in + N_MIXER_WEIGHTS]
    x2_ref, stre_ref, stim_ref, hist_ref = refs[n_in + N_MIXER_WEIGHTS:n_in + N_MIXER_WEIGHTS + 4]
    scratch = refs[n_in + N_MIXER_WEIGHTS + 4:]
    per = len(scratch) // N_PAR
    lanes = [scratch[p * per:(p + 1) * per] for p in range(N_PAR)]
    x_ref = ins[0]

    tile_in_stream = pl.program_id(0) % tiles_per_stream
    n_lane_blocks = D_MODEL // LANES
    half_w = SSM_WIDTH // 2
    ar_full = atab[0:1, :]
    ai_full = atab[1:2, :]

    if is_prompt:
        @pl.when(pl.program_id(0) == 0)
        def _():
            for p in range(N_PAR):
                cre_s, cim_s, pcarry = lanes[p][8], lanes[p][9], lanes[p][10]
                cre_s[...] = jnp.zeros_like(cre_s)
                cim_s[...] = jnp.zeros_like(cim_s)
                pcarry[...] = jnp.zeros_like(pcarry)

    def a_half(h):
        f0 = h * HALF_FLAT
        return (jnp.broadcast_to(ar_full[:, f0:f0 + HALF_FLAT], (N_CHAIN, HALF_FLAT)),
                jnp.broadcast_to(ai_full[:, f0:f0 + HALF_FLAT], (N_CHAIN, HALF_FLAT)))

    def project(p):
        xperm, hbuf = lanes[p][0], lanes[p][1]
        for j in range(N_CHAIN):
            for cb in range(n_lane_blocks):
                xperm[cb, pl.ds(j, CHAIN_LEN, stride=N_CHAIN), :] = x_ref[
                    p, CHAIN_LEN * j:CHAIN_LEN * (j + 1), cb * LANES:(cb + 1) * LANES]
        x = jnp.concatenate([xperm[cb] for cb in range(n_lane_blocks)], axis=1)
        xn = _rms(x, gmix[...]).astype(BF16)
        proj = jnp.dot(xn, win[...], preferred_element_type=F32)
        ub = proj[:, :SSM_WIDTH].astype(BF16)
        for h in range(2):
            hbuf[:, h * N_FLAT:(h + 1) * N_FLAT] = jnp.dot(
                ub[:, h * half_w:(h + 1) * half_w], wb[h], preferred_element_type=F32)
        return x, proj

    def recur(p):
        hbuf = lanes[p][1]
        if is_prompt:
            fre, fim, hre, him, cre_s, cim_s = lanes[p][4:10]
            zeros = jnp.zeros((N_CHAIN, HALF_FLAT), F32)
            for h in range(2):
                f0 = h * HALF_FLAT
                ar, ai = a_half(h)
                lr, li = _scan_half(hbuf, h, ar, ai, zeros, zeros, store=False)
                fre[:, f0:f0 + HALF_FLAT] = lr
                fim[:, f0:f0 + HALF_FLAT] = li
            fresh = tile_in_stream == 0
            hre[0:1, :] = jnp.where(fresh, 0.0, cre_s[...])
            him[0:1, :] = jnp.where(fresh, 0.0, cim_s[...])
            p_re = atab[2:3, :]
            p_im = atab[3:4, :]
            for j in range(N_CHAIN - 1):
                sr = hre[j:j + 1, :]
                si = him[j:j + 1, :]
                hre[j + 1:j + 2, :] = fre[j:j + 1, :] + p_re * sr - p_im * si
                him[j + 1:j + 2, :] = fim[j:j + 1, :] + p_re * si + p_im * sr
            init_re = hre[...]
            init_im = him[...]
        else:
            init_re = ins[1][p]
            init_im = ins[2][p]
        fin_re = []
        fin_im = []
        for h in range(2):
            f0 = h * HALF_FLAT
            ar, ai = a_half(h)
            er, ei = _scan_half(hbuf, h, ar, ai, init_re[:, f0:f0 + HALF_FLAT],
                                init_im[:, f0:f0 + HALF_FLAT], store=True)
            fin_re.append(er)
            fin_im.append(ei)
        end_re = jnp.concatenate(fin_re, axis=1)
        end_im = jnp.concatenate(fin_im, axis=1)
        if is_prompt:
            cre_s[...] = end_re[N_CHAIN - 1:N_CHAIN, :]
            cim_s[...] = end_im[N_CHAIN - 1:N_CHAIN, :]
            stre_ref[p] = end_re[N_CHAIN - 1:N_CHAIN, :]
            stim_ref[p] = end_im[N_CHAIN - 1:N_CHAIN, :]
        else:
            stre_ref[p] = end_re
            stim_ref[p] = end_im

    def finish(p, x, proj):
        hbuf, res, xpbuf = lanes[p][1], lanes[p][2], lanes[p][3]
        u_s = proj[:, :SSM_WIDTH]
        u_p = proj[:, SSM_WIDTH:SSM_WIDTH + POOL_WIDTH]
        gate_s = proj[:, SSM_WIDTH + POOL_WIDTH:SSM_WIDTH + POOL_WIDTH + D_MODEL]
        gate_p = proj[:, SSM_WIDTH + POOL_WIDTH + D_MODEL:]
        ys = []
        for h in range(2):
            c0 = h * N_FLAT
            h_re = hbuf[:, c0:c0 + HALF_FLAT].astype(BF16)
            h_im = hbuf[:, c0 + HALF_FLAT:c0 + N_FLAT].astype(BF16)
            ys.append(jnp.dot(h_re, wcre[h], preferred_element_type=F32)
                      - jnp.dot(h_im, wcim[h], preferred_element_type=F32))
        y = jnp.concatenate(ys, axis=1) + dsk[...] * u_s
        g = _gelu_tanh(y).astype(BF16)
        glu = jnp.dot(g, wglu[...], preferred_element_type=F32)
        o_s = glu[:, :D_MODEL] * _sigmoid(glu[:, D_MODEL:])

        xpbuf[HIST_ROWS:HIST_ROWS + TM, :] = u_p
        tail = u_p[TM - HIST_ROWS:, :]
        row = lax.broadcasted_iota(jnp.int32, (TM, 1), 0)
        if is_prompt:
            pcarry = lanes[p][10]
            first_chain = (lax.broadcasted_iota(jnp.int32, (HIST_ROWS, POOL_WIDTH), 0) % N_CHAIN) == 0
            carried = jnp.where(tile_in_stream == 0, 0.0, pcarry[...])
            xpbuf[0:HIST_ROWS, :] = jnp.where(first_chain, carried, pltpu.roll(tail, 1, 0))
            new_carry = pltpu.roll(tail, HIST_ROWS - (N_CHAIN - 1), 0)
            pcarry[...] = new_carry
            hist_ref[p] = new_carry
            pos1 = tile_in_stream * TM + CHAIN_LEN * (row % N_CHAIN) + row // N_CHAIN + 1
        else:
            xpbuf[0:HIST_ROWS, :] = ins[3][p]
            hist_ref[p] = tail
            pos1 = PAST_LEN + row // N_CHAIN + 1

        o_ps = []
        for gi, w in enumerate(POOL_WINDOWS):
            c0 = gi * POOL_GROUP_IN
            acc = xpbuf[HIST_ROWS:HIST_ROWS + TM, c0:c0 + POOL_GROUP_IN]
            for k in range(1, w):
                acc = acc + xpbuf[HIST_ROWS - N_CHAIN * k:HIST_ROWS - N_CHAIN * k + TM, c0:c0 + POOL_GROUP_IN]
            cnt = jnp.minimum(w, pos1).astype(F32)
            pooled = acc / cnt
            z = (pooled - u_p[:, c0:c0 + POOL_GROUP_IN]).astype(BF16)
            o_ps.append(jnp.dot(z, poolw[gi], preferred_element_type=F32))
        o_p = jnp.concatenate(o_ps, axis=1) * pscale[...]

        merged = (_sigmoid(gate_s) * o_s + _sigmoid(gate_p) * o_p).astype(BF16)
        x2 = x + jnp.dot(merged, wout[...], preferred_element_type=F32)
        for cb in range(n_lane_blocks):
            res[cb] = x2[:, cb * LANES:(cb + 1) * LANES]
        for j in range(N_CHAIN):
            for cb in range(n_lane_blocks):
                x2_ref[p, CHAIN_LEN * j:CHAIN_LEN * (j + 1), cb * LANES:(cb + 1) * LANES] = res[
                    cb, pl.ds(j, CHAIN_LEN, stride=N_CHAIN), :]

    projected = [None] * N_PAR
    for k in range(N_PAR + 2):
        if 0 <= k - 2 < N_PAR:
            finish(k - 2, *projected[k - 2])
        if 0 <= k - 1 < N_PAR:
            recur(k - 1)
        if k < N_PAR:
            projected[k] = project(k)


def _const_spec(shape):
    nd = len(shape)
    return pl.BlockSpec(shape, lambda i, _nd=nd: (0,) * _nd)


def _mixer_weight_specs():
    return [
        _const_spec((1, D_MODEL)),
        _const_spec((D_MODEL, 3 * D_MODEL)),
        _const_spec((2, SSM_WIDTH // 2, N_FLAT)),
        _const_spec((2, HALF_FLAT, SSM_WIDTH // 2)),
        _const_spec((2, HALF_FLAT, SSM_WIDTH // 2)),
        _const_spec((1, SSM_WIDTH)),
        _const_spec((SSM_WIDTH, 2 * D_MODEL)),
        _const_spec((len(POOL_WINDOWS), POOL_GROUP_IN, POOL_GROUP_OUT)),
        _const_spec((1, D_MODEL)),
        _const_spec((D_MODEL, D_MODEL)),
        _const_spec((4, N_FLAT)),
    ]


def _mixer_common_scratch():
    return [
        pltpu.VMEM((D_MODEL // LANES, TM, LANES), F32),
        pltpu.VMEM((TM, 2 * N_FLAT), F32),
        pltpu.VMEM((D_MODEL // LANES, TM, LANES), F32),
        pltpu.VMEM((HIST_ROWS + TM, POOL_WIDTH), F32),
    ]


def _mixer_prompt(x, weights):
    n_streams, seq, _ = x.shape
    assert n_streams % N_PAR == 0 and seq % TM == 0
    tiles_per_stream = seq // TM
    blk = lambda shape: pl.BlockSpec(shape, lambda i: (i // tiles_per_stream, 0, 0))
    row_spec = pl.BlockSpec((N_PAR, TM, D_MODEL), lambda i: (i // tiles_per_stream, i % tiles_per_stream, 0))
    lane_scratch = _mixer_common_scratch() + [
        pltpu.VMEM((N_CHAIN, N_FLAT), F32), pltpu.VMEM((N_CHAIN, N_FLAT), F32),
        pltpu.VMEM((N_CHAIN, N_FLAT), F32), pltpu.VMEM((N_CHAIN, N_FLAT), F32),
        pltpu.VMEM((1, N_FLAT), F32), pltpu.VMEM((1, N_FLAT), F32),
        pltpu.VMEM((HIST_ROWS, POOL_WIDTH), F32),
    ]
    return pl.pallas_call(
        functools.partial(_mixer_kernel, True, tiles_per_stream),
        grid=(n_streams // N_PAR * tiles_per_stream,),
        in_specs=[row_spec] + _mixer_weight_specs(),
        out_specs=[row_spec, blk((N_PAR, 1, N_FLAT)), blk((N_PAR, 1, N_FLAT)),
                   blk((N_PAR, HIST_ROWS, POOL_WIDTH))],
        out_shape=[jax.ShapeDtypeStruct((n_streams, seq, D_MODEL), F32),
                   jax.ShapeDtypeStruct((n_streams, 1, N_FLAT), F32),
                   jax.ShapeDtypeStruct((n_streams, 1, N_FLAT), F32),
                   jax.ShapeDtypeStruct((n_streams, HIST_ROWS, POOL_WIDTH), F32)],
        scratch_shapes=lane_scratch * N_PAR,
        compiler_params=pltpu.CompilerParams(dimension_semantics=("arbitrary",), vmem_limit_bytes=VMEM_LIMIT),
        name="mixer_prompt",
    )(x, *weights)


def _mixer_sample(x, h0_re, h0_im, cache_t, weights):
    n_streams, seq, _ = x.shape
    assert seq == CHAIN_LEN and n_streams % (N_CHAIN * N_PAR) == 0
    n_tiles = n_streams // N_CHAIN
    blk = lambda shape: pl.BlockSpec(shape, lambda i: (i, 0, 0))
    row_spec = blk((N_PAR, TM, D_MODEL))
    st_spec = blk((N_PAR, N_CHAIN, N_FLAT))
    hist_spec = blk((N_PAR, HIST_ROWS, POOL_WIDTH))
    tiles = lambda v: v.reshape((n_tiles, -1) + v.shape[-1:])
    return pl.pallas_call(
        functools.partial(_mixer_kernel, False, 1),
        grid=(n_tiles // N_PAR,),
        in_specs=[row_spec, st_spec, st_spec, hist_spec] + _mixer_weight_specs(),
        out_specs=[row_spec, st_spec, st_spec, hist_spec],
        out_shape=[jax.ShapeDtypeStruct((n_tiles, TM, D_MODEL), F32),
                   jax.ShapeDtypeStruct((n_tiles, N_CHAIN, N_FLAT), F32),
                   jax.ShapeDtypeStruct((n_tiles, N_CHAIN, N_FLAT), F32),
                   jax.ShapeDtypeStruct((n_tiles, HIST_ROWS, POOL_WIDTH), F32)],
        scratch_shapes=_mixer_common_scratch() * N_PAR,
        compiler_params=pltpu.CompilerParams(dimension_semantics=("arbitrary",), vmem_limit_bytes=VMEM_LIMIT),
        name="mixer_sample",
    )(tiles(x), tiles(h0_re), tiles(h0_im), cache_t, *weights)


def _to_chunks(lo, hi):
    n = lo.shape[0] // CHUNK
    half = D_MODEL // 2
    both = jnp.concatenate([lo.reshape(n, CHUNK, half), hi.reshape(n, CHUNK, half)], axis=1)
    return both.astype(BF16)


def _from_chunks(blk):
    n = blk.shape[0]
    half = D_MODEL // 2
    f = blk.astype(F32)
    lo = f[:, :CHUNK, :].reshape(n * CHUNK, half).astype(BF16)
    hi = f[:, CHUNK:, :].reshape(n * CHUNK, half).astype(BF16)
    return lo, hi


def _route_kernel(n_prompt_steps, x2p_ref, x2s_ref, g_ref, wr_ref, ltri_ref, utri_ref,
                  xs_ref, route_ref, nch_ref):
    is_prompt = pl.program_id(0) < n_prompt_steps
    tiles = range(STEP_TILES)
    each = lambda fn, *cols: [fn(*(c[h] for c in cols)) for h in tiles]
    row_max = lambda v: jnp.max(v, axis=1, keepdims=True)
    row_min = lambda v: jnp.min(v, axis=1, keepdims=True)
    row_sum = lambda v: jnp.sum(v, axis=1, keepdims=True)

    lane = lax.broadcasted_iota(jnp.int32, (TM, LANES), 1)
    lane_f = lane.astype(F32)
    big = jnp.float32(1 << 20)
    neg = jnp.float32(-jnp.inf)
    gmask = lane < N_EXPERT_GROUPS
    eid = lane - N_EXPERT_GROUPS
    lane_grp = (eid >> 3).astype(F32)
    is_expert = (eid >= 0) & (eid < N_EXPERTS)

    xn = [_rms(jnp.where(is_prompt, x2p_ref[pl.ds(h * TM, TM), :], x2s_ref[pl.ds(h * TM, TM), :]),
               g_ref[...]).astype(BF16) for h in tiles]
    logits = each(lambda v: jnp.dot(v, wr_ref[...], preferred_element_type=F32), xn)
    m = each(lambda lg: row_max(jnp.where(gmask, lg, neg)), logits)
    grp = each(lambda lg, mm: row_min(jnp.where(gmask & (lg == mm), lane_f, big)), logits, m)
    wg = each(lambda lg, mm: 1.0 / row_sum(jnp.where(gmask, jnp.exp(lg - mm), 0.0)), logits, m)
    emask = each(lambda g: is_expert & (lane_grp == g), grp)
    v1 = each(lambda lg, em: row_max(jnp.where(em, lg, neg)), logits, emask)
    i1 = each(lambda lg, em, v: row_min(jnp.where(em & (lg == v), lane_f, big)), logits, emask, v1)
    emask2 = each(lambda em, i: em & (lane_f != i), emask, i1)
    v2 = each(lambda lg, em: row_max(jnp.where(em, lg, neg)), logits, emask2)
    i2 = each(lambda lg, em, v: row_min(jnp.where(em & (lg == v), lane_f, big)), logits, emask2, v2)
    e21 = each(lambda a, b: jnp.exp(b - a), v1, v2)
    w1 = each(lambda g, e: g / (1.0 + e), wg, e21)
    w2 = each(lambda g, e: g * e / (1.0 + e), wg, e21)

    a1 = each(lambda i: lane_f == (i - N_EXPERT_GROUPS), i1)
    a2 = each(lambda i: lane_f == (i - N_EXPERT_GROUPS), i2)
    a = each(lambda p, q: (p | q).astype(F32), a1, a2)
    before = each(lambda v: jnp.dot(ltri_ref[...], v.astype(BF16), preferred_element_type=F32), a)
    cnt = each(lambda v: jnp.sum(v, axis=0, keepdims=True), a)
    nch16 = each(lambda c: jnp.broadcast_to(jnp.floor((c + (CHUNK - 1)) * (1.0 / CHUNK)), (16, LANES)), cnt)
    start = each(lambda n: jnp.dot(n.astype(BF16), utri_ref[...], preferred_element_type=F32), nch16)
    slot = each(lambda bf, st: bf + CHUNK * st[0:1, :], before, start)
    d1 = each(lambda p, sl: row_sum(jnp.where(p, sl, 0.0)), a1, slot)
    d2 = each(lambda p, sl: row_sum(jnp.where(p, sl, 0.0)), a2, slot)
    route = each(lambda p, q, u, v: jnp.where(lane == 0, p, jnp.where(lane == 1, q, jnp.where(
        lane == 2, u, jnp.where(lane == 3, v, 0.0)))), d1, d2, w1, w2)

    dest = lax.broadcasted_iota(jnp.int32, (CAP_ROWS, TM), 0)
    half = D_MODEL // 2
    for h in tiles:
        route_ref[pl.ds(h * TM, TM), :] = route[h]
        nch_ref[h] = nch16[h][0:8, :].astype(jnp.int32)
        dt = jnp.transpose(jnp.where(lane < 2, route[h], 0.0)).astype(jnp.int32)
        perm = ((dest == dt[0:1, :]) | (dest == dt[1:2, :])).astype(F32).astype(BF16)
        lo = jnp.dot(perm, xn[h][:, :half], preferred_element_type=F32)
        hi = jnp.dot(perm, xn[h][:, half:], preferred_element_type=F32)
        xs_ref[pl.ds(h * CAP_CHUNKS, CAP_CHUNKS)] = _to_chunks(lo, hi)


def _route(x2p, x2s, g_ffn, w_router):
    n_prompt_tiles = x2p.shape[0] // TM
    n_tiles = n_prompt_tiles + x2s.shape[0] // TM
    assert n_prompt_tiles % STEP_TILES == 0 and n_tiles % STEP_TILES == 0
    n_prompt_steps = n_prompt_tiles // STEP_TILES
    rows = STEP_TILES * TM
    r = jnp.arange(TM)
    ltri = (r[None, :] < r[:, None]).astype(BF16)
    e = jnp.arange(LANES)
    utri = (e[:, None] < e[None, :]).astype(BF16)
    return pl.pallas_call(
        functools.partial(_route_kernel, n_prompt_steps),
        grid=(n_tiles // STEP_TILES,),
        in_specs=[pl.BlockSpec((rows, D_MODEL), lambda i: (jnp.minimum(i, n_prompt_steps - 1), 0)),
                  pl.BlockSpec((rows, D_MODEL), lambda i: (jnp.maximum(i - n_prompt_steps, 0), 0)),
                  _const_spec((1, D_MODEL)), _const_spec((D_MODEL, LANES)),
                  _const_spec((TM, TM)), _const_spec((LANES, LANES))],
        out_specs=[pl.BlockSpec((STEP_TILES * CAP_CHUNKS, 2 * CHUNK, D_MODEL // 2), lambda i: (i, 0, 0)),
                   pl.BlockSpec((rows, LANES), lambda i: (i, 0)),
                   pl.BlockSpec((STEP_TILES, 8, LANES), lambda i: (i, 0, 0))],
        out_shape=[jax.ShapeDtypeStruct((n_tiles * CAP_CHUNKS, 2 * CHUNK, D_MODEL // 2), BF16),
                   jax.ShapeDtypeStruct((n_tiles * TM, LANES), F32),
                   jax.ShapeDtypeStruct((n_tiles, 8, LANES), jnp.int32)],
        compiler_params=pltpu.CompilerParams(dimension_semantics=("arbitrary",), vmem_limit_bytes=VMEM_LIMIT),
        name="route_sort",
    )(x2p, x2s, g_ffn, w_router, ltri, utri)


def _spare_chunk(k):
    return (1 + k // N_SPARE_SLOTS) * CAP_CHUNKS + MAX_USED_CHUNKS + k % N_SPARE_SLOTS


def _chunk_tables(nch, n_expert_tiles):
    n_tiles = nch.shape[0]
    i32 = jnp.int32
    start = jnp.cumsum(nch, axis=1) - nch
    off = jnp.cumsum(nch, axis=0) - nch
    per_expert = jnp.sum(nch, axis=0)
    tiles_e = (per_expert + TILE_CHUNKS - 1) // TILE_CHUNKS
    cum_tiles = jnp.cumsum(tiles_e)
    first_tile = cum_tiles - tiles_e
    n_active = cum_tiles[-1].astype(i32)
    t = jnp.arange(n_expert_tiles, dtype=i32)
    te = jnp.minimum(jnp.sum((t[:, None] >= cum_tiles[None, :]).astype(i32), axis=1), N_EXPERTS - 1)
    onehot = te[:, None] == jnp.arange(N_EXPERTS, dtype=i32)[None, :]
    pick = lambda tab: jnp.sum(jnp.where(onehot[:, :, None], tab.T[None, :, :], 0), axis=1)
    off_t, nch_t, start_t = pick(off), pick(nch), pick(start)
    k = t - jnp.sum(jnp.where(onehot, first_tile[None, :], 0), axis=1)
    q = (TILE_CHUNKS * k)[:, None] + jnp.arange(TILE_CHUNKS, dtype=i32)[None, :]
    in_run = (off_t[:, None, :] <= q[:, :, None]) & (q[:, :, None] < (off_t + nch_t)[:, None, :])
    run_src = (jnp.arange(n_tiles, dtype=i32) * CAP_CHUNKS)[None, :] + start_t - off_t
    src = jnp.sum(jnp.where(in_run, run_src[:, None, :], 0), axis=-1) + q
    valid = jnp.any(in_run, axis=-1) & (t < n_active)[:, None]
    src = jnp.where(valid, src, ZERO_CHUNK).astype(i32)
    spare = _spare_chunk((t % N_RING)[:, None] * TILE_CHUNKS + jnp.arange(TILE_CHUNKS, dtype=i32)[None, :])
    dst = jnp.where(valid, src, spare).astype(i32)
    dst = jnp.concatenate([spare[1:N_RING], dst], axis=0)
    return first_tile.astype(i32), tiles_e.astype(i32), src.reshape(-1), dst.reshape(-1), n_active.reshape(1)


N_RING = 5


def _expert_kernel(first_ref, ntile_ref, src_ref, dst_ref, nact_ref, xs_hbm, wg_ref, wu_ref, wd_ref, ys_hbm,
                   *scratch):
    xbufs = scratch[:N_RING]
    obufs = scratch[N_RING:2 * N_RING]
    gsem, ssem, wg16, wu16, wd16 = scratch[2 * N_RING:]
    e = pl.program_id(0)
    n_active = nact_ref[0]
    ahead = N_RING - 1

    def gather_copy(tile, slot, c):
        return pltpu.make_async_copy(xs_hbm.at[src_ref[tile * TILE_CHUNKS + c]], xbufs[slot].at[c],
                                     gsem.at[slot])

    def scatter_copy(tile, slot, c):
        return pltpu.make_async_copy(obufs[slot].at[c], ys_hbm.at[dst_ref[(tile + ahead) * TILE_CHUNKS + c]],
                                     ssem.at[slot])

    def start_all(copy, tile, slot):
        for c in range(TILE_CHUNKS):
            copy(tile, slot, c).start()

    def wait_all(copy, tile, slot):
        for c in range(TILE_CHUNKS):
            copy(tile, slot, c).wait()

    @pl.when(e == 0)
    def _():
        for v in range(ahead):
            start_all(gather_copy, v, v)
        for u in range(-ahead, 0):
            obufs[u % N_RING][...] = jnp.zeros_like(obufs[u % N_RING])
        for u in range(-ahead, -1):
            start_all(scatter_copy, u, u % N_RING)

    wg16[...] = wg_ref[0].astype(BF16)
    wu16[...] = wu_ref[0].astype(BF16)
    wd16[...] = wd_ref[0].astype(BF16)

    def do_tile(tile, slot):
        nxt = (slot + ahead) % N_RING
        old = (slot + 1) % N_RING
        wait_all(gather_copy, tile, slot)
        lo, hi = _from_chunks(xbufs[slot][...])
        x = jnp.concatenate([lo, hi], axis=1)
        gate = jnp.dot(x, wg16[...], preferred_element_type=F32)
        start_all(scatter_copy, tile - 1, nxt)
        up = jnp.dot(x, wu16[...], preferred_element_type=F32)
        hmid = (gate * _sigmoid(gate) * up).astype(BF16)
        start_all(gather_copy, tile + ahead, nxt)
        y = jnp.dot(hmid, wd16[...], preferred_element_type=F32)
        half = D_MODEL // 2
        wait_all(scatter_copy, tile - ahead, old)
        obufs[slot][...] = _to_chunks(y[:, :half], y[:, half:])

    def tile_body(k, carry):
        tile = first_ref[e] + k
        for slot in range(N_RING):
            @pl.when(tile % N_RING == slot)
            def _():
                do_tile(tile, slot)
        return carry

    lax.fori_loop(0, ntile_ref[e], tile_body, 0)

    @pl.when(e == pl.num_programs(0) - 1)
    def _():
        last = n_active - 1
        for slot in range(N_RING):
            @pl.when(last % N_RING == slot)
            def _():
                for d in range(N_RING - 2, 0, -1):
                    wait_all(scatter_copy, last - d, (slot - d) % N_RING)
                start_all(scatter_copy, last, slot)
                wait_all(scatter_copy, last, slot)
                for d in range(1, N_RING):
                    wait_all(gather_copy, last + d, (slot + d) % N_RING)


def _experts(first_tile, tiles_e, src, dst, n_active, xs, w_gate, w_up, w_down):
    chunk_shape = (2 * CHUNK, D_MODEL // 2)
    assert _spare_chunk(N_RING * TILE_CHUNKS - 1) < xs.shape[0]
    tile_buf = pltpu.VMEM((TILE_CHUNKS,) + chunk_shape, BF16)
    grid_spec = pltpu.PrefetchScalarGridSpec(
        num_scalar_prefetch=5,
        grid=(N_EXPERTS,),
        in_specs=[pl.BlockSpec(memory_space=pl.ANY),
                  pl.BlockSpec((1, D_MODEL, D_EXPERT), lambda e, *_: (e, 0, 0)),
                  pl.BlockSpec((1, D_MODEL, D_EXPERT), lambda e, *_: (e, 0, 0)),
                  pl.BlockSpec((1, D_EXPERT, D_MODEL), lambda e, *_: (e, 0, 0))],
        out_specs=pl.BlockSpec(memory_space=pl.ANY),
        scratch_shapes=[tile_buf] * (2 * N_RING) + [
                        pltpu.SemaphoreType.DMA((N_RING,)),
                        pltpu.SemaphoreType.DMA((N_RING,)),
                        pltpu.VMEM((D_MODEL, D_EXPERT), BF16),
                        pltpu.VMEM((D_MODEL, D_EXPERT), BF16),
                        pltpu.VMEM((D_EXPERT, D_MODEL), BF16)],
    )
    return pl.pallas_call(
        _expert_kernel,
        grid_spec=grid_spec,
        out_shape=jax.ShapeDtypeStruct(xs.shape, xs.dtype),
        input_output_aliases={5: 0},
        compiler_params=pltpu.CompilerParams(dimension_semantics=("arbitrary",), vmem_limit_bytes=VMEM_LIMIT),
        name="expert_ffn",
    )(first_tile, tiles_e, src, dst, n_active, xs, w_gate, w_up, w_down)


def _combine_tiles(x_ref, route_ref, gfin_ref, ys_ref, out_ref):
    for h in range(STEP_TILES):
        rows = pl.ds(h * TM, TM)
        lo, hi = _from_chunks(ys_ref[pl.ds(h * CAP_CHUNKS, CAP_CHUNKS)])
        route = route_ref[rows, :]
        d1 = route[:, 0:1].astype(jnp.int32)
        d2 = route[:, 1:2].astype(jnp.int32)
        w1 = route[:, 2:3]
        w2 = route[:, 3:4]
        dest = lax.broadcasted_iota(jnp.int32, (TM, CAP_ROWS), 1)
        sel = jnp.where(dest == d1, w1, jnp.where(dest == d2, w2, 0.0)).astype(BF16)
        moe = jnp.concatenate([jnp.dot(sel, lo, preferred_element_type=F32),
                               jnp.dot(sel, hi, preferred_element_type=F32)], axis=1)
        out_ref[rows, :] = _rms(x_ref[rows, :] + moe, gfin_ref[...])


N_COMBINE_BUFFERS = 3


def _combine_kernel(n_prompt_steps, n_sample_steps, x2p_hbm, x2s_hbm, route_hbm, gfin_hbm, ys_hbm,
                    outp_hbm, outs_hbm):
    rows = STEP_TILES * TM

    def stream(n_steps, first_step, x_hbm, out_hbm):
        deep = pl.Buffered(N_COMBINE_BUFFERS) if n_steps >= N_COMBINE_BUFFERS else None
        pltpu.emit_pipeline(
            _combine_tiles,
            grid=(n_steps,),
            in_specs=[pl.BlockSpec((rows, D_MODEL), lambda i: (i, 0), pipeline_mode=deep),
                      pl.BlockSpec((rows, LANES), lambda i: (i + first_step, 0)),
                      pl.BlockSpec((1, D_MODEL), lambda i: (0, 0)),
                      pl.BlockSpec((STEP_TILES * CAP_CHUNKS, 2 * CHUNK, D_MODEL // 2),
                                   lambda i: (i + first_step, 0, 0), pipeline_mode=deep)],
            out_specs=[pl.BlockSpec((rows, D_MODEL), lambda i: (i, 0))],
        )(x_hbm, route_hbm, gfin_hbm, ys_hbm, out_hbm)

    stream(n_prompt_steps, 0, x2p_hbm, outp_hbm)
    stream(n_sample_steps, n_prompt_steps, x2s_hbm, outs_hbm)


def _combine(x2p, x2s, route, g_final, ys):
    n_prompt_tiles = x2p.shape[0] // TM
    n_sample_tiles = x2s.shape[0] // TM
    assert n_prompt_tiles % STEP_TILES == 0 and n_sample_tiles % STEP_TILES == 0
    any_spec = pl.BlockSpec(memory_space=pl.ANY)
    return pl.pallas_call(
        functools.partial(_combine_kernel, n_prompt_tiles // STEP_TILES, n_sample_tiles // STEP_TILES),
        in_specs=[any_spec] * 5,
        out_specs=[any_spec] * 2,
        out_shape=[jax.ShapeDtypeStruct(x2p.shape, F32), jax.ShapeDtypeStruct(x2s.shape, F32)],
        compiler_params=pltpu.CompilerParams(vmem_limit_bytes=VMEM_LIMIT),
        name="combine_norm",
    )(x2p, x2s, route, g_final, ys)


def kernel(x_prompt, x_sample, state_ssm_re, state_ssm_im, cache_pool, g_mix, w_in, ssm_a_re, ssm_a_im,
           ssm_log_dt, ssm_b_re, ssm_b_im, ssm_c_re, ssm_c_im, ssm_d, w_glu_a, w_glu_b, pool_w, pool_scale,
           w_out, g_ffn, w_router_group, w_router_expert, w_exp_gate, w_exp_up, w_exp_down, g_final):
    li = 0
    n_pb, seq_p, _ = x_prompt.shape
    n_sb, seq_s, _ = x_sample.shape

    a_tab, wb, wc_re, wc_im = _discretise(
        ssm_a_re[li], ssm_a_im[li], ssm_log_dt[li], ssm_b_re[li], ssm_b_im[li], ssm_c_re[li], ssm_c_im[li])
    weights = [
        g_mix[li].reshape(1, D_MODEL),
        w_in[li].astype(BF16),
        wb,
        wc_re,
        wc_im,
        ssm_d[li].reshape(1, SSM_WIDTH),
        jnp.concatenate([w_glu_a[li], w_glu_b[li]], axis=1).astype(BF16),
        pool_w[li].astype(BF16),
        pool_scale[li].reshape(1, D_MODEL),
        w_out[li].astype(BF16),
        a_tab,
    ]

    x2p, stp_re, stp_im, histp = _mixer_prompt(x_prompt, weights)
    x2p = x2p.reshape(n_pb * seq_p, D_MODEL)

    n_stiles = n_sb // N_CHAIN
    cache16 = jnp.pad(cache_pool[li], ((0, 0), (1, 0), (0, 0)))
    cache_t = cache16.reshape(n_stiles, N_CHAIN, 16, POOL_WIDTH).transpose(0, 2, 1, 3).reshape(
        n_stiles, HIST_ROWS, POOL_WIDTH)
    x2s, sts_re, sts_im, hists = _mixer_sample(
        x_sample, state_ssm_re[li].reshape(n_sb, N_FLAT), state_ssm_im[li].reshape(n_sb, N_FLAT),
        cache_t, weights)
    x2s = x2s.reshape(n_sb * seq_s, D_MODEL)

    w_router = jnp.concatenate(
        [w_router_group[li], w_router_expert[li].reshape(D_MODEL, N_EXPERTS),
         jnp.zeros((D_MODEL, LANES - N_EXPERT_GROUPS - N_EXPERTS), F32)], axis=1).astype(BF16)
    xs, route, nch = _route(x2p, x2s, g_ffn[li].reshape(1, D_MODEL), w_router)

    n_tiles = (x2p.shape[0] + x2s.shape[0]) // TM
    max_chunks = n_tiles * (2 * TM // CHUNK + N_EXPERTS)
    n_expert_tiles = max_chunks // TILE_CHUNKS + N_EXPERTS
    first_tile, tiles_e, src, dst, n_active = _chunk_tables(nch[:, 0, :N_EXPERTS], n_expert_tiles)
    ys = _experts(first_tile, tiles_e, src, dst, n_active, xs, w_exp_gate[li], w_exp_up[li], w_exp_down[li])
    yp, ysm = _combine(x2p, x2s, route, g_final.reshape(1, D_MODEL), ys)

    sd = state_ssm_re.dtype
    cd = cache_pool.dtype
    y_prompt = yp.reshape(n_pb, seq_p, D_MODEL)
    y_sample = ysm.reshape(n_sb, seq_s, D_MODEL)
    re_p = stp_re.reshape(1, n_pb, SSM_GROUPS, SSM_STATE).astype(sd)
    im_p = stp_im.reshape(1, n_pb, SSM_GROUPS, SSM_STATE).astype(sd)
    hist_p = histp[:, ::N_CHAIN, :][:, 1:, :].reshape(1, n_pb, POOL_HIST, POOL_WIDTH).astype(cd)
    re_s = sts_re.reshape(1, n_sb, SSM_GROUPS, SSM_STATE).astype(sd)
    im_s = sts_im.reshape(1, n_sb, SSM_GROUPS, SSM_STATE).astype(sd)
    hist_s = hists.reshape(n_stiles, 16, N_CHAIN, POOL_WIDTH).transpose(0, 2, 1, 3).reshape(
        n_sb, 16, POOL_WIDTH)[:, 1:, :].reshape(1, n_sb, POOL_HIST, POOL_WIDTH).astype(cd)
    return (y_prompt, y_sample, re_p, im_p, hist_p, re_s, im_s, hist_s)
```

```python
import functools
import math

import jax
import jax.numpy as jnp
from jax import lax
from jax.experimental import pallas as pl
from jax.experimental.pallas import tpu as pltpu

F32 = jnp.float32
BF16 = jnp.bfloat16

D_MODEL = 1024
SSM_WIDTH = 512
SSM_GROUPS = 32
SSM_GROUP = 16
SSM_STATE = 64
N_FLAT = SSM_GROUPS * SSM_STATE
HALF_FLAT = N_FLAT // 2
POOL_WIDTH = 512
POOL_WINDOWS = (2, 4, 8, 16)
POOL_GROUP_IN = 128
POOL_GROUP_OUT = 256
POOL_HIST = 15
N_EXPERTS = 32
EXPERTS_PER_GROUP = 8
N_EXPERT_GROUPS = 4
D_EXPERT = 512
EPS = 1e-6
PAST_LEN = 1024

TM = 256
N_CHAIN = 8
CHAIN_LEN = TM // N_CHAIN
HIST_ROWS = 16 * N_CHAIN

CHUNK = 8
TILE_CHUNKS = TM // CHUNK
CAP_CHUNKS = 96
CAP_ROWS = CAP_CHUNKS * CHUNK
MAX_USED_CHUNKS = 2 * TM // CHUNK + N_EXPERTS * (CHUNK - 1) // CHUNK
N_SPARE_SLOTS = CAP_CHUNKS - MAX_USED_CHUNKS - 1
ZERO_CHUNK = CAP_CHUNKS - 1
LANES = 128

V7X_VMEM_BYTES = 64 * 1024 * 1024
VMEM_LIMIT = V7X_VMEM_BYTES * 13 // 16
STEP_TILES = 4


def _rms(x, g):
    r = lax.rsqrt(jnp.mean(x * x, axis=-1, keepdims=True) + EPS)
    return x * r * g


def _sigmoid(x):
    return 0.5 * jnp.tanh(0.5 * x) + 0.5


def _gelu_tanh(x):
    c = math.sqrt(2.0 / math.pi)
    return x * (0.5 * (1.0 + jnp.tanh(c * (x + 0.044715 * (x * x * x)))))


def _disc_kernel(lre_ref, lim_ref, ldt_ref, bre_ref, bim_ref, cre_ref, cim_ref,
                 tab_ref, wb_ref, wcre_ref, wcim_ref):
    lam_re = jnp.minimum(lre_ref[...], -1e-4)
    lam_im = lim_ref[...]
    dt = jnp.exp(ldt_ref[...])
    mag = jnp.exp(lam_re * dt)
    ang = lam_im * dt
    a_re = mag * jnp.cos(ang)
    a_im = mag * jnp.sin(ang)
    num_re = a_re - 1.0
    num_im = a_im
    den = lam_re * lam_re + lam_im * lam_im
    k_re = ((num_re * lam_re + num_im * lam_im) / den)[:, None, :]
    k_im = ((num_im * lam_re - num_re * lam_im) / den)[:, None, :]
    br = bre_ref[...]
    bi = bim_ref[...]
    bb_re = (k_re * br - k_im * bi).astype(BF16)
    bb_im = (k_re * bi + k_im * br).astype(BF16)
    pr, pi = a_re, a_im
    for _ in range(int(math.log2(CHAIN_LEN))):
        pr, pi = pr * pr - pi * pi, 2.0 * pr * pi
    wb_ref[...] = jnp.zeros_like(wb_ref)
    wcre_ref[...] = jnp.zeros_like(wcre_ref)
    wcim_ref[...] = jnp.zeros_like(wcim_ref)
    groups_per_half = SSM_GROUPS // 2
    for g in range(SSM_GROUPS):
        h, gl = divmod(g, groups_per_half)
        q0, n0 = gl * SSM_GROUP, gl * SSM_STATE
        wb_ref[h, q0:q0 + SSM_GROUP, n0:n0 + SSM_STATE] = bb_re[g]
        wb_ref[h, q0:q0 + SSM_GROUP, HALF_FLAT + n0:HALF_FLAT + n0 + SSM_STATE] = bb_im[g]
        wcre_ref[h, n0:n0 + SSM_STATE, q0:q0 + SSM_GROUP] = cre_ref[g].astype(BF16)
        wcim_ref[h, n0:n0 + SSM_STATE, q0:q0 + SSM_GROUP] = cim_ref[g].astype(BF16)
        for k, v in enumerate((a_re, a_im, pr, pi)):
            tab_ref[k:k + 1, g * SSM_STATE:(g + 1) * SSM_STATE] = v[g:g + 1, :]


def _discretise(a_re, a_im, log_dt, b_re, b_im, c_re, c_im):
    chan_major = lambda b: jnp.swapaxes(b, 1, 2)
    half_w = SSM_WIDTH // 2
    return pl.pallas_call(
        _disc_kernel,
        out_shape=[jax.ShapeDtypeStruct((4, N_FLAT), F32),
                   jax.ShapeDtypeStruct((2, half_w, 2 * HALF_FLAT), BF16),
                   jax.ShapeDtypeStruct((2, HALF_FLAT, half_w), BF16),
                   jax.ShapeDtypeStruct((2, HALF_FLAT, half_w), BF16)],
        name="s5_discretise",
    )(a_re, a_im, log_dt.reshape(SSM_GROUPS, 1), chan_major(b_re), chan_major(b_im),
      jnp.swapaxes(c_re, 1, 2), jnp.swapaxes(c_im, 1, 2))


def _scan_half(hbuf, h, ar, ai, init_re, init_im, store):
    cre = h * N_FLAT
    cim = cre + HALF_FLAT
    hr, hi = init_re, init_im
    for t in range(CHAIN_LEN):
        rows = pl.ds(N_CHAIN * t, N_CHAIN)
        br = hbuf[rows, cre:cre + HALF_FLAT]
        bi = hbuf[rows, cim:cim + HALF_FLAT]
        nr = ar * hr - ai * hi + br
        ni = ar * hi + ai * hr + bi
        if store:
            hbuf[rows, cre:cre + HALF_FLAT] = nr
            hbuf[rows, cim:cim + HALF_FLAT] = ni
        hr, hi = nr, ni
    return hr, hi


N_MIXER_WEIGHTS = 11
N_PAR = 2


def _mixer_kernel(is_prompt, tiles_per_stream, *refs):
    n_in = 1 if is_prompt else 4
    ins = refs[:n_in]
    (gmix, win, wb, wcre, wcim, dsk, wglu, poolw, pscale, wout, atab) = refs[n_in:n_in + N_MIXER_WEIGHTS]
    x2_ref, stre_ref, stim_ref, hist_ref = refs[n_in + N_MIXER_WEIGHTS:n_in + N_MIXER_WEIGHTS + 4]
    scratch = refs[n_in + N_MIXER_WEIGHTS + 4:]
    per = len(scratch) // N_PAR
    lanes = [scratch[p * per:(p + 1) * per] for p in range(N_PAR)]
    x_ref = ins[0]

    tile_in_stream = pl.program_id(0) % tiles_per_stream
    n_lane_blocks = D_MODEL // LANES
    half_w = SSM_WIDTH // 2
    ar_full = atab[0:1, :]
    ai_full = atab[1:2, :]

    if is_prompt:
        @pl.when(pl.program_id(0) == 0)
        def _():
            for p in range(N_PAR):
                cre_s, cim_s, pcarry = lanes[p][8], lanes[p][9], lanes[p][10]
                cre_s[...] = jnp.zeros_like(cre_s)
                cim_s[...] = jnp.zeros_like(cim_s)
                pcarry[...] = jnp.zeros_like(pcarry)

    def a_half(h):
        f0 = h * HALF_FLAT
        return (jnp.broadcast_to(ar_full[:, f0:f0 + HALF_FLAT], (N_CHAIN, HALF_FLAT)),
                jnp.broadcast_to(ai_full[:, f0:f0 + HALF_FLAT], (N_CHAIN, HALF_FLAT)))

    def project(p):
        xperm, hbuf = lanes[p][0], lanes[p][1]
        for j in range(N_CHAIN):
            for cb in range(n_lane_blocks):
                xperm[cb, pl.ds(j, CHAIN_LEN, stride=N_CHAIN), :] = x_ref[
                    p, CHAIN_LEN * j:CHAIN_LEN * (j + 1), cb * LANES:(cb + 1) * LANES]
        x = jnp.concatenate([xperm[cb] for cb in range(n_lane_blocks)], axis=1)
        xn = _rms(x, gmix[...]).astype(BF16)
        proj = jnp.dot(xn, win[...], preferred_element_type=F32)
        ub = proj[:, :SSM_WIDTH].astype(BF16)
        for h in range(2):
            hbuf[:, h * N_FLAT:(h + 1) * N_FLAT] = jnp.dot(
                ub[:, h * half_w:(h + 1) * half_w], wb[h], preferred_element_type=F32)
        return x, proj

    def recur(p):
        hbuf = lanes[p][1]
        if is_prompt:
            fre, fim, hre, him, cre_s, cim_s = lanes[p][4:10]
            zeros = jnp.zeros((N_CHAIN, HALF_FLAT), F32)
            for h in range(2):
                f0 = h * HALF_FLAT
                ar, ai = a_half(h)
                lr, li = _scan_half(hbuf, h, ar, ai, zeros, zeros, store=False)
                fre[:, f0:f0 + HALF_FLAT] = lr
                fim[:, f0:f0 + HALF_FLAT] = li
            fresh = tile_in_stream == 0
            hre[0:1, :] = jnp.where(fresh, 0.0, cre_s[...])
            him[0:1, :] = jnp.where(fresh, 0.0, cim_s[...])
            p_re = atab[2:3, :]
            p_im = atab[3:4, :]
            for j in range(N_CHAIN - 1):
                sr = hre[j:j + 1, :]
                si = him[j:j + 1, :]
                hre[j + 1:j + 2, :] = fre[j:j + 1, :] + p_re * sr - p_im * si
                him[j + 1:j + 2, :] = fim[j:j + 1, :] + p_re * si + p_im * sr
            init_re = hre[...]
            init_im = him[...]
        else:
            init_re = ins[1][p]
            init_im = ins[2][p]
        fin_re = []
        fin_im = []
        for h in range(2):
            f0 = h * HALF_FLAT
            ar, ai = a_half(h)
            er, ei = _scan_half(hbuf, h, ar, ai, init_re[:, f0:f0 + HALF_FLAT],
                                init_im[:, f0:f0 + HALF_FLAT], store=True)
            fin_re.append(er)
            fin_im.append(ei)
        end_re = jnp.concatenate(fin_re, axis=1)
        end_im = jnp.concatenate(fin_im, axis=1)
        if is_prompt:
            cre_s[...] = end_re[N_CHAIN - 1:N_CHAIN, :]
            cim_s[...] = end_im[N_CHAIN - 1:N_CHAIN, :]
            stre_ref[p] = end_re[N_CHAIN - 1:N_CHAIN, :]
            stim_ref[p] = end_im[N_CHAIN - 1:N_CHAIN, :]
        else:
            stre_ref[p] = end_re
            stim_ref[p] = end_im

    def finish(p, x, proj):
        hbuf, res, xpbuf = lanes[p][1], lanes[p][2], lanes[p][3]
        u_s = proj[:, :SSM_WIDTH]
        u_p = proj[:, SSM_WIDTH:SSM_WIDTH + POOL_WIDTH]
        gate_s = proj[:, SSM_WIDTH + POOL_WIDTH:SSM_WIDTH + POOL_WIDTH + D_MODEL]
        gate_p = proj[:, SSM_WIDTH + POOL_WIDTH + D_MODEL:]
        ys = []
        for h in range(2):
            c0 = h * N_FLAT
            h_re = hbuf[:, c0:c0 + HALF_FLAT].astype(BF16)
            h_im = hbuf[:, c0 + HALF_FLAT:c0 + N_FLAT].astype(BF16)
            ys.append(jnp.dot(h_re, wcre[h], preferred_element_type=F32)
                      - jnp.dot(h_im, wcim[h], preferred_element_type=F32))
        y = jnp.concatenate(ys, axis=1) + dsk[...] * u_s
        g = _gelu_tanh(y).astype(BF16)
        glu = jnp.dot(g, wglu[...], preferred_element_type=F32)
        o_s = glu[:, :D_MODEL] * _sigmoid(glu[:, D_MODEL:])

        xpbuf[HIST_ROWS:HIST_ROWS + TM, :] = u_p
        tail = u_p[TM - HIST_ROWS:, :]
        row = lax.broadcasted_iota(jnp.int32, (TM, 1), 0)
        if is_prompt:
            pcarry = lanes[p][10]
            first_chain = (lax.broadcasted_iota(jnp.int32, (HIST_ROWS, POOL_WIDTH), 0) % N_CHAIN) == 0
            carried = jnp.where(tile_in_stream == 0, 0.0, pcarry[...])
            xpbuf[0:HIST_ROWS, :] = jnp.where(first_chain, carried, pltpu.roll(tail, 1, 0))
            new_carry = pltpu.roll(tail, HIST_ROWS - (N_CHAIN - 1), 0)
            pcarry[...] = new_carry
            hist_ref[p] = new_carry
            pos1 = tile_in_stream * TM + CHAIN_LEN * (row % N_CHAIN) + row // N_CHAIN + 1
        else:
            xpbuf[0:HIST_ROWS, :] = ins[3][p]
            hist_ref[p] = tail
            pos1 = PAST_LEN + row // N_CHAIN + 1

        o_ps = []
        for gi, w in enumerate(POOL_WINDOWS):
            c0 = gi * POOL_GROUP_IN
            acc = xpbuf[HIST_ROWS:HIST_ROWS + TM, c0:c0 + POOL_GROUP_IN]
            for k in range(1, w):
                acc = acc + xpbuf[HIST_ROWS - N_CHAIN * k:HIST_ROWS - N_CHAIN * k + TM, c0:c0 + POOL_GROUP_IN]
            cnt = jnp.minimum(w, pos1).astype(F32)
            pooled = acc / cnt
            z = (pooled - u_p[:, c0:c0 + POOL_GROUP_IN]).astype(BF16)
            o_ps.append(jnp.dot(z, poolw[gi], preferred_element_type=F32))
        o_p = jnp.concatenate(o_ps, axis=1) * pscale[...]

        merged = (_sigmoid(gate_s) * o_s + _sigmoid(gate_p) * o_p).astype(BF16)
        x2 = x + jnp.dot(merged, wout[...], preferred_element_type=F32)
        for cb in range(n_lane_blocks):
            res[cb] = x2[:, cb * LANES:(cb + 1) * LANES]
        for j in range(N_CHAIN):
            for cb in range(n_lane_blocks):
                x2_ref[p, CHAIN_LEN * j:CHAIN_LEN * (j + 1), cb * LANES:(cb + 1) * LANES] = res[
                    cb, pl.ds(j, CHAIN_LEN, stride=N_CHAIN), :]

    projected = [None] * N_PAR
    for k in range(N_PAR + 2):
        if 0 <= k - 2 < N_PAR:
            finish(k - 2, *projected[k - 2])
        if 0 <= k - 1 < N_PAR:
            recur(k - 1)
        if k < N_PAR:
            projected[k] = project(k)


def _const_spec(shape):
    nd = len(shape)
    return pl.BlockSpec(shape, lambda i, _nd=nd: (0,) * _nd)


def _mixer_weight_specs():
    return [
        _const_spec((1, D_MODEL)),
        _const_spec((D_MODEL, 3 * D_MODEL)),
        _const_spec((2, SSM_WIDTH // 2, N_FLAT)),
        _const_spec((2, HALF_FLAT, SSM_WIDTH // 2)),
        _const_spec((2, HALF_FLAT, SSM_WIDTH // 2)),
        _const_spec((1, SSM_WIDTH)),
        _const_spec((SSM_WIDTH, 2 * D_MODEL)),
        _const_spec((len(POOL_WINDOWS), POOL_GROUP_IN, POOL_GROUP_OUT)),
        _const_spec((1, D_MODEL)),
        _const_spec((D_MODEL, D_MODEL)),
        _const_spec((4, N_FLAT)),
    ]


def _mixer_common_scratch():
    return [
        pltpu.VMEM((D_MODEL // LANES, TM, LANES), F32),
        pltpu.VMEM((TM, 2 * N_FLAT), F32),
        pltpu.VMEM((D_MODEL // LANES, TM, LANES), F32),
        pltpu.VMEM((HIST_ROWS + TM, POOL_WIDTH), F32),
    ]


def _mixer_prompt(x, weights):
    n_streams, seq, _ = x.shape
    assert n_streams % N_PAR == 0 and seq % TM == 0
    tiles_per_stream = seq // TM
    blk = lambda shape: pl.BlockSpec(shape, lambda i: (i // tiles_per_stream, 0, 0))
    row_spec = pl.BlockSpec((N_PAR, TM, D_MODEL), lambda i: (i // tiles_per_stream, i % tiles_per_stream, 0))
    lane_scratch = _mixer_common_scratch() + [
        pltpu.VMEM((N_CHAIN, N_FLAT), F32), pltpu.VMEM((N_CHAIN, N_FLAT), F32),
        pltpu.VMEM((N_CHAIN, N_FLAT), F32), pltpu.VMEM((N_CHAIN, N_FLAT), F32),
        pltpu.VMEM((1, N_FLAT), F32), pltpu.VMEM((1, N_FLAT), F32),
        pltpu.VMEM((HIST_ROWS, POOL_WIDTH), F32),
    ]
    return pl.pallas_call(
        functools.partial(_mixer_kernel, True, tiles_per_stream),
        grid=(n_streams // N_PAR * tiles_per_stream,),
        in_specs=[row_spec] + _mixer_weight_specs(),
        out_specs=[row_spec, blk((N_PAR, 1, N_FLAT)), blk((N_PAR, 1, N_FLAT)),
                   blk((N_PAR, HIST_ROWS, POOL_WIDTH))],
        out_shape=[jax.ShapeDtypeStruct((n_streams, seq, D_MODEL), F32),
                   jax.ShapeDtypeStruct((n_streams, 1, N_FLAT), F32),
                   jax.ShapeDtypeStruct((n_streams, 1, N_FLAT), F32),
                   jax.ShapeDtypeStruct((n_streams, HIST_ROWS, POOL_WIDTH), F32)],
        scratch_shapes=lane_scratch * N_PAR,
        compiler_params=pltpu.CompilerParams(dimension_semantics=("arbitrary",), vmem_limit_bytes=VMEM_LIMIT),
        name="mixer_prompt",
    )(x, *weights)


def _mixer_sample(x, h0_re, h0_im, cache_t, weights):
    n_streams, seq, _ = x.shape
    assert seq == CHAIN_LEN and n_streams % (N_CHAIN * N_PAR) == 0
    n_tiles = n_streams // N_CHAIN
    blk = lambda shape: pl.BlockSpec(shape, lambda i: (i, 0, 0))
    row_spec = blk((N_PAR, TM, D_MODEL))
    st_spec = blk((N_PAR, N_CHAIN, N_FLAT))
    hist_spec = blk((N_PAR, HIST_ROWS, POOL_WIDTH))
    tiles = lambda v: v.reshape((n_tiles, -1) + v.shape[-1:])
    return pl.pallas_call(
        functools.partial(_mixer_kernel, False, 1),
        grid=(n_tiles // N_PAR,),
        in_specs=[row_spec, st_spec, st_spec, hist_spec] + _mixer_weight_specs(),
        out_specs=[row_spec, st_spec, st_spec, hist_spec],
        out_shape=[jax.ShapeDtypeStruct((n_tiles, TM, D_MODEL), F32),
                   jax.ShapeDtypeStruct((n_tiles, N_CHAIN, N_FLAT), F32),
                   jax.ShapeDtypeStruct((n_tiles, N_CHAIN, N_FLAT), F32),
                   jax.ShapeDtypeStruct((n_tiles, HIST_ROWS, POOL_WIDTH), F32)],
        scratch_shapes=_mixer_common_scratch() * N_PAR,
        compiler_params=pltpu.CompilerParams(dimension_semantics=("arbitrary",), vmem_limit_bytes=VMEM_LIMIT),
        name="mixer_sample",
    )(tiles(x), tiles(h0_re), tiles(h0_im), cache_t, *weights)


def _to_chunks(lo, hi):
    n = lo.shape[0] // CHUNK
    half = D_MODEL // 2
    both = jnp.concatenate([lo.reshape(n, CHUNK, half), hi.reshape(n, CHUNK, half)], axis=1)
    return both.astype(BF16)


def _from_chunks(blk):
    n = blk.shape[0]
    half = D_MODEL // 2
    f = blk.astype(F32)
    lo = f[:, :CHUNK, :].reshape(n * CHUNK, half).astype(BF16)
    hi = f[:, CHUNK:, :].reshape(n * CHUNK, half).astype(BF16)
    return lo, hi


def _route_kernel(n_prompt_steps, x2p_ref, x2s_ref, g_ref, wr_ref, ltri_ref, utri_ref,
                  xs_ref, route_ref, nch_ref):
    is_prompt = pl.program_id(0) < n_prompt_steps
    tiles = range(STEP_TILES)
    each = lambda fn, *cols: [fn(*(c[h] for c in cols)) for h in tiles]
    row_max = lambda v: jnp.max(v, axis=1, keepdims=True)
    row_min = lambda v: jnp.min(v, axis=1, keepdims=True)
    row_sum = lambda v: jnp.sum(v, axis=1, keepdims=True)

    lane = lax.broadcasted_iota(jnp.int32, (TM, LANES), 1)
    lane_f = lane.astype(F32)
    big = jnp.float32(1 << 20)
    neg = jnp.float32(-jnp.inf)
    gmask = lane < N_EXPERT_GROUPS
    eid = lane - N_EXPERT_GROUPS
    lane_grp = (eid >> 3).astype(F32)
    is_expert = (eid >= 0) & (eid < N_EXPERTS)

    xn = [_rms(jnp.where(is_prompt, x2p_ref[pl.ds(h * TM, TM), :], x2s_ref[pl.ds(h * TM, TM), :]),
               g_ref[...]).astype(BF16) for h in tiles]
    logits = each(lambda v: jnp.dot(v, wr_ref[...], preferred_element_type=F32), xn)
    m = each(lambda lg: row_max(jnp.where(gmask, lg, neg)), logits)
    grp = each(lambda lg, mm: row_min(jnp.where(gmask & (lg == mm), lane_f, big)), logits, m)
    wg = each(lambda lg, mm: 1.0 / row_sum(jnp.where(gmask, jnp.exp(lg - mm), 0.0)), logits, m)
    emask = each(lambda g: is_expert & (lane_grp == g), grp)
    v1 = each(lambda lg, em: row_max(jnp.where(em, lg, neg)), logits, emask)
    i1 = each(lambda lg, em, v: row_min(jnp.where(em & (lg == v), lane_f, big)), logits, emask, v1)
    emask2 = each(lambda em, i: em & (lane_f != i), emask, i1)
    v2 = each(lambda lg, em: row_max(jnp.where(em, lg, neg)), logits, emask2)
    i2 = each(lambda lg, em, v: row_min(jnp.where(em & (lg == v), lane_f, big)), logits, emask2, v2)
    e21 = each(lambda a, b: jnp.exp(b - a), v1, v2)
    w1 = each(lambda g, e: g / (1.0 + e), wg, e21)
    w2 = each(lambda g, e: g * e / (1.0 + e), wg, e21)

    a1 = each(lambda i: lane_f == (i - N_EXPERT_GROUPS), i1)
    a2 = each(lambda i: lane_f == (i - N_EXPERT_GROUPS), i2)
    a = each(lambda p, q: (p | q).astype(F32), a1, a2)
    before = each(lambda v: jnp.dot(ltri_ref[...], v.astype(BF16), preferred_element_type=F32), a)
    cnt = each(lambda v: jnp.sum(v, axis=0, keepdims=True), a)
    nch16 = each(lambda c: jnp.broadcast_to(jnp.floor((c + (CHUNK - 1)) * (1.0 / CHUNK)), (16, LANES)), cnt)
    start = each(lambda n: jnp.dot(n.astype(BF16), utri_ref[...], preferred_element_type=F32), nch16)
    slot = each(lambda bf, st: bf + CHUNK * st[0:1, :], before, start)
    d1 = each(lambda p, sl: row_sum(jnp.where(p, sl, 0.0)), a1, slot)
    d2 = each(lambda p, sl: row_sum(jnp.where(p, sl, 0.0)), a2, slot)
    route = each(lambda p, q, u, v: jnp.where(lane == 0, p, jnp.where(lane == 1, q, jnp.where(
        lane == 2, u, jnp.where(lane == 3, v, 0.0)))), d1, d2, w1, w2)

    dest = lax.broadcasted_iota(jnp.int32, (CAP_ROWS, TM), 0)
    half = D_MODEL // 2
    for h in tiles:
        route_ref[pl.ds(h * TM, TM), :] = route[h]
        nch_ref[h] = nch16[h][0:8, :].astype(jnp.int32)
        dt = jnp.transpose(jnp.where(lane < 2, route[h], 0.0)).astype(jnp.int32)
        perm = ((dest == dt[0:1, :]) | (dest == dt[1:2, :])).astype(F32).astype(BF16)
        lo = jnp.dot(perm, xn[h][:, :half], preferred_element_type=F32)
        hi = jnp.dot(perm, xn[h][:, half:], preferred_element_type=F32)
        xs_ref[pl.ds(h * CAP_CHUNKS, CAP_CHUNKS)] = _to_chunks(lo, hi)


def _route(x2p, x2s, g_ffn, w_router):
    n_prompt_tiles = x2p.shape[0] // TM
    n_tiles = n_prompt_tiles + x2s.shape[0] // TM
    assert n_prompt_tiles % STEP_TILES == 0 and n_tiles % STEP_TILES == 0
    n_prompt_steps = n_prompt_tiles // STEP_TILES
    rows = STEP_TILES * TM
    r = jnp.arange(TM)
    ltri = (r[None, :] < r[:, None]).astype(BF16)
    e = jnp.arange(LANES)
    utri = (e[:, None] < e[None, :]).astype(BF16)
    return pl.pallas_call(
        functools.partial(_route_kernel, n_prompt_steps),
        grid=(n_tiles // STEP_TILES,),
        in_specs=[pl.BlockSpec((rows, D_MODEL), lambda i: (jnp.minimum(i, n_prompt_steps - 1), 0)),
                  pl.BlockSpec((rows, D_MODEL), lambda i: (jnp.maximum(i - n_prompt_steps, 0), 0)),
                  _const_spec((1, D_MODEL)), _const_spec((D_MODEL, LANES)),
                  _const_spec((TM, TM)), _const_spec((LANES, LANES))],
        out_specs=[pl.BlockSpec((STEP_TILES * CAP_CHUNKS, 2 * CHUNK, D_MODEL // 2), lambda i: (i, 0, 0)),
                   pl.BlockSpec((rows, LANES), lambda i: (i, 0)),
                   pl.BlockSpec((STEP_TILES, 8, LANES), lambda i: (i, 0, 0))],
        out_shape=[jax.ShapeDtypeStruct((n_tiles * CAP_CHUNKS, 2 * CHUNK, D_MODEL // 2), BF16),
                   jax.ShapeDtypeStruct((n_tiles * TM, LANES), F32),
                   jax.ShapeDtypeStruct((n_tiles, 8, LANES), jnp.int32)],
        compiler_params=pltpu.CompilerParams(dimension_semantics=("arbitrary",), vmem_limit_bytes=VMEM_LIMIT),
        name="route_sort",
    )(x2p, x2s, g_ffn, w_router, ltri, utri)


def _spare_chunk(k):
    return (1 + k // N_SPARE_SLOTS) * CAP_CHUNKS + MAX_USED_CHUNKS + k % N_SPARE_SLOTS


def _chunk_tables(nch, n_expert_tiles):
    n_tiles = nch.shape[0]
    i32 = jnp.int32
    start = jnp.cumsum(nch, axis=1) - nch
    off = jnp.cumsum(nch, axis=0) - nch
    per_expert = jnp.sum(nch, axis=0)
    tiles_e = (per_expert + TILE_CHUNKS - 1) // TILE_CHUNKS
    cum_tiles = jnp.cumsum(tiles_e)
    first_tile = cum_tiles - tiles_e
    n_active = cum_tiles[-1].astype(i32)
    t = jnp.arange(n_expert_tiles, dtype=i32)
    te = jnp.minimum(jnp.sum((t[:, None] >= cum_tiles[None, :]).astype(i32), axis=1), N_EXPERTS - 1)
    onehot = te[:, None] == jnp.arange(N_EXPERTS, dtype=i32)[None, :]
    pick = lambda tab: jnp.sum(jnp.where(onehot[:, :, None], tab.T[None, :, :], 0), axis=1)
    off_t, nch_t, start_t = pick(off), pick(nch), pick(start)
    k = t - jnp.sum(jnp.where(onehot, first_tile[None, :], 0), axis=1)
    q = (TILE_CHUNKS * k)[:, None] + jnp.arange(TILE_CHUNKS, dtype=i32)[None, :]
    in_run = (off_t[:, None, :] <= q[:, :, None]) & (q[:, :, None] < (off_t + nch_t)[:, None, :])
    run_src = (jnp.arange(n_tiles, dtype=i32) * CAP_CHUNKS)[None, :] + start_t - off_t
    src = jnp.sum(jnp.where(in_run, run_src[:, None, :], 0), axis=-1) + q
    valid = jnp.any(in_run, axis=-1) & (t < n_active)[:, None]
    src = jnp.where(valid, src, ZERO_CHUNK).astype(i32)
    spare = _spare_chunk((t % N_RING)[:, None] * TILE_CHUNKS + jnp.arange(TILE_CHUNKS, dtype=i32)[None, :])
    dst = jnp.where(valid, src, spare).astype(i32)
    dst = jnp.concatenate([spare[1:N_RING], dst], axis=0)
    return first_tile.astype(i32), tiles_e.astype(i32), src.reshape(-1), dst.reshape(-1), n_active.reshape(1)


N_RING = 5


def _expert_kernel(first_ref, ntile_ref, src_ref, dst_ref, nact_ref, xs_hbm, wg_ref, wu_ref, wd_ref, ys_hbm,
                   *scratch):
    xbufs = scratch[:N_RING]
    obufs = scratch[N_RING:2 * N_RING]
    gsem, ssem, wg16, wu16, wd16 = scratch[2 * N_RING:]
    e = pl.program_id(0)
    n_active = nact_ref[0]
    ahead = N_RING - 1

    def gather_copy(tile, slot, c):
        return pltpu.make_async_copy(xs_hbm.at[src_ref[tile * TILE_CHUNKS + c]], xbufs[slot].at[c],
                                     gsem.at[slot])

    def scatter_copy(tile, slot, c):
        return pltpu.make_async_copy(obufs[slot].at[c], ys_hbm.at[dst_ref[(tile + ahead) * TILE_CHUNKS + c]],
                                     ssem.at[slot])

    def start_all(copy, tile, slot):
        for c in range(TILE_CHUNKS):
            copy(tile, slot, c).start()

    def wait_all(copy, tile, slot):
        for c in range(TILE_CHUNKS):
            copy(tile, slot, c).wait()

    @pl.when(e == 0)
    def _():
        for v in range(ahead):
            start_all(gather_copy, v, v)
        for u in range(-ahead, 0):
            obufs[u % N_RING][...] = jnp.zeros_like(obufs[u % N_RING])
        for u in range(-ahead, -1):
            start_all(scatter_copy, u, u % N_RING)

    def do_tile(tile, slot, cast_weights):
        nxt = (slot + ahead) % N_RING
        old = (slot + 1) % N_RING
        wait_all(gather_copy, tile, slot)
        lo, hi = _from_chunks(xbufs[slot][...])
        x = jnp.concatenate([lo, hi], axis=1)
        if cast_weights:
            wg16[...] = wg_ref[0].astype(BF16)
        gate = jnp.dot(x, wg16[...], preferred_element_type=F32)
        start_all(scatter_copy, tile - 1, nxt)
        if cast_weights:
            wu16[...] = wu_ref[0].astype(BF16)
        up = jnp.dot(x, wu16[...], preferred_element_type=F32)
        hmid = (gate * _sigmoid(gate) * up).astype(BF16)
        start_all(gather_copy, tile + ahead, nxt)
        if cast_weights:
            wd16[...] = wd_ref[0].astype(BF16)
        y = jnp.dot(hmid, wd16[...], preferred_element_type=F32)
        half = D_MODEL // 2
        wait_all(scatter_copy, tile - ahead, old)
        obufs[slot][...] = _to_chunks(y[:, :half], y[:, half:])

    def tile_body(k, carry, cast_weights=False):
        tile = first_ref[e] + k
        for slot in range(N_RING):
            @pl.when(tile % N_RING == slot)
            def _():
                do_tile(tile, slot, cast_weights)
        return carry

    @pl.when(ntile_ref[e] > 0)
    def _():
        tile_body(0, 0, cast_weights=True)

    lax.fori_loop(1, ntile_ref[e], tile_body, 0)

    @pl.when(e == pl.num_programs(0) - 1)
    def _():
        last = n_active - 1
        for slot in range(N_RING):
            @pl.when(last % N_RING == slot)
            def _():
                for d in range(N_RING - 2, 0, -1):
                    wait_all(scatter_copy, last - d, (slot - d) % N_RING)
                start_all(scatter_copy, last, slot)
                wait_all(scatter_copy, last, slot)
                for d in range(1, N_RING):
                    wait_all(gather_copy, last + d, (slot + d) % N_RING)


def _experts(first_tile, tiles_e, src, dst, n_active, xs, w_gate, w_up, w_down):
    chunk_shape = (2 * CHUNK, D_MODEL // 2)
    assert _spare_chunk(N_RING * TILE_CHUNKS - 1) < xs.shape[0]
    tile_buf = pltpu.VMEM((TILE_CHUNKS,) + chunk_shape, BF16)
    grid_spec = pltpu.PrefetchScalarGridSpec(
        num_scalar_prefetch=5,
        grid=(N_EXPERTS,),
        in_specs=[pl.BlockSpec(memory_space=pl.ANY),
                  pl.BlockSpec((1, D_MODEL, D_EXPERT), lambda e, *_: (e, 0, 0)),
                  pl.BlockSpec((1, D_MODEL, D_EXPERT), lambda e, *_: (e, 0, 0)),
                  pl.BlockSpec((1, D_EXPERT, D_MODEL), lambda e, *_: (e, 0, 0))],
        out_specs=pl.BlockSpec(memory_space=pl.ANY),
        scratch_shapes=[tile_buf] * (2 * N_RING) + [
                        pltpu.SemaphoreType.DMA((N_RING,)),
                        pltpu.SemaphoreType.DMA((N_RING,)),
                        pltpu.VMEM((D_MODEL, D_EXPERT), BF16),
                        pltpu.VMEM((D_MODEL, D_EXPERT), BF16),
                        pltpu.VMEM((D_EXPERT, D_MODEL), BF16)],
    )
    return pl.pallas_call(
        _expert_kernel,
        grid_spec=grid_spec,
        out_shape=jax.ShapeDtypeStruct(xs.shape, xs.dtype),
        input_output_aliases={5: 0},
        compiler_params=pltpu.CompilerParams(dimension_semantics=("arbitrary",), vmem_limit_bytes=VMEM_LIMIT),
        name="expert_ffn",
    )(first_tile, tiles_e, src, dst, n_active, xs, w_gate, w_up, w_down)


def _combine_kernel(n_prompt_steps, x2p_ref, x2s_ref, route_ref, gfin_ref, ys_ref, outp_ref, outs_ref):
    i = pl.program_id(0)
    is_prompt = i < n_prompt_steps
    outs = []
    for h in range(STEP_TILES):
        rows = pl.ds(h * TM, TM)
        lo, hi = _from_chunks(ys_ref[pl.ds(h * CAP_CHUNKS, CAP_CHUNKS)])
        route = route_ref[rows, :]
        d1 = route[:, 0:1].astype(jnp.int32)
        d2 = route[:, 1:2].astype(jnp.int32)
        w1 = route[:, 2:3]
        w2 = route[:, 3:4]
        dest = lax.broadcasted_iota(jnp.int32, (TM, CAP_ROWS), 1)
        sel = jnp.where(dest == d1, w1, jnp.where(dest == d2, w2, 0.0)).astype(BF16)
        moe = jnp.concatenate([jnp.dot(sel, lo, preferred_element_type=F32),
                               jnp.dot(sel, hi, preferred_element_type=F32)], axis=1)
        x = jnp.where(is_prompt, x2p_ref[rows, :], x2s_ref[rows, :])
        outs.append(_rms(x + moe, gfin_ref[...]))

    @pl.when(is_prompt)
    def _():
        for h in range(STEP_TILES):
            outp_ref[pl.ds(h * TM, TM), :] = outs[h]

    @pl.when(jnp.logical_not(is_prompt))
    def _():
        for h in range(STEP_TILES):
            outs_ref[pl.ds(h * TM, TM), :] = outs[h]


def _combine(x2p, x2s, route, g_final, ys):
    n_prompt_tiles = x2p.shape[0] // TM
    n_tiles = n_prompt_tiles + x2s.shape[0] // TM
    assert n_prompt_tiles % STEP_TILES == 0 and n_tiles % STEP_TILES == 0
    n_prompt_steps = n_prompt_tiles // STEP_TILES
    rows = STEP_TILES * TM
    p_spec = pl.BlockSpec((rows, D_MODEL), lambda i: (jnp.minimum(i, n_prompt_steps - 1), 0))
    s_spec = pl.BlockSpec((rows, D_MODEL), lambda i: (jnp.maximum(i - n_prompt_steps, 0), 0))
    return pl.pallas_call(
        functools.partial(_combine_kernel, n_prompt_steps),
        grid=(n_tiles // STEP_TILES,),
        in_specs=[p_spec, s_spec,
                  pl.BlockSpec((rows, LANES), lambda i: (i, 0)),
                  _const_spec((1, D_MODEL)),
                  pl.BlockSpec((STEP_TILES * CAP_CHUNKS, 2 * CHUNK, D_MODEL // 2), lambda i: (i, 0, 0))],
        out_specs=[p_spec, s_spec],
        out_shape=[jax.ShapeDtypeStruct(x2p.shape, F32), jax.ShapeDtypeStruct(x2s.shape, F32)],
        compiler_params=pltpu.CompilerParams(dimension_semantics=("arbitrary",), vmem_limit_bytes=VMEM_LIMIT),
        name="combine_norm",
    )(x2p, x2s, route, g_final, ys)


def kernel(x_prompt, x_sample, state_ssm_re, state_ssm_im, cache_pool, g_mix, w_in, ssm_a_re, ssm_a_im,
           ssm_log_dt, ssm_b_re, ssm_b_im, ssm_c_re, ssm_c_im, ssm_d, w_glu_a, w_glu_b, pool_w, pool_scale,
           w_out, g_ffn, w_router_group, w_router_expert, w_exp_gate, w_exp_up, w_exp_down, g_final):
    li = 0
    n_pb, seq_p, _ = x_prompt.shape
    n_sb, seq_s, _ = x_sample.shape

    a_tab, wb, wc_re, wc_im = _discretise(
        ssm_a_re[li], ssm_a_im[li], ssm_log_dt[li], ssm_b_re[li], ssm_b_im[li], ssm_c_re[li], ssm_c_im[li])
    weights = [
        g_mix[li].reshape(1, D_MODEL),
        w_in[li].astype(BF16),
        wb,
        wc_re,
        wc_im,
        ssm_d[li].reshape(1, SSM_WIDTH),
        jnp.concatenate([w_glu_a[li], w_glu_b[li]], axis=1).astype(BF16),
        pool_w[li].astype(BF16),
        pool_scale[li].reshape(1, D_MODEL),
        w_out[li].astype(BF16),
        a_tab,
    ]

    x2p, stp_re, stp_im, histp = _mixer_prompt(x_prompt, weights)
    x2p = x2p.reshape(n_pb * seq_p, D_MODEL)

    n_stiles = n_sb // N_CHAIN
    cache16 = jnp.pad(cache_pool[li], ((0, 0), (1, 0), (0, 0)))
    cache_t = cache16.reshape(n_stiles, N_CHAIN, 16, POOL_WIDTH).transpose(0, 2, 1, 3).reshape(
        n_stiles, HIST_ROWS, POOL_WIDTH)
    x2s, sts_re, sts_im, hists = _mixer_sample(
        x_sample, state_ssm_re[li].reshape(n_sb, N_FLAT), state_ssm_im[li].reshape(n_sb, N_FLAT),
        cache_t, weights)
    x2s = x2s.reshape(n_sb * seq_s, D_MODEL)

    w_router = jnp.concatenate(
        [w_router_group[li], w_router_expert[li].reshape(D_MODEL, N_EXPERTS),
         jnp.zeros((D_MODEL, LANES - N_EXPERT_GROUPS - N_EXPERTS), F32)], axis=1).astype(BF16)
    xs, route, nch = _route(x2p, x2s, g_ffn[li].reshape(1, D_MODEL), w_router)

    n_tiles = (x2p.shape[0] + x2s.shape[0]) // TM
    max_chunks = n_tiles * (2 * TM // CHUNK + N_EXPERTS)
    n_expert_tiles = max_chunks // TILE_CHUNKS + N_EXPERTS
    first_tile, tiles_e, src, dst, n_active = _chunk_tables(nch[:, 0, :N_EXPERTS], n_expert_tiles)
    ys = _experts(first_tile, tiles_e, src, dst, n_active, xs, w_exp_gate[li], w_exp_up[li], w_exp_down[li])
    yp, ysm = _combine(x2p, x2s, route, g_final.reshape(1, D_MODEL), ys)

    sd = state_ssm_re.dtype
    cd = cache_pool.dtype
    y_prompt = yp.reshape(n_pb, seq_p, D_MODEL)
    y_sample = ysm.reshape(n_sb, seq_s, D_MODEL)
    re_p = stp_re.reshape(1, n_pb, SSM_GROUPS, SSM_STATE).astype(sd)
    im_p = stp_im.reshape(1, n_pb, SSM_GROUPS, SSM_STATE).astype(sd)
    hist_p = histp[:, ::N_CHAIN, :][:, 1:, :].reshape(1, n_pb, POOL_HIST, POOL_WIDTH).astype(cd)
    re_s = sts_re.reshape(1, n_sb, SSM_GROUPS, SSM_STATE).astype(sd)
    im_s = sts_im.reshape(1, n_sb, SSM_GROUPS, SSM_STATE).astype(sd)
    hist_s = hists.reshape(n_stiles, 16, N_CHAIN, POOL_WIDTH).transpose(0, 2, 1, 3).reshape(
        n_sb, 16, POOL_WIDTH)[:, 1:, :].reshape(1, n_sb, POOL_HIST, POOL_WIDTH).astype(cd)
    return (y_prompt, y_sample, re_p, im_p, hist_p, re_s, im_s, hist_s)
```

```python
import functools
import math

import jax
import jax.numpy as jnp
from jax import lax
from jax.experimental import pallas as pl
from jax.experimental.pallas import tpu as pltpu

F32 = jnp.float32
BF16 = jnp.bfloat16

D_MODEL = 1024
SSM_WIDTH = 512
SSM_GROUPS = 32
SSM_GROUP = 16
SSM_STATE = 64
N_FLAT = SSM_GROUPS * SSM_STATE
HALF_FLAT = N_FLAT // 2
POOL_WIDTH = 512
POOL_WINDOWS = (2, 4, 8, 16)
POOL_GROUP_IN = 128
POOL_GROUP_OUT = 256
POOL_HIST = 15
N_EXPERTS = 32
EXPERTS_PER_GROUP = 8
N_EXPERT_GROUPS = 4
D_EXPERT = 512
EPS = 1e-6
PAST_LEN = 1024

TM = 256
N_CHAIN = 8
CHAIN_LEN = TM // N_CHAIN
HIST_ROWS = 16 * N_CHAIN

CHUNK = 8
TILE_CHUNKS = TM // CHUNK
CAP_CHUNKS = 96
CAP_ROWS = CAP_CHUNKS * CHUNK
MAX_USED_CHUNKS = 2 * TM // CHUNK + N_EXPERTS * (CHUNK - 1) // CHUNK
N_SPARE_SLOTS = CAP_CHUNKS - MAX_USED_CHUNKS - 1
ZERO_CHUNK = CAP_CHUNKS - 1
LANES = 128

V7X_VMEM_BYTES = 64 * 1024 * 1024
VMEM_LIMIT = V7X_VMEM_BYTES * 13 // 16
STEP_TILES = 4


def _rms(x, g):
    r = lax.rsqrt(jnp.mean(x * x, axis=-1, keepdims=True) + EPS)
    return x * r * g


def _sigmoid(x):
    return 0.5 * jnp.tanh(0.5 * x) + 0.5


def _gelu_tanh(x):
    c = math.sqrt(2.0 / math.pi)
    return x * (0.5 * (1.0 + jnp.tanh(c * (x + 0.044715 * (x * x * x)))))


def _disc_kernel(lre_ref, lim_ref, ldt_ref, bre_ref, bim_ref, cre_ref, cim_ref,
                 tab_ref, wb_ref, wcre_ref, wcim_ref):
    lam_re = jnp.minimum(lre_ref[...], -1e-4)
    lam_im = lim_ref[...]
    dt = jnp.exp(ldt_ref[...])
    mag = jnp.exp(lam_re * dt)
    ang = lam_im * dt
    a_re = mag * jnp.cos(ang)
    a_im = mag * jnp.sin(ang)
    num_re = a_re - 1.0
    num_im = a_im
    den = lam_re * lam_re + lam_im * lam_im
    k_re = ((num_re * lam_re + num_im * lam_im) / den)[:, None, :]
    k_im = ((num_im * lam_re - num_re * lam_im) / den)[:, None, :]
    br = bre_ref[...]
    bi = bim_ref[...]
    bb_re = (k_re * br - k_im * bi).astype(BF16)
    bb_im = (k_re * bi + k_im * br).astype(BF16)
    pr, pi = a_re, a_im
    for _ in range(int(math.log2(CHAIN_LEN))):
        pr, pi = pr * pr - pi * pi, 2.0 * pr * pi
    wb_ref[...] = jnp.zeros_like(wb_ref)
    wcre_ref[...] = jnp.zeros_like(wcre_ref)
    wcim_ref[...] = jnp.zeros_like(wcim_ref)
    groups_per_half = SSM_GROUPS // 2
    for g in range(SSM_GROUPS):
        h, gl = divmod(g, groups_per_half)
        q0, n0 = gl * SSM_GROUP, gl * SSM_STATE
        wb_ref[h, q0:q0 + SSM_GROUP, n0:n0 + SSM_STATE] = bb_re[g]
        wb_ref[h, q0:q0 + SSM_GROUP, HALF_FLAT + n0:HALF_FLAT + n0 + SSM_STATE] = bb_im[g]
        wcre_ref[h, n0:n0 + SSM_STATE, q0:q0 + SSM_GROUP] = cre_ref[g].astype(BF16)
        wcim_ref[h, n0:n0 + SSM_STATE, q0:q0 + SSM_GROUP] = cim_ref[g].astype(BF16)
        for k, v in enumerate((a_re, a_im, pr, pi)):
            tab_ref[k:k + 1, g * SSM_STATE:(g + 1) * SSM_STATE] = v[g:g + 1, :]


def _discretise(a_re, a_im, log_dt, b_re, b_im, c_re, c_im):
    chan_major = lambda b: jnp.swapaxes(b, 1, 2)
    half_w = SSM_WIDTH // 2
    return pl.pallas_call(
        _disc_kernel,
        out_shape=[jax.ShapeDtypeStruct((4, N_FLAT), F32),
                   jax.ShapeDtypeStruct((2, half_w, 2 * HALF_FLAT), BF16),
                   jax.ShapeDtypeStruct((2, HALF_FLAT, half_w), BF16),
                   jax.ShapeDtypeStruct((2, HALF_FLAT, half_w), BF16)],
        name="s5_discretise",
    )(a_re, a_im, log_dt.reshape(SSM_GROUPS, 1), chan_major(b_re), chan_major(b_im),
      jnp.swapaxes(c_re, 1, 2), jnp.swapaxes(c_im, 1, 2))


def _scan_half(hbuf, h, ar, ai, init_re, init_im, store):
    cre = h * N_FLAT
    cim = cre + HALF_FLAT
    hr, hi = init_re, init_im
    for t in range(CHAIN_LEN):
        rows = pl.ds(N_CHAIN * t, N_CHAIN)
        br = hbuf[rows, cre:cre + HALF_FLAT]
        bi = hbuf[rows, cim:cim + HALF_FLAT]
        nr = ar * hr - ai * hi + br
        ni = ar * hi + ai * hr + bi
        if store:
            hbuf[rows, cre:cre + HALF_FLAT] = nr
            hbuf[rows, cim:cim + HALF_FLAT] = ni
        hr, hi = nr, ni
    return hr, hi


N_MIXER_WEIGHTS = 11
N_PAR = 2


def _mixer_kernel(is_prompt, tiles_per_stream, *refs):
    n_in = 1 if is_prompt else 4
    ins = refs[:n_in]
    (gmix, win, wb, wcre, wcim, dsk, wglu, poolw, pscale, wout, atab) = refs[n_in:n_in + N_MIXER_WEIGHTS]
    x2_ref, stre_ref, stim_ref, hist_ref = refs[n_in + N_MIXER_WEIGHTS:n_in + N_MIXER_WEIGHTS + 4]
    scratch = refs[n_in + N_MIXER_WEIGHTS + 4:]
    per = len(scratch) // N_PAR
    lanes = [scratch[p * per:(p + 1) * per] for p in range(N_PAR)]
    x_ref = ins[0]

    tile_in_stream = pl.program_id(0) % tiles_per_stream
    n_lane_blocks = D_MODEL // LANES
    half_w = SSM_WIDTH // 2
    ar_full = atab[0:1, :]
    ai_full = atab[1:2, :]

    if is_prompt:
        @pl.when(pl.program_id(0) == 0)
        def _():
            for p in range(N_PAR):
                cre_s, cim_s, pcarry = lanes[p][8], lanes[p][9], lanes[p][10]
                cre_s[...] = jnp.zeros_like(cre_s)
                cim_s[...] = jnp.zeros_like(cim_s)
                pcarry[...] = jnp.zeros_like(pcarry)

    def a_half(h):
        f0 = h * HALF_FLAT
        return (jnp.broadcast_to(ar_full[:, f0:f0 + HALF_FLAT], (N_CHAIN, HALF_FLAT)),
                jnp.broadcast_to(ai_full[:, f0:f0 + HALF_FLAT], (N_CHAIN, HALF_FLAT)))

    def project(p):
        xperm, hbuf = lanes[p][0], lanes[p][1]
        for j in range(N_CHAIN):
            for cb in range(n_lane_blocks):
                xperm[cb, pl.ds(j, CHAIN_LEN, stride=N_CHAIN), :] = x_ref[
                    p, CHAIN_LEN * j:CHAIN_LEN * (j + 1), cb * LANES:(cb + 1) * LANES]
        x = jnp.concatenate([xperm[cb] for cb in range(n_lane_blocks)], axis=1)
        xn = _rms(x, gmix[...]).astype(BF16)
        proj = jnp.dot(xn, win[...], preferred_element_type=F32)
        ub = proj[:, :SSM_WIDTH].astype(BF16)
        for h in range(2):
            hbuf[:, h * N_FLAT:(h + 1) * N_FLAT] = jnp.dot(
                ub[:, h * half_w:(h + 1) * half_w], wb[h], preferred_element_type=F32)
        return x, proj

    def recur(p):
        hbuf = lanes[p][1]
        if is_prompt:
            fre, fim, hre, him, cre_s, cim_s = lanes[p][4:10]
            zeros = jnp.zeros((N_CHAIN, HALF_FLAT), F32)
            for h in range(2):
                f0 = h * HALF_FLAT
                ar, ai = a_half(h)
                lr, li = _scan_half(hbuf, h, ar, ai, zeros, zeros, store=False)
                fre[:, f0:f0 + HALF_FLAT] = lr
                fim[:, f0:f0 + HALF_FLAT] = li
            fresh = tile_in_stream == 0
            hre[0:1, :] = jnp.where(fresh, 0.0, cre_s[...])
            him[0:1, :] = jnp.where(fresh, 0.0, cim_s[...])
            p_re = atab[2:3, :]
            p_im = atab[3:4, :]
            for j in range(N_CHAIN - 1):
                sr = hre[j:j + 1, :]
                si = him[j:j + 1, :]
                hre[j + 1:j + 2, :] = fre[j:j + 1, :] + p_re * sr - p_im * si
                him[j + 1:j + 2, :] = fim[j:j + 1, :] + p_re * si + p_im * sr
            init_re = hre[...]
            init_im = him[...]
        else:
            init_re = ins[1][p]
            init_im = ins[2][p]
        fin_re = []
        fin_im = []
        for h in range(2):
            f0 = h * HALF_FLAT
            ar, ai = a_half(h)
            er, ei = _scan_half(hbuf, h, ar, ai, init_re[:, f0:f0 + HALF_FLAT],
                                init_im[:, f0:f0 + HALF_FLAT], store=True)
            fin_re.append(er)
            fin_im.append(ei)
        end_re = jnp.concatenate(fin_re, axis=1)
        end_im = jnp.concatenate(fin_im, axis=1)
        if is_prompt:
            cre_s[...] = end_re[N_CHAIN - 1:N_CHAIN, :]
            cim_s[...] = end_im[N_CHAIN - 1:N_CHAIN, :]
            stre_ref[p] = end_re[N_CHAIN - 1:N_CHAIN, :]
            stim_ref[p] = end_im[N_CHAIN - 1:N_CHAIN, :]
        else:
            stre_ref[p] = end_re
            stim_ref[p] = end_im

    def finish(p, x, proj):
        hbuf, res, xpbuf = lanes[p][1], lanes[p][2], lanes[p][3]
        u_s = proj[:, :SSM_WIDTH]
        u_p = proj[:, SSM_WIDTH:SSM_WIDTH + POOL_WIDTH]
        gate_s = proj[:, SSM_WIDTH + POOL_WIDTH:SSM_WIDTH + POOL_WIDTH + D_MODEL]
        gate_p = proj[:, SSM_WIDTH + POOL_WIDTH + D_MODEL:]
        ys = []
        for h in range(2):
            c0 = h * N_FLAT
            h_re = hbuf[:, c0:c0 + HALF_FLAT].astype(BF16)
            h_im = hbuf[:, c0 + HALF_FLAT:c0 + N_FLAT].astype(BF16)
            ys.append(jnp.dot(h_re, wcre[h], preferred_element_type=F32)
                      - jnp.dot(h_im, wcim[h], preferred_element_type=F32))
        y = jnp.concatenate(ys, axis=1) + dsk[...] * u_s
        g = _gelu_tanh(y).astype(BF16)
        glu = jnp.dot(g, wglu[...], preferred_element_type=F32)
        o_s = glu[:, :D_MODEL] * _sigmoid(glu[:, D_MODEL:])

        xpbuf[HIST_ROWS:HIST_ROWS + TM, :] = u_p
        tail = u_p[TM - HIST_ROWS:, :]
        row = lax.broadcasted_iota(jnp.int32, (TM, 1), 0)
        if is_prompt:
            pcarry = lanes[p][10]
            first_chain = (lax.broadcasted_iota(jnp.int32, (HIST_ROWS, POOL_WIDTH), 0) % N_CHAIN) == 0
            carried = jnp.where(tile_in_stream == 0, 0.0, pcarry[...])
            xpbuf[0:HIST_ROWS, :] = jnp.where(first_chain, carried, pltpu.roll(tail, 1, 0))
            new_carry = pltpu.roll(tail, HIST_ROWS - (N_CHAIN - 1), 0)
            pcarry[...] = new_carry
            hist_ref[p] = new_carry
            pos1 = tile_in_stream * TM + CHAIN_LEN * (row % N_CHAIN) + row // N_CHAIN + 1
        else:
            xpbuf[0:HIST_ROWS, :] = ins[3][p]
            hist_ref[p] = tail
            pos1 = PAST_LEN + row // N_CHAIN + 1

        o_ps = []
        for gi, w in enumerate(POOL_WINDOWS):
            c0 = gi * POOL_GROUP_IN
            acc = xpbuf[HIST_ROWS:HIST_ROWS + TM, c0:c0 + POOL_GROUP_IN]
            for k in range(1, w):
                acc = acc + xpbuf[HIST_ROWS - N_CHAIN * k:HIST_ROWS - N_CHAIN * k + TM, c0:c0 + POOL_GROUP_IN]
            cnt = jnp.minimum(w, pos1).astype(F32)
            pooled = acc / cnt
            z = (pooled - u_p[:, c0:c0 + POOL_GROUP_IN]).astype(BF16)
            o_ps.append(jnp.dot(z, poolw[gi], preferred_element_type=F32))
        o_p = jnp.concatenate(o_ps, axis=1) * pscale[...]

        merged = (_sigmoid(gate_s) * o_s + _sigmoid(gate_p) * o_p).astype(BF16)
        x2 = x + jnp.dot(merged, wout[...], preferred_element_type=F32)
        for cb in range(n_lane_blocks):
            res[cb] = x2[:, cb * LANES:(cb + 1) * LANES]
        for j in range(N_CHAIN):
            for cb in range(n_lane_blocks):
                x2_ref[p, CHAIN_LEN * j:CHAIN_LEN * (j + 1), cb * LANES:(cb + 1) * LANES] = res[
                    cb, pl.ds(j, CHAIN_LEN, stride=N_CHAIN), :]

    projected = [None] * N_PAR
    for k in range(N_PAR + 2):
        if 0 <= k - 2 < N_PAR:
            finish(k - 2, *projected[k - 2])
        if 0 <= k - 1 < N_PAR:
            recur(k - 1)
        if k < N_PAR:
            projected[k] = project(k)


def _const_spec(shape):
    nd = len(shape)
    return pl.BlockSpec(shape, lambda i, _nd=nd: (0,) * _nd)


def _mixer_weight_specs():
    return [
        _const_spec((1, D_MODEL)),
        _const_spec((D_MODEL, 3 * D_MODEL)),
        _const_spec((2, SSM_WIDTH // 2, N_FLAT)),
        _const_spec((2, HALF_FLAT, SSM_WIDTH // 2)),
        _const_spec((2, HALF_FLAT, SSM_WIDTH // 2)),
        _const_spec((1, SSM_WIDTH)),
        _const_spec((SSM_WIDTH, 2 * D_MODEL)),
        _const_spec((len(POOL_WINDOWS), POOL_GROUP_IN, POOL_GROUP_OUT)),
        _const_spec((1, D_MODEL)),
        _const_spec((D_MODEL, D_MODEL)),
        _const_spec((4, N_FLAT)),
    ]


def _mixer_common_scratch():
    return [
        pltpu.VMEM((D_MODEL // LANES, TM, LANES), F32),
        pltpu.VMEM((TM, 2 * N_FLAT), F32),
        pltpu.VMEM((D_MODEL // LANES, TM, LANES), F32),
        pltpu.VMEM((HIST_ROWS + TM, POOL_WIDTH), F32),
    ]


def _mixer_prompt(x, weights):
    n_streams, seq, _ = x.shape
    assert n_streams % N_PAR == 0 and seq % TM == 0
    tiles_per_stream = seq // TM
    blk = lambda shape: pl.BlockSpec(shape, lambda i: (i // tiles_per_stream, 0, 0))
    row_spec = pl.BlockSpec((N_PAR, TM, D_MODEL), lambda i: (i // tiles_per_stream, i % tiles_per_stream, 0))
    lane_scratch = _mixer_common_scratch() + [
        pltpu.VMEM((N_CHAIN, N_FLAT), F32), pltpu.VMEM((N_CHAIN, N_FLAT), F32),
        pltpu.VMEM((N_CHAIN, N_FLAT), F32), pltpu.VMEM((N_CHAIN, N_FLAT), F32),
        pltpu.VMEM((1, N_FLAT), F32), pltpu.VMEM((1, N_FLAT), F32),
        pltpu.VMEM((HIST_ROWS, POOL_WIDTH), F32),
    ]
    return pl.pallas_call(
        functools.partial(_mixer_kernel, True, tiles_per_stream),
        grid=(n_streams // N_PAR * tiles_per_stream,),
        in_specs=[row_spec] + _mixer_weight_specs(),
        out_specs=[row_spec, blk((N_PAR, 1, N_FLAT)), blk((N_PAR, 1, N_FLAT)),
                   blk((N_PAR, HIST_ROWS, POOL_WIDTH))],
        out_shape=[jax.ShapeDtypeStruct((n_streams, seq, D_MODEL), F32),
                   jax.ShapeDtypeStruct((n_streams, 1, N_FLAT), F32),
                   jax.ShapeDtypeStruct((n_streams, 1, N_FLAT), F32),
                   jax.ShapeDtypeStruct((n_streams, HIST_ROWS, POOL_WIDTH), F32)],
        scratch_shapes=lane_scratch * N_PAR,
        compiler_params=pltpu.CompilerParams(dimension_semantics=("arbitrary",), vmem_limit_bytes=VMEM_LIMIT),
        name="mixer_prompt",
    )(x, *weights)


def _mixer_sample(x, h0_re, h0_im, cache_t, weights):
    n_streams, seq, _ = x.shape
    assert seq == CHAIN_LEN and n_streams % (N_CHAIN * N_PAR) == 0
    n_tiles = n_streams // N_CHAIN
    blk = lambda shape: pl.BlockSpec(shape, lambda i: (i, 0, 0))
    row_spec = blk((N_PAR, TM, D_MODEL))
    st_spec = blk((N_PAR, N_CHAIN, N_FLAT))
    hist_spec = blk((N_PAR, HIST_ROWS, POOL_WIDTH))
    tiles = lambda v: v.reshape((n_tiles, -1) + v.shape[-1:])
    return pl.pallas_call(
        functools.partial(_mixer_kernel, False, 1),
        grid=(n_tiles // N_PAR,),
        in_specs=[row_spec, st_spec, st_spec, hist_spec] + _mixer_weight_specs(),
        out_specs=[row_spec, st_spec, st_spec, hist_spec],
        out_shape=[jax.ShapeDtypeStruct((n_tiles, TM, D_MODEL), F32),
                   jax.ShapeDtypeStruct((n_tiles, N_CHAIN, N_FLAT), F32),
                   jax.ShapeDtypeStruct((n_tiles, N_CHAIN, N_FLAT), F32),
                   jax.ShapeDtypeStruct((n_tiles, HIST_ROWS, POOL_WIDTH), F32)],
        scratch_shapes=_mixer_common_scratch() * N_PAR,
        compiler_params=pltpu.CompilerParams(dimension_semantics=("arbitrary",), vmem_limit_bytes=VMEM_LIMIT),
        name="mixer_sample",
    )(tiles(x), tiles(h0_re), tiles(h0_im), cache_t, *weights)


def _to_chunks(lo, hi):
    n = lo.shape[0] // CHUNK
    half = D_MODEL // 2
    both = jnp.concatenate([lo.reshape(n, CHUNK, half), hi.reshape(n, CHUNK, half)], axis=1)
    return both.astype(BF16)


def _from_chunks(blk):
    n = blk.shape[0]
    half = D_MODEL // 2
    f = blk.astype(F32)
    lo = f[:, :CHUNK, :].reshape(n * CHUNK, half).astype(BF16)
    hi = f[:, CHUNK:, :].reshape(n * CHUNK, half).astype(BF16)
    return lo, hi


def _route_kernel(n_prompt_steps, x2p_ref, x2s_ref, g_ref, wr_ref, ltri_ref, utri_ref,
                  xs_ref, route_ref, nch_ref):
    is_prompt = pl.program_id(0) < n_prompt_steps
    tiles = range(STEP_TILES)
    each = lambda fn, *cols: [fn(*(c[h] for c in cols)) for h in tiles]
    row_max = lambda v: jnp.max(v, axis=1, keepdims=True)
    row_min = lambda v: jnp.min(v, axis=1, keepdims=True)
    row_sum = lambda v: jnp.sum(v, axis=1, keepdims=True)

    lane = lax.broadcasted_iota(jnp.int32, (TM, LANES), 1)
    lane_f = lane.astype(F32)
    big = jnp.float32(1 << 20)
    neg = jnp.float32(-jnp.inf)
    gmask = lane < N_EXPERT_GROUPS
    eid = lane - N_EXPERT_GROUPS
    lane_grp = (eid >> 3).astype(F32)
    is_expert = (eid >= 0) & (eid < N_EXPERTS)

    xn = [_rms(jnp.where(is_prompt, x2p_ref[pl.ds(h * TM, TM), :], x2s_ref[pl.ds(h * TM, TM), :]),
               g_ref[...]).astype(BF16) for h in tiles]
    logits = each(lambda v: jnp.dot(v, wr_ref[...], preferred_element_type=F32), xn)
    m = each(lambda lg: row_max(jnp.where(gmask, lg, neg)), logits)
    grp = each(lambda lg, mm: row_min(jnp.where(gmask & (lg == mm), lane_f, big)), logits, m)
    wg = each(lambda lg, mm: 1.0 / row_sum(jnp.where(gmask, jnp.exp(lg - mm), 0.0)), logits, m)
    emask = each(lambda g: is_expert & (lane_grp == g), grp)
    v1 = each(lambda lg, em: row_max(jnp.where(em, lg, neg)), logits, emask)
    i1 = each(lambda lg, em, v: row_min(jnp.where(em & (lg == v), lane_f, big)), logits, emask, v1)
    emask2 = each(lambda em, i: em & (lane_f != i), emask, i1)
    v2 = each(lambda lg, em: row_max(jnp.where(em, lg, neg)), logits, emask2)
    i2 = each(lambda lg, em, v: row_min(jnp.where(em & (lg == v), lane_f, big)), logits, emask2, v2)
    e21 = each(lambda a, b: jnp.exp(b - a), v1, v2)
    w1 = each(lambda g, e: g / (1.0 + e), wg, e21)
    w2 = each(lambda g, e: g * e / (1.0 + e), wg, e21)

    a1 = each(lambda i: lane_f == (i - N_EXPERT_GROUPS), i1)
    a2 = each(lambda i: lane_f == (i - N_EXPERT_GROUPS), i2)
    a = each(lambda p, q: (p | q).astype(F32), a1, a2)
    before = each(lambda v: jnp.dot(ltri_ref[...], v.astype(BF16), preferred_element_type=F32), a)
    cnt = each(lambda v: jnp.sum(v, axis=0, keepdims=True), a)
    nch16 = each(lambda c: jnp.broadcast_to(jnp.floor((c + (CHUNK - 1)) * (1.0 / CHUNK)), (16, LANES)), cnt)
    start = each(lambda n: jnp.dot(n.astype(BF16), utri_ref[...], preferred_element_type=F32), nch16)
    slot = each(lambda bf, st: bf + CHUNK * st[0:1, :], before, start)
    d1 = each(lambda p, sl: row_sum(jnp.where(p, sl, 0.0)), a1, slot)
    d2 = each(lambda p, sl: row_sum(jnp.where(p, sl, 0.0)), a2, slot)
    route = each(lambda p, q, u, v: jnp.where(lane == 0, p, jnp.where(lane == 1, q, jnp.where(
        lane == 2, u, jnp.where(lane == 3, v, 0.0)))), d1, d2, w1, w2)

    dest = lax.broadcasted_iota(jnp.int32, (CAP_ROWS, TM), 0)
    half = D_MODEL // 2
    for h in tiles:
        route_ref[pl.ds(h * TM, TM), :] = route[h]
        nch_ref[h] = nch16[h][0:8, :].astype(jnp.int32)
        dt = jnp.transpose(jnp.where(lane < 2, route[h], 0.0)).astype(jnp.int32)
        perm = ((dest == dt[0:1, :]) | (dest == dt[1:2, :])).astype(F32).astype(BF16)
        lo = jnp.dot(perm, xn[h][:, :half], preferred_element_type=F32)
        hi = jnp.dot(perm, xn[h][:, half:], preferred_element_type=F32)
        xs_ref[pl.ds(h * CAP_CHUNKS, CAP_CHUNKS)] = _to_chunks(lo, hi)


def _route(x2p, x2s, g_ffn, w_router):
    n_prompt_tiles = x2p.shape[0] // TM
    n_tiles = n_prompt_tiles + x2s.shape[0] // TM
    assert n_prompt_tiles % STEP_TILES == 0 and n_tiles % STEP_TILES == 0
    n_prompt_steps = n_prompt_tiles // STEP_TILES
    rows = STEP_TILES * TM
    r = jnp.arange(TM)
    ltri = (r[None, :] < r[:, None]).astype(BF16)
    e = jnp.arange(LANES)
    utri = (e[:, None] < e[None, :]).astype(BF16)
    return pl.pallas_call(
        functools.partial(_route_kernel, n_prompt_steps),
        grid=(n_tiles // STEP_TILES,),
        in_specs=[pl.BlockSpec((rows, D_MODEL), lambda i: (jnp.minimum(i, n_prompt_steps - 1), 0)),
                  pl.BlockSpec((rows, D_MODEL), lambda i: (jnp.maximum(i - n_prompt_steps, 0), 0)),
                  _const_spec((1, D_MODEL)), _const_spec((D_MODEL, LANES)),
                  _const_spec((TM, TM)), _const_spec((LANES, LANES))],
        out_specs=[pl.BlockSpec((STEP_TILES * CAP_CHUNKS, 2 * CHUNK, D_MODEL // 2), lambda i: (i, 0, 0)),
                   pl.BlockSpec((rows, LANES), lambda i: (i, 0)),
                   pl.BlockSpec((STEP_TILES, 8, LANES), lambda i: (i, 0, 0))],
        out_shape=[jax.ShapeDtypeStruct((n_tiles * CAP_CHUNKS, 2 * CHUNK, D_MODEL // 2), BF16),
                   jax.ShapeDtypeStruct((n_tiles * TM, LANES), F32),
                   jax.ShapeDtypeStruct((n_tiles, 8, LANES), jnp.int32)],
        compiler_params=pltpu.CompilerParams(dimension_semantics=("arbitrary",), vmem_limit_bytes=VMEM_LIMIT),
        name="route_sort",
    )(x2p, x2s, g_ffn, w_router, ltri, utri)


def _spare_chunk(k):
    return (1 + k // N_SPARE_SLOTS) * CAP_CHUNKS + MAX_USED_CHUNKS + k % N_SPARE_SLOTS


def _chunk_tables(nch, n_expert_tiles):
    n_tiles = nch.shape[0]
    i32 = jnp.int32
    start = jnp.cumsum(nch, axis=1) - nch
    off = jnp.cumsum(nch, axis=0) - nch
    per_expert = jnp.sum(nch, axis=0)
    tiles_e = (per_expert + TILE_CHUNKS - 1) // TILE_CHUNKS
    cum_tiles = jnp.cumsum(tiles_e)
    first_tile = cum_tiles - tiles_e
    n_active = cum_tiles[-1].astype(i32)
    t = jnp.arange(n_expert_tiles, dtype=i32)
    te = jnp.minimum(jnp.sum((t[:, None] >= cum_tiles[None, :]).astype(i32), axis=1), N_EXPERTS - 1)
    onehot = te[:, None] == jnp.arange(N_EXPERTS, dtype=i32)[None, :]
    pick = lambda tab: jnp.sum(jnp.where(onehot[:, :, None], tab.T[None, :, :], 0), axis=1)
    off_t, nch_t, start_t = pick(off), pick(nch), pick(start)
    k = t - jnp.sum(jnp.where(onehot, first_tile[None, :], 0), axis=1)
    q = (TILE_CHUNKS * k)[:, None] + jnp.arange(TILE_CHUNKS, dtype=i32)[None, :]
    in_run = (off_t[:, None, :] <= q[:, :, None]) & (q[:, :, None] < (off_t + nch_t)[:, None, :])
    run_src = (jnp.arange(n_tiles, dtype=i32) * CAP_CHUNKS)[None, :] + start_t - off_t
    src = jnp.sum(jnp.where(in_run, run_src[:, None, :], 0), axis=-1) + q
    valid = jnp.any(in_run, axis=-1) & (t < n_active)[:, None]
    src = jnp.where(valid, src, ZERO_CHUNK).astype(i32)
    spare = _spare_chunk((t % N_RING)[:, None] * TILE_CHUNKS + jnp.arange(TILE_CHUNKS, dtype=i32)[None, :])
    dst = jnp.where(valid, src, spare).astype(i32)
    dst = jnp.concatenate([spare[1:N_RING], dst], axis=0)
    return first_tile.astype(i32), tiles_e.astype(i32), src.reshape(-1), dst.reshape(-1), n_active.reshape(1)


N_RING = 5


def _expert_kernel(first_ref, ntile_ref, src_ref, dst_ref, nact_ref, xs_hbm, wg_ref, wu_ref, wd_ref, ys_hbm,
                   *scratch):
    xbufs = scratch[:N_RING]
    obufs = scratch[N_RING:2 * N_RING]
    gsem, ssem, wg16, wu16, wd16 = scratch[2 * N_RING:]
    e = pl.program_id(0)
    n_active = nact_ref[0]
    ahead = N_RING - 1

    def gather_copy(tile, slot, c):
        return pltpu.make_async_copy(xs_hbm.at[src_ref[tile * TILE_CHUNKS + c]], xbufs[slot].at[c],
                                     gsem.at[slot])

    def scatter_copy(tile, slot, c):
        return pltpu.make_async_copy(obufs[slot].at[c], ys_hbm.at[dst_ref[(tile + ahead) * TILE_CHUNKS + c]],
                                     ssem.at[slot])

    def start_all(copy, tile, slot):
        for c in range(TILE_CHUNKS):
            copy(tile, slot, c).start()

    def wait_all(copy, tile, slot):
        for c in range(TILE_CHUNKS):
            copy(tile, slot, c).wait()

    @pl.when(e == 0)
    def _():
        for v in range(ahead):
            start_all(gather_copy, v, v)
        for u in range(-ahead, 0):
            obufs[u % N_RING][...] = jnp.zeros_like(obufs[u % N_RING])
        for u in range(-ahead, -1):
            start_all(scatter_copy, u, u % N_RING)

    def do_tile(tile, slot, cast_weights):
        nxt = (slot + ahead) % N_RING
        old = (slot + 1) % N_RING
        wait_all(gather_copy, tile, slot)
        lo, hi = _from_chunks(xbufs[slot][...])
        x = jnp.concatenate([lo, hi], axis=1)
        if cast_weights:
            wg16[...] = wg_ref[0].astype(BF16)
        gate = jnp.dot(x, wg16[...], preferred_element_type=F32)
        start_all(scatter_copy, tile - 1, nxt)
        if cast_weights:
            wu16[...] = wu_ref[0].astype(BF16)
        up = jnp.dot(x, wu16[...], preferred_element_type=F32)
        hmid = (gate * _sigmoid(gate) * up).astype(BF16)
        start_all(gather_copy, tile + ahead, nxt)
        if cast_weights:
            wd16[...] = wd_ref[0].astype(BF16)
        y = jnp.dot(hmid, wd16[...], preferred_element_type=F32)
        half = D_MODEL // 2
        wait_all(scatter_copy, tile - ahead, old)
        obufs[slot][...] = _to_chunks(y[:, :half], y[:, half:])

    def tile_body(k, carry, cast_weights=False):
        tile = first_ref[e] + k
        for slot in range(N_RING):
            @pl.when(tile % N_RING == slot)
            def _():
                do_tile(tile, slot, cast_weights)
        return carry

    @pl.when(ntile_ref[e] > 0)
    def _():
        tile_body(0, 0, cast_weights=True)

    lax.fori_loop(1, ntile_ref[e], tile_body, 0)

    @pl.when(e == pl.num_programs(0) - 1)
    def _():
        last = n_active - 1
        for slot in range(N_RING):
            @pl.when(last % N_RING == slot)
            def _():
                for d in range(N_RING - 2, 0, -1):
                    wait_all(scatter_copy, last - d, (slot - d) % N_RING)
                start_all(scatter_copy, last, slot)
                wait_all(scatter_copy, last, slot)
                for d in range(1, N_RING):
                    wait_all(gather_copy, last + d, (slot + d) % N_RING)


def _experts(first_tile, tiles_e, src, dst, n_active, xs, w_gate, w_up, w_down):
    chunk_shape = (2 * CHUNK, D_MODEL // 2)
    assert _spare_chunk(N_RING * TILE_CHUNKS - 1) < xs.shape[0]
    tile_buf = pltpu.VMEM((TILE_CHUNKS,) + chunk_shape, BF16)
    grid_spec = pltpu.PrefetchScalarGridSpec(
        num_scalar_prefetch=5,
        grid=(N_EXPERTS,),
        in_specs=[pl.BlockSpec(memory_space=pl.ANY),
                  pl.BlockSpec((1, D_MODEL, D_EXPERT), lambda e, *_: (e, 0, 0)),
                  pl.BlockSpec((1, D_MODEL, D_EXPERT), lambda e, *_: (e, 0, 0)),
                  pl.BlockSpec((1, D_EXPERT, D_MODEL), lambda e, *_: (e, 0, 0))],
        out_specs=pl.BlockSpec(memory_space=pl.ANY),
        scratch_shapes=[tile_buf] * (2 * N_RING) + [
                        pltpu.SemaphoreType.DMA((N_RING,)),
                        pltpu.SemaphoreType.DMA((N_RING,)),
                        pltpu.VMEM((D_MODEL, D_EXPERT), BF16),
                        pltpu.VMEM((D_MODEL, D_EXPERT), BF16),
                        pltpu.VMEM((D_EXPERT, D_MODEL), BF16)],
    )
    return pl.pallas_call(
        _expert_kernel,
        grid_spec=grid_spec,
        out_shape=jax.ShapeDtypeStruct(xs.shape, xs.dtype),
        input_output_aliases={5: 0},
        compiler_params=pltpu.CompilerParams(dimension_semantics=("arbitrary",), vmem_limit_bytes=VMEM_LIMIT),
        name="expert_ffn",
    )(first_tile, tiles_e, src, dst, n_active, xs, w_gate, w_up, w_down)


def _combine_tiles(x_ref, route_ref, gfin_ref, ys_ref, out_ref):
    for h in range(STEP_TILES):
        rows = pl.ds(h * TM, TM)
        lo, hi = _from_chunks(ys_ref[pl.ds(h * CAP_CHUNKS, CAP_CHUNKS)])
        route = route_ref[rows, :]
        d1 = route[:, 0:1].astype(jnp.int32)
        d2 = route[:, 1:2].astype(jnp.int32)
        w1 = route[:, 2:3]
        w2 = route[:, 3:4]
        dest = lax.broadcasted_iota(jnp.int32, (TM, CAP_ROWS), 1)
        sel = jnp.where(dest == d1, w1, jnp.where(dest == d2, w2, 0.0)).astype(BF16)
        moe = jnp.concatenate([jnp.dot(sel, lo, preferred_element_type=F32),
                               jnp.dot(sel, hi, preferred_element_type=F32)], axis=1)
        out_ref[rows, :] = _rms(x_ref[rows, :] + moe, gfin_ref[...])


N_COMBINE_BUFFERS = 3


def _combine_kernel(n_prompt_steps, n_sample_steps, x2p_hbm, x2s_hbm, route_hbm, gfin_hbm, ys_hbm,
                    outp_hbm, outs_hbm):
    rows = STEP_TILES * TM

    def stream(n_steps, first_step, x_hbm, out_hbm):
        deep = pl.Buffered(N_COMBINE_BUFFERS) if n_steps >= N_COMBINE_BUFFERS else None
        pltpu.emit_pipeline(
            _combine_tiles,
            grid=(n_steps,),
            in_specs=[pl.BlockSpec((rows, D_MODEL), lambda i: (i, 0), pipeline_mode=deep),
                      pl.BlockSpec((rows, LANES), lambda i: (i + first_step, 0)),
                      pl.BlockSpec((1, D_MODEL), lambda i: (0, 0)),
                      pl.BlockSpec((STEP_TILES * CAP_CHUNKS, 2 * CHUNK, D_MODEL // 2),
                                   lambda i: (i + first_step, 0, 0), pipeline_mode=deep)],
            out_specs=[pl.BlockSpec((rows, D_MODEL), lambda i: (i, 0))],
        )(x_hbm, route_hbm, gfin_hbm, ys_hbm, out_hbm)

    stream(n_prompt_steps, 0, x2p_hbm, outp_hbm)
    stream(n_sample_steps, n_prompt_steps, x2s_hbm, outs_hbm)


def _combine(x2p, x2s, route, g_final, ys):
    n_prompt_tiles = x2p.shape[0] // TM
    n_sample_tiles = x2s.shape[0] // TM
    assert n_prompt_tiles % STEP_TILES == 0 and n_sample_tiles % STEP_TILES == 0
    any_spec = pl.BlockSpec(memory_space=pl.ANY)
    return pl.pallas_call(
        functools.partial(_combine_kernel, n_prompt_tiles // STEP_TILES, n_sample_tiles // STEP_TILES),
        in_specs=[any_spec] * 5,
        out_specs=[any_spec] * 2,
        out_shape=[jax.ShapeDtypeStruct(x2p.shape, F32), jax.ShapeDtypeStruct(x2s.shape, F32)],
        compiler_params=pltpu.CompilerParams(vmem_limit_bytes=VMEM_LIMIT),
        name="combine_norm",
    )(x2p, x2s, route, g_final, ys)


def kernel(x_prompt, x_sample, state_ssm_re, state_ssm_im, cache_pool, g_mix, w_in, ssm_a_re, ssm_a_im,
           ssm_log_dt, ssm_b_re, ssm_b_im, ssm_c_re, ssm_c_im, ssm_d, w_glu_a, w_glu_b, pool_w, pool_scale,
           w_out, g_ffn, w_router_group, w_router_expert, w_exp_gate, w_exp_up, w_exp_down, g_final):
    li = 0
    n_pb, seq_p, _ = x_prompt.shape
    n_sb, seq_s, _ = x_sample.shape

    a_tab, wb, wc_re, wc_im = _discretise(
        ssm_a_re[li], ssm_a_im[li], ssm_log_dt[li], ssm_b_re[li], ssm_b_im[li], ssm_c_re[li], ssm_c_im[li])
    weights = [
        g_mix[li].reshape(1, D_MODEL),
        w_in[li].astype(BF16),
        wb,
        wc_re,
        wc_im,
        ssm_d[li].reshape(1, SSM_WIDTH),
        jnp.concatenate([w_glu_a[li], w_glu_b[li]], axis=1).astype(BF16),
        pool_w[li].astype(BF16),
        pool_scale[li].reshape(1, D_MODEL),
        w_out[li].astype(BF16),
        a_tab,
    ]

    x2p, stp_re, stp_im, histp = _mixer_prompt(x_prompt, weights)
    x2p = x2p.reshape(n_pb * seq_p, D_MODEL)

    n_stiles = n_sb // N_CHAIN
    cache16 = jnp.pad(cache_pool[li], ((0, 0), (1, 0), (0, 0)))
    cache_t = cache16.reshape(n_stiles, N_CHAIN, 16, POOL_WIDTH).transpose(0, 2, 1, 3).reshape(
        n_stiles, HIST_ROWS, POOL_WIDTH)
    x2s, sts_re, sts_im, hists = _mixer_sample(
        x_sample, state_ssm_re[li].reshape(n_sb, N_FLAT), state_ssm_im[li].reshape(n_sb, N_FLAT),
        cache_t, weights)
    x2s = x2s.reshape(n_sb * seq_s, D_MODEL)

    w_router = jnp.concatenate(
        [w_router_group[li], w_router_expert[li].reshape(D_MODEL, N_EXPERTS),
         jnp.zeros((D_MODEL, LANES - N_EXPERT_GROUPS - N_EXPERTS), F32)], axis=1).astype(BF16)
    xs, route, nch = _route(x2p, x2s, g_ffn[li].reshape(1, D_MODEL), w_router)

    n_tiles = (x2p.shape[0] + x2s.shape[0]) // TM
    max_chunks = n_tiles * (2 * TM // CHUNK + N_EXPERTS)
    n_expert_tiles = max_chunks // TILE_CHUNKS + N_EXPERTS
    first_tile, tiles_e, src, dst, n_active = _chunk_tables(nch[:, 0, :N_EXPERTS], n_expert_tiles)
    ys = _experts(first_tile, tiles_e, src, dst, n_active, xs, w_exp_gate[li], w_exp_up[li], w_exp_down[li])
    yp, ysm = _combine(x2p, x2s, route, g_final.reshape(1, D_MODEL), ys)

    sd = state_ssm_re.dtype
    cd = cache_pool.dtype
    y_prompt = yp.reshape(n_pb, seq_p, D_MODEL)
    y_sample = ysm.reshape(n_sb, seq_s, D_MODEL)
    re_p = stp_re.reshape(1, n_pb, SSM_GROUPS, SSM_STATE).astype(sd)
    im_p = stp_im.reshape(1, n_pb, SSM_GROUPS, SSM_STATE).astype(sd)
    hist_p = histp[:, ::N_CHAIN, :][:, 1:, :].reshape(1, n_pb, POOL_HIST, POOL_WIDTH).astype(cd)
    re_s = sts_re.reshape(1, n_sb, SSM_GROUPS, SSM_STATE).astype(sd)
    im_s = sts_im.reshape(1, n_sb, SSM_GROUPS, SSM_STATE).astype(sd)
    hist_s = hists.reshape(n_stiles, 16, N_CHAIN, POOL_WIDTH).transpose(0, 2, 1, 3).reshape(
        n_sb, 16, POOL_WIDTH)[:, 1:, :].reshape(1, n_sb, POOL_HIST, POOL_WIDTH).astype(cd)
    return (y_prompt, y_sample, re_p, im_p, hist_p, re_s, im_s, hist_s)
```

```python
import functools
import math

import jax
import jax.numpy as jnp
from jax import lax
from jax.experimental import pallas as pl
from jax.experimental.pallas import tpu as pltpu

F32 = jnp.float32
BF16 = jnp.bfloat16

D_MODEL = 1024
SSM_WIDTH = 512
SSM_GROUPS = 32
SSM_GROUP = 16
SSM_STATE = 64
N_FLAT = SSM_GROUPS * SSM_STATE
HALF_FLAT = N_FLAT // 2
POOL_WIDTH = 512
POOL_WINDOWS = (2, 4, 8, 16)
POOL_GROUP_IN = 128
POOL_GROUP_OUT = 256
POOL_HIST = 15
N_EXPERTS = 32
EXPERTS_PER_GROUP = 8
N_EXPERT_GROUPS = 4
D_EXPERT = 512
EPS = 1e-6
PAST_LEN = 1024

TM = 256
N_CHAIN = 8
CHAIN_LEN = TM // N_CHAIN
HIST_ROWS = 16 * N_CHAIN

CHUNK = 8
TILE_CHUNKS = TM // CHUNK
CAP_CHUNKS = 96
CAP_ROWS = CAP_CHUNKS * CHUNK
MAX_USED_CHUNKS = 2 * TM // CHUNK + N_EXPERTS * (CHUNK - 1) // CHUNK
N_SPARE_SLOTS = CAP_CHUNKS - MAX_USED_CHUNKS - 1
ZERO_CHUNK = CAP_CHUNKS - 1
LANES = 128

V7X_VMEM_BYTES = 64 * 1024 * 1024
VMEM_LIMIT = V7X_VMEM_BYTES * 13 // 16
STEP_TILES = 4


def _rms(x, g):
    r = lax.rsqrt(jnp.mean(x * x, axis=-1, keepdims=True) + EPS)
    return x * r * g


def _sigmoid(x):
    return 0.5 * jnp.tanh(0.5 * x) + 0.5


def _gelu_tanh(x):
    c = math.sqrt(2.0 / math.pi)
    return x * (0.5 * (1.0 + jnp.tanh(c * (x + 0.044715 * (x * x * x)))))


def _disc_kernel(lre_ref, lim_ref, ldt_ref, bre_ref, bim_ref, cre_ref, cim_ref,
                 tab_ref, wb_ref, wcre_ref, wcim_ref):
    lam_re = jnp.minimum(lre_ref[...], -1e-4)
    lam_im = lim_ref[...]
    dt = jnp.exp(ldt_ref[...])
    mag = jnp.exp(lam_re * dt)
    ang = lam_im * dt
    a_re = mag * jnp.cos(ang)
    a_im = mag * jnp.sin(ang)
    num_re = a_re - 1.0
    num_im = a_im
    den = lam_re * lam_re + lam_im * lam_im
    k_re = ((num_re * lam_re + num_im * lam_im) / den)[:, None, :]
    k_im = ((num_im * lam_re - num_re * lam_im) / den)[:, None, :]
    br = bre_ref[...]
    bi = bim_ref[...]
    bb_re = (k_re * br - k_im * bi).astype(BF16)
    bb_im = (k_re * bi + k_im * br).astype(BF16)
    pr, pi = a_re, a_im
    for _ in range(int(math.log2(CHAIN_LEN))):
        pr, pi = pr * pr - pi * pi, 2.0 * pr * pi
    wb_ref[...] = jnp.zeros_like(wb_ref)
    wcre_ref[...] = jnp.zeros_like(wcre_ref)
    wcim_ref[...] = jnp.zeros_like(wcim_ref)
    groups_per_half = SSM_GROUPS // 2
    for g in range(SSM_GROUPS):
        h, gl = divmod(g, groups_per_half)
        q0, n0 = gl * SSM_GROUP, gl * SSM_STATE
        wb_ref[h, q0:q0 + SSM_GROUP, n0:n0 + SSM_STATE] = bb_re[g]
        wb_ref[h, q0:q0 + SSM_GROUP, HALF_FLAT + n0:HALF_FLAT + n0 + SSM_STATE] = bb_im[g]
        wcre_ref[h, n0:n0 + SSM_STATE, q0:q0 + SSM_GROUP] = cre_ref[g].astype(BF16)
        wcim_ref[h, n0:n0 + SSM_STATE, q0:q0 + SSM_GROUP] = cim_ref[g].astype(BF16)
        for k, v in enumerate((a_re, a_im, pr, pi)):
            tab_ref[k:k + 1, g * SSM_STATE:(g + 1) * SSM_STATE] = v[g:g + 1, :]


def _discretise(a_re, a_im, log_dt, b_re, b_im, c_re, c_im):
    chan_major = lambda b: jnp.swapaxes(b, 1, 2)
    half_w = SSM_WIDTH // 2
    return pl.pallas_call(
        _disc_kernel,
        out_shape=[jax.ShapeDtypeStruct((4, N_FLAT), F32),
                   jax.ShapeDtypeStruct((2, half_w, 2 * HALF_FLAT), BF16),
                   jax.ShapeDtypeStruct((2, HALF_FLAT, half_w), BF16),
                   jax.ShapeDtypeStruct((2, HALF_FLAT, half_w), BF16)],
        name="s5_discretise",
    )(a_re, a_im, log_dt.reshape(SSM_GROUPS, 1), chan_major(b_re), chan_major(b_im),
      jnp.swapaxes(c_re, 1, 2), jnp.swapaxes(c_im, 1, 2))


def _scan_half(hbuf, h, ar, ai, init_re, init_im, store):
    cre = h * N_FLAT
    cim = cre + HALF_FLAT
    hr, hi = init_re, init_im
    for t in range(CHAIN_LEN):
        rows = pl.ds(N_CHAIN * t, N_CHAIN)
        br = hbuf[rows, cre:cre + HALF_FLAT]
        bi = hbuf[rows, cim:cim + HALF_FLAT]
        nr = ar * hr - ai * hi + br
        ni = ar * hi + ai * hr + bi
        if store:
            hbuf[rows, cre:cre + HALF_FLAT] = nr
            hbuf[rows, cim:cim + HALF_FLAT] = ni
        hr, hi = nr, ni
    return hr, hi


N_MIXER_WEIGHTS = 11
N_PAR = 2


def _mixer_kernel(is_prompt, tiles_per_stream, *refs):
    n_in = 1 if is_prompt else 4
    ins = refs[:n_in]
    (gmix, win, wb, wcre, wcim, dsk, wglu, poolw, pscale, wout, atab) = refs[n_in:n_in + N_MIXER_WEIGHTS]
    x2_ref, stre_ref, stim_ref, hist_ref = refs[n_in + N_MIXER_WEIGHTS:n_in + N_MIXER_WEIGHTS + 4]
    scratch = refs[n_in + N_MIXER_WEIGHTS + 4:]
    per = len(scratch) // N_PAR
    lanes = [scratch[p * per:(p + 1) * per] for p in range(N_PAR)]
    x_ref = ins[0]

    tile_in_stream = pl.program_id(0) % tiles_per_stream
    n_lane_blocks = D_MODEL // LANES
    half_w = SSM_WIDTH // 2
    ar_full = atab[0:1, :]
    ai_full = atab[1:2, :]

    if is_prompt:
        @pl.when(pl.program_id(0) == 0)
        def _():
            for p in range(N_PAR):
                cre_s, cim_s, pcarry = lanes[p][8], lanes[p][9], lanes[p][10]
                cre_s[...] = jnp.zeros_like(cre_s)
                cim_s[...] = jnp.zeros_like(cim_s)
                pcarry[...] = jnp.zeros_like(pcarry)

    def a_half(h):
        f0 = h * HALF_FLAT
        return (jnp.broadcast_to(ar_full[:, f0:f0 + HALF_FLAT], (N_CHAIN, HALF_FLAT)),
                jnp.broadcast_to(ai_full[:, f0:f0 + HALF_FLAT], (N_CHAIN, HALF_FLAT)))

    def project(p):
        xperm, hbuf = lanes[p][0], lanes[p][1]
        for j in range(N_CHAIN):
            for cb in range(n_lane_blocks):
                xperm[cb, pl.ds(j, CHAIN_LEN, stride=N_CHAIN), :] = x_ref[
                    p, CHAIN_LEN * j:CHAIN_LEN * (j + 1), cb * LANES:(cb + 1) * LANES]
        x = jnp.concatenate([xperm[cb] for cb in range(n_lane_blocks)], axis=1)
        xn = _rms(x, gmix[...]).astype(BF16)
        proj = jnp.dot(xn, win[...], preferred_element_type=F32)
        ub = proj[:, :SSM_WIDTH].astype(BF16)
        for h in range(2):
            hbuf[:, h * N_FLAT:(h + 1) * N_FLAT] = jnp.dot(
                ub[:, h * half_w:(h + 1) * half_w], wb[h], preferred_element_type=F32)
        return x, proj

    def recur(p):
        hbuf = lanes[p][1]
        if is_prompt:
            fre, fim, hre, him, cre_s, cim_s = lanes[p][4:10]
            zeros = jnp.zeros((N_CHAIN, HALF_FLAT), F32)
            for h in range(2):
                f0 = h * HALF_FLAT
                ar, ai = a_half(h)
                lr, li = _scan_half(hbuf, h, ar, ai, zeros, zeros, store=False)
                fre[:, f0:f0 + HALF_FLAT] = lr
                fim[:, f0:f0 + HALF_FLAT] = li
            fresh = tile_in_stream == 0
            hre[0:1, :] = jnp.where(fresh, 0.0, cre_s[...])
            him[0:1, :] = jnp.where(fresh, 0.0, cim_s[...])
            p_re = atab[2:3, :]
            p_im = atab[3:4, :]
            for j in range(N_CHAIN - 1):
                sr = hre[j:j + 1, :]
                si = him[j:j + 1, :]
                hre[j + 1:j + 2, :] = fre[j:j + 1, :] + p_re * sr - p_im * si
                him[j + 1:j + 2, :] = fim[j:j + 1, :] + p_re * si + p_im * sr
            init_re = hre[...]
            init_im = him[...]
        else:
            init_re = ins[1][p]
            init_im = ins[2][p]
        fin_re = []
        fin_im = []
        for h in range(2):
            f0 = h * HALF_FLAT
            ar, ai = a_half(h)
            er, ei = _scan_half(hbuf, h, ar, ai, init_re[:, f0:f0 + HALF_FLAT],
                                init_im[:, f0:f0 + HALF_FLAT], store=True)
            fin_re.append(er)
            fin_im.append(ei)
        end_re = jnp.concatenate(fin_re, axis=1)
        end_im = jnp.concatenate(fin_im, axis=1)
        if is_prompt:
            cre_s[...] = end_re[N_CHAIN - 1:N_CHAIN, :]
            cim_s[...] = end_im[N_CHAIN - 1:N_CHAIN, :]
            stre_ref[p] = end_re[N_CHAIN - 1:N_CHAIN, :]
            stim_ref[p] = end_im[N_CHAIN - 1:N_CHAIN, :]
        else:
            stre_ref[p] = end_re
            stim_ref[p] = end_im

    def finish(p, x, proj):
        hbuf, res, xpbuf = lanes[p][1], lanes[p][2], lanes[p][3]
        u_s = proj[:, :SSM_WIDTH]
        u_p = proj[:, SSM_WIDTH:SSM_WIDTH + POOL_WIDTH]
        gate_s = proj[:, SSM_WIDTH + POOL_WIDTH:SSM_WIDTH + POOL_WIDTH + D_MODEL]
        gate_p = proj[:, SSM_WIDTH + POOL_WIDTH + D_MODEL:]
        ys = []
        for h in range(2):
            c0 = h * N_FLAT
            h_re = hbuf[:, c0:c0 + HALF_FLAT].astype(BF16)
            h_im = hbuf[:, c0 + HALF_FLAT:c0 + N_FLAT].astype(BF16)
            ys.append(jnp.dot(h_re, wcre[h], preferred_element_type=F32)
                      - jnp.dot(h_im, wcim[h], preferred_element_type=F32))
        y = jnp.concatenate(ys, axis=1) + dsk[...] * u_s
        g = _gelu_tanh(y).astype(BF16)
        glu = jnp.dot(g, wglu[...], preferred_element_type=F32)
        o_s = glu[:, :D_MODEL] * _sigmoid(glu[:, D_MODEL:])

        xpbuf[HIST_ROWS:HIST_ROWS + TM, :] = u_p
        tail = u_p[TM - HIST_ROWS:, :]
        row = lax.broadcasted_iota(jnp.int32, (TM, 1), 0)
        if is_prompt:
            pcarry = lanes[p][10]
            first_chain = (lax.broadcasted_iota(jnp.int32, (HIST_ROWS, POOL_WIDTH), 0) % N_CHAIN) == 0
            carried = jnp.where(tile_in_stream == 0, 0.0, pcarry[...])
            xpbuf[0:HIST_ROWS, :] = jnp.where(first_chain, carried, pltpu.roll(tail, 1, 0))
            new_carry = pltpu.roll(tail, HIST_ROWS - (N_CHAIN - 1), 0)
            pcarry[...] = new_carry
            hist_ref[p] = new_carry
            pos1 = tile_in_stream * TM + CHAIN_LEN * (row % N_CHAIN) + row // N_CHAIN + 1
        else:
            xpbuf[0:HIST_ROWS, :] = ins[3][p]
            hist_ref[p] = tail
            pos1 = PAST_LEN + row // N_CHAIN + 1

        o_ps = []
        for gi, w in enumerate(POOL_WINDOWS):
            c0 = gi * POOL_GROUP_IN
            acc = xpbuf[HIST_ROWS:HIST_ROWS + TM, c0:c0 + POOL_GROUP_IN]
            for k in range(1, w):
                acc = acc + xpbuf[HIST_ROWS - N_CHAIN * k:HIST_ROWS - N_CHAIN * k + TM, c0:c0 + POOL_GROUP_IN]
            cnt = jnp.minimum(w, pos1).astype(F32)
            pooled = acc / cnt
            z = (pooled - u_p[:, c0:c0 + POOL_GROUP_IN]).astype(BF16)
            o_ps.append(jnp.dot(z, poolw[gi], preferred_element_type=F32))
        o_p = jnp.concatenate(o_ps, axis=1) * pscale[...]

        merged = (_sigmoid(gate_s) * o_s + _sigmoid(gate_p) * o_p).astype(BF16)
        x2 = x + jnp.dot(merged, wout[...], preferred_element_type=F32)
        for cb in range(n_lane_blocks):
            res[cb] = x2[:, cb * LANES:(cb + 1) * LANES]
        for j in range(N_CHAIN):
            for cb in range(n_lane_blocks):
                x2_ref[p, CHAIN_LEN * j:CHAIN_LEN * (j + 1), cb * LANES:(cb + 1) * LANES] = res[
                    cb, pl.ds(j, CHAIN_LEN, stride=N_CHAIN), :]

    projected = [None] * N_PAR
    for k in range(N_PAR + 2):
        if 0 <= k - 2 < N_PAR:
            finish(k - 2, *projected[k - 2])
        if 0 <= k - 1 < N_PAR:
            recur(k - 1)
        if k < N_PAR:
            projected[k] = project(k)


def _const_spec(shape):
    nd = len(shape)
    return pl.BlockSpec(shape, lambda i, _nd=nd: (0,) * _nd)


def _mixer_weight_specs():
    return [
        _const_spec((1, D_MODEL)),
        _const_spec((D_MODEL, 3 * D_MODEL)),
        _const_spec((2, SSM_WIDTH // 2, N_FLAT)),
        _const_spec((2, HALF_FLAT, SSM_WIDTH // 2)),
        _const_spec((2, HALF_FLAT, SSM_WIDTH // 2)),
        _const_spec((1, SSM_WIDTH)),
        _const_spec((SSM_WIDTH, 2 * D_MODEL)),
        _const_spec((len(POOL_WINDOWS), POOL_GROUP_IN, POOL_GROUP_OUT)),
        _const_spec((1, D_MODEL)),
        _const_spec((D_MODEL, D_MODEL)),
        _const_spec((4, N_FLAT)),
    ]


def _mixer_common_scratch():
    return [
        pltpu.VMEM((D_MODEL // LANES, TM, LANES), F32),
        pltpu.VMEM((TM, 2 * N_FLAT), F32),
        pltpu.VMEM((D_MODEL // LANES, TM, LANES), F32),
        pltpu.VMEM((HIST_ROWS + TM, POOL_WIDTH), F32),
    ]


def _mixer_prompt(x, weights):
    n_streams, seq, _ = x.shape
    assert n_streams % N_PAR == 0 and seq % TM == 0
    tiles_per_stream = seq // TM
    blk = lambda shape: pl.BlockSpec(shape, lambda i: (i // tiles_per_stream, 0, 0))
    row_spec = pl.BlockSpec((N_PAR, TM, D_MODEL), lambda i: (i // tiles_per_stream, i % tiles_per_stream, 0))
    lane_scratch = _mixer_common_scratch() + [
        pltpu.VMEM((N_CHAIN, N_FLAT), F32), pltpu.VMEM((N_CHAIN, N_FLAT), F32),
        pltpu.VMEM((N_CHAIN, N_FLAT), F32), pltpu.VMEM((N_CHAIN, N_FLAT), F32),
        pltpu.VMEM((1, N_FLAT), F32), pltpu.VMEM((1, N_FLAT), F32),
        pltpu.VMEM((HIST_ROWS, POOL_WIDTH), F32),
    ]
    return pl.pallas_call(
        functools.partial(_mixer_kernel, True, tiles_per_stream),
        grid=(n_streams // N_PAR * tiles_per_stream,),
        in_specs=[row_spec] + _mixer_weight_specs(),
        out_specs=[row_spec, blk((N_PAR, 1, N_FLAT)), blk((N_PAR, 1, N_FLAT)),
                   blk((N_PAR, HIST_ROWS, POOL_WIDTH))],
        out_shape=[jax.ShapeDtypeStruct((n_streams, seq, D_MODEL), F32),
                   jax.ShapeDtypeStruct((n_streams, 1, N_FLAT), F32),
                   jax.ShapeDtypeStruct((n_streams, 1, N_FLAT), F32),
                   jax.ShapeDtypeStruct((n_streams, HIST_ROWS, POOL_WIDTH), F32)],
        scratch_shapes=lane_scratch * N_PAR,
        compiler_params=pltpu.CompilerParams(dimension_semantics=("arbitrary",), vmem_limit_bytes=VMEM_LIMIT),
        name="mixer_prompt",
    )(x, *weights)


def _mixer_sample(x, h0_re, h0_im, cache_t, weights):
    n_streams, seq, _ = x.shape
    assert seq == CHAIN_LEN and n_streams % (N_CHAIN * N_PAR) == 0
    n_tiles = n_streams // N_CHAIN
    blk = lambda shape: pl.BlockSpec(shape, lambda i: (i, 0, 0))
    row_spec = blk((N_PAR, TM, D_MODEL))
    st_spec = blk((N_PAR, N_CHAIN, N_FLAT))
    hist_spec = blk((N_PAR, HIST_ROWS, POOL_WIDTH))
    tiles = lambda v: v.reshape((n_tiles, -1) + v.shape[-1:])
    return pl.pallas_call(
        functools.partial(_mixer_kernel, False, 1),
        grid=(n_tiles // N_PAR,),
        in_specs=[row_spec, st_spec, st_spec, hist_spec] + _mixer_weight_specs(),
        out_specs=[row_spec, st_spec, st_spec, hist_spec],
        out_shape=[jax.ShapeDtypeStruct((n_tiles, TM, D_MODEL), F32),
                   jax.ShapeDtypeStruct((n_tiles, N_CHAIN, N_FLAT), F32),
                   jax.ShapeDtypeStruct((n_tiles, N_CHAIN, N_FLAT), F32),
                   jax.ShapeDtypeStruct((n_tiles, HIST_ROWS, POOL_WIDTH), F32)],
        scratch_shapes=_mixer_common_scratch() * N_PAR,
        compiler_params=pltpu.CompilerParams(dimension_semantics=("arbitrary",), vmem_limit_bytes=VMEM_LIMIT),
        name="mixer_sample",
    )(tiles(x), tiles(h0_re), tiles(h0_im), cache_t, *weights)


def _to_chunks(lo, hi):
    n = lo.shape[0] // CHUNK
    half = D_MODEL // 2
    both = jnp.concatenate([lo.reshape(n, CHUNK, half), hi.reshape(n, CHUNK, half)], axis=1)
    return both.astype(BF16)


def _from_chunks(blk):
    n = blk.shape[0]
    half = D_MODEL // 2
    f = blk.astype(F32)
    lo = f[:, :CHUNK, :].reshape(n * CHUNK, half).astype(BF16)
    hi = f[:, CHUNK:, :].reshape(n * CHUNK, half).astype(BF16)
    return lo, hi


def _route_tiles(x_ref, g_ref, wr_ref, ltri_ref, utri_ref, xs_ref, route_ref, nch_ref):
    tiles = range(STEP_TILES)
    each = lambda fn, *cols: [fn(*(c[h] for c in cols)) for h in tiles]
    row_max = lambda v: jnp.max(v, axis=1, keepdims=True)
    row_min = lambda v: jnp.min(v, axis=1, keepdims=True)
    row_sum = lambda v: jnp.sum(v, axis=1, keepdims=True)

    lane = lax.broadcasted_iota(jnp.int32, (TM, LANES), 1)
    lane_f = lane.astype(F32)
    big = jnp.float32(1 << 20)
    neg = jnp.float32(-jnp.inf)
    gmask = lane < N_EXPERT_GROUPS
    eid = lane - N_EXPERT_GROUPS
    lane_grp = (eid >> 3).astype(F32)
    is_expert = (eid >= 0) & (eid < N_EXPERTS)

    xn = [_rms(x_ref[pl.ds(h * TM, TM), :], g_ref[...]).astype(BF16) for h in tiles]
    logits = each(lambda v: jnp.dot(v, wr_ref[...], preferred_element_type=F32), xn)
    m = each(lambda lg: row_max(jnp.where(gmask, lg, neg)), logits)
    grp = each(lambda lg, mm: row_min(jnp.where(gmask & (lg == mm), lane_f, big)), logits, m)
    wg = each(lambda lg, mm: 1.0 / row_sum(jnp.where(gmask, jnp.exp(lg - mm), 0.0)), logits, m)
    emask = each(lambda g: is_expert & (lane_grp == g), grp)
    v1 = each(lambda lg, em: row_max(jnp.where(em, lg, neg)), logits, emask)
    i1 = each(lambda lg, em, v: row_min(jnp.where(em & (lg == v), lane_f, big)), logits, emask, v1)
    emask2 = each(lambda em, i: em & (lane_f != i), emask, i1)
    v2 = each(lambda lg, em: row_max(jnp.where(em, lg, neg)), logits, emask2)
    i2 = each(lambda lg, em, v: row_min(jnp.where(em & (lg == v), lane_f, big)), logits, emask2, v2)
    e21 = each(lambda a, b: jnp.exp(b - a), v1, v2)
    w1 = each(lambda g, e: g / (1.0 + e), wg, e21)
    w2 = each(lambda g, e: g * e / (1.0 + e), wg, e21)

    a1 = each(lambda i: lane_f == (i - N_EXPERT_GROUPS), i1)
    a2 = each(lambda i: lane_f == (i - N_EXPERT_GROUPS), i2)
    a = each(lambda p, q: (p | q).astype(F32), a1, a2)
    before = each(lambda v: jnp.dot(ltri_ref[...], v.astype(BF16), preferred_element_type=F32), a)
    cnt = each(lambda v: jnp.sum(v, axis=0, keepdims=True), a)
    nch16 = each(lambda c: jnp.broadcast_to(jnp.floor((c + (CHUNK - 1)) * (1.0 / CHUNK)), (16, LANES)), cnt)
    start = each(lambda n: jnp.dot(n.astype(BF16), utri_ref[...], preferred_element_type=F32), nch16)
    slot = each(lambda bf, st: bf + CHUNK * st[0:1, :], before, start)
    d1 = each(lambda p, sl: row_sum(jnp.where(p, sl, 0.0)), a1, slot)
    d2 = each(lambda p, sl: row_sum(jnp.where(p, sl, 0.0)), a2, slot)
    route = each(lambda p, q, u, v: jnp.where(lane == 0, p, jnp.where(lane == 1, q, jnp.where(
        lane == 2, u, jnp.where(lane == 3, v, 0.0)))), d1, d2, w1, w2)

    dest = lax.broadcasted_iota(jnp.int32, (CAP_ROWS, TM), 0)
    half = D_MODEL // 2
    for h in tiles:
        route_ref[pl.ds(h * TM, TM), :] = route[h]
        nch_ref[h] = nch16[h][0:8, :].astype(jnp.int32)
        dt = jnp.transpose(jnp.where(lane < 2, route[h], 0.0)).astype(jnp.int32)
        perm = ((dest == dt[0:1, :]) | (dest == dt[1:2, :])).astype(F32).astype(BF16)
        lo = jnp.dot(perm, xn[h][:, :half], preferred_element_type=F32)
        hi = jnp.dot(perm, xn[h][:, half:], preferred_element_type=F32)
        xs_ref[pl.ds(h * CAP_CHUNKS, CAP_CHUNKS)] = _to_chunks(lo, hi)


N_ROUTE_BUFFERS = 3


def _route_kernel(n_prompt_steps, n_sample_steps, x2p_hbm, x2s_hbm, g_hbm, wr_hbm, ltri_hbm, utri_hbm,
                  xs_hbm, route_hbm, nch_hbm):
    rows = STEP_TILES * TM
    whole = lambda shape: pl.BlockSpec(shape, lambda i: (0,) * len(shape))

    def stream(n_steps, first_step, x_hbm):
        deep = pl.Buffered(N_ROUTE_BUFFERS) if n_steps >= N_ROUTE_BUFFERS else None
        pltpu.emit_pipeline(
            _route_tiles,
            grid=(n_steps,),
            in_specs=[pl.BlockSpec((rows, D_MODEL), lambda i: (i, 0), pipeline_mode=deep),
                      whole((1, D_MODEL)), whole((D_MODEL, LANES)), whole((TM, TM)), whole((LANES, LANES))],
            out_specs=[pl.BlockSpec((STEP_TILES * CAP_CHUNKS, 2 * CHUNK, D_MODEL // 2),
                                    lambda i: (i + first_step, 0, 0)),
                       pl.BlockSpec((rows, LANES), lambda i: (i + first_step, 0)),
                       pl.BlockSpec((STEP_TILES, 8, LANES), lambda i: (i + first_step, 0, 0))],
        )(x_hbm, g_hbm, wr_hbm, ltri_hbm, utri_hbm, xs_hbm, route_hbm, nch_hbm)

    stream(n_prompt_steps, 0, x2p_hbm)
    stream(n_sample_steps, n_prompt_steps, x2s_hbm)


def _route(x2p, x2s, g_ffn, w_router):
    n_prompt_tiles = x2p.shape[0] // TM
    n_sample_tiles = x2s.shape[0] // TM
    n_tiles = n_prompt_tiles + n_sample_tiles
    assert n_prompt_tiles % STEP_TILES == 0 and n_sample_tiles % STEP_TILES == 0
    any_spec = pl.BlockSpec(memory_space=pl.ANY)
    r = jnp.arange(TM)
    ltri = (r[None, :] < r[:, None]).astype(BF16)
    e = jnp.arange(LANES)
    utri = (e[:, None] < e[None, :]).astype(BF16)
    return pl.pallas_call(
        functools.partial(_route_kernel, n_prompt_tiles // STEP_TILES, n_sample_tiles // STEP_TILES),
        in_specs=[any_spec] * 6,
        out_specs=[any_spec] * 3,
        out_shape=[jax.ShapeDtypeStruct((n_tiles * CAP_CHUNKS, 2 * CHUNK, D_MODEL // 2), BF16),
                   jax.ShapeDtypeStruct((n_tiles * TM, LANES), F32),
                   jax.ShapeDtypeStruct((n_tiles, 8, LANES), jnp.int32)],
        compiler_params=pltpu.CompilerParams(vmem_limit_bytes=VMEM_LIMIT),
        name="route_sort",
    )(x2p, x2s, g_ffn, w_router, ltri, utri)


def _spare_chunk(k):
    return (1 + k // N_SPARE_SLOTS) * CAP_CHUNKS + MAX_USED_CHUNKS + k % N_SPARE_SLOTS


def _chunk_tables(nch, n_expert_tiles):
    n_tiles = nch.shape[0]
    i32 = jnp.int32
    start = jnp.cumsum(nch, axis=1) - nch
    off = jnp.cumsum(nch, axis=0) - nch
    per_expert = jnp.sum(nch, axis=0)
    tiles_e = (per_expert + TILE_CHUNKS - 1) // TILE_CHUNKS
    cum_tiles = jnp.cumsum(tiles_e)
    first_tile = cum_tiles - tiles_e
    n_active = cum_tiles[-1].astype(i32)
    t = jnp.arange(n_expert_tiles, dtype=i32)
    te = jnp.minimum(jnp.sum((t[:, None] >= cum_tiles[None, :]).astype(i32), axis=1), N_EXPERTS - 1)
    onehot = te[:, None] == jnp.arange(N_EXPERTS, dtype=i32)[None, :]
    pick = lambda tab: jnp.sum(jnp.where(onehot[:, :, None], tab.T[None, :, :], 0), axis=1)
    off_t, nch_t, start_t = pick(off), pick(nch), pick(start)
    k = t - jnp.sum(jnp.where(onehot, first_tile[None, :], 0), axis=1)
    q = (TILE_CHUNKS * k)[:, None] + jnp.arange(TILE_CHUNKS, dtype=i32)[None, :]
    in_run = (off_t[:, None, :] <= q[:, :, None]) & (q[:, :, None] < (off_t + nch_t)[:, None, :])
    run_src = (jnp.arange(n_tiles, dtype=i32) * CAP_CHUNKS)[None, :] + start_t - off_t
    src = jnp.sum(jnp.where(in_run, run_src[:, None, :], 0), axis=-1) + q
    valid = jnp.any(in_run, axis=-1) & (t < n_active)[:, None]
    src = jnp.where(valid, src, ZERO_CHUNK).astype(i32)
    spare = _spare_chunk((t % N_RING)[:, None] * TILE_CHUNKS + jnp.arange(TILE_CHUNKS, dtype=i32)[None, :])
    dst = jnp.where(valid, src, spare).astype(i32)
    dst = jnp.concatenate([spare[1:N_RING], dst], axis=0)
    return first_tile.astype(i32), tiles_e.astype(i32), src.reshape(-1), dst.reshape(-1), n_active.reshape(1)


N_RING = 5


def _expert_kernel(first_ref, ntile_ref, src_ref, dst_ref, nact_ref, xs_hbm, wg_ref, wu_ref, wd_ref, ys_hbm,
                   *scratch):
    xbufs = scratch[:N_RING]
    obufs = scratch[N_RING:2 * N_RING]
    gsem, ssem, wg16, wu16, wd16 = scratch[2 * N_RING:]
    e = pl.program_id(0)
    n_active = nact_ref[0]
    ahead = N_RING - 1

    def gather_copy(tile, slot, c):
        return pltpu.make_async_copy(xs_hbm.at[src_ref[tile * TILE_CHUNKS + c]], xbufs[slot].at[c],
                                     gsem.at[slot])

    def scatter_copy(tile, slot, c):
        return pltpu.make_async_copy(obufs[slot].at[c], ys_hbm.at[dst_ref[(tile + ahead) * TILE_CHUNKS + c]],
                                     ssem.at[slot])

    def start_all(copy, tile, slot):
        for c in range(TILE_CHUNKS):
            copy(tile, slot, c).start()

    def wait_all(copy, tile, slot):
        for c in range(TILE_CHUNKS):
            copy(tile, slot, c).wait()

    @pl.when(e == 0)
    def _():
        for v in range(ahead):
            start_all(gather_copy, v, v)
        for u in range(-ahead, 0):
            obufs[u % N_RING][...] = jnp.zeros_like(obufs[u % N_RING])
        for u in range(-ahead, -1):
            start_all(scatter_copy, u, u % N_RING)

    def do_tile(tile, slot, cast_weights):
        nxt = (slot + ahead) % N_RING
        old = (slot + 1) % N_RING
        wait_all(gather_copy, tile, slot)
        lo, hi = _from_chunks(xbufs[slot][...])
        x = jnp.concatenate([lo, hi], axis=1)
        if cast_weights:
            wg16[...] = wg_ref[0].astype(BF16)
        gate = jnp.dot(x, wg16[...], preferred_element_type=F32)
        start_all(scatter_copy, tile - 1, nxt)
        if cast_weights:
            wu16[...] = wu_ref[0].astype(BF16)
        up = jnp.dot(x, wu16[...], preferred_element_type=F32)
        hmid = (gate * _sigmoid(gate) * up).astype(BF16)
        start_all(gather_copy, tile + ahead, nxt)
        if cast_weights:
            wd16[...] = wd_ref[0].astype(BF16)
        y = jnp.dot(hmid, wd16[...], preferred_element_type=F32)
        half = D_MODEL // 2
        wait_all(scatter_copy, tile - ahead, old)
        obufs[slot][...] = _to_chunks(y[:, :half], y[:, half:])

    def tile_body(k, carry, cast_weights=False):
        tile = first_ref[e] + k
        for slot in range(N_RING):
            @pl.when(tile % N_RING == slot)
            def _():
                do_tile(tile, slot, cast_weights)
        return carry

    @pl.when(ntile_ref[e] > 0)
    def _():
        tile_body(0, 0, cast_weights=True)

    lax.fori_loop(1, ntile_ref[e], tile_body, 0)

    @pl.when(e == pl.num_programs(0) - 1)
    def _():
        last = n_active - 1
        for slot in range(N_RING):
            @pl.when(last % N_RING == slot)
            def _():
                for d in range(N_RING - 2, 0, -1):
                    wait_all(scatter_copy, last - d, (slot - d) % N_RING)
                start_all(scatter_copy, last, slot)
                wait_all(scatter_copy, last, slot)
                for d in range(1, N_RING):
                    wait_all(gather_copy, last + d, (slot + d) % N_RING)


def _experts(first_tile, tiles_e, src, dst, n_active, xs, w_gate, w_up, w_down):
    chunk_shape = (2 * CHUNK, D_MODEL // 2)
    assert _spare_chunk(N_RING * TILE_CHUNKS - 1) < xs.shape[0]
    tile_buf = pltpu.VMEM((TILE_CHUNKS,) + chunk_shape, BF16)
    grid_spec = pltpu.PrefetchScalarGridSpec(
        num_scalar_prefetch=5,
        grid=(N_EXPERTS,),
        in_specs=[pl.BlockSpec(memory_space=pl.ANY),
                  pl.BlockSpec((1, D_MODEL, D_EXPERT), lambda e, *_: (e, 0, 0)),
                  pl.BlockSpec((1, D_MODEL, D_EXPERT), lambda e, *_: (e, 0, 0)),
                  pl.BlockSpec((1, D_EXPERT, D_MODEL), lambda e, *_: (e, 0, 0))],
        out_specs=pl.BlockSpec(memory_space=pl.ANY),
        scratch_shapes=[tile_buf] * (2 * N_RING) + [
                        pltpu.SemaphoreType.DMA((N_RING,)),
                        pltpu.SemaphoreType.DMA((N_RING,)),
                        pltpu.VMEM((D_MODEL, D_EXPERT), BF16),
                        pltpu.VMEM((D_MODEL, D_EXPERT), BF16),
                        pltpu.VMEM((D_EXPERT, D_MODEL), BF16)],
    )
    return pl.pallas_call(
        _expert_kernel,
        grid_spec=grid_spec,
        out_shape=jax.ShapeDtypeStruct(xs.shape, xs.dtype),
        input_output_aliases={5: 0},
        compiler_params=pltpu.CompilerParams(dimension_semantics=("arbitrary",), vmem_limit_bytes=VMEM_LIMIT),
        name="expert_ffn",
    )(first_tile, tiles_e, src, dst, n_active, xs, w_gate, w_up, w_down)


def _combine_tiles(x_ref, route_ref, gfin_ref, ys_ref, out_ref):
    for h in range(STEP_TILES):
        rows = pl.ds(h * TM, TM)
        lo, hi = _from_chunks(ys_ref[pl.ds(h * CAP_CHUNKS, CAP_CHUNKS)])
        route = route_ref[rows, :]
        d1 = route[:, 0:1].astype(jnp.int32)
        d2 = route[:, 1:2].astype(jnp.int32)
        w1 = route[:, 2:3]
        w2 = route[:, 3:4]
        dest = lax.broadcasted_iota(jnp.int32, (TM, CAP_ROWS), 1)
        sel = jnp.where(dest == d1, w1, jnp.where(dest == d2, w2, 0.0)).astype(BF16)
        moe = jnp.concatenate([jnp.dot(sel, lo, preferred_element_type=F32),
                               jnp.dot(sel, hi, preferred_element_type=F32)], axis=1)
        out_ref[rows, :] = _rms(x_ref[rows, :] + moe, gfin_ref[...])


N_COMBINE_BUFFERS = 3


def _combine_kernel(n_prompt_steps, n_sample_steps, x2p_hbm, x2s_hbm, route_hbm, gfin_hbm, ys_hbm,
                    outp_hbm, outs_hbm):
    rows = STEP_TILES * TM

    def stream(n_steps, first_step, x_hbm, out_hbm):
        deep = pl.Buffered(N_COMBINE_BUFFERS) if n_steps >= N_COMBINE_BUFFERS else None
        pltpu.emit_pipeline(
            _combine_tiles,
            grid=(n_steps,),
            in_specs=[pl.BlockSpec((rows, D_MODEL), lambda i: (i, 0), pipeline_mode=deep),
                      pl.BlockSpec((rows, LANES), lambda i: (i + first_step, 0)),
                      pl.BlockSpec((1, D_MODEL), lambda i: (0, 0)),
                      pl.BlockSpec((STEP_TILES * CAP_CHUNKS, 2 * CHUNK, D_MODEL // 2),
                                   lambda i: (i + first_step, 0, 0), pipeline_mode=deep)],
            out_specs=[pl.BlockSpec((rows, D_MODEL), lambda i: (i, 0))],
        )(x_hbm, route_hbm, gfin_hbm, ys_hbm, out_hbm)

    stream(n_prompt_steps, 0, x2p_hbm, outp_hbm)
    stream(n_sample_steps, n_prompt_steps, x2s_hbm, outs_hbm)


def _combine(x2p, x2s, route, g_final, ys):
    n_prompt_tiles = x2p.shape[0] // TM
    n_sample_tiles = x2s.shape[0] // TM
    assert n_prompt_tiles % STEP_TILES == 0 and n_sample_tiles % STEP_TILES == 0
    any_spec = pl.BlockSpec(memory_space=pl.ANY)
    return pl.pallas_call(
        functools.partial(_combine_kernel, n_prompt_tiles // STEP_TILES, n_sample_tiles // STEP_TILES),
        in_specs=[any_spec] * 5,
        out_specs=[any_spec] * 2,
        out_shape=[jax.ShapeDtypeStruct(x2p.shape, F32), jax.ShapeDtypeStruct(x2s.shape, F32)],
        compiler_params=pltpu.CompilerParams(vmem_limit_bytes=VMEM_LIMIT),
        name="combine_norm",
    )(x2p, x2s, route, g_final, ys)


def kernel(x_prompt, x_sample, state_ssm_re, state_ssm_im, cache_pool, g_mix, w_in, ssm_a_re, ssm_a_im,
           ssm_log_dt, ssm_b_re, ssm_b_im, ssm_c_re, ssm_c_im, ssm_d, w_glu_a, w_glu_b, pool_w, pool_scale,
           w_out, g_ffn, w_router_group, w_router_expert, w_exp_gate, w_exp_up, w_exp_down, g_final):
    li = 0
    n_pb, seq_p, _ = x_prompt.shape
    n_sb, seq_s, _ = x_sample.shape

    a_tab, wb, wc_re, wc_im = _discretise(
        ssm_a_re[li], ssm_a_im[li], ssm_log_dt[li], ssm_b_re[li], ssm_b_im[li], ssm_c_re[li], ssm_c_im[li])
    weights = [
        g_mix[li].reshape(1, D_MODEL),
        w_in[li].astype(BF16),
        wb,
        wc_re,
        wc_im,
        ssm_d[li].reshape(1, SSM_WIDTH),
        jnp.concatenate([w_glu_a[li], w_glu_b[li]], axis=1).astype(BF16),
        pool_w[li].astype(BF16),
        pool_scale[li].reshape(1, D_MODEL),
        w_out[li].astype(BF16),
        a_tab,
    ]

    x2p, stp_re, stp_im, histp = _mixer_prompt(x_prompt, weights)
    x2p = x2p.reshape(n_pb * seq_p, D_MODEL)

    n_stiles = n_sb // N_CHAIN
    cache16 = jnp.pad(cache_pool[li], ((0, 0), (1, 0), (0, 0)))
    cache_t = cache16.reshape(n_stiles, N_CHAIN, 16, POOL_WIDTH).transpose(0, 2, 1, 3).reshape(
        n_stiles, HIST_ROWS, POOL_WIDTH)
    x2s, sts_re, sts_im, hists = _mixer_sample(
        x_sample, state_ssm_re[li].reshape(n_sb, N_FLAT), state_ssm_im[li].reshape(n_sb, N_FLAT),
        cache_t, weights)
    x2s = x2s.reshape(n_sb * seq_s, D_MODEL)

    w_router = jnp.concatenate(
        [w_router_group[li], w_router_expert[li].reshape(D_MODEL, N_EXPERTS),
         jnp.zeros((D_MODEL, LANES - N_EXPERT_GROUPS - N_EXPERTS), F32)], axis=1).astype(BF16)
    xs, route, nch = _route(x2p, x2s, g_ffn[li].reshape(1, D_MODEL), w_router)

    n_tiles = (x2p.shape[0] + x2s.shape[0]) // TM
    max_chunks = n_tiles * (2 * TM // CHUNK + N_EXPERTS)
    n_expert_tiles = max_chunks // TILE_CHUNKS + N_EXPERTS
    first_tile, tiles_e, src, dst, n_active = _chunk_tables(nch[:, 0, :N_EXPERTS], n_expert_tiles)
    ys = _experts(first_tile, tiles_e, src, dst, n_active, xs, w_exp_gate[li], w_exp_up[li], w_exp_down[li])
    yp, ysm = _combine(x2p, x2s, route, g_final.reshape(1, D_MODEL), ys)

    sd = state_ssm_re.dtype
    cd = cache_pool.dtype
    y_prompt = yp.reshape(n_pb, seq_p, D_MODEL)
    y_sample = ysm.reshape(n_sb, seq_s, D_MODEL)
    re_p = stp_re.reshape(1, n_pb, SSM_GROUPS, SSM_STATE).astype(sd)
    im_p = stp_im.reshape(1, n_pb, SSM_GROUPS, SSM_STATE).astype(sd)
    hist_p = histp[:, ::N_CHAIN, :][:, 1:, :].reshape(1, n_pb, POOL_HIST, POOL_WIDTH).astype(cd)
    re_s = sts_re.reshape(1, n_sb, SSM_GROUPS, SSM_STATE).astype(sd)
    im_s = sts_im.reshape(1, n_sb, SSM_GROUPS, SSM_STATE).astype(sd)
    hist_s = hists.reshape(n_stiles, 16, N_CHAIN, POOL_WIDTH).transpose(0, 2, 1, 3).reshape(
        n_sb, 16, POOL_WIDTH)[:, 1:, :].reshape(1, n_sb, POOL_HIST, POOL_WIDTH).astype(cd)
    return (y_prompt, y_sample, re_p, im_p, hist_p, re_s, im_s, hist_s)
```
